```python
import jax, jax.numpy as jnp
from jax import lax
import numpy as np

D_MODEL = 2048
BATCH = 8
SEQ = 2048
DEPTH = 2

MEM_TOKENS = 256
MEM_HEADS = 4
MEM_HEAD_DIM = 256
MEM_WIDTH = MEM_HEADS * MEM_HEAD_DIM
MLA_HEADS = 8
Q_LORA_RANK = 512
KV_LORA_RANK = 256
QK_NOPE_DIM = 128
QK_ROPE_DIM = 64
QK_HEAD_DIM = QK_NOPE_DIM + QK_ROPE_DIM
V_HEAD_DIM = 128
MLA_WIDTH = MLA_HEADS * V_HEAD_DIM
ROPE_THETA = 10000.0
Q_BLOCK = 128
POOL_WINDOWS = (2, 4, 8, 16)
POOL_GROUPS = 4
POOL_GROUP_DIM = 256
POOL_WIDTH = POOL_GROUPS * POOL_GROUP_DIM
N_BRANCHES = 3
IN_SPLITS = (Q_LORA_RANK,
             Q_LORA_RANK + KV_LORA_RANK + QK_ROPE_DIM,
             Q_LORA_RANK + KV_LORA_RANK + QK_ROPE_DIM + POOL_WIDTH,
             Q_LORA_RANK + KV_LORA_RANK + QK_ROPE_DIM + POOL_WIDTH + MEM_WIDTH)
IN_WIDTH = IN_SPLITS[-1] + N_BRANCHES * D_MODEL
D_FF_DENSE = 5632
N_EXPERTS = 8
TOP_K = 2
D_FF_EXPERT = 7168
MOE_BLOCK = 256
N_DENSE_LAYERS = (DEPTH + 1) // 2
N_MOE_LAYERS = DEPTH // 2
NORM_EPS = 1e-6

kernel_name = 'hybrid_mla_pool_memory_moe_block'


def rmsnorm(x, g):
    xf = x.astype(jnp.float32)
    y = xf * lax.rsqrt(jnp.mean(xf * xf, axis=-1, keepdims=True) + NORM_EPS)
    return (y * g.astype(jnp.float32)).astype(x.dtype)


def rope_tables(positions):
    inv_freq = ROPE_THETA ** (-jnp.arange(0, QK_ROPE_DIM, 2, dtype=jnp.float32) / QK_ROPE_DIM)
    ang = positions.astype(jnp.float32)[..., None] * inv_freq
    return jnp.cos(ang)[:, :, None, :], jnp.sin(ang)[:, :, None, :]


def apply_rope(x, cos, sin):
    half = x.shape[-1] // 2
    x1, x2 = x[..., :half], x[..., half:]
    c = cos.astype(x.dtype)
    s = sin.astype(x.dtype)
    return jnp.concatenate([x1 * c - x2 * s, x2 * c + x1 * s], axis=-1)


def rope_tail(x, cos, sin):
    return jnp.concatenate([x[..., :QK_NOPE_DIM], apply_rope(x[..., QK_NOPE_DIM:], cos, sin)], axis=-1)


def causal_block_attention(q, k, v):
    B, S, H, Dh = q.shape
    nq = S // Q_BLOCK
    scale = Dh ** -0.5
    qb = q.reshape(B, nq, Q_BLOCK, H, Dh).transpose(1, 0, 2, 3, 4)
    kpos = jnp.arange(S)

    def one_block(args):
        qi, i = args
        s = jnp.einsum('bqhd,bkhd->bhqk', qi, k, preferred_element_type=jnp.float32) * scale
        qpos = i * Q_BLOCK + jnp.arange(Q_BLOCK)
        s = jnp.where(qpos[:, None] >= kpos[None, :], s, -jnp.inf)
        p = jax.nn.softmax(s, axis=-1).astype(v.dtype)
        return jnp.einsum('bhqk,bkhd->bqhd', p, v)

    o = lax.map(one_block, (qb, jnp.arange(nq)))
    return o.transpose(1, 0, 2, 3, 4).reshape(B, S, H, v.shape[-1])


def causal_multiscale_pool(u):
    B, S, G, C = u.shape
    uf = u.astype(jnp.float32)
    cs = jnp.concatenate([jnp.zeros((B, 1, G, C), jnp.float32), jnp.cumsum(uf, axis=1)], axis=1)
    t1 = jnp.arange(1, S + 1, dtype=jnp.float32)
    outs = []
    for g, w in enumerate(POOL_WINDOWS):
        c = cs[:, :, g]
        lo = jnp.concatenate([jnp.zeros((B, w - 1, C), jnp.float32), c[:, :S + 1 - w]], axis=1)
        mean = (c[:, 1:] - lo) / jnp.minimum(t1, float(w))[None, :, None]
        outs.append(mean - uf[:, :, g])
    return jnp.stack(outs, axis=2).astype(u.dtype)


def memory_cross_attention(q, k, v):
    s = jnp.einsum('bshd,bmhd->bhsm', q, k, preferred_element_type=jnp.float32) * (q.shape[-1] ** -0.5)
    p = jax.nn.softmax(s, axis=-1).astype(v.dtype)
    return jnp.einsum('bhsm,bmhd->bshd', p, v)


def token_mixer(h, mem_n, cos, sin, w_in, mla_q_a_norm_g, mla_w_uq, mla_kv_a_norm_g, mla_w_ukv,
                mla_q_norm_g, mla_k_norm_g, mla_w_out, pool_w, pool_scale, pool_w_out,
                mem_w_kv, mem_q_norm_g, mem_k_norm_g, mem_w_out, w_o):
    B, S, _ = h.shape
    M = mem_n.shape[1]
    z = h @ w_in
    cq, ckv, u_pool, q_mem, gate_logits = jnp.split(z, IN_SPLITS, axis=-1)

    q = (rmsnorm(cq, mla_q_a_norm_g) @ mla_w_uq).reshape(B, S, MLA_HEADS, QK_HEAD_DIM)
    c_kv, k_pe = ckv[..., :KV_LORA_RANK], ckv[..., KV_LORA_RANK:]
    kv = (rmsnorm(c_kv, mla_kv_a_norm_g) @ mla_w_ukv).reshape(B, S, MLA_HEADS, QK_NOPE_DIM + V_HEAD_DIM)
    k_nope, v = kv[..., :QK_NOPE_DIM], kv[..., QK_NOPE_DIM:]
    k = jnp.concatenate([k_nope, jnp.broadcast_to(k_pe[:, :, None, :], (B, S, MLA_HEADS, QK_ROPE_DIM))], axis=-1)
    q = rope_tail(rmsnorm(q, mla_q_norm_g), cos, sin)
    k = rope_tail(rmsnorm(k, mla_k_norm_g), cos, sin)
    o_mla = causal_block_attention(q, k, v).reshape(B, S, MLA_WIDTH) @ mla_w_out

    pooled = causal_multiscale_pool(u_pool.reshape(B, S, POOL_GROUPS, POOL_GROUP_DIM))
    mixed = jnp.einsum('bsgc,gcd->bsgd', pooled, pool_w).reshape(B, S, POOL_WIDTH) * pool_scale
    o_pool = mixed @ pool_w_out

    kv_m = mem_n @ mem_w_kv
    k_m = rmsnorm(kv_m[..., :MEM_WIDTH].reshape(B, M, MEM_HEADS, MEM_HEAD_DIM), mem_k_norm_g)
    v_m = kv_m[..., MEM_WIDTH:].reshape(B, M, MEM_HEADS, MEM_HEAD_DIM)
    q_m = rmsnorm(q_mem.reshape(B, S, MEM_HEADS, MEM_HEAD_DIM), mem_q_norm_g)
    o_mem = memory_cross_attention(q_m, k_m, v_m).reshape(B, S, MEM_WIDTH) @ mem_w_out

    gates = jax.nn.sigmoid(gate_logits.reshape(B, S, N_BRANCHES, D_MODEL))
    merged = gates[:, :, 0] * o_mla + gates[:, :, 1] * o_pool + gates[:, :, 2] * o_mem
    return merged @ w_o


def dense_swiglu(h, w_gate, w_up, w_down):
    return (jax.nn.silu(h @ w_gate) * (h @ w_up)) @ w_down


def moe_swiglu(h, router_w, router_b, w_gate, w_up, w_down):
    B, S, D = h.shape
    T = B * S
    A = T * TOP_K
    xt = h.reshape(T, D)
    logits = xt.astype(jnp.float32) @ router_w.astype(jnp.float32) + router_b.astype(jnp.float32)
    top_logit, top_idx = lax.top_k(logits, TOP_K)
    top_gate = jax.nn.softmax(top_logit, axis=-1)
    flat_e = top_idx.reshape(A)
    flat_tok = jnp.repeat(jnp.arange(T, dtype=jnp.int32), TOP_K)
    flat_w = top_gate.reshape(A)
    order = jnp.argsort(flat_e, stable=True)
    se, stok, sw = flat_e[order], flat_tok[order], flat_w[order]
    counts = jnp.bincount(flat_e, length=N_EXPERTS)
    starts = jnp.cumsum(counts) - counts
    padded = (counts + MOE_BLOCK - 1) // MOE_BLOCK * MOE_BLOCK
    pends = jnp.cumsum(padded)
    pstarts = pends - padded
    dest = pstarts[se] + (jnp.arange(A) - starts[se])
    n_blocks = -(-A // MOE_BLOCK) + N_EXPERTS
    n_slots = n_blocks * MOE_BLOCK
    slot_tok = jnp.full((n_slots,), T, jnp.int32).at[dest].set(stok)
    slot_w = jnp.zeros((n_slots,), jnp.float32).at[dest].set(sw)
    block_e = jnp.clip(jnp.searchsorted(pends, jnp.arange(n_blocks) * MOE_BLOCK, side='right'), 0, N_EXPERTS - 1)
    x_pad = jnp.concatenate([xt, jnp.zeros((1, D), xt.dtype)], axis=0)
    xs = x_pad[slot_tok].reshape(n_blocks, MOE_BLOCK, D)

    def expert_block(args):
        xb, e = args
        return (jax.nn.silu(xb @ w_gate[e]) * (xb @ w_up[e])) @ w_down[e]

    ys = lax.map(expert_block, (xs, block_e)).reshape(n_slots, D)
    y = jax.ops.segment_sum(ys * slot_w[:, None].astype(ys.dtype), slot_tok, num_segments=T + 1)[:T]
    return y.reshape(B, S, D)


def setup_inputs(seed: int = 0) -> dict:
    key = jax.random.key(seed)
    ks = iter(jax.random.split(key, 32))
    L, ND, NM = DEPTH, N_DENSE_LAYERS, N_MOE_LAYERS

    def w(shape, fan_in):
        return jax.random.normal(next(ks), shape, jnp.float32) * (fan_in ** -0.5)

    def gain(shape, s=0.02):
        return 1.0 + s * jax.random.normal(next(ks), shape, jnp.float32)

    x = jax.random.normal(next(ks), (BATCH, SEQ, D_MODEL), jnp.float32)
    mem = jax.random.normal(next(ks), (BATCH, MEM_TOKENS, D_MODEL), jnp.float32)
    positions = (jnp.arange(SEQ, dtype=jnp.int32)[None, :]
                 + jax.random.randint(next(ks), (BATCH, 1), 0, 1024, dtype=jnp.int32))
    return {
        'x': x,
        'mem': mem,
        'positions': positions,
        'attn_norm_g': gain((L, D_MODEL)),
        'w_in': w((L, D_MODEL, IN_WIDTH), D_MODEL),
        'mla_q_a_norm_g': gain((L, Q_LORA_RANK)),
        'mla_w_uq': w((L, Q_LORA_RANK, MLA_HEADS * QK_HEAD_DIM), Q_LORA_RANK),
        'mla_kv_a_norm_g': gain((L, KV_LORA_RANK)),
        'mla_w_ukv': w((L, KV_LORA_RANK, MLA_HEADS * (QK_NOPE_DIM + V_HEAD_DIM)), KV_LORA_RANK),
        'mla_q_norm_g': gain((L, QK_HEAD_DIM)),
        'mla_k_norm_g': gain((L, QK_HEAD_DIM)),
        'mla_w_out': w((L, MLA_WIDTH, D_MODEL), MLA_WIDTH),
        'pool_w': w((L, POOL_GROUPS, POOL_GROUP_DIM, POOL_GROUP_DIM), POOL_GROUP_DIM),
        'pool_scale': gain((L, POOL_WIDTH), 0.1),
        'pool_w_out': w((L, POOL_WIDTH, D_MODEL), POOL_WIDTH),
        'mem_norm_g': gain((L, D_MODEL)),
        'mem_w_kv': w((L, D_MODEL, 2 * MEM_WIDTH), D_MODEL),
        'mem_q_norm_g': gain((L, MEM_HEAD_DIM)),
        'mem_k_norm_g': gain((L, MEM_HEAD_DIM)),
        'mem_w_out': w((L, MEM_WIDTH, D_MODEL), MEM_WIDTH),
        'w_o': w((L, D_MODEL, D_MODEL), D_MODEL),
        'ffn_norm_g': gain((L, D_MODEL)),
        'dense_w_gate': w((ND, D_MODEL, D_FF_DENSE), D_MODEL),
        'dense_w_up': w((ND, D_MODEL, D_FF_DENSE), D_MODEL),
        'dense_w_down': w((ND, D_FF_DENSE, D_MODEL), D_FF_DENSE),
        'router_w': w((NM, D_MODEL, N_EXPERTS), D_MODEL),
        'router_b': 0.01 * jax.random.normal(next(ks), (NM, N_EXPERTS), jnp.float32),
        'moe_w_gate': w((NM, N_EXPERTS, D_MODEL, D_FF_EXPERT), D_MODEL),
        'moe_w_up': w((NM, N_EXPERTS, D_MODEL, D_FF_EXPERT), D_MODEL),
        'moe_w_down': w((NM, N_EXPERTS, D_FF_EXPERT, D_MODEL), D_FF_EXPERT),
    }


def reference(x, mem, positions, attn_norm_g, w_in, mla_q_a_norm_g, mla_w_uq, mla_kv_a_norm_g,
              mla_w_ukv, mla_q_norm_g, mla_k_norm_g, mla_w_out, pool_w, pool_scale, pool_w_out,
              mem_norm_g, mem_w_kv, mem_q_norm_g, mem_k_norm_g, mem_w_out, w_o, ffn_norm_g,
              dense_w_gate, dense_w_up, dense_w_down, router_w, router_b, moe_w_gate, moe_w_up,
              moe_w_down):
    cos, sin = rope_tables(positions)
    y = x
    for l in range(DEPTH):
        h = rmsnorm(y, attn_norm_g[l])
        mem_n = rmsnorm(mem, mem_norm_g[l])
        y = y + token_mixer(h, mem_n, cos, sin, w_in[l], mla_q_a_norm_g[l], mla_w_uq[l],
                            mla_kv_a_norm_g[l], mla_w_ukv[l], mla_q_norm_g[l], mla_k_norm_g[l],
                            mla_w_out[l], pool_w[l], pool_scale[l], pool_w_out[l], mem_w_kv[l],
                            mem_q_norm_g[l], mem_k_norm_g[l], mem_w_out[l], w_o[l])
        h = rmsnorm(y, ffn_norm_g[l])
        i = l // 2
        if l % 2 == 0:
            y = y + dense_swiglu(h, dense_w_gate[i], dense_w_up[i], dense_w_down[i])
        else:
            y = y + moe_swiglu(h, router_w[i], router_b[i], moe_w_gate[i], moe_w_up[i], moe_w_down[i])
    return y
```

```python
import functools

import numpy as np
import jax
import jax.numpy as jnp
from jax import lax
from jax.experimental import pallas as pl
from jax.experimental.pallas import tpu as pltpu

F32 = jnp.float32
BF16 = jnp.bfloat16

MEM_HEADS = 4
MEM_HEAD_DIM = 256
MEM_WIDTH = MEM_HEADS * MEM_HEAD_DIM
MLA_HEADS = 8
Q_LORA_RANK = 512
KV_LORA_RANK = 256
QK_NOPE_DIM = 128
QK_ROPE_DIM = 64
QK_HEAD_DIM = QK_NOPE_DIM + QK_ROPE_DIM
V_HEAD_DIM = 128
MLA_WIDTH = MLA_HEADS * V_HEAD_DIM
ROPE_THETA = 10000.0
POOL_WINDOWS = (2, 4, 8, 16)
POOL_GROUPS = 4
POOL_GROUP_DIM = 256
POOL_WIDTH = POOL_GROUPS * POOL_GROUP_DIM
N_BRANCHES = 3
N_EXPERTS = 8
TOP_K = 2
NORM_EPS = 1e-6

LANES = 128
HEAD_PAD = 2 * LANES
ROPE_HALF = QK_ROPE_DIM // 2
POOL_HALO = 16

ZA_POOL = 0
ZA_QMEM = POOL_WIDTH
ZA_CQ = ZA_QMEM + MEM_WIDTH
ZA_CKV = ZA_CQ + Q_LORA_RANK
ZA_KPE = ZA_CKV + KV_LORA_RANK
ZA_WIDTH = 3072

VMEM_LIMIT = 52 * 1024 * 1024


def _cp(*sem):
    return pltpu.CompilerParams(dimension_semantics=sem, vmem_limit_bytes=VMEM_LIMIT)


def _rms(x, eps_dim):
    return lax.rsqrt(jnp.sum(x * x, axis=-1, keepdims=True) * (1.0 / eps_dim) + NORM_EPS)


def _rmsnorm_kernel(x_ref, g_ref, o_ref):
    x = x_ref[...]
    o_ref[...] = (x * _rms(x, x.shape[-1]) * g_ref[...]).astype(o_ref.dtype)


def rmsnorm(x, g, out_dtype, tm=512):
    T, D = x.shape
    return pl.pallas_call(
        _rmsnorm_kernel,
        grid=(T // tm,),
        in_specs=[pl.BlockSpec((tm, D), lambda i: (i, 0)),
                  pl.BlockSpec((1, D), lambda i: (0, 0))],
        out_specs=pl.BlockSpec((tm, D), lambda i: (i, 0)),
        out_shape=jax.ShapeDtypeStruct((T, D), out_dtype),
        compiler_params=_cp("parallel"),
        name="rmsnorm",
    )(x, g.reshape(1, D))


def _norm_router_kernel(x_ref, g_ref, rw_ref, rb_ref, h_ref, idx_ref, gate_ref):
    x = x_ref[...]
    h = x * _rms(x, x.shape[-1]) * g_ref[...]
    h_ref[...] = h
    logits = jnp.dot(h, rw_ref[...], preferred_element_type=F32,
                     precision=lax.Precision.HIGHEST) + rb_ref[...]
    lane = lax.broadcasted_iota(jnp.int32, logits.shape, 1)
    neg = -jnp.inf
    l1 = jnp.where(lane < N_EXPERTS, logits, neg)
    m1 = jnp.max(l1, axis=-1, keepdims=True)
    i1 = jnp.min(jnp.where(l1 == m1, lane, LANES), axis=-1, keepdims=True)
    l2 = jnp.where(lane == i1, neg, l1)
    m2 = jnp.max(l2, axis=-1, keepdims=True)
    i2 = jnp.min(jnp.where(l2 == m2, lane, LANES), axis=-1, keepdims=True)
    e = jnp.exp(m2 - m1)
    g1 = 1.0 / (1.0 + e)
    g2 = e / (1.0 + e)
    idx_ref[...] = jnp.where(lane == 0, i1, jnp.where(lane == 1, i2, 0))
    gate_ref[...] = jnp.where(lane == 0, g1, jnp.where(lane == 1, g2, 0.0))


def norm_router(x, g, router_w, router_b, tm=512):
    T, D = x.shape
    E = router_w.shape[1]
    rw = jnp.zeros((D, LANES), F32).at[:, :E].set(router_w)
    rb = jnp.zeros((1, LANES), F32).at[0, :E].set(router_b)
    return pl.pallas_call(
        _norm_router_kernel,
        grid=(T // tm,),
        in_specs=[pl.BlockSpec((tm, D), lambda i: (i, 0)),
                  pl.BlockSpec((1, D), lambda i: (0, 0)),
                  pl.BlockSpec((D, LANES), lambda i: (0, 0)),
                  pl.BlockSpec((1, LANES), lambda i: (0, 0))],
        out_specs=[pl.BlockSpec((tm, D), lambda i: (i, 0)),
                   pl.BlockSpec((tm, LANES), lambda i: (i, 0)),
                   pl.BlockSpec((tm, LANES), lambda i: (i, 0))],
        out_shape=[jax.ShapeDtypeStruct((T, D), F32),
                   jax.ShapeDtypeStruct((T, LANES), jnp.int32),
                   jax.ShapeDtypeStruct((T, LANES), F32)],
        compiler_params=_cp("parallel"),
        name="norm_router",
    )(x, g.reshape(1, D), rw, rb)


def _mm_kernel(a_ref, w_ref, o_ref):
    o_ref[...] = jnp.dot(a_ref[...], w_ref[...], preferred_element_type=F32).astype(o_ref.dtype)


def _mm_res_kernel(a_ref, w_ref, r_ref, o_ref):
    o_ref[...] = (r_ref[...] + jnp.dot(a_ref[...], w_ref[...], preferred_element_type=F32)).astype(o_ref.dtype)


def matmul(a, w, out_dtype, residual=None, tm=1024, tn=1024):
    M, K = a.shape
    N = w.shape[1]
    tm, tn = min(tm, M), min(tn, N)
    in_specs = [pl.BlockSpec((tm, K), lambda i, j: (i, 0)),
                pl.BlockSpec((K, tn), lambda i, j: (0, j))]
    args = [a, w]
    body = _mm_kernel
    if residual is not None:
        in_specs.append(pl.BlockSpec((tm, tn), lambda i, j: (i, j)))
        args.append(residual)
        body = _mm_res_kernel
    return pl.pallas_call(
        body,
        grid=(M // tm, N // tn),
        in_specs=in_specs,
        out_specs=pl.BlockSpec((tm, tn), lambda i, j: (i, j)),
        out_shape=jax.ShapeDtypeStruct((M, N), out_dtype),
        compiler_params=_cp("parallel", "arbitrary"),
        name="matmul",
    )(*args)


def _rope_table_kernel(pos_ref, invf_ref, o_ref):
    ang = pos_ref[...].astype(F32) * invf_ref[...]
    lane = lax.broadcasted_iota(jnp.int32, ang.shape, 1)
    c = jnp.cos(ang)
    s = jnp.sin(ang)
    o_ref[:, 0:LANES] = jnp.where(lane < QK_ROPE_DIM, c, 0.0)
    o_ref[:, LANES:2 * LANES] = jnp.where(lane < ROPE_HALF, -s, 0.0)
    o_ref[:, 2 * LANES:3 * LANES] = jnp.where((lane >= ROPE_HALF) & (lane < QK_ROPE_DIM), s, 0.0)


def rope_tables(positions, tm=512):
    T = positions.size
    inv_freq = ROPE_THETA ** (-np.arange(0, QK_ROPE_DIM, 2, dtype=np.float32) / QK_ROPE_DIM)
    invf = np.zeros((1, LANES), np.float32)
    invf[0, :ROPE_HALF] = inv_freq
    invf[0, ROPE_HALF:QK_ROPE_DIM] = inv_freq
    return pl.pallas_call(
        _rope_table_kernel,
        grid=(T // tm,),
        in_specs=[pl.BlockSpec((tm, 1), lambda i: (i, 0)),
                  pl.BlockSpec((1, LANES), lambda i: (0, 0))],
        out_specs=pl.BlockSpec((tm, 3 * LANES), lambda i: (i, 0)),
        out_shape=jax.ShapeDtypeStruct((T, 3 * LANES), F32),
        compiler_params=_cp("parallel"),
        name="rope_tables",
    )(positions.reshape(T, 1), jnp.asarray(invf))


def _rope(x, tab):
    c = tab[:, 0:LANES]
    s1 = tab[:, LANES:2 * LANES]
    s2 = tab[:, 2 * LANES:3 * LANES]
    return x * c + pltpu.roll(x, LANES - ROPE_HALF, 1) * s1 + pltpu.roll(x, ROPE_HALF, 1) * s2


def _mla_prep_kernel(cq_ref, ckv_ref, kpe_ref, tab_ref, wuq_ref, wuk_ref, wuv_ref,
                     gqa_ref, gkva_ref, gq_ref, gk_ref, q_ref, k_ref, v_ref):
    tab = tab_ref[...]
    cq = cq_ref[...]
    cqn = (cq * _rms(cq, Q_LORA_RANK) * gqa_ref[...]).astype(BF16)
    qf = jnp.dot(cqn, wuq_ref[...], preferred_element_type=F32)
    ckv = ckv_ref[...]
    ckvn = (ckv * _rms(ckv, KV_LORA_RANK) * gkva_ref[...]).astype(BF16)
    kn = jnp.dot(ckvn, wuk_ref[...], preferred_element_type=F32)
    v_ref[...] = jnp.dot(ckvn, wuv_ref[...], preferred_element_type=F32).astype(v_ref.dtype)
    gq = gq_ref[...]
    gk = gk_ref[...]
    kpe = kpe_ref[...]
    ss_pe = jnp.sum(kpe * kpe, axis=-1, keepdims=True)
    kpe_rot = _rope(kpe * gk[:, LANES:], tab)
    scale = QK_HEAD_DIM ** -0.5
    for h in range(MLA_HEADS):
        q0 = qf[:, h * HEAD_PAD:h * HEAD_PAD + LANES]
        q1 = qf[:, h * HEAD_PAD + LANES:(h + 1) * HEAD_PAD]
        ss = jnp.sum(q0 * q0, axis=-1, keepdims=True) + jnp.sum(q1 * q1, axis=-1, keepdims=True)
        rq = lax.rsqrt(ss * (1.0 / QK_HEAD_DIM) + NORM_EPS) * scale
        q_ref[:, h * HEAD_PAD:h * HEAD_PAD + LANES] = (q0 * gq[:, :LANES] * rq).astype(q_ref.dtype)
        q_ref[:, h * HEAD_PAD + LANES:(h + 1) * HEAD_PAD] = (_rope(q1 * gq[:, LANES:], tab) * rq).astype(q_ref.dtype)
        k0 = kn[:, h * LANES:(h + 1) * LANES]
        ssk = jnp.sum(k0 * k0, axis=-1, keepdims=True) + ss_pe
        rk = lax.rsqrt(ssk * (1.0 / QK_HEAD_DIM) + NORM_EPS)
        k_ref[:, h * HEAD_PAD:h * HEAD_PAD + LANES] = (k0 * gk[:, :LANES] * rk).astype(k_ref.dtype)
        k_ref[:, h * HEAD_PAD + LANES:(h + 1) * HEAD_PAD] = (kpe_rot * rk).astype(k_ref.dtype)


def mla_prep(za, tab, wuq, wuk, wuv, gqa, gkva, gq, gk, tm=512):
    T = za.shape[0]
    HP = MLA_HEADS * HEAD_PAD
    full = lambda shape: pl.BlockSpec(shape, lambda i: (0, 0))
    return pl.pallas_call(
        _mla_prep_kernel,
        grid=(T // tm,),
        in_specs=[pl.BlockSpec((tm, Q_LORA_RANK), lambda i: (i, ZA_CQ // Q_LORA_RANK)),
                  pl.BlockSpec((tm, KV_LORA_RANK), lambda i: (i, ZA_CKV // KV_LORA_RANK)),
                  pl.BlockSpec((tm, LANES), lambda i: (i, ZA_KPE // LANES)),
                  pl.BlockSpec((tm, 3 * LANES), lambda i: (i, 0)),
                  full((Q_LORA_RANK, HP)), full((KV_LORA_RANK, MLA_WIDTH)), full((KV_LORA_RANK, MLA_WIDTH)),
                  full((1, Q_LORA_RANK)), full((1, KV_LORA_RANK)), full((1, HEAD_PAD)), full((1, HEAD_PAD))],
        out_specs=[pl.BlockSpec((tm, HP), lambda i: (i, 0)),
                   pl.BlockSpec((tm, HP), lambda i: (i, 0)),
                   pl.BlockSpec((tm, MLA_WIDTH), lambda i: (i, 0))],
        out_shape=[jax.ShapeDtypeStruct((T, HP), BF16),
                   jax.ShapeDtypeStruct((T, HP), BF16),
                   jax.ShapeDtypeStruct((T, MLA_WIDTH), BF16)],
        compiler_params=_cp("parallel"),
        name="mla_prep",
    )(za, za, za, tab, wuq, wuk, wuv, gqa, gkva, gq, gk)


def _causal_attn_kernel(q_ref, k_ref, v_ref, o_ref, *, tq):
    S = q_ref.shape[0]
    row = lax.broadcasted_iota(jnp.int32, (tq, tq), 0)
    col = lax.broadcasted_iota(jnp.int32, (tq, tq), 1)
    tri = col <= row
    for qi in range(S // tq):
        kl = (qi + 1) * tq
        q = q_ref[qi * tq:kl, :]
        s = lax.dot_general(q, k_ref[0:kl, :], (((1,), (1,)), ((), ())), preferred_element_type=F32)
        s_diag = jnp.where(tri, s[:, kl - tq:kl], -jnp.inf)
        if qi > 0:
            s = jnp.concatenate([s[:, :kl - tq], s_diag], axis=1)
        else:
            s = s_diag
        m = jnp.max(s, axis=-1, keepdims=True)
        p = jnp.exp(s - m)
        l = jnp.sum(p, axis=-1, keepdims=True)
        o = jnp.dot(p.astype(BF16), v_ref[0:kl, :], preferred_element_type=F32)
        o_ref[qi * tq:kl, :] = (o / l).astype(o_ref.dtype)


def causal_attention(q, k, v, B, S, tq=256):
    T = q.shape[0]
    return pl.pallas_call(
        functools.partial(_causal_attn_kernel, tq=tq),
        grid=(B, MLA_HEADS),
        in_specs=[pl.BlockSpec((S, HEAD_PAD), lambda b, h: (b, h)),
                  pl.BlockSpec((S, HEAD_PAD), lambda b, h: (b, h)),
                  pl.BlockSpec((S, V_HEAD_DIM), lambda b, h: (b, h))],
        out_specs=pl.BlockSpec((S, V_HEAD_DIM), lambda b, h: (b, h)),
        out_shape=jax.ShapeDtypeStruct((T, MLA_WIDTH), BF16),
        compiler_params=_cp("parallel", "parallel"),
        name="causal_attention",
    )(q, k, v)


def _pool_kernel(u_ref, w_ref, sc_ref, o_ref, pad_ref, *, rc):
    S = u_ref.shape[0]
    C = POOL_GROUP_DIM
    t1 = lax.broadcasted_iota(jnp.int32, (rc, C), 0).astype(F32) + 1.0
    for g, win in enumerate(POOL_WINDOWS):
        pad_ref[0:POOL_HALO, :] = jnp.zeros((POOL_HALO, C), F32)
        pad_ref[POOL_HALO:POOL_HALO + S, :] = u_ref[:, g * C:(g + 1) * C]
        for r0 in range(0, S, rc):
            acc = pad_ref[POOL_HALO + r0:POOL_HALO + r0 + rc, :]
            tok = acc
            for kk in range(1, win):
                acc = acc + pad_ref[POOL_HALO + r0 - kk:POOL_HALO + r0 - kk + rc, :]
            cnt = jnp.minimum(t1 + float(r0), float(win))
            pooled = (acc / cnt - tok).astype(BF16)
            mixed = jnp.dot(pooled, w_ref[g], preferred_element_type=F32) * sc_ref[:, g * C:(g + 1) * C]
            o_ref[r0:r0 + rc, g * C:(g + 1) * C] = mixed.astype(o_ref.dtype)


def pool_mixer(za, pool_w, pool_scale, B, S, rc=512):
    T = za.shape[0]
    rc = min(rc, S)
    return pl.pallas_call(
        functools.partial(_pool_kernel, rc=rc),
        grid=(B,),
        in_specs=[pl.BlockSpec((S, POOL_WIDTH), lambda b: (b, ZA_POOL // POOL_WIDTH)),
                  pl.BlockSpec((POOL_GROUPS, POOL_GROUP_DIM, POOL_GROUP_DIM), lambda b: (0, 0, 0)),
                  pl.BlockSpec((1, POOL_WIDTH), lambda b: (0, 0))],
        out_specs=pl.BlockSpec((S, POOL_WIDTH), lambda b: (b, 0)),
        out_shape=jax.ShapeDtypeStruct((T, POOL_WIDTH), BF16),
        scratch_shapes=[pltpu.VMEM((POOL_HALO + S, POOL_GROUP_DIM), F32)],
        compiler_params=_cp("parallel"),
        name="pool_mixer",
    )(za, pool_w, pool_scale)


def _mem_kv_kernel(m_ref, g_ref, w_ref, gk_ref, k_ref, v_ref):
    x = m_ref[...]
    xn = (x * _rms(x, x.shape[-1]) * g_ref[...]).astype(BF16)
    kv = jnp.dot(xn, w_ref[...], preferred_element_type=F32)
    for h in range(MEM_HEADS):
        kh = kv[:, h * MEM_HEAD_DIM:(h + 1) * MEM_HEAD_DIM]
        k_ref[:, h * MEM_HEAD_DIM:(h + 1) * MEM_HEAD_DIM] = (
            kh * _rms(kh, MEM_HEAD_DIM) * gk_ref[...]).astype(k_ref.dtype)
    v_ref[...] = kv[:, MEM_WIDTH:].astype(v_ref.dtype)


def mem_kv(mem2d, g, w_kv, gk, B, M):
    D = mem2d.shape[1]
    return pl.pallas_call(
        _mem_kv_kernel,
        grid=(B,),
        in_specs=[pl.BlockSpec((M, D), lambda b: (b, 0)),
                  pl.BlockSpec((1, D), lambda b: (0, 0)),
                  pl.BlockSpec((D, 2 * MEM_WIDTH), lambda b: (0, 0)),
                  pl.BlockSpec((1, MEM_HEAD_DIM), lambda b: (0, 0))],
        out_specs=[pl.BlockSpec((M, MEM_WIDTH), lambda b: (b, 0)),
                   pl.BlockSpec((M, MEM_WIDTH), lambda b: (b, 0))],
        out_shape=[jax.ShapeDtypeStruct((B * M, MEM_WIDTH), BF16),
                   jax.ShapeDtypeStruct((B * M, MEM_WIDTH), BF16)],
        compiler_params=_cp("parallel"),
        name="mem_kv",
    )(mem2d, g, w_kv, gk)


def _mem_attn_kernel(q_ref, k_ref, v_ref, gq_ref, o_ref):
    scale = MEM_HEAD_DIM ** -0.5
    for h in range(MEM_HEADS):
        sl = slice(h * MEM_HEAD_DIM, (h + 1) * MEM_HEAD_DIM)
        qh = q_ref[:, sl]
        qn = (qh * (_rms(qh, MEM_HEAD_DIM) * scale) * gq_ref[...]).astype(BF16)
        s = lax.dot_general(qn, k_ref[:, sl], (((1,), (1,)), ((), ())), preferred_element_type=F32)
        m = jnp.max(s, axis=-1, keepdims=True)
        p = jnp.exp(s - m)
        l = jnp.sum(p, axis=-1, keepdims=True)
        o = jnp.dot(p.astype(BF16), v_ref[:, sl], preferred_element_type=F32)
        o_ref[:, sl] = (o / l).astype(o_ref.dtype)


def mem_attention(za, k_m, v_m, gq, B, S, M, tq=512):
    T = za.shape[0]
    tq = min(tq, S)
    nq = S // tq
    return pl.pallas_call(
        _mem_attn_kernel,
        grid=(B, nq),
        in_specs=[pl.BlockSpec((tq, MEM_WIDTH), lambda b, i: (b * nq + i, ZA_QMEM // MEM_WIDTH)),
                  pl.BlockSpec((M, MEM_WIDTH), lambda b, i: (b, 0)),
                  pl.BlockSpec((M, MEM_WIDTH), lambda b, i: (b, 0)),
                  pl.BlockSpec((1, MEM_HEAD_DIM), lambda b, i: (0, 0))],
        out_specs=pl.BlockSpec((tq, MEM_WIDTH), lambda b, i: (b * nq + i, 0)),
        out_shape=jax.ShapeDtypeStruct((T, MEM_WIDTH), BF16),
        compiler_params=_cp("parallel", "arbitrary"),
        name="mem_attention",
    )(za, k_m, v_m, gq)


def _merge_kernel(h_ref, a_ref, p_ref, m_ref, g0_ref, g1_ref, g2_ref, w0_ref, w1_ref, w2_ref, o_ref):
    h = h_ref[...]
    acc = None
    for x_ref, g_ref, w_ref in ((a_ref, g0_ref, w0_ref), (p_ref, g1_ref, w1_ref), (m_ref, g2_ref, w2_ref)):
        logit = jnp.dot(h, g_ref[...], preferred_element_type=F32)
        gate = 1.0 / (1.0 + jnp.exp(-logit))
        br = jnp.dot(x_ref[...], w_ref[...], preferred_element_type=F32)
        acc = gate * br if acc is None else acc + gate * br
    o_ref[...] = acc.astype(o_ref.dtype)


def merge_branches(h, a, p, m, w_gates, w_a, w_p, w_m, tm=512, tn=512):
    T, D = h.shape
    W = a.shape[1]
    nj = D // tn
    row = lambda width: pl.BlockSpec((tm, width), lambda i, j: (i, 0))
    gate_spec = lambda b: pl.BlockSpec((D, tn), lambda i, j, b=b: (0, b * nj + j))
    out_w = pl.BlockSpec((W, tn), lambda i, j: (0, j))
    return pl.pallas_call(
        _merge_kernel,
        grid=(T // tm, nj),
        in_specs=[row(D), row(W), row(W), row(W),
                  gate_spec(0), gate_spec(1), gate_spec(2), out_w, out_w, out_w],
        out_specs=pl.BlockSpec((tm, tn), lambda i, j: (i, j)),
        out_shape=jax.ShapeDtypeStruct((T, D), BF16),
        compiler_params=_cp("parallel", "arbitrary"),
        name="merge_branches",
    )(h, a, p, m, w_gates, w_gates, w_gates, w_a, w_p, w_m)


def _swiglu_kernel(be_ref, x_ref, wg_ref, wu_ref, wd_ref, *rest, has_res):
    del be_ref
    if has_res:
        r_ref, o_ref, xb_ref = rest
    else:
        o_ref, xb_ref = rest
    f = pl.program_id(1)

    @pl.when(f == 0)
    def _():
        xb_ref[...] = x_ref[...].astype(BF16)

    xb = xb_ref[...]
    a = jnp.dot(xb, wg_ref[0], preferred_element_type=F32)
    b = jnp.dot(xb, wu_ref[0], preferred_element_type=F32)
    act = (a / (1.0 + jnp.exp(-a)) * b).astype(BF16)
    y = jnp.dot(act, wd_ref[0], preferred_element_type=F32)

    @pl.when(f == 0)
    def _():
        if has_res:
            o_ref[...] = r_ref[...] + y
        else:
            o_ref[...] = y

    @pl.when(f > 0)
    def _():
        o_ref[...] += y


def swiglu(x, block_e, w_gate, w_up, w_down, residual=None, tm=512, tf=512):
    R, D = x.shape
    F = w_gate.shape[2]
    has_res = residual is not None
    in_specs = [pl.BlockSpec((tm, D), lambda i, f, be: (i, 0)),
                pl.BlockSpec((1, D, tf), lambda i, f, be: (be[i], 0, f)),
                pl.BlockSpec((1, D, tf), lambda i, f, be: (be[i], 0, f)),
                pl.BlockSpec((1, tf, D), lambda i, f, be: (be[i], f, 0))]
    args = [x, w_gate, w_up, w_down]
    if has_res:
        in_specs.append(pl.BlockSpec((tm, D), lambda i, f, be: (i, 0)))
        args.append(residual)
    return pl.pallas_call(
        functools.partial(_swiglu_kernel, has_res=has_res),
        grid_spec=pltpu.PrefetchScalarGridSpec(
            num_scalar_prefetch=1,
            grid=(R // tm, F // tf),
            in_specs=in_specs,
            out_specs=pl.BlockSpec((tm, D), lambda i, f, be: (i, 0)),
            scratch_shapes=[pltpu.VMEM((tm, D), BF16)]),
        out_shape=jax.ShapeDtypeStruct((R, D), F32),
        compiler_params=_cp("parallel", "arbitrary"),
        name="swiglu",
    )(block_e, *args)


def _row_copy(src_hbm, dst_ref, src_row, dst_row, sem):
    return pltpu.make_async_copy(src_hbm.at[pl.ds(src_row, 1)], dst_ref.at[pl.ds(dst_row, 1)], sem)


def _dispatch_kernel(tok_ref, src_hbm, out_hbm, sem, *, rows):
    base = pl.program_id(0) * rows

    def start(r, c):
        _row_copy(src_hbm, out_hbm, tok_ref[base + r], base + r, sem).start()
        return c

    lax.fori_loop(0, rows, start, 0)

    def wait(r, c):
        _row_copy(src_hbm, out_hbm, 0, base + r, sem).wait()
        return c

    lax.fori_loop(0, rows, wait, 0)


def dispatch_rows(h, slot_tok, rows=512):
    n_slots = slot_tok.shape[0]
    D = h.shape[1]
    return pl.pallas_call(
        functools.partial(_dispatch_kernel, rows=rows),
        grid_spec=pltpu.PrefetchScalarGridSpec(
            num_scalar_prefetch=1,
            grid=(n_slots // rows,),
            in_specs=[pl.BlockSpec(memory_space=pl.ANY)],
            out_specs=pl.BlockSpec(memory_space=pl.ANY),
            scratch_shapes=[pltpu.SemaphoreType.DMA(())]),
        out_shape=jax.ShapeDtypeStruct((n_slots, D), h.dtype),
        compiler_params=_cp("arbitrary"),
        name="moe_dispatch",
    )(slot_tok, h)


def _combine_kernel(dest_ref, y_ref, gate_ref, ys_hbm, o_ref, buf_ref, sem, *, tb):
    base = pl.program_id(0) * tb

    def start(r, c):
        for k in range(TOP_K):
            _row_copy(ys_hbm, buf_ref.at[k], dest_ref[TOP_K * (base + r) + k], r, sem).start()
        return c

    lax.fori_loop(0, tb, start, 0)

    def wait(r, c):
        for k in range(TOP_K):
            _row_copy(ys_hbm, buf_ref.at[k], 0, r, sem).wait()
        return c

    lax.fori_loop(0, tb, wait, 0)
    g = gate_ref[...]
    o_ref[...] = y_ref[...] + g[:, 0:1] * buf_ref[0] + g[:, 1:2] * buf_ref[1]


def combine_rows(y, gates, ys, dest, tb=256):
    T, D = y.shape
    return pl.pallas_call(
        functools.partial(_combine_kernel, tb=tb),
        grid_spec=pltpu.PrefetchScalarGridSpec(
            num_scalar_prefetch=1,
            grid=(T // tb,),
            in_specs=[pl.BlockSpec((tb, D), lambda i, d: (i, 0)),
                      pl.BlockSpec((tb, LANES), lambda i, d: (i, 0)),
                      pl.BlockSpec(memory_space=pl.ANY)],
            out_specs=pl.BlockSpec((tb, D), lambda i, d: (i, 0)),
            scratch_shapes=[pltpu.VMEM((TOP_K, tb, D), F32),
                            pltpu.SemaphoreType.DMA(())]),
        out_shape=jax.ShapeDtypeStruct((T, D), F32),
        compiler_params=_cp("arbitrary"),
        name="moe_combine",
    )(dest, y, gates, ys)


def _routing_tables(top_idx, T, block):
    A = T * TOP_K
    E = N_EXPERTS
    flat_e = top_idx.reshape(A)
    onehot = (flat_e[:, None] == jnp.arange(E, dtype=jnp.int32)[None, :]).astype(jnp.int32)
    csum = jnp.cumsum(onehot, axis=0)
    counts = csum[-1]
    rank = jnp.sum((csum - 1) * onehot, axis=1)
    padded = (counts + block - 1) // block * block
    pends = jnp.cumsum(padded)
    pstarts = pends - padded
    starts = jnp.cumsum(counts) - counts
    dest = (pstarts[flat_e] + rank).astype(jnp.int32)
    n_blocks = A // block + E
    n_slots = n_blocks * block
    order = jnp.argsort(flat_e, stable=True).astype(jnp.int32)
    slot = jnp.arange(n_slots, dtype=jnp.int32)
    slot_e = jnp.clip(jnp.searchsorted(pends, slot, side='right'), 0, E - 1).astype(jnp.int32)
    r = slot - pstarts[slot_e]
    valid = r < counts[slot_e]
    src = jnp.clip(starts[slot_e] + r, 0, A - 1)
    slot_tok = jnp.where(valid, order[src] // TOP_K, 0).astype(jnp.int32)
    block_e = slot_e[::block]
    return slot_tok, dest, block_e


def _token_mixer(y, h, mem2d, tab, B, S, M, l, w_in, mla_q_a_norm_g, mla_w_uq, mla_kv_a_norm_g, mla_w_ukv,
                 mla_q_norm_g, mla_k_norm_g, mla_w_out, pool_w, pool_scale, pool_w_out,
                 mem_norm_g, mem_w_kv, mem_q_norm_g, mem_k_norm_g, mem_w_out, w_o):
    D = h.shape[1]
    wl = w_in[l]
    s0, s1, s2, s3 = (Q_LORA_RANK, Q_LORA_RANK + KV_LORA_RANK + QK_ROPE_DIM,
                      Q_LORA_RANK + KV_LORA_RANK + QK_ROPE_DIM + POOL_WIDTH,
                      Q_LORA_RANK + KV_LORA_RANK + QK_ROPE_DIM + POOL_WIDTH + MEM_WIDTH)
    used = POOL_WIDTH + MEM_WIDTH + s1
    w_a = jnp.concatenate([wl[:, s1:s2], wl[:, s2:s3], wl[:, :s1],
                           jnp.zeros((D, ZA_WIDTH - used), F32)], axis=1).astype(BF16)
    w_gates = wl[:, s3:].astype(BF16)
    za = matmul(h, w_a, F32)

    wuq = jnp.pad(mla_w_uq[l].reshape(Q_LORA_RANK, MLA_HEADS, QK_HEAD_DIM),
                  ((0, 0), (0, 0), (0, HEAD_PAD - QK_HEAD_DIM))).reshape(Q_LORA_RANK, MLA_HEADS * HEAD_PAD)
    wukv = mla_w_ukv[l].reshape(KV_LORA_RANK, MLA_HEADS, QK_NOPE_DIM + V_HEAD_DIM)
    wuk = wukv[:, :, :QK_NOPE_DIM].reshape(KV_LORA_RANK, MLA_WIDTH)
    wuv = wukv[:, :, QK_NOPE_DIM:].reshape(KV_LORA_RANK, MLA_WIDTH)
    pad_g = lambda g: jnp.pad(g, (0, HEAD_PAD - QK_HEAD_DIM)).reshape(1, HEAD_PAD)
    q, k, v = mla_prep(za, tab, wuq.astype(BF16), wuk.astype(BF16), wuv.astype(BF16),
                       mla_q_a_norm_g[l].reshape(1, -1), mla_kv_a_norm_g[l].reshape(1, -1),
                       pad_g(mla_q_norm_g[l]), pad_g(mla_k_norm_g[l]))
    attn = causal_attention(q, k, v, B, S)

    mixed = pool_mixer(za, pool_w[l].astype(BF16), pool_scale[l].reshape(1, -1), B, S)

    k_m, v_m = mem_kv(mem2d, mem_norm_g[l].reshape(1, -1), mem_w_kv[l].astype(BF16),
                      mem_k_norm_g[l].reshape(1, -1), B, M)
    o_mem = mem_attention(za, k_m, v_m, mem_q_norm_g[l].reshape(1, -1), B, S, M)

    merged = merge_branches(h, attn, mixed, o_mem, w_gates, mla_w_out[l].astype(BF16),
                            pool_w_out[l].astype(BF16), mem_w_out[l].astype(BF16))
    return matmul(merged, w_o[l].astype(BF16), F32, residual=y)


def kernel(x, mem, positions, attn_norm_g, w_in, mla_q_a_norm_g, mla_w_uq, mla_kv_a_norm_g, mla_w_ukv, mla_q_norm_g, mla_k_norm_g, mla_w_out, pool_w, pool_scale, pool_w_out, mem_norm_g, mem_w_kv, mem_q_norm_g, mem_k_norm_g, mem_w_out, w_o, ffn_norm_g, dense_w_gate, dense_w_up, dense_w_down, router_w, router_b, moe_w_gate, moe_w_up, moe_w_down):
    B, S, D = x.shape
    M = mem.shape[1]
    T = B * S
    depth = attn_norm_g.shape[0]
    moe_block = 512
    y = x.reshape(T, D)
    mem2d = mem.reshape(B * M, D)
    tab = rope_tables(positions)
    for l in range(depth):
        h = rmsnorm(y, attn_norm_g[l], BF16)
        y = _token_mixer(y, h, mem2d, tab, B, S, M, l, w_in, mla_q_a_norm_g, mla_w_uq, mla_kv_a_norm_g,
                         mla_w_ukv, mla_q_norm_g, mla_k_norm_g, mla_w_out, pool_w, pool_scale, pool_w_out,
                         mem_norm_g, mem_w_kv, mem_q_norm_g, mem_k_norm_g, mem_w_out, w_o)
        i = l // 2
        if l % 2 == 0:
            h = rmsnorm(y, ffn_norm_g[l], BF16)
            block_e = jnp.zeros((T // 512,), jnp.int32)
            y = swiglu(h, block_e, dense_w_gate[i:i + 1].astype(BF16), dense_w_up[i:i + 1].astype(BF16),
                       dense_w_down[i:i + 1].astype(BF16), residual=y)
        else:
            hf, idx, gates = norm_router(y, ffn_norm_g[l], router_w[i], router_b[i])
            slot_tok, dest, block_e = _routing_tables(idx[:, :TOP_K], T, moe_block)
            xs = dispatch_rows(hf, slot_tok)
            ys = swiglu(xs, block_e, moe_w_gate[i].astype(BF16), moe_w_up[i].astype(BF16),
                        moe_w_down[i].astype(BF16), tm=moe_block)
            y = combine_rows(y, gates, ys, dest)
    return y.reshape(B, S, D)
```

```python
import functools

import numpy as np
import jax
import jax.numpy as jnp
from jax import lax
from jax.experimental import pallas as pl
from jax.experimental.pallas import tpu as pltpu

F32 = jnp.float32
BF16 = jnp.bfloat16

MEM_HEADS = 4
MEM_HEAD_DIM = 256
MEM_WIDTH = MEM_HEADS * MEM_HEAD_DIM
MLA_HEADS = 8
Q_LORA_RANK = 512
KV_LORA_RANK = 256
QK_NOPE_DIM = 128
QK_ROPE_DIM = 64
QK_HEAD_DIM = QK_NOPE_DIM + QK_ROPE_DIM
V_HEAD_DIM = 128
MLA_WIDTH = MLA_HEADS * V_HEAD_DIM
ROPE_THETA = 10000.0
POOL_WINDOWS = (2, 4, 8, 16)
POOL_GROUPS = 4
POOL_GROUP_DIM = 256
POOL_WIDTH = POOL_GROUPS * POOL_GROUP_DIM
N_BRANCHES = 3
N_EXPERTS = 8
TOP_K = 2
NORM_EPS = 1e-6

LANES = 128
HEAD_PAD = 2 * LANES
ROPE_HALF = QK_ROPE_DIM // 2
POOL_HALO = 16

ZA_POOL = 0
ZA_QMEM = POOL_WIDTH
ZA_CQ = ZA_QMEM + MEM_WIDTH
ZA_CKV = ZA_CQ + Q_LORA_RANK
ZA_KPE = ZA_CKV + KV_LORA_RANK
ZA_WIDTH = 3072

VMEM_LIMIT = 52 * 1024 * 1024


def _cp(*sem):
    return pltpu.CompilerParams(dimension_semantics=sem, vmem_limit_bytes=VMEM_LIMIT)


def _rms(x, eps_dim):
    return lax.rsqrt(jnp.sum(x * x, axis=-1, keepdims=True) * (1.0 / eps_dim) + NORM_EPS)


def _rmsnorm_kernel(x_ref, g_ref, o_ref):
    x = x_ref[...]
    o_ref[...] = (x * _rms(x, x.shape[-1]) * g_ref[...]).astype(o_ref.dtype)


def rmsnorm(x, g, out_dtype, tm=512):
    T, D = x.shape
    return pl.pallas_call(
        _rmsnorm_kernel,
        grid=(T // tm,),
        in_specs=[pl.BlockSpec((tm, D), lambda i: (i, 0)),
                  pl.BlockSpec((1, D), lambda i: (0, 0))],
        out_specs=pl.BlockSpec((tm, D), lambda i: (i, 0)),
        out_shape=jax.ShapeDtypeStruct((T, D), out_dtype),
        compiler_params=_cp("parallel"),
        name="rmsnorm",
    )(x, g.reshape(1, D))


def _norm_router_kernel(x_ref, g_ref, rw_ref, rb_ref, h_ref, idx_ref, gate_ref):
    x = x_ref[...]
    h = x * _rms(x, x.shape[-1]) * g_ref[...]
    h_ref[...] = h
    logits = jnp.dot(h, rw_ref[...], preferred_element_type=F32,
                     precision=lax.Precision.HIGHEST) + rb_ref[...]
    lane = lax.broadcasted_iota(jnp.int32, logits.shape, 1)
    neg = -jnp.inf
    l1 = jnp.where(lane < N_EXPERTS, logits, neg)
    m1 = jnp.max(l1, axis=-1, keepdims=True)
    i1 = jnp.min(jnp.where(l1 == m1, lane, LANES), axis=-1, keepdims=True)
    l2 = jnp.where(lane == i1, neg, l1)
    m2 = jnp.max(l2, axis=-1, keepdims=True)
    i2 = jnp.min(jnp.where(l2 == m2, lane, LANES), axis=-1, keepdims=True)
    e = jnp.exp(m2 - m1)
    g1 = 1.0 / (1.0 + e)
    g2 = e / (1.0 + e)
    idx_ref[...] = jnp.where(lane == 0, i1, jnp.where(lane == 1, i2, 0))
    gate_ref[...] = jnp.where(lane == 0, g1, jnp.where(lane == 1, g2, 0.0))


def norm_router(x, g, router_w, router_b, tm=512):
    T, D = x.shape
    E = router_w.shape[1]
    rw = jnp.zeros((D, LANES), F32).at[:, :E].set(router_w)
    rb = jnp.zeros((1, LANES), F32).at[0, :E].set(router_b)
    return pl.pallas_call(
        _norm_router_kernel,
        grid=(T // tm,),
        in_specs=[pl.BlockSpec((tm, D), lambda i: (i, 0)),
                  pl.BlockSpec((1, D), lambda i: (0, 0)),
                  pl.BlockSpec((D, LANES), lambda i: (0, 0)),
                  pl.BlockSpec((1, LANES), lambda i: (0, 0))],
        out_specs=[pl.BlockSpec((tm, D), lambda i: (i, 0)),
                   pl.BlockSpec((tm, LANES), lambda i: (i, 0)),
                   pl.BlockSpec((tm, LANES), lambda i: (i, 0))],
        out_shape=[jax.ShapeDtypeStruct((T, D), F32),
                   jax.ShapeDtypeStruct((T, LANES), jnp.int32),
                   jax.ShapeDtypeStruct((T, LANES), F32)],
        compiler_params=_cp("parallel"),
        name="norm_router",
    )(x, g.reshape(1, D), rw, rb)


def _mm_kernel(a_ref, w_ref, o_ref):
    o_ref[...] = jnp.dot(a_ref[...], w_ref[...], preferred_element_type=F32).astype(o_ref.dtype)


def _mm_res_kernel(a_ref, w_ref, r_ref, o_ref):
    o_ref[...] = (r_ref[...] + jnp.dot(a_ref[...], w_ref[...], preferred_element_type=F32)).astype(o_ref.dtype)


def matmul(a, w, out_dtype, residual=None, tm=1024, tn=1024):
    M, K = a.shape
    N = w.shape[1]
    tm, tn = min(tm, M), min(tn, N)
    in_specs = [pl.BlockSpec((tm, K), lambda i, j: (i, 0)),
                pl.BlockSpec((K, tn), lambda i, j: (0, j))]
    args = [a, w]
    body = _mm_kernel
    if residual is not None:
        in_specs.append(pl.BlockSpec((tm, tn), lambda i, j: (i, j)))
        args.append(residual)
        body = _mm_res_kernel
    return pl.pallas_call(
        body,
        grid=(M // tm, N // tn),
        in_specs=in_specs,
        out_specs=pl.BlockSpec((tm, tn), lambda i, j: (i, j)),
        out_shape=jax.ShapeDtypeStruct((M, N), out_dtype),
        compiler_params=_cp("parallel", "arbitrary"),
        name="matmul",
    )(*args)


def _rope_table_kernel(pos_ref, invf_ref, o_ref):
    ang = pos_ref[...].astype(F32) * invf_ref[...]
    lane = lax.broadcasted_iota(jnp.int32, ang.shape, 1)
    c = jnp.cos(ang)
    s = jnp.sin(ang)
    o_ref[:, 0:LANES] = jnp.where(lane < QK_ROPE_DIM, c, 0.0)
    o_ref[:, LANES:2 * LANES] = jnp.where(lane < ROPE_HALF, -s, 0.0)
    o_ref[:, 2 * LANES:3 * LANES] = jnp.where((lane >= ROPE_HALF) & (lane < QK_ROPE_DIM), s, 0.0)


def rope_tables(positions, tm=512):
    T = positions.size
    inv_freq = ROPE_THETA ** (-np.arange(0, QK_ROPE_DIM, 2, dtype=np.float32) / QK_ROPE_DIM)
    invf = np.zeros((1, LANES), np.float32)
    invf[0, :ROPE_HALF] = inv_freq
    invf[0, ROPE_HALF:QK_ROPE_DIM] = inv_freq
    return pl.pallas_call(
        _rope_table_kernel,
        grid=(T // tm,),
        in_specs=[pl.BlockSpec((tm, 1), lambda i: (i, 0)),
                  pl.BlockSpec((1, LANES), lambda i: (0, 0))],
        out_specs=pl.BlockSpec((tm, 3 * LANES), lambda i: (i, 0)),
        out_shape=jax.ShapeDtypeStruct((T, 3 * LANES), F32),
        compiler_params=_cp("parallel"),
        name="rope_tables",
    )(positions.reshape(T, 1), jnp.asarray(invf))


def _rope(x, tab):
    c = tab[:, 0:LANES]
    s1 = tab[:, LANES:2 * LANES]
    s2 = tab[:, 2 * LANES:3 * LANES]
    return x * c + pltpu.roll(x, LANES - ROPE_HALF, 1) * s1 + pltpu.roll(x, ROPE_HALF, 1) * s2


def _mla_prep_kernel(cq_ref, ckv_ref, kpe_ref, tab_ref, wuq_ref, wuk_ref, wuv_ref,
                     gqa_ref, gkva_ref, gq_ref, gk_ref, q_ref, k_ref, v_ref):
    tab = tab_ref[...]
    cq = cq_ref[...]
    cqn = (cq * _rms(cq, Q_LORA_RANK) * gqa_ref[...]).astype(BF16)
    qf = jnp.dot(cqn, wuq_ref[...], preferred_element_type=F32)
    ckv = ckv_ref[...]
    ckvn = (ckv * _rms(ckv, KV_LORA_RANK) * gkva_ref[...]).astype(BF16)
    kn = jnp.dot(ckvn, wuk_ref[...], preferred_element_type=F32)
    v_ref[...] = jnp.dot(ckvn, wuv_ref[...], preferred_element_type=F32).astype(v_ref.dtype)
    gq = gq_ref[...]
    gk = gk_ref[...]
    kpe = kpe_ref[...]
    ss_pe = jnp.sum(kpe * kpe, axis=-1, keepdims=True)
    kpe_rot = _rope(kpe * gk[:, LANES:], tab)
    scale = QK_HEAD_DIM ** -0.5
    for h in range(MLA_HEADS):
        q0 = qf[:, h * HEAD_PAD:h * HEAD_PAD + LANES]
        q1 = qf[:, h * HEAD_PAD + LANES:(h + 1) * HEAD_PAD]
        ss = jnp.sum(q0 * q0, axis=-1, keepdims=True) + jnp.sum(q1 * q1, axis=-1, keepdims=True)
        rq = lax.rsqrt(ss * (1.0 / QK_HEAD_DIM) + NORM_EPS) * scale
        q_ref[:, h * HEAD_PAD:h * HEAD_PAD + LANES] = (q0 * gq[:, :LANES] * rq).astype(q_ref.dtype)
        q_ref[:, h * HEAD_PAD + LANES:(h + 1) * HEAD_PAD] = (_rope(q1 * gq[:, LANES:], tab) * rq).astype(q_ref.dtype)
        k0 = kn[:, h * LANES:(h + 1) * LANES]
        ssk = jnp.sum(k0 * k0, axis=-1, keepdims=True) + ss_pe
        rk = lax.rsqrt(ssk * (1.0 / QK_HEAD_DIM) + NORM_EPS)
        k_ref[:, h * HEAD_PAD:h * HEAD_PAD + LANES] = (k0 * gk[:, :LANES] * rk).astype(k_ref.dtype)
        k_ref[:, h * HEAD_PAD + LANES:(h + 1) * HEAD_PAD] = (kpe_rot * rk).astype(k_ref.dtype)


def mla_prep(za, tab, wuq, wuk, wuv, gqa, gkva, gq, gk, tm=512):
    T = za.shape[0]
    HP = MLA_HEADS * HEAD_PAD
    full = lambda shape: pl.BlockSpec(shape, lambda i: (0, 0))
    return pl.pallas_call(
        _mla_prep_kernel,
        grid=(T // tm,),
        in_specs=[pl.BlockSpec((tm, Q_LORA_RANK), lambda i: (i, ZA_CQ // Q_LORA_RANK)),
                  pl.BlockSpec((tm, KV_LORA_RANK), lambda i: (i, ZA_CKV // KV_LORA_RANK)),
                  pl.BlockSpec((tm, LANES), lambda i: (i, ZA_KPE // LANES)),
                  pl.BlockSpec((tm, 3 * LANES), lambda i: (i, 0)),
                  full((Q_LORA_RANK, HP)), full((KV_LORA_RANK, MLA_WIDTH)), full((KV_LORA_RANK, MLA_WIDTH)),
                  full((1, Q_LORA_RANK)), full((1, KV_LORA_RANK)), full((1, HEAD_PAD)), full((1, HEAD_PAD))],
        out_specs=[pl.BlockSpec((tm, HP), lambda i: (i, 0)),
                   pl.BlockSpec((tm, HP), lambda i: (i, 0)),
                   pl.BlockSpec((tm, MLA_WIDTH), lambda i: (i, 0))],
        out_shape=[jax.ShapeDtypeStruct((T, HP), BF16),
                   jax.ShapeDtypeStruct((T, HP), BF16),
                   jax.ShapeDtypeStruct((T, MLA_WIDTH), BF16)],
        compiler_params=_cp("parallel"),
        name="mla_prep",
    )(za, za, za, tab, wuq, wuk, wuv, gqa, gkva, gq, gk)


def _causal_attn_kernel(q_ref, k_ref, v_ref, o_ref, *, tq):
    S = q_ref.shape[0]
    row = lax.broadcasted_iota(jnp.int32, (tq, tq), 0)
    col = lax.broadcasted_iota(jnp.int32, (tq, tq), 1)
    tri = col <= row
    for qi in range(S // tq):
        kl = (qi + 1) * tq
        q = q_ref[qi * tq:kl, :]
        s = lax.dot_general(q, k_ref[0:kl, :], (((1,), (1,)), ((), ())), preferred_element_type=F32)
        s_diag = jnp.where(tri, s[:, kl - tq:kl], -jnp.inf)
        if qi > 0:
            s = jnp.concatenate([s[:, :kl - tq], s_diag], axis=1)
        else:
            s = s_diag
        m = jnp.max(s, axis=-1, keepdims=True)
        p = jnp.exp(s - m)
        l = jnp.sum(p, axis=-1, keepdims=True)
        o = jnp.dot(p.astype(BF16), v_ref[0:kl, :], preferred_element_type=F32)
        o_ref[qi * tq:kl, :] = (o / l).astype(o_ref.dtype)


def causal_attention(q, k, v, B, S, tq=256):
    T = q.shape[0]
    return pl.pallas_call(
        functools.partial(_causal_attn_kernel, tq=tq),
        grid=(B, MLA_HEADS),
        in_specs=[pl.BlockSpec((S, HEAD_PAD), lambda b, h: (b, h)),
                  pl.BlockSpec((S, HEAD_PAD), lambda b, h: (b, h)),
                  pl.BlockSpec((S, V_HEAD_DIM), lambda b, h: (b, h))],
        out_specs=pl.BlockSpec((S, V_HEAD_DIM), lambda b, h: (b, h)),
        out_shape=jax.ShapeDtypeStruct((T, MLA_WIDTH), BF16),
        compiler_params=_cp("parallel", "parallel"),
        name="causal_attention",
    )(q, k, v)


def _pool_kernel(u_ref, w_ref, sc_ref, o_ref, pad_ref, *, rc):
    S = u_ref.shape[0]
    C = POOL_GROUP_DIM
    t1 = lax.broadcasted_iota(jnp.int32, (rc, C), 0).astype(F32) + 1.0
    for g, win in enumerate(POOL_WINDOWS):
        pad_ref[0:POOL_HALO, :] = jnp.zeros((POOL_HALO, C), F32)
        pad_ref[POOL_HALO:POOL_HALO + S, :] = u_ref[:, g * C:(g + 1) * C]
        for r0 in range(0, S, rc):
            acc = pad_ref[POOL_HALO + r0:POOL_HALO + r0 + rc, :]
            tok = acc
            for kk in range(1, win):
                acc = acc + pad_ref[POOL_HALO + r0 - kk:POOL_HALO + r0 - kk + rc, :]
            cnt = jnp.minimum(t1 + float(r0), float(win))
            pooled = (acc / cnt - tok).astype(BF16)
            mixed = jnp.dot(pooled, w_ref[g], preferred_element_type=F32) * sc_ref[:, g * C:(g + 1) * C]
            o_ref[r0:r0 + rc, g * C:(g + 1) * C] = mixed.astype(o_ref.dtype)


def pool_mixer(za, pool_w, pool_scale, B, S, rc=512):
    T = za.shape[0]
    rc = min(rc, S)
    return pl.pallas_call(
        functools.partial(_pool_kernel, rc=rc),
        grid=(B,),
        in_specs=[pl.BlockSpec((S, POOL_WIDTH), lambda b: (b, ZA_POOL // POOL_WIDTH)),
                  pl.BlockSpec((POOL_GROUPS, POOL_GROUP_DIM, POOL_GROUP_DIM), lambda b: (0, 0, 0)),
                  pl.BlockSpec((1, POOL_WIDTH), lambda b: (0, 0))],
        out_specs=pl.BlockSpec((S, POOL_WIDTH), lambda b: (b, 0)),
        out_shape=jax.ShapeDtypeStruct((T, POOL_WIDTH), BF16),
        scratch_shapes=[pltpu.VMEM((POOL_HALO + S, POOL_GROUP_DIM), F32)],
        compiler_params=_cp("parallel"),
        name="pool_mixer",
    )(za, pool_w, pool_scale)


def _mem_kv_kernel(m_ref, g_ref, w_ref, gk_ref, k_ref, v_ref):
    x = m_ref[...]
    xn = (x * _rms(x, x.shape[-1]) * g_ref[...]).astype(BF16)
    kv = jnp.dot(xn, w_ref[...], preferred_element_type=F32)
    for h in range(MEM_HEADS):
        kh = kv[:, h * MEM_HEAD_DIM:(h + 1) * MEM_HEAD_DIM]
        k_ref[:, h * MEM_HEAD_DIM:(h + 1) * MEM_HEAD_DIM] = (
            kh * _rms(kh, MEM_HEAD_DIM) * gk_ref[...]).astype(k_ref.dtype)
    v_ref[...] = kv[:, MEM_WIDTH:].astype(v_ref.dtype)


def mem_kv(mem2d, g, w_kv, gk, B, M):
    D = mem2d.shape[1]
    return pl.pallas_call(
        _mem_kv_kernel,
        grid=(B,),
        in_specs=[pl.BlockSpec((M, D), lambda b: (b, 0)),
                  pl.BlockSpec((1, D), lambda b: (0, 0)),
                  pl.BlockSpec((D, 2 * MEM_WIDTH), lambda b: (0, 0)),
                  pl.BlockSpec((1, MEM_HEAD_DIM), lambda b: (0, 0))],
        out_specs=[pl.BlockSpec((M, MEM_WIDTH), lambda b: (b, 0)),
                   pl.BlockSpec((M, MEM_WIDTH), lambda b: (b, 0))],
        out_shape=[jax.ShapeDtypeStruct((B * M, MEM_WIDTH), BF16),
                   jax.ShapeDtypeStruct((B * M, MEM_WIDTH), BF16)],
        compiler_params=_cp("parallel"),
        name="mem_kv",
    )(mem2d, g, w_kv, gk)


def _mem_attn_kernel(q_ref, k_ref, v_ref, gq_ref, o_ref):
    scale = MEM_HEAD_DIM ** -0.5
    for h in range(MEM_HEADS):
        sl = slice(h * MEM_HEAD_DIM, (h + 1) * MEM_HEAD_DIM)
        qh = q_ref[:, sl]
        qn = (qh * (_rms(qh, MEM_HEAD_DIM) * scale) * gq_ref[...]).astype(BF16)
        s = lax.dot_general(qn, k_ref[:, sl], (((1,), (1,)), ((), ())), preferred_element_type=F32)
        m = jnp.max(s, axis=-1, keepdims=True)
        p = jnp.exp(s - m)
        l = jnp.sum(p, axis=-1, keepdims=True)
        o = jnp.dot(p.astype(BF16), v_ref[:, sl], preferred_element_type=F32)
        o_ref[:, sl] = (o / l).astype(o_ref.dtype)


def mem_attention(za, k_m, v_m, gq, B, S, M, tq=512):
    T = za.shape[0]
    tq = min(tq, S)
    nq = S // tq
    return pl.pallas_call(
        _mem_attn_kernel,
        grid=(B, nq),
        in_specs=[pl.BlockSpec((tq, MEM_WIDTH), lambda b, i: (b * nq + i, ZA_QMEM // MEM_WIDTH)),
                  pl.BlockSpec((M, MEM_WIDTH), lambda b, i: (b, 0)),
                  pl.BlockSpec((M, MEM_WIDTH), lambda b, i: (b, 0)),
                  pl.BlockSpec((1, MEM_HEAD_DIM), lambda b, i: (0, 0))],
        out_specs=pl.BlockSpec((tq, MEM_WIDTH), lambda b, i: (b * nq + i, 0)),
        out_shape=jax.ShapeDtypeStruct((T, MEM_WIDTH), BF16),
        compiler_params=_cp("parallel", "arbitrary"),
        name="mem_attention",
    )(za, k_m, v_m, gq)


def _merge_kernel(h_ref, a_ref, p_ref, m_ref, g0_ref, g1_ref, g2_ref, w0_ref, w1_ref, w2_ref, o_ref):
    h = h_ref[...]
    acc = None
    for x_ref, g_ref, w_ref in ((a_ref, g0_ref, w0_ref), (p_ref, g1_ref, w1_ref), (m_ref, g2_ref, w2_ref)):
        logit = jnp.dot(h, g_ref[...], preferred_element_type=F32)
        gate = 1.0 / (1.0 + jnp.exp(-logit))
        br = jnp.dot(x_ref[...], w_ref[...], preferred_element_type=F32)
        acc = gate * br if acc is None else acc + gate * br
    o_ref[...] = acc.astype(o_ref.dtype)


def merge_branches(h, a, p, m, w_gates, w_a, w_p, w_m, tm=512, tn=512):
    T, D = h.shape
    W = a.shape[1]
    nj = D // tn
    row = lambda width: pl.BlockSpec((tm, width), lambda i, j: (i, 0))
    gate_spec = lambda b: pl.BlockSpec((D, tn), lambda i, j, b=b: (0, b * nj + j))
    out_w = pl.BlockSpec((W, tn), lambda i, j: (0, j))
    return pl.pallas_call(
        _merge_kernel,
        grid=(T // tm, nj),
        in_specs=[row(D), row(W), row(W), row(W),
                  gate_spec(0), gate_spec(1), gate_spec(2), out_w, out_w, out_w],
        out_specs=pl.BlockSpec((tm, tn), lambda i, j: (i, j)),
        out_shape=jax.ShapeDtypeStruct((T, D), BF16),
        compiler_params=_cp("parallel", "arbitrary"),
        name="merge_branches",
    )(h, a, p, m, w_gates, w_gates, w_gates, w_a, w_p, w_m)


def _swiglu_accumulate(xb, wg_ref, wu_ref, wd_ref, o_ref):
    a = jnp.dot(xb, wg_ref[0], preferred_element_type=F32)
    b = jnp.dot(xb, wu_ref[0], preferred_element_type=F32)
    act = (a / (1.0 + jnp.exp(-a)) * b).astype(BF16)
    o_ref[...] += jnp.dot(act, wd_ref[0], preferred_element_type=F32)


def _dense_swiglu_kernel(x_ref, wg_ref, wu_ref, wd_ref, r_ref, o_ref):
    @pl.when(pl.program_id(1) == 0)
    def _():
        o_ref[...] = r_ref[...]

    _swiglu_accumulate(x_ref[...], wg_ref, wu_ref, wd_ref, o_ref)


def dense_swiglu(x, w_gate, w_up, w_down, residual, tm=1024, tf=512):
    T, D = x.shape
    F = w_gate.shape[2]
    tm = min(tm, T)
    return pl.pallas_call(
        _dense_swiglu_kernel,
        grid=(T // tm, F // tf),
        in_specs=[pl.BlockSpec((tm, D), lambda i, f: (i, 0)),
                  pl.BlockSpec((1, D, tf), lambda i, f: (0, 0, f)),
                  pl.BlockSpec((1, D, tf), lambda i, f: (0, 0, f)),
                  pl.BlockSpec((1, tf, D), lambda i, f: (0, f, 0)),
                  pl.BlockSpec((tm, D), lambda i, f: (i, 0), pipeline_mode=pl.Buffered(1))],
        out_specs=pl.BlockSpec((tm, D), lambda i, f: (i, 0)),
        out_shape=jax.ShapeDtypeStruct((T, D), F32),
        compiler_params=_cp("parallel", "arbitrary"),
        name="dense_swiglu",
    )(x, w_gate, w_up, w_down, residual)


def _row_copy(src_hbm, dst_ref, src_row, dst_row, sem):
    return pltpu.make_async_copy(src_hbm.at[pl.ds(src_row, 1)], dst_ref.at[pl.ds(dst_row, 1)], sem)


def _moe_swiglu_kernel(tok_ref, be_ref, nu_ref, h_hbm, wg_ref, wu_ref, wd_ref, o_ref,
                       xg_ref, xb_ref, sem, *, tm, nf):
    del be_ref
    i = pl.program_id(0)
    f = pl.program_id(1)
    n_used = nu_ref[0]
    rows_per_step = -(-tm // nf)
    n_fetch = rows_per_step * nf

    def fetch(block, r, slot):
        tok = tok_ref[block * tm + jnp.minimum(r, tm - 1)]
        return _row_copy(h_hbm, xg_ref.at[slot], tok, r, sem.at[slot])

    def wait_rows(slot):
        def wait(r, c):
            _row_copy(h_hbm, xg_ref.at[slot], 0, r, sem.at[slot]).wait()
            return c
        lax.fori_loop(0, n_fetch, wait, 0)

    @pl.when(i < n_used)
    def _():
        slot = i % 2

        @pl.when((i == 0) & (f == 0))
        def _():
            def start(r, c):
                fetch(0, r, 0).start()
                return c
            lax.fori_loop(0, n_fetch, start, 0)

        @pl.when(f == 0)
        def _():
            wait_rows(slot)
            xb_ref[...] = xg_ref[slot, 0:tm, :].astype(BF16)
            o_ref[...] = jnp.zeros_like(o_ref)

        nxt = jnp.minimum(i + 1, n_used - 1)
        for j in range(rows_per_step):
            fetch(nxt, f * rows_per_step + j, 1 - slot).start()

        _swiglu_accumulate(xb_ref[...], wg_ref, wu_ref, wd_ref, o_ref)

        @pl.when((i == n_used - 1) & (f == nf - 1))
        def _():
            wait_rows(1 - slot)


def moe_swiglu(h, slot_tok, block_e, n_used, w_gate, w_up, w_down, tm, tf=1024):
    n_slots = slot_tok.shape[0]
    D = h.shape[1]
    F = w_gate.shape[2]
    while F % tf:
        tf //= 2
    nf = F // tf
    xg_rows = -(-(-(-tm // nf) * nf) // 8) * 8
    blk = lambda i, nu: jnp.minimum(i, nu[0] - 1)
    chunk = lambda i, f, nu: jnp.where(i < nu[0], f, nf - 1)
    return pl.pallas_call(
        functools.partial(_moe_swiglu_kernel, tm=tm, nf=nf),
        grid_spec=pltpu.PrefetchScalarGridSpec(
            num_scalar_prefetch=3,
            grid=(n_slots // tm, nf),
            in_specs=[pl.BlockSpec(memory_space=pl.ANY),
                      pl.BlockSpec((1, D, tf), lambda i, f, tok, be, nu: (be[blk(i, nu)], 0, chunk(i, f, nu))),
                      pl.BlockSpec((1, D, tf), lambda i, f, tok, be, nu: (be[blk(i, nu)], 0, chunk(i, f, nu))),
                      pl.BlockSpec((1, tf, D), lambda i, f, tok, be, nu: (be[blk(i, nu)], chunk(i, f, nu), 0))],
            out_specs=pl.BlockSpec((tm, D), lambda i, f, tok, be, nu: (blk(i, nu), 0)),
            scratch_shapes=[pltpu.VMEM((2, xg_rows, D), F32),
                            pltpu.VMEM((tm, D), BF16),
                            pltpu.SemaphoreType.DMA((2,))]),
        out_shape=jax.ShapeDtypeStruct((n_slots, D), F32),
        compiler_params=_cp("arbitrary", "arbitrary"),
        name="moe_swiglu",
    )(slot_tok, block_e, n_used, h, w_gate, w_up, w_down)


def _combine_kernel(dest_ref, y_ref, gate_ref, ys_hbm, o_ref, buf_ref, sem, *, tb):
    base = pl.program_id(0) * tb

    def start(r, c):
        for k in range(TOP_K):
            _row_copy(ys_hbm, buf_ref.at[k], dest_ref[TOP_K * (base + r) + k], r, sem).start()
        return c

    lax.fori_loop(0, tb, start, 0)

    def wait(r, c):
        for k in range(TOP_K):
            _row_copy(ys_hbm, buf_ref.at[k], 0, r, sem).wait()
        return c

    lax.fori_loop(0, tb, wait, 0)
    g = gate_ref[...]
    o_ref[...] = y_ref[...] + g[:, 0:1] * buf_ref[0] + g[:, 1:2] * buf_ref[1]


def combine_rows(y, gates, ys, dest, tb=256):
    T, D = y.shape
    return pl.pallas_call(
        functools.partial(_combine_kernel, tb=tb),
        grid_spec=pltpu.PrefetchScalarGridSpec(
            num_scalar_prefetch=1,
            grid=(T // tb,),
            in_specs=[pl.BlockSpec((tb, D), lambda i, d: (i, 0)),
                      pl.BlockSpec((tb, LANES), lambda i, d: (i, 0)),
                      pl.BlockSpec(memory_space=pl.ANY)],
            out_specs=pl.BlockSpec((tb, D), lambda i, d: (i, 0)),
            scratch_shapes=[pltpu.VMEM((TOP_K, tb, D), F32),
                            pltpu.SemaphoreType.DMA(())]),
        out_shape=jax.ShapeDtypeStruct((T, D), F32),
        compiler_params=_cp("arbitrary"),
        name="moe_combine",
    )(dest, y, gates, ys)


def _routing_tables(top_idx, T, block):
    A = T * TOP_K
    E = N_EXPERTS
    flat_e = top_idx.reshape(A)
    onehot = (flat_e[:, None] == jnp.arange(E, dtype=jnp.int32)[None, :]).astype(jnp.int32)
    csum = jnp.cumsum(onehot, axis=0)
    counts = csum[-1]
    rank = jnp.sum((csum - 1) * onehot, axis=1)
    padded = (counts + block - 1) // block * block
    pends = jnp.cumsum(padded)
    pstarts = pends - padded
    starts = jnp.cumsum(counts) - counts
    dest = (pstarts[flat_e] + rank).astype(jnp.int32)
    n_blocks = A // block + E
    n_slots = n_blocks * block
    order = jnp.argsort(flat_e, stable=True).astype(jnp.int32)
    slot = jnp.arange(n_slots, dtype=jnp.int32)
    slot_e = jnp.clip(jnp.searchsorted(pends, slot, side='right'), 0, E - 1).astype(jnp.int32)
    r = slot - pstarts[slot_e]
    valid = r < counts[slot_e]
    src = jnp.clip(starts[slot_e] + r, 0, A - 1)
    slot_tok = jnp.where(valid, order[src] // TOP_K, 0).astype(jnp.int32)
    block_e = slot_e[::block]
    n_used = (pends[-1:] // block).astype(jnp.int32)
    return slot_tok, dest, block_e, n_used


def _token_mixer(y, h, mem2d, tab, B, S, M, l, w_in, mla_q_a_norm_g, mla_w_uq, mla_kv_a_norm_g, mla_w_ukv,
                 mla_q_norm_g, mla_k_norm_g, mla_w_out, pool_w, pool_scale, pool_w_out,
                 mem_norm_g, mem_w_kv, mem_q_norm_g, mem_k_norm_g, mem_w_out, w_o):
    D = h.shape[1]
    wl = w_in[l]
    s0, s1, s2, s3 = (Q_LORA_RANK, Q_LORA_RANK + KV_LORA_RANK + QK_ROPE_DIM,
                      Q_LORA_RANK + KV_LORA_RANK + QK_ROPE_DIM + POOL_WIDTH,
                      Q_LORA_RANK + KV_LORA_RANK + QK_ROPE_DIM + POOL_WIDTH + MEM_WIDTH)
    used = POOL_WIDTH + MEM_WIDTH + s1
    w_a = jnp.concatenate([wl[:, s1:s2], wl[:, s2:s3], wl[:, :s1],
                           jnp.zeros((D, ZA_WIDTH - used), F32)], axis=1).astype(BF16)
    w_gates = wl[:, s3:].astype(BF16)
    za = matmul(h, w_a, F32)

    wuq = jnp.pad(mla_w_uq[l].reshape(Q_LORA_RANK, MLA_HEADS, QK_HEAD_DIM),
                  ((0, 0), (0, 0), (0, HEAD_PAD - QK_HEAD_DIM))).reshape(Q_LORA_RANK, MLA_HEADS * HEAD_PAD)
    wukv = mla_w_ukv[l].reshape(KV_LORA_RANK, MLA_HEADS, QK_NOPE_DIM + V_HEAD_DIM)
    wuk = wukv[:, :, :QK_NOPE_DIM].reshape(KV_LORA_RANK, MLA_WIDTH)
    wuv = wukv[:, :, QK_NOPE_DIM:].reshape(KV_LORA_RANK, MLA_WIDTH)
    pad_g = lambda g: jnp.pad(g, (0, HEAD_PAD - QK_HEAD_DIM)).reshape(1, HEAD_PAD)
    q, k, v = mla_prep(za, tab, wuq.astype(BF16), wuk.astype(BF16), wuv.astype(BF16),
                       mla_q_a_norm_g[l].reshape(1, -1), mla_kv_a_norm_g[l].reshape(1, -1),
                       pad_g(mla_q_norm_g[l]), pad_g(mla_k_norm_g[l]))
    attn = causal_attention(q, k, v, B, S)

    mixed = pool_mixer(za, pool_w[l].astype(BF16), pool_scale[l].reshape(1, -1), B, S)

    k_m, v_m = mem_kv(mem2d, mem_norm_g[l].reshape(1, -1), mem_w_kv[l].astype(BF16),
                      mem_k_norm_g[l].reshape(1, -1), B, M)
    o_mem = mem_attention(za, k_m, v_m, mem_q_norm_g[l].reshape(1, -1), B, S, M)

    merged = merge_branches(h, attn, mixed, o_mem, w_gates, mla_w_out[l].astype(BF16),
                            pool_w_out[l].astype(BF16), mem_w_out[l].astype(BF16))
    return matmul(merged, w_o[l].astype(BF16), F32, residual=y)


def kernel(x, mem, positions, attn_norm_g, w_in, mla_q_a_norm_g, mla_w_uq, mla_kv_a_norm_g, mla_w_ukv, mla_q_norm_g, mla_k_norm_g, mla_w_out, pool_w, pool_scale, pool_w_out, mem_norm_g, mem_w_kv, mem_q_norm_g, mem_k_norm_g, mem_w_out, w_o, ffn_norm_g, dense_w_gate, dense_w_up, dense_w_down, router_w, router_b, moe_w_gate, moe_w_up, moe_w_down):
    B, S, D = x.shape
    M = mem.shape[1]
    T = B * S
    depth = attn_norm_g.shape[0]
    moe_block = 512
    y = x.reshape(T, D)
    mem2d = mem.reshape(B * M, D)
    tab = rope_tables(positions)
    for l in range(depth):
        h = rmsnorm(y, attn_norm_g[l], BF16)
        y = _token_mixer(y, h, mem2d, tab, B, S, M, l, w_in, mla_q_a_norm_g, mla_w_uq, mla_kv_a_norm_g,
                         mla_w_ukv, mla_q_norm_g, mla_k_norm_g, mla_w_out, pool_w, pool_scale, pool_w_out,
                         mem_norm_g, mem_w_kv, mem_q_norm_g, mem_k_norm_g, mem_w_out, w_o)
        i = l // 2
        if l % 2 == 0:
            h = rmsnorm(y, ffn_norm_g[l], BF16)
            y = dense_swiglu(h, dense_w_gate[i:i + 1].astype(BF16), dense_w_up[i:i + 1].astype(BF16),
                             dense_w_down[i:i + 1].astype(BF16), y)
        else:
            hf, idx, gates = norm_router(y, ffn_norm_g[l], router_w[i], router_b[i])
            slot_tok, dest, block_e, n_used = _routing_tables(idx[:, :TOP_K], T, moe_block)
            ys = moe_swiglu(hf, slot_tok, block_e, n_used, moe_w_gate[i].astype(BF16),
                            moe_w_up[i].astype(BF16), moe_w_down[i].astype(BF16), tm=moe_block)
            y = combine_rows(y, gates, ys, dest)
    return y.reshape(B, S, D)
```

```python
import functools

import numpy as np
import jax
import jax.numpy as jnp
from jax import lax
from jax.experimental import pallas as pl
from jax.experimental.pallas import tpu as pltpu

F32 = jnp.float32
BF16 = jnp.bfloat16

MEM_HEADS = 4
MEM_HEAD_DIM = 256
MEM_WIDTH = MEM_HEADS * MEM_HEAD_DIM
MLA_HEADS = 8
Q_LORA_RANK = 512
KV_LORA_RANK = 256
QK_NOPE_DIM = 128
QK_ROPE_DIM = 64
QK_HEAD_DIM = QK_NOPE_DIM + QK_ROPE_DIM
V_HEAD_DIM = 128
MLA_WIDTH = MLA_HEADS * V_HEAD_DIM
ROPE_THETA = 10000.0
POOL_WINDOWS = (2, 4, 8, 16)
POOL_GROUPS = 4
POOL_GROUP_DIM = 256
POOL_WIDTH = POOL_GROUPS * POOL_GROUP_DIM
N_BRANCHES = 3
N_EXPERTS = 8
TOP_K = 2
NORM_EPS = 1e-6

LANES = 128
HEAD_PAD = 2 * LANES
ROPE_HALF = QK_ROPE_DIM // 2
POOL_HALO = 16

ZA_POOL = 0
ZA_QMEM = POOL_WIDTH
ZA_CQ = ZA_QMEM + MEM_WIDTH
ZA_CKV = ZA_CQ + Q_LORA_RANK
ZA_KPE = ZA_CKV + KV_LORA_RANK
ZA_WIDTH = 3072

VMEM_LIMIT = 52 * 1024 * 1024


def _cp(*sem):
    return pltpu.CompilerParams(dimension_semantics=sem, vmem_limit_bytes=VMEM_LIMIT)


def _rms(x, eps_dim):
    return lax.rsqrt(jnp.sum(x * x, axis=-1, keepdims=True) * (1.0 / eps_dim) + NORM_EPS)


def _rmsnorm_kernel(x_ref, g_ref, o_ref):
    x = x_ref[...]
    o_ref[...] = (x * _rms(x, x.shape[-1]) * g_ref[...]).astype(o_ref.dtype)


def rmsnorm(x, g, out_dtype, tm=512):
    T, D = x.shape
    return pl.pallas_call(
        _rmsnorm_kernel,
        grid=(T // tm,),
        in_specs=[pl.BlockSpec((tm, D), lambda i: (i, 0)),
                  pl.BlockSpec((1, D), lambda i: (0, 0))],
        out_specs=pl.BlockSpec((tm, D), lambda i: (i, 0)),
        out_shape=jax.ShapeDtypeStruct((T, D), out_dtype),
        compiler_params=_cp("parallel"),
        name="rmsnorm",
    )(x, g.reshape(1, D))


def _norm_router_kernel(x_ref, g_ref, rw_ref, rb_ref, h_ref, idx_ref, gate_ref):
    x = x_ref[...]
    h = x * _rms(x, x.shape[-1]) * g_ref[...]
    h_ref[...] = h
    logits = jnp.dot(h, rw_ref[...], preferred_element_type=F32,
                     precision=lax.Precision.HIGHEST) + rb_ref[...]
    lane = lax.broadcasted_iota(jnp.int32, logits.shape, 1)
    neg = -jnp.inf
    l1 = jnp.where(lane < N_EXPERTS, logits, neg)
    m1 = jnp.max(l1, axis=-1, keepdims=True)
    i1 = jnp.min(jnp.where(l1 == m1, lane, LANES), axis=-1, keepdims=True)
    l2 = jnp.where(lane == i1, neg, l1)
    m2 = jnp.max(l2, axis=-1, keepdims=True)
    i2 = jnp.min(jnp.where(l2 == m2, lane, LANES), axis=-1, keepdims=True)
    e = jnp.exp(m2 - m1)
    g1 = 1.0 / (1.0 + e)
    g2 = e / (1.0 + e)
    idx_ref[...] = jnp.where(lane == 0, i1, jnp.where(lane == 1, i2, 0))
    gate_ref[...] = jnp.where(lane == 0, g1, jnp.where(lane == 1, g2, 0.0))


def norm_router(x, g, router_w, router_b, tm=512):
    T, D = x.shape
    E = router_w.shape[1]
    rw = jnp.zeros((D, LANES), F32).at[:, :E].set(router_w)
    rb = jnp.zeros((1, LANES), F32).at[0, :E].set(router_b)
    return pl.pallas_call(
        _norm_router_kernel,
        grid=(T // tm,),
        in_specs=[pl.BlockSpec((tm, D), lambda i: (i, 0)),
                  pl.BlockSpec((1, D), lambda i: (0, 0)),
                  pl.BlockSpec((D, LANES), lambda i: (0, 0)),
                  pl.BlockSpec((1, LANES), lambda i: (0, 0))],
        out_specs=[pl.BlockSpec((tm, D), lambda i: (i, 0)),
                   pl.BlockSpec((tm, LANES), lambda i: (i, 0)),
                   pl.BlockSpec((tm, LANES), lambda i: (i, 0))],
        out_shape=[jax.ShapeDtypeStruct((T, D), F32),
                   jax.ShapeDtypeStruct((T, LANES), jnp.int32),
                   jax.ShapeDtypeStruct((T, LANES), F32)],
        compiler_params=_cp("parallel"),
        name="norm_router",
    )(x, g.reshape(1, D), rw, rb)


def _mm_kernel(a_ref, w_ref, o_ref):
    o_ref[...] = jnp.dot(a_ref[...], w_ref[0], preferred_element_type=F32).astype(o_ref.dtype)


def _mm_res_kernel(a_ref, w_ref, r_ref, o_ref):
    o_ref[...] = (r_ref[...] + jnp.dot(a_ref[...], w_ref[0], preferred_element_type=F32)).astype(o_ref.dtype)


def matmul(a, w, l, N, out_dtype, residual=None, tm=1024, tn=1024):
    M, K = a.shape
    tm, tn = min(tm, M), min(tn, N)
    in_specs = [pl.BlockSpec((tm, K), lambda i, j: (i, 0)),
                pl.BlockSpec((1, K, tn), lambda i, j: (l, 0, j))]
    args = [a, w]
    body = _mm_kernel
    if residual is not None:
        in_specs.append(pl.BlockSpec((tm, tn), lambda i, j: (i, j)))
        args.append(residual)
        body = _mm_res_kernel
    return pl.pallas_call(
        body,
        grid=(M // tm, N // tn),
        in_specs=in_specs,
        out_specs=pl.BlockSpec((tm, tn), lambda i, j: (i, j)),
        out_shape=jax.ShapeDtypeStruct((M, N), out_dtype),
        compiler_params=_cp("parallel", "arbitrary"),
        name="matmul",
    )(*args)


def _rope_table_kernel(pos_ref, invf_ref, o_ref):
    ang = pos_ref[...].astype(F32) * invf_ref[...]
    lane = lax.broadcasted_iota(jnp.int32, ang.shape, 1)
    c = jnp.cos(ang)
    s = jnp.sin(ang)
    o_ref[:, 0:LANES] = jnp.where(lane < QK_ROPE_DIM, c, 0.0)
    o_ref[:, LANES:2 * LANES] = jnp.where(lane < ROPE_HALF, -s, 0.0)
    o_ref[:, 2 * LANES:3 * LANES] = jnp.where((lane >= ROPE_HALF) & (lane < QK_ROPE_DIM), s, 0.0)


def rope_tables(positions, tm=512):
    T = positions.size
    inv_freq = ROPE_THETA ** (-np.arange(0, QK_ROPE_DIM, 2, dtype=np.float32) / QK_ROPE_DIM)
    invf = np.zeros((1, LANES), np.float32)
    invf[0, :ROPE_HALF] = inv_freq
    invf[0, ROPE_HALF:QK_ROPE_DIM] = inv_freq
    return pl.pallas_call(
        _rope_table_kernel,
        grid=(T // tm,),
        in_specs=[pl.BlockSpec((tm, 1), lambda i: (i, 0)),
                  pl.BlockSpec((1, LANES), lambda i: (0, 0))],
        out_specs=pl.BlockSpec((tm, 3 * LANES), lambda i: (i, 0)),
        out_shape=jax.ShapeDtypeStruct((T, 3 * LANES), F32),
        compiler_params=_cp("parallel"),
        name="rope_tables",
    )(positions.reshape(T, 1), jnp.asarray(invf))


def _rope(x, tab):
    c = tab[:, 0:LANES]
    s1 = tab[:, LANES:2 * LANES]
    s2 = tab[:, 2 * LANES:3 * LANES]
    return x * c + pltpu.roll(x, LANES - ROPE_HALF, 1) * s1 + pltpu.roll(x, ROPE_HALF, 1) * s2


def _mla_prep_kernel(cq_ref, ckv_ref, kpe_ref, tab_ref, wuq_ref, wuk_ref, wuv_ref,
                     gqa_ref, gkva_ref, gq_ref, gk_ref, q_ref, k_ref, v_ref):
    tab = tab_ref[...]
    cq = cq_ref[...]
    cqn = (cq * _rms(cq, Q_LORA_RANK) * gqa_ref[...]).astype(BF16)
    qf = jnp.dot(cqn, wuq_ref[...], preferred_element_type=F32)
    ckv = ckv_ref[...]
    ckvn = (ckv * _rms(ckv, KV_LORA_RANK) * gkva_ref[...]).astype(BF16)
    kn = jnp.dot(ckvn, wuk_ref[...], preferred_element_type=F32)
    v_ref[...] = jnp.dot(ckvn, wuv_ref[...], preferred_element_type=F32).astype(v_ref.dtype)
    gq = gq_ref[...]
    gk = gk_ref[...]
    kpe = kpe_ref[...]
    ss_pe = jnp.sum(kpe * kpe, axis=-1, keepdims=True)
    kpe_rot = _rope(kpe * gk[:, LANES:], tab)
    scale = QK_HEAD_DIM ** -0.5
    for h in range(MLA_HEADS):
        q0 = qf[:, h * HEAD_PAD:h * HEAD_PAD + LANES]
        q1 = qf[:, h * HEAD_PAD + LANES:(h + 1) * HEAD_PAD]
        ss = jnp.sum(q0 * q0, axis=-1, keepdims=True) + jnp.sum(q1 * q1, axis=-1, keepdims=True)
        rq = lax.rsqrt(ss * (1.0 / QK_HEAD_DIM) + NORM_EPS) * scale
        q_ref[:, h * HEAD_PAD:h * HEAD_PAD + LANES] = (q0 * gq[:, :LANES] * rq).astype(q_ref.dtype)
        q_ref[:, h * HEAD_PAD + LANES:(h + 1) * HEAD_PAD] = (_rope(q1 * gq[:, LANES:], tab) * rq).astype(q_ref.dtype)
        k0 = kn[:, h * LANES:(h + 1) * LANES]
        ssk = jnp.sum(k0 * k0, axis=-1, keepdims=True) + ss_pe
        rk = lax.rsqrt(ssk * (1.0 / QK_HEAD_DIM) + NORM_EPS)
        k_ref[:, h * HEAD_PAD:h * HEAD_PAD + LANES] = (k0 * gk[:, :LANES] * rk).astype(k_ref.dtype)
        k_ref[:, h * HEAD_PAD + LANES:(h + 1) * HEAD_PAD] = (kpe_rot * rk).astype(k_ref.dtype)


def mla_prep(za, tab, wuq, wuk, wuv, gqa, gkva, gq, gk, tm=512):
    T = za.shape[0]
    HP = MLA_HEADS * HEAD_PAD
    full = lambda shape: pl.BlockSpec(shape, lambda i: (0, 0))
    return pl.pallas_call(
        _mla_prep_kernel,
        grid=(T // tm,),
        in_specs=[pl.BlockSpec((tm, Q_LORA_RANK), lambda i: (i, ZA_CQ // Q_LORA_RANK)),
                  pl.BlockSpec((tm, KV_LORA_RANK), lambda i: (i, ZA_CKV // KV_LORA_RANK)),
                  pl.BlockSpec((tm, LANES), lambda i: (i, ZA_KPE // LANES)),
                  pl.BlockSpec((tm, 3 * LANES), lambda i: (i, 0)),
                  full((Q_LORA_RANK, HP)), full((KV_LORA_RANK, MLA_WIDTH)), full((KV_LORA_RANK, MLA_WIDTH)),
                  full((1, Q_LORA_RANK)), full((1, KV_LORA_RANK)), full((1, HEAD_PAD)), full((1, HEAD_PAD))],
        out_specs=[pl.BlockSpec((tm, HP), lambda i: (i, 0)),
                   pl.BlockSpec((tm, HP), lambda i: (i, 0)),
                   pl.BlockSpec((tm, MLA_WIDTH), lambda i: (i, 0))],
        out_shape=[jax.ShapeDtypeStruct((T, HP), BF16),
                   jax.ShapeDtypeStruct((T, HP), BF16),
                   jax.ShapeDtypeStruct((T, MLA_WIDTH), BF16)],
        compiler_params=_cp("parallel"),
        name="mla_prep",
    )(za, za, za, tab, wuq, wuk, wuv, gqa, gkva, gq, gk)


def _causal_attn_kernel(q_ref, k_ref, v_ref, *rest, tq, n_cast, n_alias):
    cast_src = rest[:n_cast]
    o_ref = rest[n_cast + n_alias]
    cast_dst = rest[n_cast + n_alias + 1:]
    for s_ref, d_ref in zip(cast_src, cast_dst):
        d_ref[...] = s_ref[...].astype(d_ref.dtype)
    S = q_ref.shape[0]
    row = lax.broadcasted_iota(jnp.int32, (tq, tq), 0)
    col = lax.broadcasted_iota(jnp.int32, (tq, tq), 1)
    tri = col <= row
    for qi in range(S // tq):
        kl = (qi + 1) * tq
        q = q_ref[qi * tq:kl, :]
        s = lax.dot_general(q, k_ref[0:kl, :], (((1,), (1,)), ((), ())), preferred_element_type=F32)
        s_diag = jnp.where(tri, s[:, kl - tq:kl], -jnp.inf)
        if qi > 0:
            s = jnp.concatenate([s[:, :kl - tq], s_diag], axis=1)
        else:
            s = s_diag
        m = jnp.max(s, axis=-1, keepdims=True)
        p = jnp.exp(s - m)
        l = jnp.sum(p, axis=-1, keepdims=True)
        o = jnp.dot(p.astype(BF16), v_ref[0:kl, :], preferred_element_type=F32)
        o_ref[qi * tq:kl, :] = (o / l).astype(o_ref.dtype)


def causal_attention(q, k, v, B, S, casts=(), tq=256):
    T = q.shape[0]
    n_steps = B * MLA_HEADS
    in_specs = [pl.BlockSpec((S, HEAD_PAD), lambda b, h: (b, h)),
                pl.BlockSpec((S, HEAD_PAD), lambda b, h: (b, h)),
                pl.BlockSpec((S, V_HEAD_DIM), lambda b, h: (b, h))]
    out_specs = [pl.BlockSpec((S, V_HEAD_DIM), lambda b, h: (b, h))]
    out_shape = [jax.ShapeDtypeStruct((T, MLA_WIDTH), BF16)]
    args = [q, k, v]
    alias_args, aliases = [], {}
    for n, (src, dst, part, n_parts) in enumerate(casts):
        rows, cols = src.shape
        rb = rows // (n_steps * n_parts)
        assert rb * n_steps * n_parts == rows and rb % 16 == 0
        spec = pl.BlockSpec((rb, cols), lambda b, h, off=part * n_steps: (off + b * MLA_HEADS + h, 0))
        in_specs.append(spec)
        args.append(src)
        out_specs.append(spec)
        out_shape.append(jax.ShapeDtypeStruct((rows, cols), BF16))
        if dst is not None:
            aliases[3 + len(casts) + len(alias_args)] = 1 + n
            alias_args.append(dst)
    in_specs += [pl.BlockSpec(memory_space=pl.ANY)] * len(alias_args)
    return pl.pallas_call(
        functools.partial(_causal_attn_kernel, tq=tq, n_cast=len(casts), n_alias=len(alias_args)),
        grid=(B, MLA_HEADS),
        in_specs=in_specs,
        out_specs=out_specs,
        out_shape=out_shape,
        input_output_aliases=aliases,
        compiler_params=_cp("parallel", "parallel"),
        name="causal_attention",
    )(*args, *alias_args)


def _pool_kernel(u_ref, w_ref, sc_ref, o_ref, pad_ref, *, rc):
    S = u_ref.shape[0]
    C = POOL_GROUP_DIM
    t1 = lax.broadcasted_iota(jnp.int32, (rc, C), 0).astype(F32) + 1.0
    for g, win in enumerate(POOL_WINDOWS):
        pad_ref[0:POOL_HALO, :] = jnp.zeros((POOL_HALO, C), F32)
        pad_ref[POOL_HALO:POOL_HALO + S, :] = u_ref[:, g * C:(g + 1) * C]
        for r0 in range(0, S, rc):
            acc = pad_ref[POOL_HALO + r0:POOL_HALO + r0 + rc, :]
            tok = acc
            for kk in range(1, win):
                acc = acc + pad_ref[POOL_HALO + r0 - kk:POOL_HALO + r0 - kk + rc, :]
            cnt = jnp.minimum(t1 + float(r0), float(win))
            pooled = (acc / cnt - tok).astype(BF16)
            mixed = jnp.dot(pooled, w_ref[g], preferred_element_type=F32) * sc_ref[:, g * C:(g + 1) * C]
            o_ref[r0:r0 + rc, g * C:(g + 1) * C] = mixed.astype(o_ref.dtype)


def pool_mixer(za, pool_w, pool_scale, B, S, rc=512):
    T = za.shape[0]
    rc = min(rc, S)
    return pl.pallas_call(
        functools.partial(_pool_kernel, rc=rc),
        grid=(B,),
        in_specs=[pl.BlockSpec((S, POOL_WIDTH), lambda b: (b, ZA_POOL // POOL_WIDTH)),
                  pl.BlockSpec((POOL_GROUPS, POOL_GROUP_DIM, POOL_GROUP_DIM), lambda b: (0, 0, 0)),
                  pl.BlockSpec((1, POOL_WIDTH), lambda b: (0, 0))],
        out_specs=pl.BlockSpec((S, POOL_WIDTH), lambda b: (b, 0)),
        out_shape=jax.ShapeDtypeStruct((T, POOL_WIDTH), BF16),
        scratch_shapes=[pltpu.VMEM((POOL_HALO + S, POOL_GROUP_DIM), F32)],
        compiler_params=_cp("parallel"),
        name="pool_mixer",
    )(za, pool_w, pool_scale)


def _mem_kv_kernel(m_ref, g_ref, w_ref, gk_ref, k_ref, v_ref):
    x = m_ref[...]
    xn = (x * _rms(x, x.shape[-1]) * g_ref[...]).astype(BF16)
    kv = jnp.dot(xn, w_ref[...], preferred_element_type=F32)
    for h in range(MEM_HEADS):
        kh = kv[:, h * MEM_HEAD_DIM:(h + 1) * MEM_HEAD_DIM]
        k_ref[:, h * MEM_HEAD_DIM:(h + 1) * MEM_HEAD_DIM] = (
            kh * _rms(kh, MEM_HEAD_DIM) * gk_ref[...]).astype(k_ref.dtype)
    v_ref[...] = kv[:, MEM_WIDTH:].astype(v_ref.dtype)


def mem_kv(mem2d, g, w_kv, gk, B, M):
    D = mem2d.shape[1]
    return pl.pallas_call(
        _mem_kv_kernel,
        grid=(B,),
        in_specs=[pl.BlockSpec((M, D), lambda b: (b, 0)),
                  pl.BlockSpec((1, D), lambda b: (0, 0)),
                  pl.BlockSpec((D, 2 * MEM_WIDTH), lambda b: (0, 0)),
                  pl.BlockSpec((1, MEM_HEAD_DIM), lambda b: (0, 0))],
        out_specs=[pl.BlockSpec((M, MEM_WIDTH), lambda b: (b, 0)),
                   pl.BlockSpec((M, MEM_WIDTH), lambda b: (b, 0))],
        out_shape=[jax.ShapeDtypeStruct((B * M, MEM_WIDTH), BF16),
                   jax.ShapeDtypeStruct((B * M, MEM_WIDTH), BF16)],
        compiler_params=_cp("parallel"),
        name="mem_kv",
    )(mem2d, g, w_kv, gk)


def _mem_attn_kernel(q_ref, k_ref, v_ref, gq_ref, o_ref):
    scale = MEM_HEAD_DIM ** -0.5
    for h in range(MEM_HEADS):
        sl = slice(h * MEM_HEAD_DIM, (h + 1) * MEM_HEAD_DIM)
        qh = q_ref[:, sl]
        qn = (qh * (_rms(qh, MEM_HEAD_DIM) * scale) * gq_ref[...]).astype(BF16)
        s = lax.dot_general(qn, k_ref[:, sl], (((1,), (1,)), ((), ())), preferred_element_type=F32)
        m = jnp.max(s, axis=-1, keepdims=True)
        p = jnp.exp(s - m)
        l = jnp.sum(p, axis=-1, keepdims=True)
        o = jnp.dot(p.astype(BF16), v_ref[:, sl], preferred_element_type=F32)
        o_ref[:, sl] = (o / l).astype(o_ref.dtype)


def mem_attention(za, k_m, v_m, gq, B, S, M, tq=512):
    T = za.shape[0]
    tq = min(tq, S)
    nq = S // tq
    return pl.pallas_call(
        _mem_attn_kernel,
        grid=(B, nq),
        in_specs=[pl.BlockSpec((tq, MEM_WIDTH), lambda b, i: (b * nq + i, ZA_QMEM // MEM_WIDTH)),
                  pl.BlockSpec((M, MEM_WIDTH), lambda b, i: (b, 0)),
                  pl.BlockSpec((M, MEM_WIDTH), lambda b, i: (b, 0)),
                  pl.BlockSpec((1, MEM_HEAD_DIM), lambda b, i: (0, 0))],
        out_specs=pl.BlockSpec((tq, MEM_WIDTH), lambda b, i: (b * nq + i, 0)),
        out_shape=jax.ShapeDtypeStruct((T, MEM_WIDTH), BF16),
        compiler_params=_cp("parallel", "arbitrary"),
        name="mem_attention",
    )(za, k_m, v_m, gq)


def _merge_kernel(h_ref, a_ref, p_ref, m_ref, g0_ref, g1_ref, g2_ref, w0_ref, w1_ref, w2_ref, o_ref):
    h = h_ref[...]
    acc = None
    for x_ref, g_ref, w_ref in ((a_ref, g0_ref, w0_ref), (p_ref, g1_ref, w1_ref), (m_ref, g2_ref, w2_ref)):
        logit = jnp.dot(h, g_ref[0], preferred_element_type=F32)
        gate = 1.0 / (1.0 + jnp.exp(-logit))
        br = jnp.dot(x_ref[...], w_ref[0], preferred_element_type=F32)
        acc = gate * br if acc is None else acc + gate * br
    o_ref[...] = acc.astype(o_ref.dtype)


def merge_branches(h, a, p, m, w_all, w_a, w_p, w_m, l, tm=512, tn=512):
    T, D = h.shape
    W = a.shape[1]
    nj = D // tn
    g0 = ZA_WIDTH // tn
    row = lambda width: pl.BlockSpec((tm, width), lambda i, j: (i, 0))
    gate_spec = lambda b: pl.BlockSpec((1, D, tn), lambda i, j, b=b: (l, 0, g0 + b * nj + j))
    out_w = pl.BlockSpec((1, W, tn), lambda i, j: (l, 0, j))
    return pl.pallas_call(
        _merge_kernel,
        grid=(T // tm, nj),
        in_specs=[row(D), row(W), row(W), row(W),
                  gate_spec(0), gate_spec(1), gate_spec(2), out_w, out_w, out_w],
        out_specs=pl.BlockSpec((tm, tn), lambda i, j: (i, j)),
        out_shape=jax.ShapeDtypeStruct((T, D), BF16),
        compiler_params=_cp("parallel", "arbitrary"),
        name="merge_branches",
    )(h, a, p, m, w_all, w_all, w_all, w_a, w_p, w_m)


def _w_in_prep_kernel(w_ref, o_ref):
    x = w_ref[0]
    s1 = Q_LORA_RANK + KV_LORA_RANK + QK_ROPE_DIM
    s2 = s1 + POOL_WIDTH
    s3 = s2 + MEM_WIDTH
    rows = x.shape[0]
    o_ref[0, :, ZA_POOL:ZA_POOL + POOL_WIDTH] = x[:, s1:s2].astype(BF16)
    o_ref[0, :, ZA_QMEM:ZA_QMEM + MEM_WIDTH] = x[:, s2:s3].astype(BF16)
    o_ref[0, :, ZA_CQ:ZA_CQ + s1] = x[:, 0:s1].astype(BF16)
    o_ref[0, :, ZA_CQ + s1:ZA_WIDTH] = jnp.zeros((rows, ZA_WIDTH - ZA_CQ - s1), BF16)
    o_ref[0, :, ZA_WIDTH:] = x[:, s3:].astype(BF16)


def prep_w_in(w_in, tr=256):
    L, D, win = w_in.shape
    wout = ZA_WIDTH + N_BRANCHES * D
    return pl.pallas_call(
        _w_in_prep_kernel,
        grid=(L, D // tr),
        in_specs=[pl.BlockSpec((1, tr, win), lambda l, i: (l, i, 0))],
        out_specs=pl.BlockSpec((1, tr, wout), lambda l, i: (l, i, 0)),
        out_shape=jax.ShapeDtypeStruct((L, D, wout), BF16),
        compiler_params=_cp("parallel", "parallel"),
        name="prep_w_in",
    )(w_in)


def _swiglu_accumulate(xb, wg_ref, wu_ref, wd_ref, o_ref):
    a = jnp.dot(xb, wg_ref[0], preferred_element_type=F32)
    b = jnp.dot(xb, wu_ref[0], preferred_element_type=F32)
    act = (a / (1.0 + jnp.exp(-a)) * b).astype(BF16)
    o_ref[...] += jnp.dot(act, wd_ref[0], preferred_element_type=F32)


def _dense_swiglu_kernel(x_ref, wg_ref, wu_ref, wd_ref, r_ref, o_ref):
    @pl.when(pl.program_id(1) == 0)
    def _():
        o_ref[...] = r_ref[...]

    _swiglu_accumulate(x_ref[...], wg_ref, wu_ref, wd_ref, o_ref)


def dense_swiglu(x, w_gate, w_up, w_down, residual, tm=1024, tf=512):
    T, D = x.shape
    F = w_gate.shape[2]
    tm = min(tm, T)
    return pl.pallas_call(
        _dense_swiglu_kernel,
        grid=(T // tm, F // tf),
        in_specs=[pl.BlockSpec((tm, D), lambda i, f: (i, 0)),
                  pl.BlockSpec((1, D, tf), lambda i, f: (0, 0, f)),
                  pl.BlockSpec((1, D, tf), lambda i, f: (0, 0, f)),
                  pl.BlockSpec((1, tf, D), lambda i, f: (0, f, 0)),
                  pl.BlockSpec((tm, D), lambda i, f: (i, 0), pipeline_mode=pl.Buffered(1))],
        out_specs=pl.BlockSpec((tm, D), lambda i, f: (i, 0)),
        out_shape=jax.ShapeDtypeStruct((T, D), F32),
        compiler_params=_cp("parallel", "arbitrary"),
        name="dense_swiglu",
    )(x, w_gate, w_up, w_down, residual)


def _row_copy(src_hbm, dst_ref, src_row, dst_row, sem):
    return pltpu.make_async_copy(src_hbm.at[pl.ds(src_row, 1)], dst_ref.at[pl.ds(dst_row, 1)], sem)


def _moe_swiglu_kernel(tok_ref, be_ref, nu_ref, h_hbm, wg_ref, wu_ref, wd_ref, o_ref,
                       xg_ref, xb_ref, sem, *, tm, nf):
    del be_ref
    i = pl.program_id(0)
    f = pl.program_id(1)
    n_used = nu_ref[0]
    rows_per_step = -(-tm // nf)
    n_fetch = rows_per_step * nf

    def fetch(block, r, slot):
        tok = tok_ref[block * tm + jnp.minimum(r, tm - 1)]
        return _row_copy(h_hbm, xg_ref.at[slot], tok, r, sem.at[slot])

    def wait_rows(slot):
        def wait(r, c):
            _row_copy(h_hbm, xg_ref.at[slot], 0, r, sem.at[slot]).wait()
            return c
        lax.fori_loop(0, n_fetch, wait, 0)

    @pl.when(i < n_used)
    def _():
        slot = i % 2

        @pl.when((i == 0) & (f == 0))
        def _():
            def start(r, c):
                fetch(0, r, 0).start()
                return c
            lax.fori_loop(0, n_fetch, start, 0)

        @pl.when(f == 0)
        def _():
            wait_rows(slot)
            xb_ref[...] = xg_ref[slot, 0:tm, :].astype(BF16)
            o_ref[...] = jnp.zeros_like(o_ref)

        nxt = jnp.minimum(i + 1, n_used - 1)
        for j in range(rows_per_step):
            fetch(nxt, f * rows_per_step + j, 1 - slot).start()

        _swiglu_accumulate(xb_ref[...], wg_ref, wu_ref, wd_ref, o_ref)

        @pl.when((i == n_used - 1) & (f == nf - 1))
        def _():
            wait_rows(1 - slot)


def moe_swiglu(h, slot_tok, block_e, n_used, w_gate, w_up, w_down, tm, tf=1024):
    n_slots = slot_tok.shape[0]
    D = h.shape[1]
    F = w_gate.shape[2]
    while F % tf:
        tf //= 2
    nf = F // tf
    xg_rows = -(-(-(-tm // nf) * nf) // 8) * 8
    blk = lambda i, nu: jnp.minimum(i, nu[0] - 1)
    chunk = lambda i, f, nu: jnp.where(i < nu[0], f, nf - 1)
    return pl.pallas_call(
        functools.partial(_moe_swiglu_kernel, tm=tm, nf=nf),
        grid_spec=pltpu.PrefetchScalarGridSpec(
            num_scalar_prefetch=3,
            grid=(n_slots // tm, nf),
            in_specs=[pl.BlockSpec(memory_space=pl.ANY),
                      pl.BlockSpec((1, D, tf), lambda i, f, tok, be, nu: (be[blk(i, nu)], 0, chunk(i, f, nu))),
                      pl.BlockSpec((1, D, tf), lambda i, f, tok, be, nu: (be[blk(i, nu)], 0, chunk(i, f, nu))),
                      pl.BlockSpec((1, tf, D), lambda i, f, tok, be, nu: (be[blk(i, nu)], chunk(i, f, nu), 0))],
            out_specs=pl.BlockSpec((tm, D), lambda i, f, tok, be, nu: (blk(i, nu), 0)),
            scratch_shapes=[pltpu.VMEM((2, xg_rows, D), F32),
                            pltpu.VMEM((tm, D), BF16),
                            pltpu.SemaphoreType.DMA((2,))]),
        out_shape=jax.ShapeDtypeStruct((n_slots, D), F32),
        compiler_params=_cp("arbitrary", "arbitrary"),
        name="moe_swiglu",
    )(slot_tok, block_e, n_used, h, w_gate, w_up, w_down)


def _combine_kernel(dest_ref, y_ref, gate_ref, ys_hbm, o_ref, buf_ref, sem, *, tb):
    base = pl.program_id(0) * tb

    def start(r, c):
        for k in range(TOP_K):
            _row_copy(ys_hbm, buf_ref.at[k], dest_ref[TOP_K * (base + r) + k], r, sem).start()
        return c

    lax.fori_loop(0, tb, start, 0, unroll=8)

    def wait(r, c):
        for k in range(TOP_K):
            _row_copy(ys_hbm, buf_ref.at[k], 0, r, sem).wait()
        return c

    lax.fori_loop(0, tb, wait, 0, unroll=8)
    g = gate_ref[...]
    o_ref[...] = y_ref[...] + g[:, 0:1] * buf_ref[0] + g[:, 1:2] * buf_ref[1]


def combine_rows(y, gates, ys, dest, tb=256):
    T, D = y.shape
    return pl.pallas_call(
        functools.partial(_combine_kernel, tb=tb),
        grid_spec=pltpu.PrefetchScalarGridSpec(
            num_scalar_prefetch=1,
            grid=(T // tb,),
            in_specs=[pl.BlockSpec((tb, D), lambda i, d: (i, 0)),
                      pl.BlockSpec((tb, LANES), lambda i, d: (i, 0)),
                      pl.BlockSpec(memory_space=pl.ANY)],
            out_specs=pl.BlockSpec((tb, D), lambda i, d: (i, 0)),
            scratch_shapes=[pltpu.VMEM((TOP_K, tb, D), F32),
                            pltpu.SemaphoreType.DMA(())]),
        out_shape=jax.ShapeDtypeStruct((T, D), F32),
        compiler_params=_cp("arbitrary"),
        name="moe_combine",
    )(dest, y, gates, ys)


def _routing_tables(top_idx, T, block):
    A = T * TOP_K
    E = N_EXPERTS
    flat_e = top_idx.reshape(A)
    onehot = (flat_e[:, None] == jnp.arange(E, dtype=jnp.int32)[None, :]).astype(jnp.int32)
    csum = jnp.cumsum(onehot, axis=0)
    counts = csum[-1]
    rank = jnp.sum((csum - 1) * onehot, axis=1)
    padded = (counts + block - 1) // block * block
    pends = jnp.cumsum(padded)
    pstarts = pends - padded
    starts = jnp.cumsum(counts) - counts
    dest = (pstarts[flat_e] + rank).astype(jnp.int32)
    n_blocks = A // block + E
    n_slots = n_blocks * block
    order = jnp.argsort(flat_e, stable=True).astype(jnp.int32)
    slot = jnp.arange(n_slots, dtype=jnp.int32)
    slot_e = jnp.clip(jnp.searchsorted(pends, slot, side='right'), 0, E - 1).astype(jnp.int32)
    r = slot - pstarts[slot_e]
    valid = r < counts[slot_e]
    src = jnp.clip(starts[slot_e] + r, 0, A - 1)
    slot_tok = jnp.where(valid, order[src] // TOP_K, 0).astype(jnp.int32)
    block_e = slot_e[::block]
    n_used = (pends[-1:] // block).astype(jnp.int32)
    return slot_tok, dest, block_e, n_used


def _token_mixer(y, h, mem2d, tab, B, S, M, l, casts, w_in, mla_q_a_norm_g, mla_w_uq, mla_kv_a_norm_g, mla_w_ukv,
                 mla_q_norm_g, mla_k_norm_g, mla_w_out, pool_w, pool_scale, pool_w_out,
                 mem_norm_g, mem_w_kv, mem_q_norm_g, mem_k_norm_g, mem_w_out, w_o):
    D = h.shape[1]
    za = matmul(h, w_in, l, ZA_WIDTH, F32)

    wuq = jnp.pad(mla_w_uq[l].reshape(Q_LORA_RANK, MLA_HEADS, QK_HEAD_DIM),
                  ((0, 0), (0, 0), (0, HEAD_PAD - QK_HEAD_DIM))).reshape(Q_LORA_RANK, MLA_HEADS * HEAD_PAD)
    wukv = mla_w_ukv[l].reshape(KV_LORA_RANK, MLA_HEADS, QK_NOPE_DIM + V_HEAD_DIM)
    wuk = wukv[:, :, :QK_NOPE_DIM].reshape(KV_LORA_RANK, MLA_WIDTH)
    wuv = wukv[:, :, QK_NOPE_DIM:].reshape(KV_LORA_RANK, MLA_WIDTH)
    pad_g = lambda g: jnp.pad(g, (0, HEAD_PAD - QK_HEAD_DIM)).reshape(1, HEAD_PAD)
    q, k, v = mla_prep(za, tab, wuq.astype(BF16), wuk.astype(BF16), wuv.astype(BF16),
                       mla_q_a_norm_g[l].reshape(1, -1), mla_kv_a_norm_g[l].reshape(1, -1),
                       pad_g(mla_q_norm_g[l]), pad_g(mla_k_norm_g[l]))
    attn, *cast_out = causal_attention(q, k, v, B, S, casts)

    mixed = pool_mixer(za, pool_w[l].astype(BF16), pool_scale[l].reshape(1, -1), B, S)

    k_m, v_m = mem_kv(mem2d, mem_norm_g[l].reshape(1, -1), mem_w_kv[l].astype(BF16),
                      mem_k_norm_g[l].reshape(1, -1), B, M)
    o_mem = mem_attention(za, k_m, v_m, mem_q_norm_g[l].reshape(1, -1), B, S, M)

    merged = merge_branches(h, attn, mixed, o_mem, w_in, mla_w_out, pool_w_out, mem_w_out, l)
    return matmul(merged, w_o, l, D, F32, residual=y), cast_out


def kernel(x, mem, positions, attn_norm_g, w_in, mla_q_a_norm_g, mla_w_uq, mla_kv_a_norm_g, mla_w_ukv, mla_q_norm_g, mla_k_norm_g, mla_w_out, pool_w, pool_scale, pool_w_out, mem_norm_g, mem_w_kv, mem_q_norm_g, mem_k_norm_g, mem_w_out, w_o, ffn_norm_g, dense_w_gate, dense_w_up, dense_w_down, router_w, router_b, moe_w_gate, moe_w_up, moe_w_down):
    B, S, D = x.shape
    M = mem.shape[1]
    T = B * S
    depth = attn_norm_g.shape[0]
    moe_block = 512
    y = x.reshape(T, D)
    mem2d = mem.reshape(B * M, D)
    tab = rope_tables(positions)
    w_in = prep_w_in(w_in)
    mla_w_out, pool_w_out, mem_w_out, w_o = (w.astype(BF16) for w in (mla_w_out, pool_w_out, mem_w_out, w_o))
    E, _, F_moe = moe_w_gate.shape[1:]
    moe_bf16 = {}
    for l in range(depth):
        h = rmsnorm(y, attn_norm_g[l], BF16)
        i = l // 2
        casts = ()
        if l % 2 == 0 and l + 1 < depth:
            casts = ((moe_w_gate[i].reshape(E * D, F_moe), None, 0, 1),
                     (moe_w_down[i].reshape(E * F_moe, D), None, 0, 2))
        elif l % 2 == 1:
            casts = ((moe_w_up[i].reshape(E * D, F_moe), None, 0, 1),
                     (moe_w_down[i].reshape(E * F_moe, D), moe_bf16.pop("down"), 1, 2))
        y, cast_out = _token_mixer(y, h, mem2d, tab, B, S, M, l, casts, w_in, mla_q_a_norm_g, mla_w_uq,
                                   mla_kv_a_norm_g, mla_w_ukv, mla_q_norm_g, mla_k_norm_g, mla_w_out, pool_w,
                                   pool_scale, pool_w_out, mem_norm_g, mem_w_kv, mem_q_norm_g, mem_k_norm_g,
                                   mem_w_out, w_o)
        if casts:
            moe_bf16["gate" if l % 2 == 0 else "up"], moe_bf16["down"] = cast_out
        if l % 2 == 0:
            h = rmsnorm(y, ffn_norm_g[l], BF16)
            y = dense_swiglu(h, dense_w_gate[i:i + 1].astype(BF16), dense_w_up[i:i + 1].astype(BF16),
                             dense_w_down[i:i + 1].astype(BF16), y)
        else:
            hf, idx, gates = norm_router(y, ffn_norm_g[l], router_w[i], router_b[i])
            slot_tok, dest, block_e, n_used = _routing_tables(idx[:, :TOP_K], T, moe_block)
            ys = moe_swiglu(hf, slot_tok, block_e, n_used, moe_bf16.pop("gate").reshape(E, D, F_moe),
                            moe_bf16.pop("up").reshape(E, D, F_moe),
                            moe_bf16.pop("down").reshape(E, F_moe, D), tm=moe_block)
            y = combine_rows(y, gates, ys, dest)
    return y.reshape(B, S, D)
```

```python
import functools

import numpy as np
import jax
import jax.numpy as jnp
from jax import lax
from jax.experimental import pallas as pl
from jax.experimental.pallas import tpu as pltpu

F32 = jnp.float32
BF16 = jnp.bfloat16

MEM_HEADS = 4
MEM_HEAD_DIM = 256
MEM_WIDTH = MEM_HEADS * MEM_HEAD_DIM
MLA_HEADS = 8
Q_LORA_RANK = 512
KV_LORA_RANK = 256
QK_NOPE_DIM = 128
QK_ROPE_DIM = 64
QK_HEAD_DIM = QK_NOPE_DIM + QK_ROPE_DIM
V_HEAD_DIM = 128
MLA_WIDTH = MLA_HEADS * V_HEAD_DIM
ROPE_THETA = 10000.0
POOL_WINDOWS = (2, 4, 8, 16)
POOL_GROUPS = 4
POOL_GROUP_DIM = 256
POOL_WIDTH = POOL_GROUPS * POOL_GROUP_DIM
N_BRANCHES = 3
N_EXPERTS = 8
TOP_K = 2
NORM_EPS = 1e-6

LANES = 128
SUBLANES = 8
HEAD_PAD = 2 * LANES
ROPE_HALF = QK_ROPE_DIM // 2
POOL_HALO = 16

ZA_POOL = 0
ZA_QMEM = POOL_WIDTH
ZA_CQ = ZA_QMEM + MEM_WIDTH
ZA_CKV = ZA_CQ + Q_LORA_RANK
ZA_KPE = ZA_CKV + KV_LORA_RANK
ZA_WIDTH = 3072

VMEM_LIMIT = 52 * 1024 * 1024


def _cp(*sem):
    return pltpu.CompilerParams(dimension_semantics=sem, vmem_limit_bytes=VMEM_LIMIT)


def _rms(x, eps_dim):
    return lax.rsqrt(jnp.sum(x * x, axis=-1, keepdims=True) * (1.0 / eps_dim) + NORM_EPS)


def _rmsnorm_kernel(x_ref, g_ref, o_ref):
    x = x_ref[...]
    o_ref[...] = (x * _rms(x, x.shape[-1]) * g_ref[...]).astype(o_ref.dtype)


def rmsnorm(x, g, out_dtype, tm=512):
    T, D = x.shape
    return pl.pallas_call(
        _rmsnorm_kernel,
        grid=(T // tm,),
        in_specs=[pl.BlockSpec((tm, D), lambda i: (i, 0)),
                  pl.BlockSpec((1, D), lambda i: (0, 0))],
        out_specs=pl.BlockSpec((tm, D), lambda i: (i, 0)),
        out_shape=jax.ShapeDtypeStruct((T, D), out_dtype),
        compiler_params=_cp("parallel"),
        name="rmsnorm",
    )(x, g.reshape(1, D))


def _norm_router_kernel(x_ref, g_ref, rwh_ref, rwl_ref, rb_ref, h_ref, idx_ref, gate_ref):
    x = x_ref[...]
    h = x * _rms(x, x.shape[-1]) * g_ref[...]
    h_ref[...] = h
    h_hi = h.astype(BF16)
    h_lo = (h - h_hi.astype(F32)).astype(BF16)
    logits = (jnp.dot(h_hi, rwh_ref[...], preferred_element_type=F32)
              + jnp.dot(h_lo, rwh_ref[...], preferred_element_type=F32)
              + jnp.dot(h_hi, rwl_ref[...], preferred_element_type=F32)) + rb_ref[...]
    lane = lax.broadcasted_iota(jnp.int32, logits.shape, 1)
    neg = -jnp.inf
    l1 = jnp.where(lane < N_EXPERTS, logits, neg)
    m1 = jnp.max(l1, axis=-1, keepdims=True)
    i1 = jnp.min(jnp.where(l1 == m1, lane, LANES), axis=-1, keepdims=True)
    l2 = jnp.where(lane == i1, neg, l1)
    m2 = jnp.max(l2, axis=-1, keepdims=True)
    i2 = jnp.min(jnp.where(l2 == m2, lane, LANES), axis=-1, keepdims=True)
    e = jnp.exp(m2 - m1)
    g1 = 1.0 / (1.0 + e)
    g2 = e / (1.0 + e)
    idx_ref[...] = jnp.where(lane == 0, i1, jnp.where(lane == 1, i2, 0))
    gate_ref[...] = jnp.where(lane == 0, g1, jnp.where(lane == 1, g2, 0.0))


def norm_router(x, g, router_w, router_b, tm=512):
    T, D = x.shape
    E = router_w.shape[1]
    rw = jnp.zeros((D, LANES), F32).at[:, :E].set(router_w)
    rw_hi = rw.astype(BF16)
    rw_lo = (rw - rw_hi.astype(F32)).astype(BF16)
    rb = jnp.zeros((1, LANES), F32).at[0, :E].set(router_b)
    return pl.pallas_call(
        _norm_router_kernel,
        grid=(T // tm,),
        in_specs=[pl.BlockSpec((tm, D), lambda i: (i, 0)),
                  pl.BlockSpec((1, D), lambda i: (0, 0)),
                  pl.BlockSpec((D, LANES), lambda i: (0, 0)),
                  pl.BlockSpec((D, LANES), lambda i: (0, 0)),
                  pl.BlockSpec((1, LANES), lambda i: (0, 0))],
        out_specs=[pl.BlockSpec((tm, D), lambda i: (i, 0)),
                   pl.BlockSpec((tm, LANES), lambda i: (i, 0)),
                   pl.BlockSpec((tm, LANES), lambda i: (i, 0))],
        out_shape=[jax.ShapeDtypeStruct((T, D), F32),
                   jax.ShapeDtypeStruct((T, LANES), jnp.int32),
                   jax.ShapeDtypeStruct((T, LANES), F32)],
        compiler_params=_cp("parallel"),
        name="norm_router",
    )(x, g.reshape(1, D), rw_hi, rw_lo, rb)


def _mm_kernel(a_ref, w_ref, o_ref):
    o_ref[...] = jnp.dot(a_ref[...], w_ref[0], preferred_element_type=F32).astype(o_ref.dtype)


def _mm_res_kernel(a_ref, w_ref, r_ref, o_ref):
    o_ref[...] = (r_ref[...] + jnp.dot(a_ref[...], w_ref[0], preferred_element_type=F32)).astype(o_ref.dtype)


def matmul(a, w, l, N, out_dtype, residual=None, tm=1024, tn=1024):
    M, K = a.shape
    tm, tn = min(tm, M), min(tn, N)
    in_specs = [pl.BlockSpec((tm, K), lambda i, j: (i, 0)),
                pl.BlockSpec((1, K, tn), lambda i, j: (l, 0, j))]
    args = [a, w]
    body = _mm_kernel
    if residual is not None:
        in_specs.append(pl.BlockSpec((tm, tn), lambda i, j: (i, j)))
        args.append(residual)
        body = _mm_res_kernel
    return pl.pallas_call(
        body,
        grid=(M // tm, N // tn),
        in_specs=in_specs,
        out_specs=pl.BlockSpec((tm, tn), lambda i, j: (i, j)),
        out_shape=jax.ShapeDtypeStruct((M, N), out_dtype),
        compiler_params=_cp("parallel", "arbitrary"),
        name="matmul",
    )(*args)


def _rope_table_kernel(pos_ref, invf_ref, o_ref):
    ang = pos_ref[...].astype(F32) * invf_ref[...]
    lane = lax.broadcasted_iota(jnp.int32, ang.shape, 1)
    c = jnp.cos(ang)
    s = jnp.sin(ang)
    o_ref[:, 0:LANES] = jnp.where(lane < QK_ROPE_DIM, c, 0.0)
    o_ref[:, LANES:2 * LANES] = jnp.where(lane < ROPE_HALF, -s, 0.0)
    o_ref[:, 2 * LANES:3 * LANES] = jnp.where((lane >= ROPE_HALF) & (lane < QK_ROPE_DIM), s, 0.0)


def rope_tables(positions, tm=512):
    T = positions.size
    inv_freq = ROPE_THETA ** (-np.arange(0, QK_ROPE_DIM, 2, dtype=np.float32) / QK_ROPE_DIM)
    invf = np.zeros((1, LANES), np.float32)
    invf[0, :ROPE_HALF] = inv_freq
    invf[0, ROPE_HALF:QK_ROPE_DIM] = inv_freq
    return pl.pallas_call(
        _rope_table_kernel,
        grid=(T // tm,),
        in_specs=[pl.BlockSpec((tm, 1), lambda i: (i, 0)),
                  pl.BlockSpec((1, LANES), lambda i: (0, 0))],
        out_specs=pl.BlockSpec((tm, 3 * LANES), lambda i: (i, 0)),
        out_shape=jax.ShapeDtypeStruct((T, 3 * LANES), F32),
        compiler_params=_cp("parallel"),
        name="rope_tables",
    )(positions.reshape(T, 1), jnp.asarray(invf))


def _rope(x, tab):
    c = tab[:, 0:LANES]
    s1 = tab[:, LANES:2 * LANES]
    s2 = tab[:, 2 * LANES:3 * LANES]
    return x * c + pltpu.roll(x, LANES - ROPE_HALF, 1) * s1 + pltpu.roll(x, ROPE_HALF, 1) * s2


def _mla_prep_kernel(cq_ref, ckv_ref, kpe_ref, tab_ref, wuq_ref, wuk_ref, wuv_ref,
                     gqa_ref, gkva_ref, gq_ref, gk_ref, q_ref, k_ref, v_ref):
    tab = tab_ref[...]
    cq = cq_ref[...]
    cqn = (cq * _rms(cq, Q_LORA_RANK) * gqa_ref[...]).astype(BF16)
    qf = jnp.dot(cqn, wuq_ref[...], preferred_element_type=F32)
    ckv = ckv_ref[...]
    ckvn = (ckv * _rms(ckv, KV_LORA_RANK) * gkva_ref[...]).astype(BF16)
    kn = jnp.dot(ckvn, wuk_ref[...], preferred_element_type=F32)
    v_ref[...] = jnp.dot(ckvn, wuv_ref[...], preferred_element_type=F32).astype(v_ref.dtype)
    gq = gq_ref[...]
    gk = gk_ref[...]
    kpe = kpe_ref[...]
    ss_pe = jnp.sum(kpe * kpe, axis=-1, keepdims=True)
    kpe_rot = _rope(kpe * gk[:, LANES:], tab)
    scale = QK_HEAD_DIM ** -0.5
    for h in range(MLA_HEADS):
        q0 = qf[:, h * HEAD_PAD:h * HEAD_PAD + LANES]
        q1 = qf[:, h * HEAD_PAD + LANES:(h + 1) * HEAD_PAD]
        ss = jnp.sum(q0 * q0, axis=-1, keepdims=True) + jnp.sum(q1 * q1, axis=-1, keepdims=True)
        rq = lax.rsqrt(ss * (1.0 / QK_HEAD_DIM) + NORM_EPS) * scale
        q_ref[:, h * HEAD_PAD:h * HEAD_PAD + LANES] = (q0 * gq[:, :LANES] * rq).astype(q_ref.dtype)
        q_ref[:, h * HEAD_PAD + LANES:(h + 1) * HEAD_PAD] = (_rope(q1 * gq[:, LANES:], tab) * rq).astype(q_ref.dtype)
        k0 = kn[:, h * LANES:(h + 1) * LANES]
        ssk = jnp.sum(k0 * k0, axis=-1, keepdims=True) + ss_pe
        rk = lax.rsqrt(ssk * (1.0 / QK_HEAD_DIM) + NORM_EPS)
        k_ref[:, h * HEAD_PAD:h * HEAD_PAD + LANES] = (k0 * gk[:, :LANES] * rk).astype(k_ref.dtype)
        k_ref[:, h * HEAD_PAD + LANES:(h + 1) * HEAD_PAD] = (kpe_rot * rk).astype(k_ref.dtype)


def mla_prep(za, tab, wuq, wuk, wuv, gqa, gkva, gq, gk, tm=512):
    T = za.shape[0]
    HP = MLA_HEADS * HEAD_PAD
    full = lambda shape: pl.BlockSpec(shape, lambda i: (0, 0))
    return pl.pallas_call(
        _mla_prep_kernel,
        grid=(T // tm,),
        in_specs=[pl.BlockSpec((tm, Q_LORA_RANK), lambda i: (i, ZA_CQ // Q_LORA_RANK)),
                  pl.BlockSpec((tm, KV_LORA_RANK), lambda i: (i, ZA_CKV // KV_LORA_RANK)),
                  pl.BlockSpec((tm, LANES), lambda i: (i, ZA_KPE // LANES)),
                  pl.BlockSpec((tm, 3 * LANES), lambda i: (i, 0)),
                  full((Q_LORA_RANK, HP)), full((KV_LORA_RANK, MLA_WIDTH)), full((KV_LORA_RANK, MLA_WIDTH)),
                  full((1, Q_LORA_RANK)), full((1, KV_LORA_RANK)), full((1, HEAD_PAD)), full((1, HEAD_PAD))],
        out_specs=[pl.BlockSpec((tm, HP), lambda i: (i, 0)),
                   pl.BlockSpec((tm, HP), lambda i: (i, 0)),
                   pl.BlockSpec((tm, MLA_WIDTH), lambda i: (i, 0))],
        out_shape=[jax.ShapeDtypeStruct((T, HP), BF16),
                   jax.ShapeDtypeStruct((T, HP), BF16),
                   jax.ShapeDtypeStruct((T, MLA_WIDTH), BF16)],
        compiler_params=_cp("parallel"),
        name="mla_prep",
    )(za, za, za, tab, wuq, wuk, wuv, gqa, gkva, gq, gk)


def _causal_attn_kernel(q_ref, k_ref, v_ref, *rest, tq, n_cast):
    cast_src = rest[:n_cast]
    o_ref = rest[n_cast]
    cast_dst = rest[n_cast + 1:]
    for s_ref, d_ref in zip(cast_src, cast_dst):
        d_ref[...] = s_ref[...].astype(d_ref.dtype)
    S = q_ref.shape[0]
    row = lax.broadcasted_iota(jnp.int32, (tq, tq), 0)
    col = lax.broadcasted_iota(jnp.int32, (tq, tq), 1)
    tri = col <= row
    for qi in range(S // tq):
        kl = (qi + 1) * tq
        q = q_ref[qi * tq:kl, :]
        s = lax.dot_general(q, k_ref[0:kl, :], (((1,), (1,)), ((), ())), preferred_element_type=F32)
        s_diag = jnp.where(tri, s[:, kl - tq:kl], -jnp.inf)
        if qi > 0:
            s = jnp.concatenate([s[:, :kl - tq], s_diag], axis=1)
        else:
            s = s_diag
        m = jnp.max(s, axis=-1, keepdims=True)
        p = jnp.exp(s - m)
        l = jnp.sum(p, axis=-1, keepdims=True)
        o = jnp.dot(p.astype(BF16), v_ref[0:kl, :], preferred_element_type=F32)
        o_ref[qi * tq:kl, :] = (o / l).astype(o_ref.dtype)


def causal_attention(q, k, v, B, S, casts=(), tq=256):
    T = q.shape[0]
    n_steps = B * MLA_HEADS
    in_specs = [pl.BlockSpec((S, HEAD_PAD), lambda b, h: (b, h)),
                pl.BlockSpec((S, HEAD_PAD), lambda b, h: (b, h)),
                pl.BlockSpec((S, V_HEAD_DIM), lambda b, h: (b, h))]
    out_specs = [pl.BlockSpec((S, V_HEAD_DIM), lambda b, h: (b, h))]
    out_shape = [jax.ShapeDtypeStruct((T, MLA_WIDTH), BF16)]
    for src, part, n_parts in casts:
        rows, cols = src.shape
        rb, cb = rows // n_steps, cols // n_parts
        assert rb * n_steps == rows and rb % 16 == 0 and cb * n_parts == cols and cb % LANES == 0
        in_specs.append(pl.BlockSpec((rb, cb), lambda b, h, part=part: (b * MLA_HEADS + h, part)))
        out_specs.append(pl.BlockSpec((rb, cb), lambda b, h: (b * MLA_HEADS + h, 0)))
        out_shape.append(jax.ShapeDtypeStruct((rows, cb), BF16))
    return pl.pallas_call(
        functools.partial(_causal_attn_kernel, tq=tq, n_cast=len(casts)),
        grid=(B, MLA_HEADS),
        in_specs=in_specs,
        out_specs=out_specs,
        out_shape=out_shape,
        compiler_params=_cp("parallel", "parallel"),
        name="causal_attention",
    )(q, k, v, *(c[0] for c in casts))


def _pool_kernel(u_ref, w_ref, sc_ref, o_ref, pad_ref, *, rc):
    S = u_ref.shape[0]
    C = POOL_GROUP_DIM
    t1 = lax.broadcasted_iota(jnp.int32, (rc, C), 0).astype(F32) + 1.0
    for g, win in enumerate(POOL_WINDOWS):
        pad_ref[0:POOL_HALO, :] = jnp.zeros((POOL_HALO, C), F32)
        pad_ref[POOL_HALO:POOL_HALO + S, :] = u_ref[:, g * C:(g + 1) * C]
        for r0 in range(0, S, rc):
            acc = pad_ref[POOL_HALO + r0:POOL_HALO + r0 + rc, :]
            tok = acc
            for kk in range(1, win):
                acc = acc + pad_ref[POOL_HALO + r0 - kk:POOL_HALO + r0 - kk + rc, :]
            cnt = jnp.minimum(t1 + float(r0), float(win))
            pooled = (acc / cnt - tok).astype(BF16)
            mixed = jnp.dot(pooled, w_ref[g], preferred_element_type=F32) * sc_ref[:, g * C:(g + 1) * C]
            o_ref[r0:r0 + rc, g * C:(g + 1) * C] = mixed.astype(o_ref.dtype)


def pool_mixer(za, pool_w, pool_scale, B, S, rc=512):
    T = za.shape[0]
    rc = min(rc, S)
    return pl.pallas_call(
        functools.partial(_pool_kernel, rc=rc),
        grid=(B,),
        in_specs=[pl.BlockSpec((S, POOL_WIDTH), lambda b: (b, ZA_POOL // POOL_WIDTH)),
                  pl.BlockSpec((POOL_GROUPS, POOL_GROUP_DIM, POOL_GROUP_DIM), lambda b: (0, 0, 0)),
                  pl.BlockSpec((1, POOL_WIDTH), lambda b: (0, 0))],
        out_specs=pl.BlockSpec((S, POOL_WIDTH), lambda b: (b, 0)),
        out_shape=jax.ShapeDtypeStruct((T, POOL_WIDTH), BF16),
        scratch_shapes=[pltpu.VMEM((POOL_HALO + S, POOL_GROUP_DIM), F32)],
        compiler_params=_cp("parallel"),
        name="pool_mixer",
    )(za, pool_w, pool_scale)


def _mem_kv_kernel(m_ref, g_ref, w_ref, gk_ref, k_ref, v_ref):
    x = m_ref[...]
    xn = (x * _rms(x, x.shape[-1]) * g_ref[...]).astype(BF16)
    kv = jnp.dot(xn, w_ref[...], preferred_element_type=F32)
    for h in range(MEM_HEADS):
        kh = kv[:, h * MEM_HEAD_DIM:(h + 1) * MEM_HEAD_DIM]
        k_ref[:, h * MEM_HEAD_DIM:(h + 1) * MEM_HEAD_DIM] = (
            kh * _rms(kh, MEM_HEAD_DIM) * gk_ref[...]).astype(k_ref.dtype)
    v_ref[...] = kv[:, MEM_WIDTH:].astype(v_ref.dtype)


def mem_kv(mem2d, g, w_kv, gk, B, M):
    D = mem2d.shape[1]
    return pl.pallas_call(
        _mem_kv_kernel,
        grid=(B,),
        in_specs=[pl.BlockSpec((M, D), lambda b: (b, 0)),
                  pl.BlockSpec((1, D), lambda b: (0, 0)),
                  pl.BlockSpec((D, 2 * MEM_WIDTH), lambda b: (0, 0)),
                  pl.BlockSpec((1, MEM_HEAD_DIM), lambda b: (0, 0))],
        out_specs=[pl.BlockSpec((M, MEM_WIDTH), lambda b: (b, 0)),
                   pl.BlockSpec((M, MEM_WIDTH), lambda b: (b, 0))],
        out_shape=[jax.ShapeDtypeStruct((B * M, MEM_WIDTH), BF16),
                   jax.ShapeDtypeStruct((B * M, MEM_WIDTH), BF16)],
        compiler_params=_cp("parallel"),
        name="mem_kv",
    )(mem2d, g, w_kv, gk)


def _mem_attn_kernel(q_ref, k_ref, v_ref, gq_ref, o_ref):
    scale = MEM_HEAD_DIM ** -0.5
    for h in range(MEM_HEADS):
        sl = slice(h * MEM_HEAD_DIM, (h + 1) * MEM_HEAD_DIM)
        qh = q_ref[:, sl]
        qn = (qh * (_rms(qh, MEM_HEAD_DIM) * scale) * gq_ref[...]).astype(BF16)
        s = lax.dot_general(qn, k_ref[:, sl], (((1,), (1,)), ((), ())), preferred_element_type=F32)
        m = jnp.max(s, axis=-1, keepdims=True)
        p = jnp.exp(s - m)
        l = jnp.sum(p, axis=-1, keepdims=True)
        o = jnp.dot(p.astype(BF16), v_ref[:, sl], preferred_element_type=F32)
        o_ref[:, sl] = (o / l).astype(o_ref.dtype)


def mem_attention(za, k_m, v_m, gq, B, S, M, tq=512):
    T = za.shape[0]
    tq = min(tq, S)
    nq = S // tq
    return pl.pallas_call(
        _mem_attn_kernel,
        grid=(B, nq),
        in_specs=[pl.BlockSpec((tq, MEM_WIDTH), lambda b, i: (b * nq + i, ZA_QMEM // MEM_WIDTH)),
                  pl.BlockSpec((M, MEM_WIDTH), lambda b, i: (b, 0)),
                  pl.BlockSpec((M, MEM_WIDTH), lambda b, i: (b, 0)),
                  pl.BlockSpec((1, MEM_HEAD_DIM), lambda b, i: (0, 0))],
        out_specs=pl.BlockSpec((tq, MEM_WIDTH), lambda b, i: (b * nq + i, 0)),
        out_shape=jax.ShapeDtypeStruct((T, MEM_WIDTH), BF16),
        compiler_params=_cp("parallel", "arbitrary"),
        name="mem_attention",
    )(za, k_m, v_m, gq)


def _merge_kernel(h_ref, a_ref, p_ref, m_ref, g0_ref, g1_ref, g2_ref, w0_ref, w1_ref, w2_ref, o_ref):
    h = h_ref[...]
    acc = None
    for x_ref, g_ref, w_ref in ((a_ref, g0_ref, w0_ref), (p_ref, g1_ref, w1_ref), (m_ref, g2_ref, w2_ref)):
        logit = jnp.dot(h, g_ref[0], preferred_element_type=F32)
        gate = 1.0 / (1.0 + jnp.exp(-logit))
        br = jnp.dot(x_ref[...], w_ref[0], preferred_element_type=F32)
        acc = gate * br if acc is None else acc + gate * br
    o_ref[...] = acc.astype(o_ref.dtype)


def merge_branches(h, a, p, m, w_all, w_a, w_p, w_m, l, tm=512, tn=512):
    T, D = h.shape
    W = a.shape[1]
    nj = D // tn
    g0 = ZA_WIDTH // tn
    row = lambda width: pl.BlockSpec((tm, width), lambda i, j: (i, 0))
    gate_spec = lambda b: pl.BlockSpec((1, D, tn), lambda i, j, b=b: (l, 0, g0 + b * nj + j))
    out_w = pl.BlockSpec((1, W, tn), lambda i, j: (l, 0, j))
    return pl.pallas_call(
        _merge_kernel,
        grid=(T // tm, nj),
        in_specs=[row(D), row(W), row(W), row(W),
                  gate_spec(0), gate_spec(1), gate_spec(2), out_w, out_w, out_w],
        out_specs=pl.BlockSpec((tm, tn), lambda i, j: (i, j)),
        out_shape=jax.ShapeDtypeStruct((T, D), BF16),
        compiler_params=_cp("parallel", "arbitrary"),
        name="merge_branches",
    )(h, a, p, m, w_all, w_all, w_all, w_a, w_p, w_m)


def _w_in_prep_kernel(w_ref, o_ref):
    x = w_ref[0]
    s1 = Q_LORA_RANK + KV_LORA_RANK + QK_ROPE_DIM
    s2 = s1 + POOL_WIDTH
    s3 = s2 + MEM_WIDTH
    rows = x.shape[0]
    o_ref[0, :, ZA_POOL:ZA_POOL + POOL_WIDTH] = x[:, s1:s2].astype(BF16)
    o_ref[0, :, ZA_QMEM:ZA_QMEM + MEM_WIDTH] = x[:, s2:s3].astype(BF16)
    o_ref[0, :, ZA_CQ:ZA_CQ + s1] = x[:, 0:s1].astype(BF16)
    o_ref[0, :, ZA_CQ + s1:ZA_WIDTH] = jnp.zeros((rows, ZA_WIDTH - ZA_CQ - s1), BF16)
    o_ref[0, :, ZA_WIDTH:] = x[:, s3:].astype(BF16)


def prep_w_in(w_in, tr=256):
    L, D, win = w_in.shape
    wout = ZA_WIDTH + N_BRANCHES * D
    return pl.pallas_call(
        _w_in_prep_kernel,
        grid=(L, D // tr),
        in_specs=[pl.BlockSpec((1, tr, win), lambda l, i: (l, i, 0))],
        out_specs=pl.BlockSpec((1, tr, wout), lambda l, i: (l, i, 0)),
        out_shape=jax.ShapeDtypeStruct((L, D, wout), BF16),
        compiler_params=_cp("parallel", "parallel"),
        name="prep_w_in",
    )(w_in)


def _swiglu_accumulate(xb, wg_ref, wu_ref, wd_refs, o_ref):
    a = jnp.dot(xb, wg_ref[0], preferred_element_type=F32)
    b = jnp.dot(xb, wu_ref[0], preferred_element_type=F32)
    act = (a / (1.0 + jnp.exp(-a)) * b).astype(BF16)
    width = o_ref.shape[1] // len(wd_refs)
    for n, wd_ref in enumerate(wd_refs):
        o_ref[:, n * width:(n + 1) * width] += jnp.dot(act, wd_ref[0], preferred_element_type=F32)


def _dense_swiglu_kernel(x_ref, wg_ref, wu_ref, wd_ref, r_ref, o_ref):
    @pl.when(pl.program_id(1) == 0)
    def _():
        o_ref[...] = r_ref[...]

    _swiglu_accumulate(x_ref[...], wg_ref, wu_ref, (wd_ref,), o_ref)


def dense_swiglu(x, w_gate, w_up, w_down, residual, tm=1024, tf=512):
    T, D = x.shape
    F = w_gate.shape[2]
    tm = min(tm, T)
    return pl.pallas_call(
        _dense_swiglu_kernel,
        grid=(T // tm, F // tf),
        in_specs=[pl.BlockSpec((tm, D), lambda i, f: (i, 0)),
                  pl.BlockSpec((1, D, tf), lambda i, f: (0, 0, f)),
                  pl.BlockSpec((1, D, tf), lambda i, f: (0, 0, f)),
                  pl.BlockSpec((1, tf, D), lambda i, f: (0, f, 0)),
                  pl.BlockSpec((tm, D), lambda i, f: (i, 0), pipeline_mode=pl.Buffered(1))],
        out_specs=pl.BlockSpec((tm, D), lambda i, f: (i, 0)),
        out_shape=jax.ShapeDtypeStruct((T, D), F32),
        compiler_params=_cp("parallel", "arbitrary"),
        name="dense_swiglu",
    )(x, w_gate, w_up, w_down, residual)


def _row_copy(src_hbm, dst_ref, src_row, dst_row, sem):
    return pltpu.make_async_copy(src_hbm.at[pl.ds(src_row, 1)], dst_ref.at[pl.ds(dst_row, 1)], sem)


def _moe_rows_per_step(tm, nf):
    return -(-tm // (nf * SUBLANES)) * SUBLANES


def _moe_swiglu_kernel(tok_ref, be_ref, nu_ref, h_hbm, wg_ref, wu_ref, wd0_ref, wd1_ref, o_ref,
                       xg_ref, xb_ref, sem, *, tm, nf):
    del be_ref
    i = pl.program_id(0)
    f = pl.program_id(1)
    n_used = nu_ref[0]
    rows_per_step = _moe_rows_per_step(tm, nf)
    n_fetch = rows_per_step * nf

    def fetch(block, r, slot):
        tok = tok_ref[block * tm + jnp.minimum(r, tm - 1)]
        return _row_copy(h_hbm, xg_ref.at[slot], tok, r, sem.at[slot])

    def wait_rows(slot):
        rows = xg_ref.at[slot, pl.ds(0, n_fetch)]
        pltpu.make_async_copy(rows, rows, sem.at[slot]).wait()

    @pl.when((i >= n_used) & (f == 0))
    def _():
        o_ref[...] = jnp.zeros_like(o_ref)

    @pl.when(i < n_used)
    def _():
        slot = i % 2

        @pl.when((i == 0) & (f == 0))
        def _():
            def start(r, c):
                fetch(0, r, 0).start()
                return c
            lax.fori_loop(0, n_fetch, start, 0)

        @pl.when(f == 0)
        def _():
            wait_rows(slot)
            xb_ref[...] = xg_ref[slot, 0:tm, :].astype(BF16)
            o_ref[...] = jnp.zeros_like(o_ref)

        nxt = jnp.minimum(i + 1, n_used - 1)
        for j in range(rows_per_step):
            fetch(nxt, f * rows_per_step + j, 1 - slot).start()

        _swiglu_accumulate(xb_ref[...], wg_ref, wu_ref, (wd0_ref, wd1_ref), o_ref)

        @pl.when((i == n_used - 1) & (f == nf - 1))
        def _():
            wait_rows(1 - slot)


def moe_swiglu(h, slot_tok, block_e, n_used, w_gate, w_up, w_down_lo, w_down_hi, tm, tf=1024):
    n_slots = slot_tok.shape[0]
    D = h.shape[1]
    F = w_gate.shape[2]
    while F % tf:
        tf //= 2
    nf = F // tf
    xg_rows = _moe_rows_per_step(tm, nf) * nf
    blk = lambda i, nu: jnp.minimum(i, nu[0] - 1)
    chunk = lambda i, f, nu: jnp.where(i < nu[0], f, nf - 1)
    up_spec = pl.BlockSpec((1, D, tf), lambda i, f, tok, be, nu: (be[blk(i, nu)], 0, chunk(i, f, nu)))
    down_spec = pl.BlockSpec((1, tf, D // 2), lambda i, f, tok, be, nu: (be[blk(i, nu)], chunk(i, f, nu), 0))
    return pl.pallas_call(
        functools.partial(_moe_swiglu_kernel, tm=tm, nf=nf),
        grid_spec=pltpu.PrefetchScalarGridSpec(
            num_scalar_prefetch=3,
            grid=(n_slots // tm, nf),
            in_specs=[pl.BlockSpec(memory_space=pl.ANY), up_spec, up_spec, down_spec, down_spec],
            out_specs=pl.BlockSpec((tm, D), lambda i, f, tok, be, nu: (i, 0)),
            scratch_shapes=[pltpu.VMEM((2, xg_rows, D), F32),
                            pltpu.VMEM((tm, D), BF16),
                            pltpu.SemaphoreType.DMA((2,))]),
        out_shape=jax.ShapeDtypeStruct((n_slots, D), F32),
        compiler_params=_cp("arbitrary", "arbitrary"),
        name="moe_swiglu",
    )(slot_tok, block_e, n_used, h, w_gate, w_up, w_down_lo, w_down_hi)


def _combine_kernel(dest_ref, y_ref, gate_ref, ys_hbm, o_ref, buf_ref, sem, *, tb):
    base = pl.program_id(0) * tb

    def start(r, c):
        for k in range(TOP_K):
            _row_copy(ys_hbm, buf_ref.at[k], dest_ref[TOP_K * (base + r) + k], r, sem).start()
        return c

    lax.fori_loop(0, tb, start, 0, unroll=8)
    pltpu.make_async_copy(buf_ref, buf_ref, sem).wait()
    g = gate_ref[...]
    o_ref[...] = y_ref[...] + g[:, 0:1] * buf_ref[0] + g[:, 1:2] * buf_ref[1]


def combine_rows(y, gates, ys, dest, tb=256):
    T, D = y.shape
    return pl.pallas_call(
        functools.partial(_combine_kernel, tb=tb),
        grid_spec=pltpu.PrefetchScalarGridSpec(
            num_scalar_prefetch=1,
            grid=(T // tb,),
            in_specs=[pl.BlockSpec((tb, D), lambda i, d: (i, 0)),
                      pl.BlockSpec((tb, LANES), lambda i, d: (i, 0)),
                      pl.BlockSpec(memory_space=pl.ANY)],
            out_specs=pl.BlockSpec((tb, D), lambda i, d: (i, 0)),
            scratch_shapes=[pltpu.VMEM((TOP_K, tb, D), F32),
                            pltpu.SemaphoreType.DMA(())]),
        out_shape=jax.ShapeDtypeStruct((T, D), F32),
        compiler_params=_cp("arbitrary"),
        name="moe_combine",
    )(dest, y, gates, ys)


def _routing_tables(top_idx, T, block):
    A = T * TOP_K
    E = N_EXPERTS
    flat_e = top_idx.reshape(A)
    onehot = (flat_e[:, None] == jnp.arange(E, dtype=jnp.int32)[None, :]).astype(jnp.int32)
    csum = jnp.cumsum(onehot, axis=0)
    counts = csum[-1]
    rank = jnp.sum((csum - 1) * onehot, axis=1)
    padded = (counts + block - 1) // block * block
    pends = jnp.cumsum(padded)
    pstarts = pends - padded
    starts = jnp.cumsum(counts) - counts
    dest = (pstarts[flat_e] + rank).astype(jnp.int32)
    n_blocks = A // block + E
    n_slots = n_blocks * block
    order = jnp.argsort(flat_e, stable=True).astype(jnp.int32)
    slot = jnp.arange(n_slots, dtype=jnp.int32)
    slot_e = jnp.clip(jnp.searchsorted(pends, slot, side='right'), 0, E - 1).astype(jnp.int32)
    r = slot - pstarts[slot_e]
    valid = r < counts[slot_e]
    src = jnp.clip(starts[slot_e] + r, 0, A - 1)
    slot_tok = jnp.where(valid, order[src] // TOP_K, 0).astype(jnp.int32)
    block_e = slot_e[::block]
    n_used = (pends[-1:] // block).astype(jnp.int32)
    return slot_tok, dest, block_e, n_used


def _token_mixer(y, h, mem2d, tab, B, S, M, l, casts, w_in, mla_q_a_norm_g, mla_w_uq, mla_kv_a_norm_g, mla_w_ukv,
                 mla_q_norm_g, mla_k_norm_g, mla_w_out, pool_w, pool_scale, pool_w_out,
                 mem_norm_g, mem_w_kv, mem_q_norm_g, mem_k_norm_g, mem_w_out, w_o):
    D = h.shape[1]
    za = matmul(h, w_in, l, ZA_WIDTH, F32)

    wuq = jnp.pad(mla_w_uq[l].reshape(Q_LORA_RANK, MLA_HEADS, QK_HEAD_DIM),
                  ((0, 0), (0, 0), (0, HEAD_PAD - QK_HEAD_DIM))).reshape(Q_LORA_RANK, MLA_HEADS * HEAD_PAD)
    wukv = mla_w_ukv[l].reshape(KV_LORA_RANK, MLA_HEADS, QK_NOPE_DIM + V_HEAD_DIM)
    wuk = wukv[:, :, :QK_NOPE_DIM].reshape(KV_LORA_RANK, MLA_WIDTH)
    wuv = wukv[:, :, QK_NOPE_DIM:].reshape(KV_LORA_RANK, MLA_WIDTH)
    pad_g = lambda g: jnp.pad(g, (0, HEAD_PAD - QK_HEAD_DIM)).reshape(1, HEAD_PAD)
    q, k, v = mla_prep(za, tab, wuq.astype(BF16), wuk.astype(BF16), wuv.astype(BF16),
                       mla_q_a_norm_g[l].reshape(1, -1), mla_kv_a_norm_g[l].reshape(1, -1),
                       pad_g(mla_q_norm_g[l]), pad_g(mla_k_norm_g[l]))
    attn, *cast_out = causal_attention(q, k, v, B, S, casts)

    mixed = pool_mixer(za, pool_w[l].astype(BF16), pool_scale[l].reshape(1, -1), B, S)

    k_m, v_m = mem_kv(mem2d, mem_norm_g[l].reshape(1, -1), mem_w_kv[l].astype(BF16),
                      mem_k_norm_g[l].reshape(1, -1), B, M)
    o_mem = mem_attention(za, k_m, v_m, mem_q_norm_g[l].reshape(1, -1), B, S, M)

    merged = merge_branches(h, attn, mixed, o_mem, w_in, mla_w_out, pool_w_out, mem_w_out, l)
    return matmul(merged, w_o, l, D, F32, residual=y), cast_out


def kernel(x, mem, positions, attn_norm_g, w_in, mla_q_a_norm_g, mla_w_uq, mla_kv_a_norm_g, mla_w_ukv, mla_q_norm_g, mla_k_norm_g, mla_w_out, pool_w, pool_scale, pool_w_out, mem_norm_g, mem_w_kv, mem_q_norm_g, mem_k_norm_g, mem_w_out, w_o, ffn_norm_g, dense_w_gate, dense_w_up, dense_w_down, router_w, router_b, moe_w_gate, moe_w_up, moe_w_down):
    B, S, D = x.shape
    M = mem.shape[1]
    T = B * S
    depth = attn_norm_g.shape[0]
    moe_block = 512
    y = x.reshape(T, D)
    mem2d = mem.reshape(B * M, D)
    tab = rope_tables(positions)
    w_in = prep_w_in(w_in)
    mla_w_out, pool_w_out, mem_w_out, w_o = (w.astype(BF16) for w in (mla_w_out, pool_w_out, mem_w_out, w_o))
    E, _, F_moe = moe_w_gate.shape[1:]
    moe_bf16 = {}
    for l in range(depth):
        h = rmsnorm(y, attn_norm_g[l], BF16)
        i = l // 2
        casts = ()
        if l % 2 == 0 and l + 1 < depth:
            casts = ((moe_w_gate[i].reshape(E * D, F_moe), 0, 1), (moe_w_down[i].reshape(E * F_moe, D), 0, 2))
        elif l % 2 == 1:
            casts = ((moe_w_up[i].reshape(E * D, F_moe), 0, 1), (moe_w_down[i].reshape(E * F_moe, D), 1, 2))
        y, cast_out = _token_mixer(y, h, mem2d, tab, B, S, M, l, casts, w_in, mla_q_a_norm_g, mla_w_uq,
                                   mla_kv_a_norm_g, mla_w_ukv, mla_q_norm_g, mla_k_norm_g, mla_w_out, pool_w,
                                   pool_scale, pool_w_out, mem_norm_g, mem_w_kv, mem_q_norm_g, mem_k_norm_g,
                                   mem_w_out, w_o)
        if casts:
            names = ("gate", "down_lo") if l % 2 == 0 else ("up", "down_hi")
            moe_bf16.update(zip(names, cast_out))
        if l % 2 == 0:
            h = rmsnorm(y, ffn_norm_g[l], BF16)
            y = dense_swiglu(h, dense_w_gate[i:i + 1].astype(BF16), dense_w_up[i:i + 1].astype(BF16),
                             dense_w_down[i:i + 1].astype(BF16), y)
        else:
            hf, idx, gates = norm_router(y, ffn_norm_g[l], router_w[i], router_b[i])
            slot_tok, dest, block_e, n_used = _routing_tables(idx[:, :TOP_K], T, moe_block)
            ys = moe_swiglu(hf, slot_tok, block_e, n_used, moe_bf16.pop("gate").reshape(E, D, F_moe),
                            moe_bf16.pop("up").reshape(E, D, F_moe),
                            moe_bf16.pop("down_lo").reshape(E, F_moe, D // 2),
                            moe_bf16.pop("down_hi").reshape(E, F_moe, D // 2), tm=moe_block)
            y = combine_rows(y, gates, ys, dest)
    return y.reshape(B, S, D)
```

```python
import functools

import numpy as np
import jax
import jax.numpy as jnp
from jax import lax
from jax.experimental import pallas as pl
from jax.experimental.pallas import tpu as pltpu

F32 = jnp.float32
BF16 = jnp.bfloat16

MEM_HEADS = 4
MEM_HEAD_DIM = 256
MEM_WIDTH = MEM_HEADS * MEM_HEAD_DIM
MLA_HEADS = 8
Q_LORA_RANK = 512
KV_LORA_RANK = 256
QK_NOPE_DIM = 128
QK_ROPE_DIM = 64
QK_HEAD_DIM = QK_NOPE_DIM + QK_ROPE_DIM
V_HEAD_DIM = 128
MLA_WIDTH = MLA_HEADS * V_HEAD_DIM
ROPE_THETA = 10000.0
POOL_WINDOWS = (2, 4, 8, 16)
POOL_GROUPS = 4
POOL_GROUP_DIM = 256
POOL_WIDTH = POOL_GROUPS * POOL_GROUP_DIM
N_BRANCHES = 3
N_EXPERTS = 8
TOP_K = 2
NORM_EPS = 1e-6

LANES = 128
SUBLANES = 8
HEAD_PAD = 2 * LANES
ROPE_HALF = QK_ROPE_DIM // 2
POOL_HALO = 16

ZA_POOL = 0
ZA_QMEM = POOL_WIDTH
ZA_CQ = ZA_QMEM + MEM_WIDTH
ZA_CKV = ZA_CQ + Q_LORA_RANK
ZA_KPE = ZA_CKV + KV_LORA_RANK
ZA_WIDTH = 3072

VMEM_LIMIT = 52 * 1024 * 1024


def _cp(*sem):
    return pltpu.CompilerParams(dimension_semantics=sem, vmem_limit_bytes=VMEM_LIMIT)


def _rms(x, eps_dim):
    return lax.rsqrt(jnp.sum(x * x, axis=-1, keepdims=True) * (1.0 / eps_dim) + NORM_EPS)


def _rmsnorm_kernel(x_ref, g_ref, o_ref):
    x = x_ref[...]
    o_ref[...] = (x * _rms(x, x.shape[-1]) * g_ref[...]).astype(o_ref.dtype)


def rmsnorm(x, g, out_dtype, tm=512):
    T, D = x.shape
    return pl.pallas_call(
        _rmsnorm_kernel,
        grid=(T // tm,),
        in_specs=[pl.BlockSpec((tm, D), lambda i: (i, 0)),
                  pl.BlockSpec((1, D), lambda i: (0, 0))],
        out_specs=pl.BlockSpec((tm, D), lambda i: (i, 0)),
        out_shape=jax.ShapeDtypeStruct((T, D), out_dtype),
        compiler_params=_cp("parallel"),
        name="rmsnorm",
    )(x, g.reshape(1, D))


def _norm_router_kernel(x_ref, g_ref, rwh_ref, rwl_ref, rb_ref, h_ref, idx_ref, gate_ref):
    x = x_ref[...]
    h = x * _rms(x, x.shape[-1]) * g_ref[...]
    h_ref[...] = h
    h_hi = h.astype(BF16)
    h_lo = (h - h_hi.astype(F32)).astype(BF16)
    logits = (jnp.dot(h_hi, rwh_ref[...], preferred_element_type=F32)
              + jnp.dot(h_lo, rwh_ref[...], preferred_element_type=F32)
              + jnp.dot(h_hi, rwl_ref[...], preferred_element_type=F32)) + rb_ref[...]
    lane = lax.broadcasted_iota(jnp.int32, logits.shape, 1)
    neg = -jnp.inf
    l1 = jnp.where(lane < N_EXPERTS, logits, neg)
    m1 = jnp.max(l1, axis=-1, keepdims=True)
    i1 = jnp.min(jnp.where(l1 == m1, lane, LANES), axis=-1, keepdims=True)
    l2 = jnp.where(lane == i1, neg, l1)
    m2 = jnp.max(l2, axis=-1, keepdims=True)
    i2 = jnp.min(jnp.where(l2 == m2, lane, LANES), axis=-1, keepdims=True)
    e = jnp.exp(m2 - m1)
    g1 = 1.0 / (1.0 + e)
    g2 = e / (1.0 + e)
    idx_ref[...] = jnp.where(lane == 0, i1, jnp.where(lane == 1, i2, 0))
    gate_ref[...] = jnp.where(lane == 0, g1, jnp.where(lane == 1, g2, 0.0))


def norm_router(x, g, router_w, router_b, tm=512):
    T, D = x.shape
    E = router_w.shape[1]
    rw = jnp.zeros((D, LANES), F32).at[:, :E].set(router_w)
    rw_hi = rw.astype(BF16)
    rw_lo = (rw - rw_hi.astype(F32)).astype(BF16)
    rb = jnp.zeros((1, LANES), F32).at[0, :E].set(router_b)
    return pl.pallas_call(
        _norm_router_kernel,
        grid=(T // tm,),
        in_specs=[pl.BlockSpec((tm, D), lambda i: (i, 0)),
                  pl.BlockSpec((1, D), lambda i: (0, 0)),
                  pl.BlockSpec((D, LANES), lambda i: (0, 0)),
                  pl.BlockSpec((D, LANES), lambda i: (0, 0)),
                  pl.BlockSpec((1, LANES), lambda i: (0, 0))],
        out_specs=[pl.BlockSpec((tm, D), lambda i: (i, 0)),
                   pl.BlockSpec((tm, LANES), lambda i: (i, 0)),
                   pl.BlockSpec((tm, LANES), lambda i: (i, 0))],
        out_shape=[jax.ShapeDtypeStruct((T, D), F32),
                   jax.ShapeDtypeStruct((T, LANES), jnp.int32),
                   jax.ShapeDtypeStruct((T, LANES), F32)],
        compiler_params=_cp("parallel"),
        name="norm_router",
    )(x, g.reshape(1, D), rw_hi, rw_lo, rb)


def _mm_kernel(a_ref, w_ref, o_ref):
    o_ref[...] = jnp.dot(a_ref[...], w_ref[0], preferred_element_type=F32).astype(o_ref.dtype)


def _mm_res_kernel(a_ref, w_ref, r_ref, o_ref):
    o_ref[...] = (r_ref[...] + jnp.dot(a_ref[...], w_ref[0], preferred_element_type=F32)).astype(o_ref.dtype)


def matmul(a, w, l, N, out_dtype, residual=None, tm=1024, tn=1024):
    M, K = a.shape
    tm, tn = min(tm, M), min(tn, N)
    in_specs = [pl.BlockSpec((tm, K), lambda i, j: (i, 0)),
                pl.BlockSpec((1, K, tn), lambda i, j: (l, 0, j))]
    args = [a, w]
    body = _mm_kernel
    if residual is not None:
        in_specs.append(pl.BlockSpec((tm, tn), lambda i, j: (i, j)))
        args.append(residual)
        body = _mm_res_kernel
    return pl.pallas_call(
        body,
        grid=(M // tm, N // tn),
        in_specs=in_specs,
        out_specs=pl.BlockSpec((tm, tn), lambda i, j: (i, j)),
        out_shape=jax.ShapeDtypeStruct((M, N), out_dtype),
        compiler_params=_cp("parallel", "arbitrary"),
        name="matmul",
    )(*args)


def _rope_table_kernel(pos_ref, invf_ref, o_ref):
    ang = pos_ref[...].astype(F32) * invf_ref[...]
    lane = lax.broadcasted_iota(jnp.int32, ang.shape, 1)
    c = jnp.cos(ang)
    s = jnp.sin(ang)
    o_ref[:, 0:LANES] = jnp.where(lane < QK_ROPE_DIM, c, 0.0)
    o_ref[:, LANES:2 * LANES] = jnp.where(lane < ROPE_HALF, -s, 0.0)
    o_ref[:, 2 * LANES:3 * LANES] = jnp.where((lane >= ROPE_HALF) & (lane < QK_ROPE_DIM), s, 0.0)


def rope_tables(positions, tm=512):
    T = positions.size
    inv_freq = ROPE_THETA ** (-np.arange(0, QK_ROPE_DIM, 2, dtype=np.float32) / QK_ROPE_DIM)
    invf = np.zeros((1, LANES), np.float32)
    invf[0, :ROPE_HALF] = inv_freq
    invf[0, ROPE_HALF:QK_ROPE_DIM] = inv_freq
    return pl.pallas_call(
        _rope_table_kernel,
        grid=(T // tm,),
        in_specs=[pl.BlockSpec((tm, 1), lambda i: (i, 0)),
                  pl.BlockSpec((1, LANES), lambda i: (0, 0))],
        out_specs=pl.BlockSpec((tm, 3 * LANES), lambda i: (i, 0)),
        out_shape=jax.ShapeDtypeStruct((T, 3 * LANES), F32),
        compiler_params=_cp("parallel"),
        name="rope_tables",
    )(positions.reshape(T, 1), jnp.asarray(invf))


def _rope(x, tab):
    c = tab[:, 0:LANES]
    s1 = tab[:, LANES:2 * LANES]
    s2 = tab[:, 2 * LANES:3 * LANES]
    return x * c + pltpu.roll(x, LANES - ROPE_HALF, 1) * s1 + pltpu.roll(x, ROPE_HALF, 1) * s2


def _mla_prep_kernel(cq_ref, ckv_ref, kpe_ref, tab_ref, wuq_ref, wuk_ref, wuv_ref,
                     gqa_ref, gkva_ref, gq_ref, gk_ref, q_ref, k_ref, v_ref):
    tab = tab_ref[...]
    cq = cq_ref[...]
    cqn = (cq * _rms(cq, Q_LORA_RANK) * gqa_ref[...]).astype(BF16)
    qf = jnp.dot(cqn, wuq_ref[...], preferred_element_type=F32)
    ckv = ckv_ref[...]
    ckvn = (ckv * _rms(ckv, KV_LORA_RANK) * gkva_ref[...]).astype(BF16)
    kn = jnp.dot(ckvn, wuk_ref[...], preferred_element_type=F32)
    v_ref[...] = jnp.dot(ckvn, wuv_ref[...], preferred_element_type=F32).astype(v_ref.dtype)
    gq = gq_ref[...]
    gk = gk_ref[...]
    kpe = kpe_ref[...]
    ss_pe = jnp.sum(kpe * kpe, axis=-1, keepdims=True)
    kpe_rot = _rope(kpe * gk[:, LANES:], tab)
    scale = QK_HEAD_DIM ** -0.5
    for h in range(MLA_HEADS):
        q0 = qf[:, h * HEAD_PAD:h * HEAD_PAD + LANES]
        q1 = qf[:, h * HEAD_PAD + LANES:(h + 1) * HEAD_PAD]
        ss = jnp.sum(q0 * q0, axis=-1, keepdims=True) + jnp.sum(q1 * q1, axis=-1, keepdims=True)
        rq = lax.rsqrt(ss * (1.0 / QK_HEAD_DIM) + NORM_EPS) * scale
        q_ref[:, h * HEAD_PAD:h * HEAD_PAD + LANES] = (q0 * gq[:, :LANES] * rq).astype(q_ref.dtype)
        q_ref[:, h * HEAD_PAD + LANES:(h + 1) * HEAD_PAD] = (_rope(q1 * gq[:, LANES:], tab) * rq).astype(q_ref.dtype)
        k0 = kn[:, h * LANES:(h + 1) * LANES]
        ssk = jnp.sum(k0 * k0, axis=-1, keepdims=True) + ss_pe
        rk = lax.rsqrt(ssk * (1.0 / QK_HEAD_DIM) + NORM_EPS)
        k_ref[:, h * HEAD_PAD:h * HEAD_PAD + LANES] = (k0 * gk[:, :LANES] * rk).astype(k_ref.dtype)
        k_ref[:, h * HEAD_PAD + LANES:(h + 1) * HEAD_PAD] = (kpe_rot * rk).astype(k_ref.dtype)


def mla_prep(za, tab, wuq, wuk, wuv, gqa, gkva, gq, gk, tm=512):
    T = za.shape[0]
    HP = MLA_HEADS * HEAD_PAD
    full = lambda shape: pl.BlockSpec(shape, lambda i: (0, 0))
    return pl.pallas_call(
        _mla_prep_kernel,
        grid=(T // tm,),
        in_specs=[pl.BlockSpec((tm, Q_LORA_RANK), lambda i: (i, ZA_CQ // Q_LORA_RANK)),
                  pl.BlockSpec((tm, KV_LORA_RANK), lambda i: (i, ZA_CKV // KV_LORA_RANK)),
                  pl.BlockSpec((tm, LANES), lambda i: (i, ZA_KPE // LANES)),
                  pl.BlockSpec((tm, 3 * LANES), lambda i: (i, 0)),
                  full((Q_LORA_RANK, HP)), full((KV_LORA_RANK, MLA_WIDTH)), full((KV_LORA_RANK, MLA_WIDTH)),
                  full((1, Q_LORA_RANK)), full((1, KV_LORA_RANK)), full((1, HEAD_PAD)), full((1, HEAD_PAD))],
        out_specs=[pl.BlockSpec((tm, HP), lambda i: (i, 0)),
                   pl.BlockSpec((tm, HP), lambda i: (i, 0)),
                   pl.BlockSpec((tm, MLA_WIDTH), lambda i: (i, 0))],
        out_shape=[jax.ShapeDtypeStruct((T, HP), BF16),
                   jax.ShapeDtypeStruct((T, HP), BF16),
                   jax.ShapeDtypeStruct((T, MLA_WIDTH), BF16)],
        compiler_params=_cp("parallel"),
        name="mla_prep",
    )(za, za, za, tab, wuq, wuk, wuv, gqa, gkva, gq, gk)


def _causal_attn_kernel(q_ref, k_ref, v_ref, *rest, tq, n_cast):
    cast_src = rest[:n_cast]
    o_ref = rest[n_cast]
    cast_dst = rest[n_cast + 1:]
    for s_ref, d_ref in zip(cast_src, cast_dst):
        d_ref[...] = s_ref[...].astype(d_ref.dtype)
    S = q_ref.shape[0]
    row = lax.broadcasted_iota(jnp.int32, (tq, tq), 0)
    col = lax.broadcasted_iota(jnp.int32, (tq, tq), 1)
    tri = col <= row
    for qi in range(S // tq):
        kl = (qi + 1) * tq
        q = q_ref[qi * tq:kl, :]
        s = lax.dot_general(q, k_ref[0:kl, :], (((1,), (1,)), ((), ())), preferred_element_type=F32)
        s_diag = jnp.where(tri, s[:, kl - tq:kl], -jnp.inf)
        if qi > 0:
            s = jnp.concatenate([s[:, :kl - tq], s_diag], axis=1)
        else:
            s = s_diag
        m = jnp.max(s, axis=-1, keepdims=True)
        p = jnp.exp(s - m)
        l = jnp.sum(p, axis=-1, keepdims=True)
        o = jnp.dot(p.astype(BF16), v_ref[0:kl, :], preferred_element_type=F32)
        o_ref[qi * tq:kl, :] = (o / l).astype(o_ref.dtype)


def causal_attention(q, k, v, B, S, casts=(), tq=256):
    T = q.shape[0]
    n_steps = B * MLA_HEADS
    in_specs = [pl.BlockSpec((S, HEAD_PAD), lambda b, h: (b, h)),
                pl.BlockSpec((S, HEAD_PAD), lambda b, h: (b, h)),
                pl.BlockSpec((S, V_HEAD_DIM), lambda b, h: (b, h))]
    out_specs = [pl.BlockSpec((S, V_HEAD_DIM), lambda b, h: (b, h))]
    out_shape = [jax.ShapeDtypeStruct((T, MLA_WIDTH), BF16)]
    for src, part, n_parts in casts:
        E, rows, cols = src.shape
        spe = n_steps // E
        rb, cb = rows // spe, cols // n_parts
        assert spe * E == n_steps and rb * spe == rows and rb % 16 == 0
        assert cb * n_parts == cols and cb % LANES == 0
        step = lambda b, h: b * MLA_HEADS + h
        in_specs.append(pl.BlockSpec((1, rb, cb), lambda b, h, part=part, spe=spe:
                                     (step(b, h) // spe, step(b, h) % spe, part)))
        out_specs.append(pl.BlockSpec((1, rb, cb), lambda b, h, spe=spe:
                                      (step(b, h) // spe, step(b, h) % spe, 0)))
        out_shape.append(jax.ShapeDtypeStruct((E, rows, cb), BF16))
    return pl.pallas_call(
        functools.partial(_causal_attn_kernel, tq=tq, n_cast=len(casts)),
        grid=(B, MLA_HEADS),
        in_specs=in_specs,
        out_specs=out_specs,
        out_shape=out_shape,
        compiler_params=_cp("parallel", "parallel"),
        name="causal_attention",
    )(q, k, v, *(c[0] for c in casts))


def _pool_kernel(u_ref, w_ref, sc_ref, o_ref, pad_ref, *, rc):
    S = u_ref.shape[0]
    C = POOL_GROUP_DIM
    t1 = lax.broadcasted_iota(jnp.int32, (rc, C), 0).astype(F32) + 1.0
    for g, win in enumerate(POOL_WINDOWS):
        pad_ref[0:POOL_HALO, :] = jnp.zeros((POOL_HALO, C), F32)
        pad_ref[POOL_HALO:POOL_HALO + S, :] = u_ref[:, g * C:(g + 1) * C]
        for r0 in range(0, S, rc):
            acc = pad_ref[POOL_HALO + r0:POOL_HALO + r0 + rc, :]
            tok = acc
            for kk in range(1, win):
                acc = acc + pad_ref[POOL_HALO + r0 - kk:POOL_HALO + r0 - kk + rc, :]
            cnt = jnp.minimum(t1 + float(r0), float(win))
            pooled = (acc / cnt - tok).astype(BF16)
            mixed = jnp.dot(pooled, w_ref[g], preferred_element_type=F32) * sc_ref[:, g * C:(g + 1) * C]
            o_ref[r0:r0 + rc, g * C:(g + 1) * C] = mixed.astype(o_ref.dtype)


def pool_mixer(za, pool_w, pool_scale, B, S, rc=512):
    T = za.shape[0]
    rc = min(rc, S)
    return pl.pallas_call(
        functools.partial(_pool_kernel, rc=rc),
        grid=(B,),
        in_specs=[pl.BlockSpec((S, POOL_WIDTH), lambda b: (b, ZA_POOL // POOL_WIDTH)),
                  pl.BlockSpec((POOL_GROUPS, POOL_GROUP_DIM, POOL_GROUP_DIM), lambda b: (0, 0, 0)),
                  pl.BlockSpec((1, POOL_WIDTH), lambda b: (0, 0))],
        out_specs=pl.BlockSpec((S, POOL_WIDTH), lambda b: (b, 0)),
        out_shape=jax.ShapeDtypeStruct((T, POOL_WIDTH), BF16),
        scratch_shapes=[pltpu.VMEM((POOL_HALO + S, POOL_GROUP_DIM), F32)],
        compiler_params=_cp("parallel"),
        name="pool_mixer",
    )(za, pool_w, pool_scale)


def _mem_kv_kernel(m_ref, g_ref, w_ref, gk_ref, k_ref, v_ref):
    x = m_ref[...]
    xn = (x * _rms(x, x.shape[-1]) * g_ref[...]).astype(BF16)
    kv = jnp.dot(xn, w_ref[...], preferred_element_type=F32)
    for h in range(MEM_HEADS):
        kh = kv[:, h * MEM_HEAD_DIM:(h + 1) * MEM_HEAD_DIM]
        k_ref[:, h * MEM_HEAD_DIM:(h + 1) * MEM_HEAD_DIM] = (
            kh * _rms(kh, MEM_HEAD_DIM) * gk_ref[...]).astype(k_ref.dtype)
    v_ref[...] = kv[:, MEM_WIDTH:].astype(v_ref.dtype)


def mem_kv(mem2d, g, w_kv, gk, B, M):
    D = mem2d.shape[1]
    return pl.pallas_call(
        _mem_kv_kernel,
        grid=(B,),
        in_specs=[pl.BlockSpec((M, D), lambda b: (b, 0)),
                  pl.BlockSpec((1, D), lambda b: (0, 0)),
                  pl.BlockSpec((D, 2 * MEM_WIDTH), lambda b: (0, 0)),
                  pl.BlockSpec((1, MEM_HEAD_DIM), lambda b: (0, 0))],
        out_specs=[pl.BlockSpec((M, MEM_WIDTH), lambda b: (b, 0)),
                   pl.BlockSpec((M, MEM_WIDTH), lambda b: (b, 0))],
        out_shape=[jax.ShapeDtypeStruct((B * M, MEM_WIDTH), BF16),
                   jax.ShapeDtypeStruct((B * M, MEM_WIDTH), BF16)],
        compiler_params=_cp("parallel"),
        name="mem_kv",
    )(mem2d, g, w_kv, gk)


def _mem_attn_kernel(q_ref, k_ref, v_ref, gq_ref, o_ref):
    scale = MEM_HEAD_DIM ** -0.5
    for h in range(MEM_HEADS):
        sl = slice(h * MEM_HEAD_DIM, (h + 1) * MEM_HEAD_DIM)
        qh = q_ref[:, sl]
        qn = (qh * (_rms(qh, MEM_HEAD_DIM) * scale) * gq_ref[...]).astype(BF16)
        s = lax.dot_general(qn, k_ref[:, sl], (((1,), (1,)), ((), ())), preferred_element_type=F32)
        m = jnp.max(s, axis=-1, keepdims=True)
        p = jnp.exp(s - m)
        l = jnp.sum(p, axis=-1, keepdims=True)
        o = jnp.dot(p.astype(BF16), v_ref[:, sl], preferred_element_type=F32)
        o_ref[:, sl] = (o / l).astype(o_ref.dtype)


def mem_attention(za, k_m, v_m, gq, B, S, M, tq=512):
    T = za.shape[0]
    tq = min(tq, S)
    nq = S // tq
    return pl.pallas_call(
        _mem_attn_kernel,
        grid=(B, nq),
        in_specs=[pl.BlockSpec((tq, MEM_WIDTH), lambda b, i: (b * nq + i, ZA_QMEM // MEM_WIDTH)),
                  pl.BlockSpec((M, MEM_WIDTH), lambda b, i: (b, 0)),
                  pl.BlockSpec((M, MEM_WIDTH), lambda b, i: (b, 0)),
                  pl.BlockSpec((1, MEM_HEAD_DIM), lambda b, i: (0, 0))],
        out_specs=pl.BlockSpec((tq, MEM_WIDTH), lambda b, i: (b * nq + i, 0)),
        out_shape=jax.ShapeDtypeStruct((T, MEM_WIDTH), BF16),
        compiler_params=_cp("parallel", "arbitrary"),
        name="mem_attention",
    )(za, k_m, v_m, gq)


def _merge_kernel(h_ref, a_ref, p_ref, m_ref, g0_ref, g1_ref, g2_ref, w0_ref, w1_ref, w2_ref, o_ref):
    h = h_ref[...]
    acc = None
    for x_ref, g_ref, w_ref in ((a_ref, g0_ref, w0_ref), (p_ref, g1_ref, w1_ref), (m_ref, g2_ref, w2_ref)):
        logit = jnp.dot(h, g_ref[0], preferred_element_type=F32)
        gate = 1.0 / (1.0 + jnp.exp(-logit))
        br = jnp.dot(x_ref[...], w_ref[0], preferred_element_type=F32)
        acc = gate * br if acc is None else acc + gate * br
    o_ref[...] = acc.astype(o_ref.dtype)


def merge_branches(h, a, p, m, w_all, w_a, w_p, w_m, l, tm=512, tn=512):
    T, D = h.shape
    W = a.shape[1]
    nj = D // tn
    g0 = ZA_WIDTH // tn
    row = lambda width: pl.BlockSpec((tm, width), lambda i, j: (i, 0))
    gate_spec = lambda b: pl.BlockSpec((1, D, tn), lambda i, j, b=b: (l, 0, g0 + b * nj + j))
    out_w = pl.BlockSpec((1, W, tn), lambda i, j: (l, 0, j))
    return pl.pallas_call(
        _merge_kernel,
        grid=(T // tm, nj),
        in_specs=[row(D), row(W), row(W), row(W),
                  gate_spec(0), gate_spec(1), gate_spec(2), out_w, out_w, out_w],
        out_specs=pl.BlockSpec((tm, tn), lambda i, j: (i, j)),
        out_shape=jax.ShapeDtypeStruct((T, D), BF16),
        compiler_params=_cp("parallel", "arbitrary"),
        name="merge_branches",
    )(h, a, p, m, w_all, w_all, w_all, w_a, w_p, w_m)


def _w_in_prep_kernel(w_ref, o_ref):
    x = w_ref[0]
    s1 = Q_LORA_RANK + KV_LORA_RANK + QK_ROPE_DIM
    s2 = s1 + POOL_WIDTH
    s3 = s2 + MEM_WIDTH
    rows = x.shape[0]
    o_ref[0, :, ZA_POOL:ZA_POOL + POOL_WIDTH] = x[:, s1:s2].astype(BF16)
    o_ref[0, :, ZA_QMEM:ZA_QMEM + MEM_WIDTH] = x[:, s2:s3].astype(BF16)
    o_ref[0, :, ZA_CQ:ZA_CQ + s1] = x[:, 0:s1].astype(BF16)
    o_ref[0, :, ZA_CQ + s1:ZA_WIDTH] = jnp.zeros((rows, ZA_WIDTH - ZA_CQ - s1), BF16)
    o_ref[0, :, ZA_WIDTH:] = x[:, s3:].astype(BF16)


def prep_w_in(w_in, tr=256):
    L, D, win = w_in.shape
    wout = ZA_WIDTH + N_BRANCHES * D
    return pl.pallas_call(
        _w_in_prep_kernel,
        grid=(L, D // tr),
        in_specs=[pl.BlockSpec((1, tr, win), lambda l, i: (l, i, 0))],
        out_specs=pl.BlockSpec((1, tr, wout), lambda l, i: (l, i, 0)),
        out_shape=jax.ShapeDtypeStruct((L, D, wout), BF16),
        compiler_params=_cp("parallel", "parallel"),
        name="prep_w_in",
    )(w_in)


def _swiglu_accumulate(xb, wg_ref, wu_ref, wd_refs, o_ref):
    a = jnp.dot(xb, wg_ref[0], preferred_element_type=F32)
    b = jnp.dot(xb, wu_ref[0], preferred_element_type=F32)
    act = (a / (1.0 + jnp.exp(-a)) * b).astype(BF16)
    width = o_ref.shape[1] // len(wd_refs)
    for n, wd_ref in enumerate(wd_refs):
        o_ref[:, n * width:(n + 1) * width] += jnp.dot(act, wd_ref[0], preferred_element_type=F32)


def _dense_swiglu_kernel(x_ref, wg_ref, wu_ref, wd_ref, r_ref, o_ref):
    @pl.when(pl.program_id(1) == 0)
    def _():
        o_ref[...] = r_ref[...]

    _swiglu_accumulate(x_ref[...], wg_ref, wu_ref, (wd_ref,), o_ref)


def dense_swiglu(x, w_gate, w_up, w_down, residual, tm=1024, tf=512):
    T, D = x.shape
    F = w_gate.shape[2]
    tm = min(tm, T)
    return pl.pallas_call(
        _dense_swiglu_kernel,
        grid=(T // tm, F // tf),
        in_specs=[pl.BlockSpec((tm, D), lambda i, f: (i, 0)),
                  pl.BlockSpec((1, D, tf), lambda i, f: (0, 0, f)),
                  pl.BlockSpec((1, D, tf), lambda i, f: (0, 0, f)),
                  pl.BlockSpec((1, tf, D), lambda i, f: (0, f, 0)),
                  pl.BlockSpec((tm, D), lambda i, f: (i, 0), pipeline_mode=pl.Buffered(1))],
        out_specs=pl.BlockSpec((tm, D), lambda i, f: (i, 0)),
        out_shape=jax.ShapeDtypeStruct((T, D), F32),
        compiler_params=_cp("parallel", "arbitrary"),
        name="dense_swiglu",
    )(x, w_gate, w_up, w_down, residual)


def _row_copy(src_hbm, dst_ref, src_row, dst_row, sem):
    return pltpu.make_async_copy(src_hbm.at[pl.ds(src_row, 1)], dst_ref.at[pl.ds(dst_row, 1)], sem)


def _moe_rows_per_step(tm, nf):
    return -(-tm // (nf * SUBLANES)) * SUBLANES


def _moe_swiglu_kernel(tok_ref, be_ref, nu_ref, h_hbm, wg_ref, wu_ref, wd0_ref, wd1_ref, o_ref,
                       xg_ref, xb_ref, sem, *, tm, nf):
    del be_ref
    i = pl.program_id(0)
    f = pl.program_id(1)
    n_used = nu_ref[0]
    rows_per_step = _moe_rows_per_step(tm, nf)
    n_fetch = rows_per_step * nf

    def fetch(block, r, slot):
        tok = tok_ref[block * tm + jnp.minimum(r, tm - 1)]
        return _row_copy(h_hbm, xg_ref.at[slot], tok, r, sem.at[slot])

    def wait_rows(slot, n_rows):
        rows = xg_ref.at[slot, pl.ds(0, n_rows)]
        pltpu.make_async_copy(rows, rows, sem.at[slot]).wait()

    @pl.when((i >= n_used) & (f == 0))
    def _():
        o_ref[...] = jnp.zeros_like(o_ref)

    @pl.when(i < n_used)
    def _():
        slot = i % 2

        @pl.when((i == 0) & (f == 0))
        def _():
            def start(r, c):
                fetch(0, r, 0).start()
                return c
            lax.fori_loop(0, n_fetch, start, 0)
            for j in range(rows_per_step):
                fetch(jnp.minimum(1, n_used - 1), j, 1).start()

        @pl.when(f == 0)
        def _():
            wait_rows(slot, n_fetch)
            xb_ref[...] = xg_ref[slot, 0:tm, :].astype(BF16)
            o_ref[...] = jnp.zeros_like(o_ref)

        wrap = (f == nf - 1).astype(jnp.int32)
        chunk = (f + 1) * (1 - wrap)
        target = jnp.minimum(i + 1 + wrap, n_used - 1)
        target_slot = (i + 1 + wrap) % 2
        for j in range(rows_per_step):
            fetch(target, chunk * rows_per_step + j, target_slot).start()

        _swiglu_accumulate(xb_ref[...], wg_ref, wu_ref, (wd0_ref, wd1_ref), o_ref)

        @pl.when((i == n_used - 1) & (f == nf - 1))
        def _():
            wait_rows(1 - slot, n_fetch)
            wait_rows(slot, rows_per_step)


def moe_swiglu(h, slot_tok, block_e, n_used, w_gate, w_up, w_down_lo, w_down_hi, tm, tf=1024):
    n_slots = slot_tok.shape[0]
    D = h.shape[1]
    F = w_gate.shape[2]
    while F % tf:
        tf //= 2
    nf = F // tf
    xg_rows = _moe_rows_per_step(tm, nf) * nf
    blk = lambda i, nu: jnp.minimum(i, nu[0] - 1)
    chunk = lambda i, f, nu: jnp.where(i < nu[0], f, nf - 1)
    up_spec = pl.BlockSpec((1, D, tf), lambda i, f, tok, be, nu: (be[blk(i, nu)], 0, chunk(i, f, nu)))
    down_spec = pl.BlockSpec((1, tf, D // 2), lambda i, f, tok, be, nu: (be[blk(i, nu)], chunk(i, f, nu), 0))
    return pl.pallas_call(
        functools.partial(_moe_swiglu_kernel, tm=tm, nf=nf),
        grid_spec=pltpu.PrefetchScalarGridSpec(
            num_scalar_prefetch=3,
            grid=(n_slots // tm, nf),
            in_specs=[pl.BlockSpec(memory_space=pl.ANY), up_spec, up_spec, down_spec, down_spec],
            out_specs=pl.BlockSpec((tm, D), lambda i, f, tok, be, nu: (i, 0)),
            scratch_shapes=[pltpu.VMEM((2, xg_rows, D), F32),
                            pltpu.VMEM((tm, D), BF16),
                            pltpu.SemaphoreType.DMA((2,))]),
        out_shape=jax.ShapeDtypeStruct((n_slots, D), F32),
        compiler_params=_cp("arbitrary", "arbitrary"),
        name="moe_swiglu",
    )(slot_tok, block_e, n_used, h, w_gate, w_up, w_down_lo, w_down_hi)


def _combine_kernel(dest_ref, y_ref, gate_ref, ys_hbm, o_ref, buf_ref, sem, *, tb):
    base = pl.program_id(0) * tb

    def start(r, c):
        for k in range(TOP_K):
            _row_copy(ys_hbm, buf_ref.at[k], dest_ref[TOP_K * (base + r) + k], r, sem).start()
        return c

    lax.fori_loop(0, tb, start, 0, unroll=8)
    pltpu.make_async_copy(buf_ref, buf_ref, sem).wait()
    g = gate_ref[...]
    o_ref[...] = y_ref[...] + g[:, 0:1] * buf_ref[0] + g[:, 1:2] * buf_ref[1]


def combine_rows(y, gates, ys, dest, tb=256):
    T, D = y.shape
    return pl.pallas_call(
        functools.partial(_combine_kernel, tb=tb),
        grid_spec=pltpu.PrefetchScalarGridSpec(
            num_scalar_prefetch=1,
            grid=(T // tb,),
            in_specs=[pl.BlockSpec((tb, D), lambda i, d: (i, 0)),
                      pl.BlockSpec((tb, LANES), lambda i, d: (i, 0)),
                      pl.BlockSpec(memory_space=pl.ANY)],
            out_specs=pl.BlockSpec((tb, D), lambda i, d: (i, 0)),
            scratch_shapes=[pltpu.VMEM((TOP_K, tb, D), F32),
                            pltpu.SemaphoreType.DMA(())]),
        out_shape=jax.ShapeDtypeStruct((T, D), F32),
        compiler_params=_cp("arbitrary"),
        name="moe_combine",
    )(dest, y, gates, ys)


def _routing_tables(top_idx, T, block):
    A = T * TOP_K
    E = N_EXPERTS
    flat_e = top_idx.reshape(A)
    onehot = (flat_e[:, None] == jnp.arange(E, dtype=jnp.int32)[None, :]).astype(jnp.int32)
    csum = jnp.cumsum(onehot, axis=0)
    counts = csum[-1]
    rank = jnp.sum((csum - 1) * onehot, axis=1)
    padded = (counts + block - 1) // block * block
    pends = jnp.cumsum(padded)
    pstarts = pends - padded
    starts = jnp.cumsum(counts) - counts
    dest = (pstarts[flat_e] + rank).astype(jnp.int32)
    n_blocks = A // block + E
    n_slots = n_blocks * block
    order = jnp.argsort(flat_e, stable=True).astype(jnp.int32)
    slot = jnp.arange(n_slots, dtype=jnp.int32)
    slot_e = jnp.clip(jnp.searchsorted(pends, slot, side='right'), 0, E - 1).astype(jnp.int32)
    r = slot - pstarts[slot_e]
    valid = r < counts[slot_e]
    src = jnp.clip(starts[slot_e] + r, 0, A - 1)
    slot_tok = jnp.where(valid, order[src] // TOP_K, 0).astype(jnp.int32)
    block_e = slot_e[::block]
    n_used = (pends[-1:] // block).astype(jnp.int32)
    return slot_tok, dest, block_e, n_used


def _token_mixer(y, h, mem2d, tab, B, S, M, l, casts, w_in, mla_q_a_norm_g, mla_w_uq, mla_kv_a_norm_g, mla_w_ukv,
                 mla_q_norm_g, mla_k_norm_g, mla_w_out, pool_w, pool_scale, pool_w_out,
                 mem_norm_g, mem_w_kv, mem_q_norm_g, mem_k_norm_g, mem_w_out, w_o):
    D = h.shape[1]
    za = matmul(h, w_in, l, ZA_WIDTH, F32)

    wuq = jnp.pad(mla_w_uq[l].reshape(Q_LORA_RANK, MLA_HEADS, QK_HEAD_DIM),
                  ((0, 0), (0, 0), (0, HEAD_PAD - QK_HEAD_DIM))).reshape(Q_LORA_RANK, MLA_HEADS * HEAD_PAD)
    wukv = mla_w_ukv[l].reshape(KV_LORA_RANK, MLA_HEADS, QK_NOPE_DIM + V_HEAD_DIM)
    wuk = wukv[:, :, :QK_NOPE_DIM].reshape(KV_LORA_RANK, MLA_WIDTH)
    wuv = wukv[:, :, QK_NOPE_DIM:].reshape(KV_LORA_RANK, MLA_WIDTH)
    pad_g = lambda g: jnp.pad(g, (0, HEAD_PAD - QK_HEAD_DIM)).reshape(1, HEAD_PAD)
    q, k, v = mla_prep(za, tab, wuq.astype(BF16), wuk.astype(BF16), wuv.astype(BF16),
                       mla_q_a_norm_g[l].reshape(1, -1), mla_kv_a_norm_g[l].reshape(1, -1),
                       pad_g(mla_q_norm_g[l]), pad_g(mla_k_norm_g[l]))
    attn, *cast_out = causal_attention(q, k, v, B, S, casts)

    mixed = pool_mixer(za, pool_w[l].astype(BF16), pool_scale[l].reshape(1, -1), B, S)

    k_m, v_m = mem_kv(mem2d, mem_norm_g[l].reshape(1, -1), mem_w_kv[l].astype(BF16),
                      mem_k_norm_g[l].reshape(1, -1), B, M)
    o_mem = mem_attention(za, k_m, v_m, mem_q_norm_g[l].reshape(1, -1), B, S, M)

    merged = merge_branches(h, attn, mixed, o_mem, w_in, mla_w_out, pool_w_out, mem_w_out, l)
    return matmul(merged, w_o, l, D, F32, residual=y), cast_out


def kernel(x, mem, positions, attn_norm_g, w_in, mla_q_a_norm_g, mla_w_uq, mla_kv_a_norm_g, mla_w_ukv, mla_q_norm_g, mla_k_norm_g, mla_w_out, pool_w, pool_scale, pool_w_out, mem_norm_g, mem_w_kv, mem_q_norm_g, mem_k_norm_g, mem_w_out, w_o, ffn_norm_g, dense_w_gate, dense_w_up, dense_w_down, router_w, router_b, moe_w_gate, moe_w_up, moe_w_down):
    B, S, D = x.shape
    M = mem.shape[1]
    T = B * S
    depth = attn_norm_g.shape[0]
    moe_block = 512
    y = x.reshape(T, D)
    mem2d = mem.reshape(B * M, D)
    tab = rope_tables(positions)
    w_in = prep_w_in(w_in)
    mla_w_out, pool_w_out, mem_w_out, w_o = (w.astype(BF16) for w in (mla_w_out, pool_w_out, mem_w_out, w_o))
    moe_bf16 = {}
    for l in range(depth):
        h = rmsnorm(y, attn_norm_g[l], BF16)
        i = l // 2
        casts = ()
        if l % 2 == 0 and l + 1 < depth:
            casts = ((moe_w_gate[i], 0, 1), (moe_w_down[i], 0, 2))
        elif l % 2 == 1:
            casts = ((moe_w_up[i], 0, 1), (moe_w_down[i], 1, 2))
        y, cast_out = _token_mixer(y, h, mem2d, tab, B, S, M, l, casts, w_in, mla_q_a_norm_g, mla_w_uq,
                                   mla_kv_a_norm_g, mla_w_ukv, mla_q_norm_g, mla_k_norm_g, mla_w_out, pool_w,
                                   pool_scale, pool_w_out, mem_norm_g, mem_w_kv, mem_q_norm_g, mem_k_norm_g,
                                   mem_w_out, w_o)
        if casts:
            names = ("gate", "down_lo") if l % 2 == 0 else ("up", "down_hi")
            moe_bf16.update(zip(names, cast_out))
        if l % 2 == 0:
            h = rmsnorm(y, ffn_norm_g[l], BF16)
            y = dense_swiglu(h, dense_w_gate[i:i + 1].astype(BF16), dense_w_up[i:i + 1].astype(BF16),
                             dense_w_down[i:i + 1].astype(BF16), y)
        else:
            hf, idx, gates = norm_router(y, ffn_norm_g[l], router_w[i], router_b[i])
            slot_tok, dest, block_e, n_used = _routing_tables(idx[:, :TOP_K], T, moe_block)
            ys = moe_swiglu(hf, slot_tok, block_e, n_used, moe_bf16.pop("gate"), moe_bf16.pop("up"),
                            moe_bf16.pop("down_lo"), moe_bf16.pop("down_hi"), tm=moe_block)
            y = combine_rows(y, gates, ys, dest)
    return y.reshape(B, S, D)
```

```python
import functools

import numpy as np
import jax
import jax.numpy as jnp
from jax import lax
from jax.experimental import pallas as pl
from jax.experimental.pallas import tpu as pltpu

F32 = jnp.float32
BF16 = jnp.bfloat16

MEM_HEADS = 4
MEM_HEAD_DIM = 256
MEM_WIDTH = MEM_HEADS * MEM_HEAD_DIM
MLA_HEADS = 8
Q_LORA_RANK = 512
KV_LORA_RANK = 256
QK_NOPE_DIM = 128
QK_ROPE_DIM = 64
QK_HEAD_DIM = QK_NOPE_DIM + QK_ROPE_DIM
V_HEAD_DIM = 128
MLA_WIDTH = MLA_HEADS * V_HEAD_DIM
ROPE_THETA = 10000.0
POOL_WINDOWS = (2, 4, 8, 16)
POOL_GROUPS = 4
POOL_GROUP_DIM = 256
POOL_WIDTH = POOL_GROUPS * POOL_GROUP_DIM
N_BRANCHES = 3
N_EXPERTS = 8
TOP_K = 2
NORM_EPS = 1e-6

LANES = 128
SUBLANES = 8
HEAD_PAD = 2 * LANES
ROPE_HALF = QK_ROPE_DIM // 2
POOL_HALO = 16

ZA_POOL = 0
ZA_QMEM = POOL_WIDTH
ZA_CQ = ZA_QMEM + MEM_WIDTH
ZA_CKV = ZA_CQ + Q_LORA_RANK
ZA_KPE = ZA_CKV + KV_LORA_RANK
ZA_WIDTH = 3072

VMEM_LIMIT = 52 * 1024 * 1024


def _cp(*sem):
    return pltpu.CompilerParams(dimension_semantics=sem, vmem_limit_bytes=VMEM_LIMIT)


def _rms(x, eps_dim):
    return lax.rsqrt(jnp.sum(x * x, axis=-1, keepdims=True) * (1.0 / eps_dim) + NORM_EPS)


def _rmsnorm_kernel(x_ref, g_ref, o_ref):
    x = x_ref[...]
    o_ref[...] = (x * _rms(x, x.shape[-1]) * g_ref[...]).astype(o_ref.dtype)


def rmsnorm(x, g, out_dtype, tm=512):
    T, D = x.shape
    return pl.pallas_call(
        _rmsnorm_kernel,
        grid=(T // tm,),
        in_specs=[pl.BlockSpec((tm, D), lambda i: (i, 0)),
                  pl.BlockSpec((1, D), lambda i: (0, 0))],
        out_specs=pl.BlockSpec((tm, D), lambda i: (i, 0)),
        out_shape=jax.ShapeDtypeStruct((T, D), out_dtype),
        compiler_params=_cp("parallel"),
        name="rmsnorm",
    )(x, g.reshape(1, D))


def _norm_router_kernel(x_ref, g_ref, rwh_ref, rwl_ref, rb_ref, h_ref, idx_ref, gate_ref):
    x = x_ref[...]
    h = x * _rms(x, x.shape[-1]) * g_ref[...]
    h_ref[...] = h
    h_hi = h.astype(BF16)
    h_lo = (h - h_hi.astype(F32)).astype(BF16)
    logits = (jnp.dot(h_hi, rwh_ref[...], preferred_element_type=F32)
              + jnp.dot(h_lo, rwh_ref[...], preferred_element_type=F32)
              + jnp.dot(h_hi, rwl_ref[...], preferred_element_type=F32)) + rb_ref[...]
    lane = lax.broadcasted_iota(jnp.int32, logits.shape, 1)
    neg = -jnp.inf
    l1 = jnp.where(lane < N_EXPERTS, logits, neg)
    m1 = jnp.max(l1, axis=-1, keepdims=True)
    i1 = jnp.min(jnp.where(l1 == m1, lane, LANES), axis=-1, keepdims=True)
    l2 = jnp.where(lane == i1, neg, l1)
    m2 = jnp.max(l2, axis=-1, keepdims=True)
    i2 = jnp.min(jnp.where(l2 == m2, lane, LANES), axis=-1, keepdims=True)
    e = jnp.exp(m2 - m1)
    g1 = 1.0 / (1.0 + e)
    g2 = e / (1.0 + e)
    idx_ref[...] = jnp.where(lane == 0, i1, jnp.where(lane == 1, i2, 0))
    gate_ref[...] = jnp.where(lane == 0, g1, jnp.where(lane == 1, g2, 0.0))


def norm_router(x, g, router_w, router_b, tm=512):
    T, D = x.shape
    E = router_w.shape[1]
    rw = jnp.zeros((D, LANES), F32).at[:, :E].set(router_w)
    rw_hi = rw.astype(BF16)
    rw_lo = (rw - rw_hi.astype(F32)).astype(BF16)
    rb = jnp.zeros((1, LANES), F32).at[0, :E].set(router_b)
    return pl.pallas_call(
        _norm_router_kernel,
        grid=(T // tm,),
        in_specs=[pl.BlockSpec((tm, D), lambda i: (i, 0)),
                  pl.BlockSpec((1, D), lambda i: (0, 0)),
                  pl.BlockSpec((D, LANES), lambda i: (0, 0)),
                  pl.BlockSpec((D, LANES), lambda i: (0, 0)),
                  pl.BlockSpec((1, LANES), lambda i: (0, 0))],
        out_specs=[pl.BlockSpec((tm, D), lambda i: (i, 0)),
                   pl.BlockSpec((tm, LANES), lambda i: (i, 0)),
                   pl.BlockSpec((tm, LANES), lambda i: (i, 0))],
        out_shape=[jax.ShapeDtypeStruct((T, D), F32),
                   jax.ShapeDtypeStruct((T, LANES), jnp.int32),
                   jax.ShapeDtypeStruct((T, LANES), F32)],
        compiler_params=_cp("parallel"),
        name="norm_router",
    )(x, g.reshape(1, D), rw_hi, rw_lo, rb)


def _mm_kernel(a_ref, w_ref, o_ref):
    o_ref[...] = jnp.dot(a_ref[...], w_ref[0], preferred_element_type=F32).astype(o_ref.dtype)


def _mm_res_kernel(a_ref, w_ref, r_ref, o_ref):
    o_ref[...] = (r_ref[...] + jnp.dot(a_ref[...], w_ref[0], preferred_element_type=F32)).astype(o_ref.dtype)


def matmul(a, w, l, N, out_dtype, residual=None, tm=1024, tn=1024):
    M, K = a.shape
    tm, tn = min(tm, M), min(tn, N)
    in_specs = [pl.BlockSpec((tm, K), lambda i, j: (i, 0)),
                pl.BlockSpec((1, K, tn), lambda i, j: (l, 0, j))]
    args = [a, w]
    body = _mm_kernel
    if residual is not None:
        in_specs.append(pl.BlockSpec((tm, tn), lambda i, j: (i, j)))
        args.append(residual)
        body = _mm_res_kernel
    return pl.pallas_call(
        body,
        grid=(M // tm, N // tn),
        in_specs=in_specs,
        out_specs=pl.BlockSpec((tm, tn), lambda i, j: (i, j)),
        out_shape=jax.ShapeDtypeStruct((M, N), out_dtype),
        compiler_params=_cp("parallel", "arbitrary"),
        name="matmul",
    )(*args)


def _mm_res_norm_kernel(a_ref, w_ref, r_ref, g_ref, o_ref, h_ref):
    y = r_ref[...] + jnp.dot(a_ref[...], w_ref[0], preferred_element_type=F32)
    o_ref[...] = y
    h_ref[...] = (y * _rms(y, y.shape[-1]) * g_ref[...]).astype(h_ref.dtype)


def matmul_residual_norm(a, w, l, residual, g, tm=512):
    M, K = a.shape
    N = w.shape[2]
    rows = lambda width: pl.BlockSpec((tm, width), lambda i: (i, 0))
    return pl.pallas_call(
        _mm_res_norm_kernel,
        grid=(M // tm,),
        in_specs=[rows(K), pl.BlockSpec((1, K, N), lambda i: (l, 0, 0)), rows(N),
                  pl.BlockSpec((1, N), lambda i: (0, 0))],
        out_specs=[rows(N), rows(N)],
        out_shape=[jax.ShapeDtypeStruct((M, N), F32), jax.ShapeDtypeStruct((M, N), BF16)],
        compiler_params=_cp("parallel"),
        name="matmul_residual_norm",
    )(a, w, residual, g.reshape(1, N))


def _rope_table_kernel(pos_ref, invf_ref, o_ref):
    ang = pos_ref[...].astype(F32) * invf_ref[...]
    lane = lax.broadcasted_iota(jnp.int32, ang.shape, 1)
    c = jnp.cos(ang)
    s = jnp.sin(ang)
    o_ref[:, 0:LANES] = jnp.where(lane < QK_ROPE_DIM, c, 0.0)
    o_ref[:, LANES:2 * LANES] = jnp.where(lane < ROPE_HALF, -s, 0.0)
    o_ref[:, 2 * LANES:3 * LANES] = jnp.where((lane >= ROPE_HALF) & (lane < QK_ROPE_DIM), s, 0.0)


def rope_tables(positions, tm=512):
    T = positions.size
    inv_freq = ROPE_THETA ** (-np.arange(0, QK_ROPE_DIM, 2, dtype=np.float32) / QK_ROPE_DIM)
    invf = np.zeros((1, LANES), np.float32)
    invf[0, :ROPE_HALF] = inv_freq
    invf[0, ROPE_HALF:QK_ROPE_DIM] = inv_freq
    return pl.pallas_call(
        _rope_table_kernel,
        grid=(T // tm,),
        in_specs=[pl.BlockSpec((tm, 1), lambda i: (i, 0)),
                  pl.BlockSpec((1, LANES), lambda i: (0, 0))],
        out_specs=pl.BlockSpec((tm, 3 * LANES), lambda i: (i, 0)),
        out_shape=jax.ShapeDtypeStruct((T, 3 * LANES), F32),
        compiler_params=_cp("parallel"),
        name="rope_tables",
    )(positions.reshape(T, 1), jnp.asarray(invf))


def _rope(x, tab):
    c = tab[:, 0:LANES]
    s1 = tab[:, LANES:2 * LANES]
    s2 = tab[:, 2 * LANES:3 * LANES]
    return x * c + pltpu.roll(x, LANES - ROPE_HALF, 1) * s1 + pltpu.roll(x, ROPE_HALF, 1) * s2


def _mla_prep_kernel(cq_ref, ckv_ref, kpe_ref, tab_ref, wuq_ref, wuk_ref, wuv_ref,
                     gqa_ref, gkva_ref, gq_ref, gk_ref, q_ref, k_ref, v_ref):
    tab = tab_ref[...]
    cq = cq_ref[...]
    cqn = (cq * _rms(cq, Q_LORA_RANK) * gqa_ref[...]).astype(BF16)
    qf = jnp.dot(cqn, wuq_ref[...], preferred_element_type=F32)
    ckv = ckv_ref[...]
    ckvn = (ckv * _rms(ckv, KV_LORA_RANK) * gkva_ref[...]).astype(BF16)
    kn = jnp.dot(ckvn, wuk_ref[...], preferred_element_type=F32)
    v_ref[...] = jnp.dot(ckvn, wuv_ref[...], preferred_element_type=F32).astype(v_ref.dtype)
    gq = gq_ref[...]
    gk = gk_ref[...]
    kpe = kpe_ref[...]
    ss_pe = jnp.sum(kpe * kpe, axis=-1, keepdims=True)
    kpe_rot = _rope(kpe * gk[:, LANES:], tab)
    scale = QK_HEAD_DIM ** -0.5
    for h in range(MLA_HEADS):
        q0 = qf[:, h * HEAD_PAD:h * HEAD_PAD + LANES]
        q1 = qf[:, h * HEAD_PAD + LANES:(h + 1) * HEAD_PAD]
        ss = jnp.sum(q0 * q0, axis=-1, keepdims=True) + jnp.sum(q1 * q1, axis=-1, keepdims=True)
        rq = lax.rsqrt(ss * (1.0 / QK_HEAD_DIM) + NORM_EPS) * scale
        q_ref[:, h * HEAD_PAD:h * HEAD_PAD + LANES] = (q0 * gq[:, :LANES] * rq).astype(q_ref.dtype)
        q_ref[:, h * HEAD_PAD + LANES:(h + 1) * HEAD_PAD] = (_rope(q1 * gq[:, LANES:], tab) * rq).astype(q_ref.dtype)
        k0 = kn[:, h * LANES:(h + 1) * LANES]
        ssk = jnp.sum(k0 * k0, axis=-1, keepdims=True) + ss_pe
        rk = lax.rsqrt(ssk * (1.0 / QK_HEAD_DIM) + NORM_EPS)
        k_ref[:, h * HEAD_PAD:h * HEAD_PAD + LANES] = (k0 * gk[:, :LANES] * rk).astype(k_ref.dtype)
        k_ref[:, h * HEAD_PAD + LANES:(h + 1) * HEAD_PAD] = (kpe_rot * rk).astype(k_ref.dtype)


def mla_prep(za, tab, wuq, wuk, wuv, gqa, gkva, gq, gk, tm=512):
    T = za.shape[0]
    HP = MLA_HEADS * HEAD_PAD
    full = lambda shape: pl.BlockSpec(shape, lambda i: (0, 0))
    return pl.pallas_call(
        _mla_prep_kernel,
        grid=(T // tm,),
        in_specs=[pl.BlockSpec((tm, Q_LORA_RANK), lambda i: (i, ZA_CQ // Q_LORA_RANK)),
                  pl.BlockSpec((tm, KV_LORA_RANK), lambda i: (i, ZA_CKV // KV_LORA_RANK)),
                  pl.BlockSpec((tm, LANES), lambda i: (i, ZA_KPE // LANES)),
                  pl.BlockSpec((tm, 3 * LANES), lambda i: (i, 0)),
                  full((Q_LORA_RANK, HP)), full((KV_LORA_RANK, MLA_WIDTH)), full((KV_LORA_RANK, MLA_WIDTH)),
                  full((1, Q_LORA_RANK)), full((1, KV_LORA_RANK)), full((1, HEAD_PAD)), full((1, HEAD_PAD))],
        out_specs=[pl.BlockSpec((tm, HP), lambda i: (i, 0)),
                   pl.BlockSpec((tm, HP), lambda i: (i, 0)),
                   pl.BlockSpec((tm, MLA_WIDTH), lambda i: (i, 0))],
        out_shape=[jax.ShapeDtypeStruct((T, HP), BF16),
                   jax.ShapeDtypeStruct((T, HP), BF16),
                   jax.ShapeDtypeStruct((T, MLA_WIDTH), BF16)],
        compiler_params=_cp("parallel"),
        name="mla_prep",
    )(za, za, za, tab, wuq, wuk, wuv, gqa, gkva, gq, gk)


def _causal_attn_kernel(q_ref, k_ref, v_ref, *rest, tq, n_cast):
    cast_src = rest[:n_cast]
    o_ref = rest[n_cast]
    cast_dst = rest[n_cast + 1:]
    for s_ref, d_ref in zip(cast_src, cast_dst):
        d_ref[...] = s_ref[...].astype(d_ref.dtype)
    S = q_ref.shape[0]
    row = lax.broadcasted_iota(jnp.int32, (tq, tq), 0)
    col = lax.broadcasted_iota(jnp.int32, (tq, tq), 1)
    tri = col <= row
    for qi in range(S // tq):
        kl = (qi + 1) * tq
        q = q_ref[qi * tq:kl, :]
        s = lax.dot_general(q, k_ref[0:kl, :], (((1,), (1,)), ((), ())), preferred_element_type=F32)
        s_diag = jnp.where(tri, s[:, kl - tq:kl], -jnp.inf)
        if qi > 0:
            s = jnp.concatenate([s[:, :kl - tq], s_diag], axis=1)
        else:
            s = s_diag
        m = jnp.max(s, axis=-1, keepdims=True)
        p = jnp.exp(s - m)
        l = jnp.sum(p, axis=-1, keepdims=True)
        o = jnp.dot(p.astype(BF16), v_ref[0:kl, :], preferred_element_type=F32)
        o_ref[qi * tq:kl, :] = (o / l).astype(o_ref.dtype)


def causal_attention(q, k, v, B, S, casts=(), tq=256):
    T = q.shape[0]
    n_steps = B * MLA_HEADS
    in_specs = [pl.BlockSpec((S, HEAD_PAD), lambda b, h: (b, h)),
                pl.BlockSpec((S, HEAD_PAD), lambda b, h: (b, h)),
                pl.BlockSpec((S, V_HEAD_DIM), lambda b, h: (b, h))]
    out_specs = [pl.BlockSpec((S, V_HEAD_DIM), lambda b, h: (b, h))]
    out_shape = [jax.ShapeDtypeStruct((T, MLA_WIDTH), BF16)]
    for src, part, n_parts in casts:
        E, rows, cols = src.shape
        spe = n_steps // E
        rb, cb = rows // spe, cols // n_parts
        assert spe * E == n_steps and rb * spe == rows and rb % 16 == 0
        assert cb * n_parts == cols and cb % LANES == 0
        step = lambda b, h: b * MLA_HEADS + h
        in_specs.append(pl.BlockSpec((1, rb, cb), lambda b, h, part=part, spe=spe:
                                     (step(b, h) // spe, step(b, h) % spe, part)))
        out_specs.append(pl.BlockSpec((1, rb, cb), lambda b, h, spe=spe:
                                      (step(b, h) // spe, step(b, h) % spe, 0)))
        out_shape.append(jax.ShapeDtypeStruct((E, rows, cb), BF16))
    return pl.pallas_call(
        functools.partial(_causal_attn_kernel, tq=tq, n_cast=len(casts)),
        grid=(B, MLA_HEADS),
        in_specs=in_specs,
        out_specs=out_specs,
        out_shape=out_shape,
        compiler_params=_cp("parallel", "parallel"),
        name="causal_attention",
    )(q, k, v, *(c[0] for c in casts))


def _pool_kernel(u_ref, w_ref, sc_ref, o_ref, pad_ref, *, rc):
    S = u_ref.shape[0]
    C = POOL_GROUP_DIM
    t1 = lax.broadcasted_iota(jnp.int32, (rc, C), 0).astype(F32) + 1.0
    for g, win in enumerate(POOL_WINDOWS):
        pad_ref[0:POOL_HALO, :] = jnp.zeros((POOL_HALO, C), F32)
        pad_ref[POOL_HALO:POOL_HALO + S, :] = u_ref[:, g * C:(g + 1) * C]
        for r0 in range(0, S, rc):
            acc = pad_ref[POOL_HALO + r0:POOL_HALO + r0 + rc, :]
            tok = acc
            for kk in range(1, win):
                acc = acc + pad_ref[POOL_HALO + r0 - kk:POOL_HALO + r0 - kk + rc, :]
            cnt = jnp.minimum(t1 + float(r0), float(win))
            pooled = (acc / cnt - tok).astype(BF16)
            mixed = jnp.dot(pooled, w_ref[g], preferred_element_type=F32) * sc_ref[:, g * C:(g + 1) * C]
            o_ref[r0:r0 + rc, g * C:(g + 1) * C] = mixed.astype(o_ref.dtype)


def pool_mixer(za, pool_w, pool_scale, B, S, rc=512):
    T = za.shape[0]
    rc = min(rc, S)
    return pl.pallas_call(
        functools.partial(_pool_kernel, rc=rc),
        grid=(B,),
        in_specs=[pl.BlockSpec((S, POOL_WIDTH), lambda b: (b, ZA_POOL // POOL_WIDTH)),
                  pl.BlockSpec((POOL_GROUPS, POOL_GROUP_DIM, POOL_GROUP_DIM), lambda b: (0, 0, 0)),
                  pl.BlockSpec((1, POOL_WIDTH), lambda b: (0, 0))],
        out_specs=pl.BlockSpec((S, POOL_WIDTH), lambda b: (b, 0)),
        out_shape=jax.ShapeDtypeStruct((T, POOL_WIDTH), BF16),
        scratch_shapes=[pltpu.VMEM((POOL_HALO + S, POOL_GROUP_DIM), F32)],
        compiler_params=_cp("parallel"),
        name="pool_mixer",
    )(za, pool_w, pool_scale)


def _mem_kv_kernel(m_ref, g_ref, w_ref, gk_ref, k_ref, v_ref):
    x = m_ref[...]
    xn = (x * _rms(x, x.shape[-1]) * g_ref[...]).astype(BF16)
    kv = jnp.dot(xn, w_ref[...], preferred_element_type=F32)
    for h in range(MEM_HEADS):
        kh = kv[:, h * MEM_HEAD_DIM:(h + 1) * MEM_HEAD_DIM]
        k_ref[:, h * MEM_HEAD_DIM:(h + 1) * MEM_HEAD_DIM] = (
            kh * _rms(kh, MEM_HEAD_DIM) * gk_ref[...]).astype(k_ref.dtype)
    v_ref[...] = kv[:, MEM_WIDTH:].astype(v_ref.dtype)


def mem_kv(mem2d, g, w_kv, gk, B, M):
    D = mem2d.shape[1]
    return pl.pallas_call(
        _mem_kv_kernel,
        grid=(B,),
        in_specs=[pl.BlockSpec((M, D), lambda b: (b, 0)),
                  pl.BlockSpec((1, D), lambda b: (0, 0)),
                  pl.BlockSpec((D, 2 * MEM_WIDTH), lambda b: (0, 0)),
                  pl.BlockSpec((1, MEM_HEAD_DIM), lambda b: (0, 0))],
        out_specs=[pl.BlockSpec((M, MEM_WIDTH), lambda b: (b, 0)),
                   pl.BlockSpec((M, MEM_WIDTH), lambda b: (b, 0))],
        out_shape=[jax.ShapeDtypeStruct((B * M, MEM_WIDTH), BF16),
                   jax.ShapeDtypeStruct((B * M, MEM_WIDTH), BF16)],
        compiler_params=_cp("parallel"),
        name="mem_kv",
    )(mem2d, g, w_kv, gk)


def _mem_attn_kernel(q_ref, k_ref, v_ref, gq_ref, o_ref):
    scale = MEM_HEAD_DIM ** -0.5
    for h in range(MEM_HEADS):
        sl = slice(h * MEM_HEAD_DIM, (h + 1) * MEM_HEAD_DIM)
        qh = q_ref[:, sl]
        qn = (qh * (_rms(qh, MEM_HEAD_DIM) * scale) * gq_ref[...]).astype(BF16)
        s = lax.dot_general(qn, k_ref[:, sl], (((1,), (1,)), ((), ())), preferred_element_type=F32)
        m = jnp.max(s, axis=-1, keepdims=True)
        p = jnp.exp(s - m)
        l = jnp.sum(p, axis=-1, keepdims=True)
        o = jnp.dot(p.astype(BF16), v_ref[:, sl], preferred_element_type=F32)
        o_ref[:, sl] = (o / l).astype(o_ref.dtype)


def mem_attention(za, k_m, v_m, gq, B, S, M, tq=512):
    T = za.shape[0]
    tq = min(tq, S)
    nq = S // tq
    return pl.pallas_call(
        _mem_attn_kernel,
        grid=(B, nq),
        in_specs=[pl.BlockSpec((tq, MEM_WIDTH), lambda b, i: (b * nq + i, ZA_QMEM // MEM_WIDTH)),
                  pl.BlockSpec((M, MEM_WIDTH), lambda b, i: (b, 0)),
                  pl.BlockSpec((M, MEM_WIDTH), lambda b, i: (b, 0)),
                  pl.BlockSpec((1, MEM_HEAD_DIM), lambda b, i: (0, 0))],
        out_specs=pl.BlockSpec((tq, MEM_WIDTH), lambda b, i: (b * nq + i, 0)),
        out_shape=jax.ShapeDtypeStruct((T, MEM_WIDTH), BF16),
        compiler_params=_cp("parallel", "arbitrary"),
        name="mem_attention",
    )(za, k_m, v_m, gq)


def _merge_kernel(h_ref, a_ref, p_ref, m_ref, g0_ref, g1_ref, g2_ref, w0_ref, w1_ref, w2_ref, o_ref):
    h = h_ref[...]
    acc = None
    for x_ref, g_ref, w_ref in ((a_ref, g0_ref, w0_ref), (p_ref, g1_ref, w1_ref), (m_ref, g2_ref, w2_ref)):
        logit = jnp.dot(h, g_ref[0], preferred_element_type=F32)
        gate = 1.0 / (1.0 + jnp.exp(-logit))
        br = jnp.dot(x_ref[...], w_ref[0], preferred_element_type=F32)
        acc = gate * br if acc is None else acc + gate * br
    o_ref[...] = acc.astype(o_ref.dtype)


def merge_branches(h, a, p, m, w_all, w_a, w_p, w_m, l, tm=512, tn=512):
    T, D = h.shape
    W = a.shape[1]
    nj = D // tn
    g0 = ZA_WIDTH // tn
    row = lambda width: pl.BlockSpec((tm, width), lambda i, j: (i, 0))
    gate_spec = lambda b: pl.BlockSpec((1, D, tn), lambda i, j, b=b: (l, 0, g0 + b * nj + j))
    out_w = pl.BlockSpec((1, W, tn), lambda i, j: (l, 0, j))
    return pl.pallas_call(
        _merge_kernel,
        grid=(T // tm, nj),
        in_specs=[row(D), row(W), row(W), row(W),
                  gate_spec(0), gate_spec(1), gate_spec(2), out_w, out_w, out_w],
        out_specs=pl.BlockSpec((tm, tn), lambda i, j: (i, j)),
        out_shape=jax.ShapeDtypeStruct((T, D), BF16),
        compiler_params=_cp("parallel", "arbitrary"),
        name="merge_branches",
    )(h, a, p, m, w_all, w_all, w_all, w_a, w_p, w_m)


def _w_in_prep_kernel(w_ref, o_ref):
    s1 = Q_LORA_RANK + KV_LORA_RANK + QK_ROPE_DIM
    s2 = s1 + POOL_WIDTH
    s3 = s2 + MEM_WIDTH
    tk = w_ref.shape[2]
    s1_up = -(-s1 // LANES) * LANES
    o_ref[0, :, ZA_POOL:ZA_POOL + POOL_WIDTH] = w_ref[0, s1:s2, :].T.astype(BF16)
    o_ref[0, :, ZA_QMEM:ZA_QMEM + MEM_WIDTH] = w_ref[0, s2:s3, :].T.astype(BF16)
    o_ref[0, :, ZA_CQ:ZA_CQ + s1] = w_ref[0, 0:s1_up, :].T[:, 0:s1].astype(BF16)
    o_ref[0, :, ZA_CQ + s1:ZA_WIDTH] = jnp.zeros((tk, ZA_WIDTH - ZA_CQ - s1), BF16)
    o_ref[0, :, ZA_WIDTH:] = w_ref[0, s3:, :].T.astype(BF16)


def prep_w_in(w_in, tk=256):
    L, D, win = w_in.shape
    wout = ZA_WIDTH + N_BRANCHES * D
    return pl.pallas_call(
        _w_in_prep_kernel,
        grid=(L, D // tk),
        in_specs=[pl.BlockSpec((1, win, tk), lambda l, i: (l, 0, i))],
        out_specs=pl.BlockSpec((1, tk, wout), lambda l, i: (l, i, 0)),
        out_shape=jax.ShapeDtypeStruct((L, D, wout), BF16),
        compiler_params=_cp("parallel", "parallel"),
        name="prep_w_in",
    )(jnp.swapaxes(w_in, 1, 2))


def _swiglu_accumulate(xb, wg_ref, wu_ref, wd_refs, o_ref):
    a = jnp.dot(xb, wg_ref[0], preferred_element_type=F32)
    b = jnp.dot(xb, wu_ref[0], preferred_element_type=F32)
    act = (a / (1.0 + jnp.exp(-a)) * b).astype(BF16)
    width = o_ref.shape[1] // len(wd_refs)
    for n, wd_ref in enumerate(wd_refs):
        o_ref[:, n * width:(n + 1) * width] += jnp.dot(act, wd_ref[0], preferred_element_type=F32)


def _dense_swiglu_kernel(x_ref, wg_ref, wu_ref, wd_ref, r_ref, o_ref):
    @pl.when(pl.program_id(1) == 0)
    def _():
        o_ref[...] = r_ref[...]

    _swiglu_accumulate(x_ref[...], wg_ref, wu_ref, (wd_ref,), o_ref)


def dense_swiglu(x, w_gate, w_up, w_down, residual, tm=1024, tf=512):
    T, D = x.shape
    F = w_gate.shape[2]
    tm = min(tm, T)
    return pl.pallas_call(
        _dense_swiglu_kernel,
        grid=(T // tm, F // tf),
        in_specs=[pl.BlockSpec((tm, D), lambda i, f: (i, 0)),
                  pl.BlockSpec((1, D, tf), lambda i, f: (0, 0, f)),
                  pl.BlockSpec((1, D, tf), lambda i, f: (0, 0, f)),
                  pl.BlockSpec((1, tf, D), lambda i, f: (0, f, 0)),
                  pl.BlockSpec((tm, D), lambda i, f: (i, 0), pipeline_mode=pl.Buffered(1))],
        out_specs=pl.BlockSpec((tm, D), lambda i, f: (i, 0)),
        out_shape=jax.ShapeDtypeStruct((T, D), F32),
        compiler_params=_cp("parallel", "arbitrary"),
        name="dense_swiglu",
    )(x, w_gate, w_up, w_down, residual)


def _row_copy(src_hbm, dst_ref, src_row, dst_row, sem):
    return pltpu.make_async_copy(src_hbm.at[pl.ds(src_row, 1)], dst_ref.at[pl.ds(dst_row, 1)], sem)


def _moe_rows_per_step(tm, nf):
    return -(-tm // (nf * SUBLANES)) * SUBLANES


def _moe_swiglu_kernel(tok_ref, be_ref, nu_ref, h_hbm, wg_ref, wu_ref, wd0_ref, wd1_ref, o_ref,
                       xg_ref, xb_ref, sem, *, tm, nf):
    del be_ref
    i = pl.program_id(0)
    f = pl.program_id(1)
    n_used = nu_ref[0]
    rows_per_step = _moe_rows_per_step(tm, nf)
    n_fetch = rows_per_step * nf

    def fetch(block, r, slot):
        tok = tok_ref[block * tm + jnp.minimum(r, tm - 1)]
        return _row_copy(h_hbm, xg_ref.at[slot], tok, r, sem.at[slot])

    def wait_rows(slot, n_rows):
        rows = xg_ref.at[slot, pl.ds(0, n_rows)]
        pltpu.make_async_copy(rows, rows, sem.at[slot]).wait()

    @pl.when((i >= n_used) & (f == 0))
    def _():
        o_ref[...] = jnp.zeros_like(o_ref)

    @pl.when(i < n_used)
    def _():
        slot = i % 2

        @pl.when((i == 0) & (f == 0))
        def _():
            def start(r, c):
                fetch(0, r, 0).start()
                return c
            lax.fori_loop(0, n_fetch, start, 0)
            for j in range(rows_per_step):
                fetch(jnp.minimum(1, n_used - 1), j, 1).start()

        @pl.when(f == 0)
        def _():
            wait_rows(slot, n_fetch)
            xb_ref[...] = xg_ref[slot, 0:tm, :].astype(BF16)
            o_ref[...] = jnp.zeros_like(o_ref)

        wrap = (f == nf - 1).astype(jnp.int32)
        chunk = (f + 1) * (1 - wrap)
        target = jnp.minimum(i + 1 + wrap, n_used - 1)
        target_slot = (i + 1 + wrap) % 2
        for j in range(rows_per_step):
            fetch(target, chunk * rows_per_step + j, target_slot).start()

        _swiglu_accumulate(xb_ref[...], wg_ref, wu_ref, (wd0_ref, wd1_ref), o_ref)

        @pl.when((i == n_used - 1) & (f == nf - 1))
        def _():
            wait_rows(1 - slot, n_fetch)
            wait_rows(slot, rows_per_step)


def moe_swiglu(h, slot_tok, block_e, n_used, w_gate, w_up, w_down_lo, w_down_hi, tm, tf=1024):
    n_slots = slot_tok.shape[0]
    D = h.shape[1]
    F = w_gate.shape[2]
    while F % tf:
        tf //= 2
    nf = F // tf
    xg_rows = _moe_rows_per_step(tm, nf) * nf
    blk = lambda i, nu: jnp.minimum(i, nu[0] - 1)
    chunk = lambda i, f, nu: jnp.where(i < nu[0], f, nf - 1)
    up_spec = pl.BlockSpec((1, D, tf), lambda i, f, tok, be, nu: (be[blk(i, nu)], 0, chunk(i, f, nu)))
    down_spec = pl.BlockSpec((1, tf, D // 2), lambda i, f, tok, be, nu: (be[blk(i, nu)], chunk(i, f, nu), 0))
    return pl.pallas_call(
        functools.partial(_moe_swiglu_kernel, tm=tm, nf=nf),
        grid_spec=pltpu.PrefetchScalarGridSpec(
            num_scalar_prefetch=3,
            grid=(n_slots // tm, nf),
            in_specs=[pl.BlockSpec(memory_space=pl.ANY), up_spec, up_spec, down_spec, down_spec],
            out_specs=pl.BlockSpec((tm, D), lambda i, f, tok, be, nu: (i, 0)),
            scratch_shapes=[pltpu.VMEM((2, xg_rows, D), F32),
                            pltpu.VMEM((tm, D), BF16),
                            pltpu.SemaphoreType.DMA((2,))]),
        out_shape=jax.ShapeDtypeStruct((n_slots, D), F32),
        compiler_params=_cp("arbitrary", "arbitrary"),
        name="moe_swiglu",
    )(slot_tok, block_e, n_used, h, w_gate, w_up, w_down_lo, w_down_hi)


def _combine_kernel(dest_ref, y_ref, gate_ref, ys_hbm, o_ref, buf_ref, sem, *, tb):
    base = pl.program_id(0) * tb

    def start(r, c):
        for k in range(TOP_K):
            _row_copy(ys_hbm, buf_ref.at[k], dest_ref[TOP_K * (base + r) + k], r, sem).start()
        return c

    lax.fori_loop(0, tb, start, 0, unroll=8)
    pltpu.make_async_copy(buf_ref, buf_ref, sem).wait()
    g = gate_ref[...]
    o_ref[...] = y_ref[...] + g[:, 0:1] * buf_ref[0] + g[:, 1:2] * buf_ref[1]


def combine_rows(y, gates, ys, dest, tb=256):
    T, D = y.shape
    return pl.pallas_call(
        functools.partial(_combine_kernel, tb=tb),
        grid_spec=pltpu.PrefetchScalarGridSpec(
            num_scalar_prefetch=1,
            grid=(T // tb,),
            in_specs=[pl.BlockSpec((tb, D), lambda i, d: (i, 0)),
                      pl.BlockSpec((tb, LANES), lambda i, d: (i, 0)),
                      pl.BlockSpec(memory_space=pl.ANY)],
            out_specs=pl.BlockSpec((tb, D), lambda i, d: (i, 0)),
            scratch_shapes=[pltpu.VMEM((TOP_K, tb, D), F32),
                            pltpu.SemaphoreType.DMA(())]),
        out_shape=jax.ShapeDtypeStruct((T, D), F32),
        compiler_params=_cp("arbitrary"),
        name="moe_combine",
    )(dest, y, gates, ys)


def _routing_tables(top_idx, T, block):
    A = T * TOP_K
    E = N_EXPERTS
    flat_e = top_idx.reshape(A)
    onehot = (flat_e[:, None] == jnp.arange(E, dtype=jnp.int32)[None, :]).astype(jnp.int32)
    csum = jnp.cumsum(onehot, axis=0)
    counts = csum[-1]
    rank = jnp.sum((csum - 1) * onehot, axis=1)
    padded = (counts + block - 1) // block * block
    pends = jnp.cumsum(padded)
    pstarts = pends - padded
    starts = jnp.cumsum(counts) - counts
    dest = (pstarts[flat_e] + rank).astype(jnp.int32)
    n_blocks = A // block + E
    n_slots = n_blocks * block
    order = jnp.argsort(flat_e, stable=True).astype(jnp.int32)
    slot = jnp.arange(n_slots, dtype=jnp.int32)
    slot_e = jnp.clip(jnp.searchsorted(pends, slot, side='right'), 0, E - 1).astype(jnp.int32)
    r = slot - pstarts[slot_e]
    valid = r < counts[slot_e]
    src = jnp.clip(starts[slot_e] + r, 0, A - 1)
    slot_tok = jnp.where(valid, order[src] // TOP_K, 0).astype(jnp.int32)
    block_e = slot_e[::block]
    n_used = (pends[-1:] // block).astype(jnp.int32)
    return slot_tok, dest, block_e, n_used


def _token_mixer(h, mem2d, tab, B, S, M, l, casts, w_in, mla_q_a_norm_g, mla_w_uq, mla_kv_a_norm_g, mla_w_ukv,
                 mla_q_norm_g, mla_k_norm_g, mla_w_out, pool_w, pool_scale, pool_w_out,
                 mem_norm_g, mem_w_kv, mem_q_norm_g, mem_k_norm_g, mem_w_out):
    za = matmul(h, w_in, l, ZA_WIDTH, F32)

    wuq = jnp.pad(mla_w_uq[l].reshape(Q_LORA_RANK, MLA_HEADS, QK_HEAD_DIM),
                  ((0, 0), (0, 0), (0, HEAD_PAD - QK_HEAD_DIM))).reshape(Q_LORA_RANK, MLA_HEADS * HEAD_PAD)
    wukv = mla_w_ukv[l].reshape(KV_LORA_RANK, MLA_HEADS, QK_NOPE_DIM + V_HEAD_DIM)
    wuk = wukv[:, :, :QK_NOPE_DIM].reshape(KV_LORA_RANK, MLA_WIDTH)
    wuv = wukv[:, :, QK_NOPE_DIM:].reshape(KV_LORA_RANK, MLA_WIDTH)
    pad_g = lambda g: jnp.pad(g, (0, HEAD_PAD - QK_HEAD_DIM)).reshape(1, HEAD_PAD)
    q, k, v = mla_prep(za, tab, wuq.astype(BF16), wuk.astype(BF16), wuv.astype(BF16),
                       mla_q_a_norm_g[l].reshape(1, -1), mla_kv_a_norm_g[l].reshape(1, -1),
                       pad_g(mla_q_norm_g[l]), pad_g(mla_k_norm_g[l]))
    attn, *cast_out = causal_attention(q, k, v, B, S, casts)

    mixed = pool_mixer(za, pool_w[l].astype(BF16), pool_scale[l].reshape(1, -1), B, S)

    k_m, v_m = mem_kv(mem2d, mem_norm_g[l].reshape(1, -1), mem_w_kv[l].astype(BF16),
                      mem_k_norm_g[l].reshape(1, -1), B, M)
    o_mem = mem_attention(za, k_m, v_m, mem_q_norm_g[l].reshape(1, -1), B, S, M)

    merged = merge_branches(h, attn, mixed, o_mem, w_in, mla_w_out, pool_w_out, mem_w_out, l)
    return merged, cast_out


def kernel(x, mem, positions, attn_norm_g, w_in, mla_q_a_norm_g, mla_w_uq, mla_kv_a_norm_g, mla_w_ukv, mla_q_norm_g, mla_k_norm_g, mla_w_out, pool_w, pool_scale, pool_w_out, mem_norm_g, mem_w_kv, mem_q_norm_g, mem_k_norm_g, mem_w_out, w_o, ffn_norm_g, dense_w_gate, dense_w_up, dense_w_down, router_w, router_b, moe_w_gate, moe_w_up, moe_w_down):
    B, S, D = x.shape
    M = mem.shape[1]
    T = B * S
    depth = attn_norm_g.shape[0]
    moe_block = 512
    y = x.reshape(T, D)
    mem2d = mem.reshape(B * M, D)
    tab = rope_tables(positions)
    w_in = prep_w_in(w_in)
    mla_w_out, pool_w_out, mem_w_out, w_o = (w.astype(BF16) for w in (mla_w_out, pool_w_out, mem_w_out, w_o))
    moe_bf16 = {}
    for l in range(depth):
        h = rmsnorm(y, attn_norm_g[l], BF16)
        i = l // 2
        casts = ()
        if l % 2 == 0 and l + 1 < depth:
            casts = ((moe_w_gate[i], 0, 1), (moe_w_down[i], 0, 2))
        elif l % 2 == 1:
            casts = ((moe_w_up[i], 0, 1), (moe_w_down[i], 1, 2))
        merged, cast_out = _token_mixer(h, mem2d, tab, B, S, M, l, casts, w_in, mla_q_a_norm_g, mla_w_uq,
                                        mla_kv_a_norm_g, mla_w_ukv, mla_q_norm_g, mla_k_norm_g, mla_w_out,
                                        pool_w, pool_scale, pool_w_out, mem_norm_g, mem_w_kv, mem_q_norm_g,
                                        mem_k_norm_g, mem_w_out)
        if casts:
            names = ("gate", "down_lo") if l % 2 == 0 else ("up", "down_hi")
            moe_bf16.update(zip(names, cast_out))
        if l % 2 == 0:
            y, h = matmul_residual_norm(merged, w_o, l, y, ffn_norm_g[l])
            y = dense_swiglu(h, dense_w_gate[i:i + 1].astype(BF16), dense_w_up[i:i + 1].astype(BF16),
                             dense_w_down[i:i + 1].astype(BF16), y)
        else:
            y = matmul(merged, w_o, l, D, F32, residual=y)
            hf, idx, gates = norm_router(y, ffn_norm_g[l], router_w[i], router_b[i])
            slot_tok, dest, block_e, n_used = _routing_tables(idx[:, :TOP_K], T, moe_block)
            ys = moe_swiglu(hf, slot_tok, block_e, n_used, moe_bf16.pop("gate"), moe_bf16.pop("up"),
                            moe_bf16.pop("down_lo"), moe_bf16.pop("down_hi"), tm=moe_block)
            y = combine_rows(y, gates, ys, dest)
    return y.reshape(B, S, D)
```

```python
import functools

import numpy as np
import jax
import jax.numpy as jnp
from jax import lax
from jax.experimental import pallas as pl
from jax.experimental.pallas import tpu as pltpu

F32 = jnp.float32
BF16 = jnp.bfloat16

MEM_HEADS = 4
MEM_HEAD_DIM = 256
MEM_WIDTH = MEM_HEADS * MEM_HEAD_DIM
MLA_HEADS = 8
Q_LORA_RANK = 512
KV_LORA_RANK = 256
QK_NOPE_DIM = 128
QK_ROPE_DIM = 64
QK_HEAD_DIM = QK_NOPE_DIM + QK_ROPE_DIM
V_HEAD_DIM = 128
MLA_WIDTH = MLA_HEADS * V_HEAD_DIM
ROPE_THETA = 10000.0
POOL_WINDOWS = (2, 4, 8, 16)
POOL_GROUPS = 4
POOL_GROUP_DIM = 256
POOL_WIDTH = POOL_GROUPS * POOL_GROUP_DIM
N_BRANCHES = 3
N_EXPERTS = 8
TOP_K = 2
NORM_EPS = 1e-6
LOG2_E = 1.4426950408889634

LANES = 128
SUBLANES = 8
HEAD_PAD = 2 * LANES
ROPE_HALF = QK_ROPE_DIM // 2
POOL_HALO = 16

ZA_POOL = 0
ZA_QMEM = POOL_WIDTH
ZA_CQ = ZA_QMEM + MEM_WIDTH
ZA_CKV = ZA_CQ + Q_LORA_RANK
ZA_KPE = ZA_CKV + KV_LORA_RANK
ZA_WIDTH = 3072

VMEM_LIMIT = 52 * 1024 * 1024


def _cp(*sem):
    return pltpu.CompilerParams(dimension_semantics=sem, vmem_limit_bytes=VMEM_LIMIT)


def _rms(x, eps_dim):
    return lax.rsqrt(jnp.sum(x * x, axis=-1, keepdims=True) * (1.0 / eps_dim) + NORM_EPS)


def _rmsnorm_kernel(x_ref, g_ref, o_ref):
    x = x_ref[...]
    o_ref[...] = (x * _rms(x, x.shape[-1]) * g_ref[...]).astype(o_ref.dtype)


def rmsnorm(x, g, out_dtype, tm=512):
    T, D = x.shape
    return pl.pallas_call(
        _rmsnorm_kernel,
        grid=(T // tm,),
        in_specs=[pl.BlockSpec((tm, D), lambda i: (i, 0)),
                  pl.BlockSpec((1, D), lambda i: (0, 0))],
        out_specs=pl.BlockSpec((tm, D), lambda i: (i, 0)),
        out_shape=jax.ShapeDtypeStruct((T, D), out_dtype),
        compiler_params=_cp("parallel"),
        name="rmsnorm",
    )(x, g.reshape(1, D))


def _norm_router_kernel(x_ref, g_ref, rwh_ref, rwl_ref, rb_ref, h_ref, idx_ref, gate_ref):
    x = x_ref[...]
    h = x * _rms(x, x.shape[-1]) * g_ref[...]
    h_ref[...] = h
    h_hi = h.astype(BF16)
    h_lo = (h - h_hi.astype(F32)).astype(BF16)
    logits = (jnp.dot(h_hi, rwh_ref[...], preferred_element_type=F32)
              + jnp.dot(h_lo, rwh_ref[...], preferred_element_type=F32)
              + jnp.dot(h_hi, rwl_ref[...], preferred_element_type=F32)) + rb_ref[...]
    lane = lax.broadcasted_iota(jnp.int32, logits.shape, 1)
    neg = -jnp.inf
    l1 = jnp.where(lane < N_EXPERTS, logits, neg)
    m1 = jnp.max(l1, axis=-1, keepdims=True)
    i1 = jnp.min(jnp.where(l1 == m1, lane, LANES), axis=-1, keepdims=True)
    l2 = jnp.where(lane == i1, neg, l1)
    m2 = jnp.max(l2, axis=-1, keepdims=True)
    i2 = jnp.min(jnp.where(l2 == m2, lane, LANES), axis=-1, keepdims=True)
    e = jnp.exp(m2 - m1)
    g1 = 1.0 / (1.0 + e)
    g2 = e / (1.0 + e)
    idx_ref[...] = jnp.where(lane == 0, i1, jnp.where(lane == 1, i2, 0))
    gate_ref[...] = jnp.where(lane == 0, g1, jnp.where(lane == 1, g2, 0.0))


def norm_router(x, g, router_w, router_b, tm=512):
    T, D = x.shape
    E = router_w.shape[1]
    rw = jnp.zeros((D, LANES), F32).at[:, :E].set(router_w)
    rw_hi = rw.astype(BF16)
    rw_lo = (rw - rw_hi.astype(F32)).astype(BF16)
    rb = jnp.zeros((1, LANES), F32).at[0, :E].set(router_b)
    return pl.pallas_call(
        _norm_router_kernel,
        grid=(T // tm,),
        in_specs=[pl.BlockSpec((tm, D), lambda i: (i, 0)),
                  pl.BlockSpec((1, D), lambda i: (0, 0)),
                  pl.BlockSpec((D, LANES), lambda i: (0, 0)),
                  pl.BlockSpec((D, LANES), lambda i: (0, 0)),
                  pl.BlockSpec((1, LANES), lambda i: (0, 0))],
        out_specs=[pl.BlockSpec((tm, D), lambda i: (i, 0)),
                   pl.BlockSpec((tm, LANES), lambda i: (i, 0)),
                   pl.BlockSpec((tm, LANES), lambda i: (i, 0))],
        out_shape=[jax.ShapeDtypeStruct((T, D), F32),
                   jax.ShapeDtypeStruct((T, LANES), jnp.int32),
                   jax.ShapeDtypeStruct((T, LANES), F32)],
        compiler_params=_cp("parallel"),
        name="norm_router",
    )(x, g.reshape(1, D), rw_hi, rw_lo, rb)


def _mm_kernel(a_ref, w_ref, o_ref):
    o_ref[...] = jnp.dot(a_ref[...], w_ref[0], preferred_element_type=F32).astype(o_ref.dtype)


def _mm_res_kernel(a_ref, w_ref, r_ref, o_ref):
    o_ref[...] = (r_ref[...] + jnp.dot(a_ref[...], w_ref[0], preferred_element_type=F32)).astype(o_ref.dtype)


def matmul(a, w, l, N, out_dtype, residual=None, tm=1024, tn=1024):
    M, K = a.shape
    tm, tn = min(tm, M), min(tn, N)
    in_specs = [pl.BlockSpec((tm, K), lambda i, j: (i, 0)),
                pl.BlockSpec((1, K, tn), lambda i, j: (l, 0, j))]
    args = [a, w]
    body = _mm_kernel
    if residual is not None:
        in_specs.append(pl.BlockSpec((tm, tn), lambda i, j: (i, j)))
        args.append(residual)
        body = _mm_res_kernel
    return pl.pallas_call(
        body,
        grid=(M // tm, N // tn),
        in_specs=in_specs,
        out_specs=pl.BlockSpec((tm, tn), lambda i, j: (i, j)),
        out_shape=jax.ShapeDtypeStruct((M, N), out_dtype),
        compiler_params=_cp("parallel", "arbitrary"),
        name="matmul",
    )(*args)


def _mm_res_norm_kernel(a_ref, w_ref, r_ref, g_ref, o_ref, h_ref):
    y = r_ref[...] + jnp.dot(a_ref[...], w_ref[0], preferred_element_type=F32)
    o_ref[...] = y
    h_ref[...] = (y * _rms(y, y.shape[-1]) * g_ref[...]).astype(h_ref.dtype)


def matmul_residual_norm(a, w, l, residual, g, tm=512):
    M, K = a.shape
    N = w.shape[2]
    rows = lambda width: pl.BlockSpec((tm, width), lambda i: (i, 0))
    return pl.pallas_call(
        _mm_res_norm_kernel,
        grid=(M // tm,),
        in_specs=[rows(K), pl.BlockSpec((1, K, N), lambda i: (l, 0, 0)), rows(N),
                  pl.BlockSpec((1, N), lambda i: (0, 0))],
        out_specs=[rows(N), rows(N)],
        out_shape=[jax.ShapeDtypeStruct((M, N), F32), jax.ShapeDtypeStruct((M, N), BF16)],
        compiler_params=_cp("parallel"),
        name="matmul_residual_norm",
    )(a, w, residual, g.reshape(1, N))


def _rope_lanes(v):
    z = jnp.zeros(v.shape[:-1] + (LANES // 2 - ROPE_HALF,), v.dtype)
    return jnp.concatenate([v[..., :ROPE_HALF], z, v[..., ROPE_HALF:], z], axis=-1)


def _rope_table_kernel(pos_ref, invf_ref, o_ref):
    ang = pos_ref[...].astype(F32) * invf_ref[...]
    lane = lax.broadcasted_iota(jnp.int32, ang.shape, 1)
    first = lane < ROPE_HALF
    second = (lane >= LANES // 2) & (lane < LANES // 2 + ROPE_HALF)
    c = jnp.cos(ang)
    s = jnp.sin(ang)
    o_ref[:, 0:LANES] = jnp.where(first | second, c, 0.0)
    o_ref[:, LANES:2 * LANES] = jnp.where(first, -s, jnp.where(second, s, 0.0))


def rope_tables(positions, tm=512):
    T = positions.size
    inv_freq = ROPE_THETA ** (-np.arange(0, QK_ROPE_DIM, 2, dtype=np.float32) / QK_ROPE_DIM)
    invf = np.zeros((1, LANES), np.float32)
    invf[0, :ROPE_HALF] = inv_freq
    invf[0, LANES // 2:LANES // 2 + ROPE_HALF] = inv_freq
    return pl.pallas_call(
        _rope_table_kernel,
        grid=(T // tm,),
        in_specs=[pl.BlockSpec((tm, 1), lambda i: (i, 0)),
                  pl.BlockSpec((1, LANES), lambda i: (0, 0))],
        out_specs=pl.BlockSpec((tm, 2 * LANES), lambda i: (i, 0)),
        out_shape=jax.ShapeDtypeStruct((T, 2 * LANES), F32),
        compiler_params=_cp("parallel"),
        name="rope_tables",
    )(positions.reshape(T, 1), jnp.asarray(invf))


def _rope(x, tab):
    return x * tab[:, 0:LANES] + pltpu.roll(x, LANES // 2, 1) * tab[:, LANES:2 * LANES]


def _mla_prep_kernel(cq_ref, ckv_ref, kpe_ref, tab_ref, wuq_ref, wuk_ref, wuv_ref,
                     gqa_ref, gkva_ref, gq_ref, gk_ref, q_ref, k_ref, v_ref):
    tab = tab_ref[...]
    cq = cq_ref[...]
    cqn = (cq * _rms(cq, Q_LORA_RANK) * gqa_ref[...]).astype(BF16)
    qf = jnp.dot(cqn, wuq_ref[...], preferred_element_type=F32)
    ckv = ckv_ref[...]
    ckvn = (ckv * _rms(ckv, KV_LORA_RANK) * gkva_ref[...]).astype(BF16)
    kn = jnp.dot(ckvn, wuk_ref[...], preferred_element_type=F32)
    v_ref[...] = jnp.dot(ckvn, wuv_ref[...], preferred_element_type=F32).astype(v_ref.dtype)
    gq = gq_ref[...]
    gk = gk_ref[...]
    kpe = kpe_ref[...]
    ss_pe = jnp.sum(kpe * kpe, axis=-1, keepdims=True)
    kpe_rot = _rope(kpe * gk[:, LANES:], tab)
    scale = QK_HEAD_DIM ** -0.5 * LOG2_E
    for h in range(MLA_HEADS):
        q0 = qf[:, h * HEAD_PAD:h * HEAD_PAD + LANES]
        q1 = qf[:, h * HEAD_PAD + LANES:(h + 1) * HEAD_PAD]
        ss = jnp.sum(q0 * q0, axis=-1, keepdims=True) + jnp.sum(q1 * q1, axis=-1, keepdims=True)
        rq = lax.rsqrt(ss * (1.0 / QK_HEAD_DIM) + NORM_EPS) * scale
        q_ref[:, h * HEAD_PAD:h * HEAD_PAD + LANES] = (q0 * gq[:, :LANES] * rq).astype(q_ref.dtype)
        q_ref[:, h * HEAD_PAD + LANES:(h + 1) * HEAD_PAD] = (_rope(q1 * gq[:, LANES:], tab) * rq).astype(q_ref.dtype)
        k0 = kn[:, h * LANES:(h + 1) * LANES]
        ssk = jnp.sum(k0 * k0, axis=-1, keepdims=True) + ss_pe
        rk = lax.rsqrt(ssk * (1.0 / QK_HEAD_DIM) + NORM_EPS)
        k_ref[:, h * HEAD_PAD:h * HEAD_PAD + LANES] = (k0 * gk[:, :LANES] * rk).astype(k_ref.dtype)
        k_ref[:, h * HEAD_PAD + LANES:(h + 1) * HEAD_PAD] = (kpe_rot * rk).astype(k_ref.dtype)


def mla_prep(za, tab, wuq, wuk, wuv, gqa, gkva, gq, gk, tm=512):
    T = za.shape[0]
    HP = MLA_HEADS * HEAD_PAD
    full = lambda shape: pl.BlockSpec(shape, lambda i: (0, 0))
    return pl.pallas_call(
        _mla_prep_kernel,
        grid=(T // tm,),
        in_specs=[pl.BlockSpec((tm, Q_LORA_RANK), lambda i: (i, ZA_CQ // Q_LORA_RANK)),
                  pl.BlockSpec((tm, KV_LORA_RANK), lambda i: (i, ZA_CKV // KV_LORA_RANK)),
                  pl.BlockSpec((tm, LANES), lambda i: (i, ZA_KPE // LANES)),
                  pl.BlockSpec((tm, 2 * LANES), lambda i: (i, 0)),
                  full((Q_LORA_RANK, HP)), full((KV_LORA_RANK, MLA_WIDTH)), full((KV_LORA_RANK, MLA_WIDTH)),
                  full((1, Q_LORA_RANK)), full((1, KV_LORA_RANK)), full((1, HEAD_PAD)), full((1, HEAD_PAD))],
        out_specs=[pl.BlockSpec((tm, HP), lambda i: (i, 0)),
                   pl.BlockSpec((tm, HP), lambda i: (i, 0)),
                   pl.BlockSpec((tm, MLA_WIDTH), lambda i: (i, 0))],
        out_shape=[jax.ShapeDtypeStruct((T, HP), BF16),
                   jax.ShapeDtypeStruct((T, HP), BF16),
                   jax.ShapeDtypeStruct((T, MLA_WIDTH), BF16)],
        compiler_params=_cp("parallel"),
        name="mla_prep",
    )(za, za, za, tab, wuq, wuk, wuv, gqa, gkva, gq, gk)


def _causal_attn_kernel(q_ref, k_ref, v_ref, *rest, tq, n_cast):
    cast_src = rest[:n_cast]
    o_ref = rest[n_cast]
    cast_dst = rest[n_cast + 1:]
    for s_ref, d_ref in zip(cast_src, cast_dst):
        d_ref[...] = s_ref[...].astype(d_ref.dtype)
    S = q_ref.shape[0]
    row = lax.broadcasted_iota(jnp.int32, (tq, tq), 0)
    col = lax.broadcasted_iota(jnp.int32, (tq, tq), 1)
    tri = col <= row
    for qi in range(S // tq):
        kl = (qi + 1) * tq
        q = q_ref[qi * tq:kl, :]
        s = lax.dot_general(q, k_ref[0:kl, :], (((1,), (1,)), ((), ())), preferred_element_type=F32)
        s_diag = jnp.where(tri, s[:, kl - tq:kl], -jnp.inf)
        if qi > 0:
            s = jnp.concatenate([s[:, :kl - tq], s_diag], axis=1)
        else:
            s = s_diag
        m = jnp.max(s, axis=-1, keepdims=True)
        p = jnp.exp2(s - m)
        l = jnp.sum(p, axis=-1, keepdims=True)
        o = jnp.dot(p.astype(BF16), v_ref[0:kl, :], preferred_element_type=F32)
        o_ref[qi * tq:kl, :] = (o / l).astype(o_ref.dtype)


def causal_attention(q, k, v, B, S, casts=(), tq=256):
    T = q.shape[0]
    n_steps = B * MLA_HEADS
    in_specs = [pl.BlockSpec((S, HEAD_PAD), lambda b, h: (b, h)),
                pl.BlockSpec((S, HEAD_PAD), lambda b, h: (b, h)),
                pl.BlockSpec((S, V_HEAD_DIM), lambda b, h: (b, h))]
    out_specs = [pl.BlockSpec((S, V_HEAD_DIM), lambda b, h: (b, h))]
    out_shape = [jax.ShapeDtypeStruct((T, MLA_WIDTH), BF16)]
    for src, part, n_parts in casts:
        E, rows, cols = src.shape
        spe = n_steps // E
        rb, cb = rows // spe, cols // n_parts
        assert spe * E == n_steps and rb * spe == rows and rb % 16 == 0
        assert cb * n_parts == cols and cb % LANES == 0
        step = lambda b, h: b * MLA_HEADS + h
        in_specs.append(pl.BlockSpec((1, rb, cb), lambda b, h, part=part, spe=spe:
                                     (step(b, h) // spe, step(b, h) % spe, part)))
        out_specs.append(pl.BlockSpec((1, rb, cb), lambda b, h, spe=spe:
                                      (step(b, h) // spe, step(b, h) % spe, 0)))
        out_shape.append(jax.ShapeDtypeStruct((E, rows, cb), BF16))
    return pl.pallas_call(
        functools.partial(_causal_attn_kernel, tq=tq, n_cast=len(casts)),
        grid=(B, MLA_HEADS),
        in_specs=in_specs,
        out_specs=out_specs,
        out_shape=out_shape,
        compiler_params=_cp("parallel", "parallel"),
        name="causal_attention",
    )(q, k, v, *(c[0] for c in casts))


def _pool_kernel(u_ref, w_ref, sc_ref, o_ref, pad_ref, *, rc):
    S = u_ref.shape[0]
    C = POOL_GROUP_DIM
    t1 = lax.broadcasted_iota(jnp.int32, (rc, C), 0).astype(F32) + 1.0
    for g, win in enumerate(POOL_WINDOWS):
        pad_ref[0:POOL_HALO, :] = jnp.zeros((POOL_HALO, C), F32)
        pad_ref[POOL_HALO:POOL_HALO + S, :] = u_ref[:, g * C:(g + 1) * C]
        for r0 in range(0, S, rc):
            acc = pad_ref[POOL_HALO + r0:POOL_HALO + r0 + rc, :]
            tok = acc
            for kk in range(1, win):
                acc = acc + pad_ref[POOL_HALO + r0 - kk:POOL_HALO + r0 - kk + rc, :]
            cnt = jnp.minimum(t1 + float(r0), float(win))
            pooled = (acc / cnt - tok).astype(BF16)
            mixed = jnp.dot(pooled, w_ref[g], preferred_element_type=F32) * sc_ref[:, g * C:(g + 1) * C]
            o_ref[r0:r0 + rc, g * C:(g + 1) * C] = mixed.astype(o_ref.dtype)


def pool_mixer(za, pool_w, pool_scale, B, S, rc=512):
    T = za.shape[0]
    rc = min(rc, S)
    return pl.pallas_call(
        functools.partial(_pool_kernel, rc=rc),
        grid=(B,),
        in_specs=[pl.BlockSpec((S, POOL_WIDTH), lambda b: (b, ZA_POOL // POOL_WIDTH)),
                  pl.BlockSpec((POOL_GROUPS, POOL_GROUP_DIM, POOL_GROUP_DIM), lambda b: (0, 0, 0)),
                  pl.BlockSpec((1, POOL_WIDTH), lambda b: (0, 0))],
        out_specs=pl.BlockSpec((S, POOL_WIDTH), lambda b: (b, 0)),
        out_shape=jax.ShapeDtypeStruct((T, POOL_WIDTH), BF16),
        scratch_shapes=[pltpu.VMEM((POOL_HALO + S, POOL_GROUP_DIM), F32)],
        compiler_params=_cp("parallel"),
        name="pool_mixer",
    )(za, pool_w, pool_scale)


def _mem_kv_kernel(m_ref, g_ref, w_ref, gk_ref, k_ref, v_ref):
    x = m_ref[...]
    xn = (x * _rms(x, x.shape[-1]) * g_ref[...]).astype(BF16)
    kv = jnp.dot(xn, w_ref[...], preferred_element_type=F32)
    for h in range(MEM_HEADS):
        kh = kv[:, h * MEM_HEAD_DIM:(h + 1) * MEM_HEAD_DIM]
        k_ref[:, h * MEM_HEAD_DIM:(h + 1) * MEM_HEAD_DIM] = (
            kh * _rms(kh, MEM_HEAD_DIM) * gk_ref[...]).astype(k_ref.dtype)
    v_ref[...] = kv[:, MEM_WIDTH:].astype(v_ref.dtype)


def mem_kv(mem2d, g, w_kv, gk, B, M):
    D = mem2d.shape[1]
    return pl.pallas_call(
        _mem_kv_kernel,
        grid=(B,),
        in_specs=[pl.BlockSpec((M, D), lambda b: (b, 0)),
                  pl.BlockSpec((1, D), lambda b: (0, 0)),
                  pl.BlockSpec((D, 2 * MEM_WIDTH), lambda b: (0, 0)),
                  pl.BlockSpec((1, MEM_HEAD_DIM), lambda b: (0, 0))],
        out_specs=[pl.BlockSpec((M, MEM_WIDTH), lambda b: (b, 0)),
                   pl.BlockSpec((M, MEM_WIDTH), lambda b: (b, 0))],
        out_shape=[jax.ShapeDtypeStruct((B * M, MEM_WIDTH), BF16),
                   jax.ShapeDtypeStruct((B * M, MEM_WIDTH), BF16)],
        compiler_params=_cp("parallel"),
        name="mem_kv",
    )(mem2d, g, w_kv, gk)


def _mem_attn_kernel(q_ref, k_ref, v_ref, gq_ref, o_ref):
    scale = MEM_HEAD_DIM ** -0.5 * LOG2_E
    for h in range(MEM_HEADS):
        sl = slice(h * MEM_HEAD_DIM, (h + 1) * MEM_HEAD_DIM)
        qh = q_ref[:, sl]
        qn = (qh * (_rms(qh, MEM_HEAD_DIM) * scale) * gq_ref[...]).astype(BF16)
        s = lax.dot_general(qn, k_ref[:, sl], (((1,), (1,)), ((), ())), preferred_element_type=F32)
        m = jnp.max(s, axis=-1, keepdims=True)
        p = jnp.exp2(s - m)
        l = jnp.sum(p, axis=-1, keepdims=True)
        o = jnp.dot(p.astype(BF16), v_ref[:, sl], preferred_element_type=F32)
        o_ref[:, sl] = (o / l).astype(o_ref.dtype)


def mem_attention(za, k_m, v_m, gq, B, S, M, tq=512):
    T = za.shape[0]
    tq = min(tq, S)
    nq = S // tq
    return pl.pallas_call(
        _mem_attn_kernel,
        grid=(B, nq),
        in_specs=[pl.BlockSpec((tq, MEM_WIDTH), lambda b, i: (b * nq + i, ZA_QMEM // MEM_WIDTH)),
                  pl.BlockSpec((M, MEM_WIDTH), lambda b, i: (b, 0)),
                  pl.BlockSpec((M, MEM_WIDTH), lambda b, i: (b, 0)),
                  pl.BlockSpec((1, MEM_HEAD_DIM), lambda b, i: (0, 0))],
        out_specs=pl.BlockSpec((tq, MEM_WIDTH), lambda b, i: (b * nq + i, 0)),
        out_shape=jax.ShapeDtypeStruct((T, MEM_WIDTH), BF16),
        compiler_params=_cp("parallel", "arbitrary"),
        name="mem_attention",
    )(za, k_m, v_m, gq)


def _merge_kernel(h_ref, a_ref, p_ref, m_ref, g0_ref, g1_ref, g2_ref, w0_ref, w1_ref, w2_ref, o_ref):
    h = h_ref[...]
    acc = None
    for x_ref, g_ref, w_ref in ((a_ref, g0_ref, w0_ref), (p_ref, g1_ref, w1_ref), (m_ref, g2_ref, w2_ref)):
        logit = jnp.dot(h, g_ref[0], preferred_element_type=F32)
        gate = 1.0 / (1.0 + jnp.exp(-logit))
        br = jnp.dot(x_ref[...], w_ref[0], preferred_element_type=F32)
        acc = gate * br if acc is None else acc + gate * br
    o_ref[...] = acc.astype(o_ref.dtype)


def merge_branches(h, a, p, m, w_all, w_a, w_p, w_m, l, tm=1024, tn=512):
    T, D = h.shape
    W = a.shape[1]
    nj = D // tn
    g0 = ZA_WIDTH // tn
    row = lambda width: pl.BlockSpec((tm, width), lambda i, j: (i, 0))
    gate_spec = lambda b: pl.BlockSpec((1, D, tn), lambda i, j, b=b: (l, 0, g0 + b * nj + j))
    out_w = pl.BlockSpec((1, W, tn), lambda i, j: (l, 0, j))
    return pl.pallas_call(
        _merge_kernel,
        grid=(T // tm, nj),
        in_specs=[row(D), row(W), row(W), row(W),
                  gate_spec(0), gate_spec(1), gate_spec(2), out_w, out_w, out_w],
        out_specs=pl.BlockSpec((tm, tn), lambda i, j: (i, j)),
        out_shape=jax.ShapeDtypeStruct((T, D), BF16),
        compiler_params=_cp("parallel", "arbitrary"),
        name="merge_branches",
    )(h, a, p, m, w_all, w_all, w_all, w_a, w_p, w_m)


def _w_in_prep_kernel(w_ref, o_ref):
    s1 = Q_LORA_RANK + KV_LORA_RANK + QK_ROPE_DIM
    s2 = s1 + POOL_WIDTH
    s3 = s2 + MEM_WIDTH
    tk = w_ref.shape[2]
    s1_up = -(-s1 // LANES) * LANES
    o_ref[0, :, ZA_POOL:ZA_POOL + POOL_WIDTH] = w_ref[0, s1:s2, :].T.astype(BF16)
    o_ref[0, :, ZA_QMEM:ZA_QMEM + MEM_WIDTH] = w_ref[0, s2:s3, :].T.astype(BF16)
    low = w_ref[0, 0:s1_up, :].T.astype(BF16)
    n_c = Q_LORA_RANK + KV_LORA_RANK
    o_ref[0, :, ZA_CQ:ZA_CQ + n_c] = low[:, 0:n_c]
    o_ref[0, :, ZA_KPE:ZA_WIDTH] = jnp.zeros((tk, ZA_WIDTH - ZA_KPE), BF16)
    o_ref[0, :, ZA_KPE:ZA_KPE + ROPE_HALF] = low[:, n_c:n_c + ROPE_HALF]
    o_ref[0, :, ZA_KPE + LANES // 2:ZA_KPE + LANES // 2 + ROPE_HALF] = low[:, n_c + ROPE_HALF:s1]
    o_ref[0, :, ZA_WIDTH:] = w_ref[0, s3:, :].T.astype(BF16)


def prep_w_in(w_in, tk=256):
    L, D, win = w_in.shape
    wout = ZA_WIDTH + N_BRANCHES * D
    return pl.pallas_call(
        _w_in_prep_kernel,
        grid=(L, D // tk),
        in_specs=[pl.BlockSpec((1, win, tk), lambda l, i: (l, 0, i))],
        out_specs=pl.BlockSpec((1, tk, wout), lambda l, i: (l, i, 0)),
        out_shape=jax.ShapeDtypeStruct((L, D, wout), BF16),
        compiler_params=_cp("parallel", "parallel"),
        name="prep_w_in",
    )(jnp.swapaxes(w_in, 1, 2))


def _swiglu_accumulate(xb, wg_ref, wu_ref, wd_refs, o_ref):
    a = jnp.dot(xb, wg_ref[0], preferred_element_type=F32)
    b = jnp.dot(xb, wu_ref[0], preferred_element_type=F32)
    act = (a / (1.0 + jnp.exp(-a)) * b).astype(BF16)
    width = o_ref.shape[1] // len(wd_refs)
    for n, wd_ref in enumerate(wd_refs):
        o_ref[:, n * width:(n + 1) * width] += jnp.dot(act, wd_ref[0], preferred_element_type=F32)


def _dense_swiglu_kernel(x_ref, wg_ref, wu_ref, wd_ref, r_ref, o_ref):
    @pl.when(pl.program_id(1) == 0)
    def _():
        o_ref[...] = r_ref[...]

    _swiglu_accumulate(x_ref[...], wg_ref, wu_ref, (wd_ref,), o_ref)


def dense_swiglu(x, w_gate, w_up, w_down, residual, tm=1024, tf=512):
    T, D = x.shape
    F = w_gate.shape[2]
    tm = min(tm, T)
    return pl.pallas_call(
        _dense_swiglu_kernel,
        grid=(T // tm, F // tf),
        in_specs=[pl.BlockSpec((tm, D), lambda i, f: (i, 0)),
                  pl.BlockSpec((1, D, tf), lambda i, f: (0, 0, f)),
                  pl.BlockSpec((1, D, tf), lambda i, f: (0, 0, f)),
                  pl.BlockSpec((1, tf, D), lambda i, f: (0, f, 0)),
                  pl.BlockSpec((tm, D), lambda i, f: (i, 0), pipeline_mode=pl.Buffered(1))],
        out_specs=pl.BlockSpec((tm, D), lambda i, f: (i, 0)),
        out_shape=jax.ShapeDtypeStruct((T, D), F32),
        compiler_params=_cp("parallel", "arbitrary"),
        name="dense_swiglu",
    )(x, w_gate, w_up, w_down, residual)


def _row_copy(src_hbm, dst_ref, src_row, dst_row, sem):
    return pltpu.make_async_copy(src_hbm.at[pl.ds(src_row, 1)], dst_ref.at[pl.ds(dst_row, 1)], sem)


def _moe_rows_per_step(tm, nf):
    return -(-tm // (nf * SUBLANES)) * SUBLANES


def _moe_swiglu_kernel(tok_ref, be_ref, nu_ref, h_hbm, wg_ref, wu_ref, wd0_ref, wd1_ref, o_ref,
                       xg_ref, xb_ref, sem, *, tm, nf):
    del be_ref
    i = pl.program_id(0)
    f = pl.program_id(1)
    n_used = nu_ref[0]
    rows_per_step = _moe_rows_per_step(tm, nf)
    n_fetch = rows_per_step * nf

    def fetch(block, r, slot):
        tok = tok_ref[block * tm + jnp.minimum(r, tm - 1)]
        return _row_copy(h_hbm, xg_ref.at[slot], tok, r, sem.at[slot])

    def wait_rows(slot, n_rows):
        rows = xg_ref.at[slot, pl.ds(0, n_rows)]
        pltpu.make_async_copy(rows, rows, sem.at[slot]).wait()

    @pl.when((i >= n_used) & (f == 0))
    def _():
        o_ref[...] = jnp.zeros_like(o_ref)

    @pl.when(i < n_used)
    def _():
        slot = i % 2

        @pl.when((i == 0) & (f == 0))
        def _():
            def start(r, c):
                fetch(0, r, 0).start()
                return c
            lax.fori_loop(0, n_fetch, start, 0)
            for j in range(rows_per_step):
                fetch(jnp.minimum(1, n_used - 1), j, 1).start()

        @pl.when(f == 0)
        def _():
            wait_rows(slot, n_fetch)
            xb_ref[...] = xg_ref[slot, 0:tm, :].astype(BF16)
            o_ref[...] = jnp.zeros_like(o_ref)

        wrap = (f == nf - 1).astype(jnp.int32)
        chunk = (f + 1) * (1 - wrap)
        target = jnp.minimum(i + 1 + wrap, n_used - 1)
        target_slot = (i + 1 + wrap) % 2
        for j in range(rows_per_step):
            fetch(target, chunk * rows_per_step + j, target_slot).start()

        _swiglu_accumulate(xb_ref[...], wg_ref, wu_ref, (wd0_ref, wd1_ref), o_ref)

        @pl.when((i == n_used - 1) & (f == nf - 1))
        def _():
            wait_rows(1 - slot, n_fetch)
            wait_rows(slot, rows_per_step)


def moe_swiglu(h, slot_tok, block_e, n_used, w_gate, w_up, w_down_lo, w_down_hi, tm, tf=1024):
    n_slots = slot_tok.shape[0]
    D = h.shape[1]
    F = w_gate.shape[2]
    while F % tf:
        tf //= 2
    nf = F // tf
    xg_rows = _moe_rows_per_step(tm, nf) * nf
    blk = lambda i, nu: jnp.minimum(i, nu[0] - 1)
    chunk = lambda i, f, nu: jnp.where(i < nu[0], f, nf - 1)
    up_spec = pl.BlockSpec((1, D, tf), lambda i, f, tok, be, nu: (be[blk(i, nu)], 0, chunk(i, f, nu)))
    down_spec = pl.BlockSpec((1, tf, D // 2), lambda i, f, tok, be, nu: (be[blk(i, nu)], chunk(i, f, nu), 0))
    return pl.pallas_call(
        functools.partial(_moe_swiglu_kernel, tm=tm, nf=nf),
        grid_spec=pltpu.PrefetchScalarGridSpec(
            num_scalar_prefetch=3,
            grid=(n_slots // tm, nf),
            in_specs=[pl.BlockSpec(memory_space=pl.ANY), up_spec, up_spec, down_spec, down_spec],
            out_specs=pl.BlockSpec((tm, D), lambda i, f, tok, be, nu: (i, 0)),
            scratch_shapes=[pltpu.VMEM((2, xg_rows, D), F32),
                            pltpu.VMEM((tm, D), BF16),
                            pltpu.SemaphoreType.DMA((2,))]),
        out_shape=jax.ShapeDtypeStruct((n_slots, D), F32),
        compiler_params=_cp("arbitrary", "arbitrary"),
        name="moe_swiglu",
    )(slot_tok, block_e, n_used, h, w_gate, w_up, w_down_lo, w_down_hi)


def _combine_kernel(dest_ref, y_ref, gate_ref, ys_hbm, o_ref, buf_ref, sem, *, tb):
    base = pl.program_id(0) * tb

    def start(r, c):
        for k in range(TOP_K):
            _row_copy(ys_hbm, buf_ref.at[k], dest_ref[TOP_K * (base + r) + k], r, sem).start()
        return c

    lax.fori_loop(0, tb, start, 0, unroll=8)
    pltpu.make_async_copy(buf_ref, buf_ref, sem).wait()
    g = gate_ref[...]
    o_ref[...] = y_ref[...] + g[:, 0:1] * buf_ref[0] + g[:, 1:2] * buf_ref[1]


def combine_rows(y, gates, ys, dest, tb=256):
    T, D = y.shape
    return pl.pallas_call(
        functools.partial(_combine_kernel, tb=tb),
        grid_spec=pltpu.PrefetchScalarGridSpec(
            num_scalar_prefetch=1,
            grid=(T // tb,),
            in_specs=[pl.BlockSpec((tb, D), lambda i, d: (i, 0)),
                      pl.BlockSpec((tb, LANES), lambda i, d: (i, 0)),
                      pl.BlockSpec(memory_space=pl.ANY)],
            out_specs=pl.BlockSpec((tb, D), lambda i, d: (i, 0)),
            scratch_shapes=[pltpu.VMEM((TOP_K, tb, D), F32),
                            pltpu.SemaphoreType.DMA(())]),
        out_shape=jax.ShapeDtypeStruct((T, D), F32),
        compiler_params=_cp("arbitrary"),
        name="moe_combine",
    )(dest, y, gates, ys)


def _routing_tables(top_idx, T, block):
    A = T * TOP_K
    E = N_EXPERTS
    flat_e = top_idx.reshape(A)
    onehot = (flat_e[:, None] == jnp.arange(E, dtype=jnp.int32)[None, :]).astype(jnp.int32)
    csum = jnp.cumsum(onehot, axis=0)
    counts = csum[-1]
    rank = jnp.sum((csum - 1) * onehot, axis=1)
    padded = (counts + block - 1) // block * block
    pends = jnp.cumsum(padded)
    pstarts = pends - padded
    starts = jnp.cumsum(counts) - counts
    dest = (pstarts[flat_e] + rank).astype(jnp.int32)
    n_blocks = A // block + E
    n_slots = n_blocks * block
    order = jnp.argsort(flat_e, stable=True).astype(jnp.int32)
    slot = jnp.arange(n_slots, dtype=jnp.int32)
    slot_e = jnp.clip(jnp.searchsorted(pends, slot, side='right'), 0, E - 1).astype(jnp.int32)
    r = slot - pstarts[slot_e]
    valid = r < counts[slot_e]
    src = jnp.clip(starts[slot_e] + r, 0, A - 1)
    slot_tok = jnp.where(valid, order[src] // TOP_K, 0).astype(jnp.int32)
    block_e = slot_e[::block]
    n_used = (pends[-1:] // block).astype(jnp.int32)
    return slot_tok, dest, block_e, n_used


def _token_mixer(h, mem2d, tab, B, S, M, l, casts, w_in, mla_q_a_norm_g, mla_w_uq, mla_kv_a_norm_g, mla_w_ukv,
                 mla_q_norm_g, mla_k_norm_g, mla_w_out, pool_w, pool_scale, pool_w_out,
                 mem_norm_g, mem_w_kv, mem_q_norm_g, mem_k_norm_g, mem_w_out):
    za = matmul(h, w_in, l, ZA_WIDTH, F32)

    head_pad = lambda v: jnp.concatenate([v[..., :QK_NOPE_DIM], _rope_lanes(v[..., QK_NOPE_DIM:])], axis=-1)
    wuq = head_pad(mla_w_uq[l].reshape(Q_LORA_RANK, MLA_HEADS, QK_HEAD_DIM)).reshape(
        Q_LORA_RANK, MLA_HEADS * HEAD_PAD)
    wukv = mla_w_ukv[l].reshape(KV_LORA_RANK, MLA_HEADS, QK_NOPE_DIM + V_HEAD_DIM)
    wuk = wukv[:, :, :QK_NOPE_DIM].reshape(KV_LORA_RANK, MLA_WIDTH)
    wuv = wukv[:, :, QK_NOPE_DIM:].reshape(KV_LORA_RANK, MLA_WIDTH)
    pad_g = lambda g: head_pad(g).reshape(1, HEAD_PAD)
    q, k, v = mla_prep(za, tab, wuq.astype(BF16), wuk.astype(BF16), wuv.astype(BF16),
                       mla_q_a_norm_g[l].reshape(1, -1), mla_kv_a_norm_g[l].reshape(1, -1),
                       pad_g(mla_q_norm_g[l]), pad_g(mla_k_norm_g[l]))
    attn, *cast_out = causal_attention(q, k, v, B, S, casts)

    mixed = pool_mixer(za, pool_w[l].astype(BF16), pool_scale[l].reshape(1, -1), B, S)

    k_m, v_m = mem_kv(mem2d, mem_norm_g[l].reshape(1, -1), mem_w_kv[l].astype(BF16),
                      mem_k_norm_g[l].reshape(1, -1), B, M)
    o_mem = mem_attention(za, k_m, v_m, mem_q_norm_g[l].reshape(1, -1), B, S, M)

    merged = merge_branches(h, attn, mixed, o_mem, w_in, mla_w_out, pool_w_out, mem_w_out, l)
    return merged, cast_out


def kernel(x, mem, positions, attn_norm_g, w_in, mla_q_a_norm_g, mla_w_uq, mla_kv_a_norm_g, mla_w_ukv, mla_q_norm_g, mla_k_norm_g, mla_w_out, pool_w, pool_scale, pool_w_out, mem_norm_g, mem_w_kv, mem_q_norm_g, mem_k_norm_g, mem_w_out, w_o, ffn_norm_g, dense_w_gate, dense_w_up, dense_w_down, router_w, router_b, moe_w_gate, moe_w_up, moe_w_down):
    B, S, D = x.shape
    M = mem.shape[1]
    T = B * S
    depth = attn_norm_g.shape[0]
    moe_block = 512
    y = x.reshape(T, D)
    mem2d = mem.reshape(B * M, D)
    tab = rope_tables(positions)
    w_in = prep_w_in(w_in)
    mla_w_out, pool_w_out, mem_w_out, w_o = (w.astype(BF16) for w in (mla_w_out, pool_w_out, mem_w_out, w_o))
    moe_bf16 = {}
    for l in range(depth):
        h = rmsnorm(y, attn_norm_g[l], BF16)
        i = l // 2
        casts = ()
        if l % 2 == 0 and l + 1 < depth:
            casts = ((moe_w_gate[i], 0, 1), (moe_w_down[i], 0, 2))
        elif l % 2 == 1:
            casts = ((moe_w_up[i], 0, 1), (moe_w_down[i], 1, 2))
        merged, cast_out = _token_mixer(h, mem2d, tab, B, S, M, l, casts, w_in, mla_q_a_norm_g, mla_w_uq,
                                        mla_kv_a_norm_g, mla_w_ukv, mla_q_norm_g, mla_k_norm_g, mla_w_out,
                                        pool_w, pool_scale, pool_w_out, mem_norm_g, mem_w_kv, mem_q_norm_g,
                                        mem_k_norm_g, mem_w_out)
        if casts:
            names = ("gate", "down_lo") if l % 2 == 0 else ("up", "down_hi")
            moe_bf16.update(zip(names, cast_out))
        if l % 2 == 0:
            y, h = matmul_residual_norm(merged, w_o, l, y, ffn_norm_g[l])
            y = dense_swiglu(h, dense_w_gate[i:i + 1].astype(BF16), dense_w_up[i:i + 1].astype(BF16),
                             dense_w_down[i:i + 1].astype(BF16), y)
        else:
            y = matmul(merged, w_o, l, D, F32, residual=y)
            hf, idx, gates = norm_router(y, ffn_norm_g[l], router_w[i], router_b[i])
            slot_tok, dest, block_e, n_used = _routing_tables(idx[:, :TOP_K], T, moe_block)
            ys = moe_swiglu(hf, slot_tok, block_e, n_used, moe_bf16.pop("gate"), moe_bf16.pop("up"),
                            moe_bf16.pop("down_lo"), moe_bf16.pop("down_hi"), tm=moe_block)
            y = combine_rows(y, gates, ys, dest)
    return y.reshape(B, S, D)
```

```python
import functools

import numpy as np
import jax
import jax.numpy as jnp
from jax import lax
from jax.experimental import pallas as pl
from jax.experimental.pallas import tpu as pltpu

F32 = jnp.float32
BF16 = jnp.bfloat16

MEM_HEADS = 4
MEM_HEAD_DIM = 256
MEM_WIDTH = MEM_HEADS * MEM_HEAD_DIM
MLA_HEADS = 8
Q_LORA_RANK = 512
KV_LORA_RANK = 256
QK_NOPE_DIM = 128
QK_ROPE_DIM = 64
QK_HEAD_DIM = QK_NOPE_DIM + QK_ROPE_DIM
V_HEAD_DIM = 128
MLA_WIDTH = MLA_HEADS * V_HEAD_DIM
ROPE_THETA = 10000.0
POOL_WINDOWS = (2, 4, 8, 16)
POOL_GROUPS = 4
POOL_GROUP_DIM = 256
POOL_WIDTH = POOL_GROUPS * POOL_GROUP_DIM
N_BRANCHES = 3
N_EXPERTS = 8
TOP_K = 2
NORM_EPS = 1e-6
LOG2_E = 1.4426950408889634

LANES = 128
SUBLANES = 8
HEAD_PAD = 2 * LANES
ROPE_HALF = QK_ROPE_DIM // 2
POOL_HALO = 16

ZA_POOL = 0
ZA_QMEM = POOL_WIDTH
ZA_CQ = ZA_QMEM + MEM_WIDTH
ZA_CKV = ZA_CQ + Q_LORA_RANK
ZA_KPE = ZA_CKV + KV_LORA_RANK
ZA_WIDTH = 3072

VMEM_LIMIT = 52 * 1024 * 1024


def _cp(*sem):
    return pltpu.CompilerParams(dimension_semantics=sem, vmem_limit_bytes=VMEM_LIMIT)


def _rms(x, eps_dim):
    return lax.rsqrt(jnp.sum(x * x, axis=-1, keepdims=True) * (1.0 / eps_dim) + NORM_EPS)


def _rmsnorm_kernel(x_ref, g_ref, o_ref):
    x = x_ref[...]
    o_ref[...] = (x * _rms(x, x.shape[-1]) * g_ref[...]).astype(o_ref.dtype)


def rmsnorm(x, g, out_dtype, tm=512):
    T, D = x.shape
    return pl.pallas_call(
        _rmsnorm_kernel,
        grid=(T // tm,),
        in_specs=[pl.BlockSpec((tm, D), lambda i: (i, 0)),
                  pl.BlockSpec((1, D), lambda i: (0, 0))],
        out_specs=pl.BlockSpec((tm, D), lambda i: (i, 0)),
        out_shape=jax.ShapeDtypeStruct((T, D), out_dtype),
        compiler_params=_cp("parallel"),
        name="rmsnorm",
    )(x, g.reshape(1, D))


def _norm_router_kernel(x_ref, g_ref, rwh_ref, rwl_ref, rb_ref, h_ref, idx_ref, gate_ref):
    x = x_ref[...]
    h = x * _rms(x, x.shape[-1]) * g_ref[...]
    h_ref[...] = h
    h_hi = h.astype(BF16)
    h_lo = (h - h_hi.astype(F32)).astype(BF16)
    logits = (jnp.dot(h_hi, rwh_ref[...], preferred_element_type=F32)
              + jnp.dot(h_lo, rwh_ref[...], preferred_element_type=F32)
              + jnp.dot(h_hi, rwl_ref[...], preferred_element_type=F32)) + rb_ref[...]
    lane = lax.broadcasted_iota(jnp.int32, logits.shape, 1)
    neg = -jnp.inf
    l1 = jnp.where(lane < N_EXPERTS, logits, neg)
    m1 = jnp.max(l1, axis=-1, keepdims=True)
    i1 = jnp.min(jnp.where(l1 == m1, lane, LANES), axis=-1, keepdims=True)
    l2 = jnp.where(lane == i1, neg, l1)
    m2 = jnp.max(l2, axis=-1, keepdims=True)
    i2 = jnp.min(jnp.where(l2 == m2, lane, LANES), axis=-1, keepdims=True)
    e = jnp.exp(m2 - m1)
    g1 = 1.0 / (1.0 + e)
    g2 = e / (1.0 + e)
    idx_ref[...] = jnp.where(lane == 0, i1, jnp.where(lane == 1, i2, 0))
    gate_ref[...] = jnp.where(lane == 0, g1, jnp.where(lane == 1, g2, 0.0))


def norm_router(x, g, router_w, router_b, tm=512):
    T, D = x.shape
    E = router_w.shape[1]
    rw = jnp.zeros((D, LANES), F32).at[:, :E].set(router_w)
    rw_hi = rw.astype(BF16)
    rw_lo = (rw - rw_hi.astype(F32)).astype(BF16)
    rb = jnp.zeros((1, LANES), F32).at[0, :E].set(router_b)
    return pl.pallas_call(
        _norm_router_kernel,
        grid=(T // tm,),
        in_specs=[pl.BlockSpec((tm, D), lambda i: (i, 0)),
                  pl.BlockSpec((1, D), lambda i: (0, 0)),
                  pl.BlockSpec((D, LANES), lambda i: (0, 0)),
                  pl.BlockSpec((D, LANES), lambda i: (0, 0)),
                  pl.BlockSpec((1, LANES), lambda i: (0, 0))],
        out_specs=[pl.BlockSpec((tm, D), lambda i: (i, 0)),
                   pl.BlockSpec((tm, LANES), lambda i: (i, 0)),
                   pl.BlockSpec((tm, LANES), lambda i: (i, 0))],
        out_shape=[jax.ShapeDtypeStruct((T, D), F32),
                   jax.ShapeDtypeStruct((T, LANES), jnp.int32),
                   jax.ShapeDtypeStruct((T, LANES), F32)],
        compiler_params=_cp("parallel"),
        name="norm_router",
    )(x, g.reshape(1, D), rw_hi, rw_lo, rb)


def _mm_kernel(a_ref, w_ref, o_ref):
    o_ref[...] = jnp.dot(a_ref[...], w_ref[0], preferred_element_type=F32).astype(o_ref.dtype)


def _mm_res_kernel(a_ref, w_ref, r_ref, o_ref):
    o_ref[...] = (r_ref[...] + jnp.dot(a_ref[...], w_ref[0], preferred_element_type=F32)).astype(o_ref.dtype)


def matmul(a, w, l, N, out_dtype, residual=None, tm=1024, tn=1024):
    M, K = a.shape
    tm, tn = min(tm, M), min(tn, N)
    in_specs = [pl.BlockSpec((tm, K), lambda i, j: (i, 0)),
                pl.BlockSpec((1, K, tn), lambda i, j: (l, 0, j))]
    args = [a, w]
    body = _mm_kernel
    if residual is not None:
        in_specs.append(pl.BlockSpec((tm, tn), lambda i, j: (i, j)))
        args.append(residual)
        body = _mm_res_kernel
    return pl.pallas_call(
        body,
        grid=(M // tm, N // tn),
        in_specs=in_specs,
        out_specs=pl.BlockSpec((tm, tn), lambda i, j: (i, j)),
        out_shape=jax.ShapeDtypeStruct((M, N), out_dtype),
        compiler_params=_cp("parallel", "arbitrary"),
        name="matmul",
    )(*args)


def _mm_res_norm_kernel(a_ref, w_ref, r_ref, g_ref, o_ref, h_ref):
    y = r_ref[...] + jnp.dot(a_ref[...], w_ref[0], preferred_element_type=F32)
    o_ref[...] = y
    h_ref[...] = (y * _rms(y, y.shape[-1]) * g_ref[...]).astype(h_ref.dtype)


def matmul_residual_norm(a, w, l, residual, g, tm=512):
    M, K = a.shape
    N = w.shape[2]
    rows = lambda width: pl.BlockSpec((tm, width), lambda i: (i, 0))
    return pl.pallas_call(
        _mm_res_norm_kernel,
        grid=(M // tm,),
        in_specs=[rows(K), pl.BlockSpec((1, K, N), lambda i: (l, 0, 0)), rows(N),
                  pl.BlockSpec((1, N), lambda i: (0, 0))],
        out_specs=[rows(N), rows(N)],
        out_shape=[jax.ShapeDtypeStruct((M, N), F32), jax.ShapeDtypeStruct((M, N), BF16)],
        compiler_params=_cp("parallel"),
        name="matmul_residual_norm",
    )(a, w, residual, g.reshape(1, N))


def _rope_lanes(v):
    z = jnp.zeros(v.shape[:-1] + (LANES // 2 - ROPE_HALF,), v.dtype)
    return jnp.concatenate([v[..., :ROPE_HALF], z, v[..., ROPE_HALF:], z], axis=-1)


def _rope_table_kernel(pos_ref, invf_ref, o_ref):
    ang = pos_ref[...].astype(F32) * invf_ref[...]
    lane = lax.broadcasted_iota(jnp.int32, ang.shape, 1)
    first = lane < ROPE_HALF
    second = (lane >= LANES // 2) & (lane < LANES // 2 + ROPE_HALF)
    c = jnp.cos(ang)
    s = jnp.sin(ang)
    o_ref[:, 0:LANES] = jnp.where(first | second, c, 0.0)
    o_ref[:, LANES:2 * LANES] = jnp.where(first, -s, jnp.where(second, s, 0.0))


def rope_tables(positions, tm=512):
    T = positions.size
    inv_freq = ROPE_THETA ** (-np.arange(0, QK_ROPE_DIM, 2, dtype=np.float32) / QK_ROPE_DIM)
    invf = np.zeros((1, LANES), np.float32)
    invf[0, :ROPE_HALF] = inv_freq
    invf[0, LANES // 2:LANES // 2 + ROPE_HALF] = inv_freq
    return pl.pallas_call(
        _rope_table_kernel,
        grid=(T // tm,),
        in_specs=[pl.BlockSpec((tm, 1), lambda i: (i, 0)),
                  pl.BlockSpec((1, LANES), lambda i: (0, 0))],
        out_specs=pl.BlockSpec((tm, 2 * LANES), lambda i: (i, 0)),
        out_shape=jax.ShapeDtypeStruct((T, 2 * LANES), F32),
        compiler_params=_cp("parallel"),
        name="rope_tables",
    )(positions.reshape(T, 1), jnp.asarray(invf))


def _rope(x, tab):
    return x * tab[:, 0:LANES] + pltpu.roll(x, LANES // 2, 1) * tab[:, LANES:2 * LANES]


def _mla_prep_kernel(cq_ref, ckv_ref, kpe_ref, tab_ref, wuq_ref, wuk_ref, wuv_ref,
                     gqa_ref, gkva_ref, gq_ref, gk_ref, q_ref, k_ref, v_ref):
    tab = tab_ref[...]
    cq = cq_ref[...]
    cqn = (cq * _rms(cq, Q_LORA_RANK) * gqa_ref[...]).astype(BF16)
    qf = jnp.dot(cqn, wuq_ref[...], preferred_element_type=F32)
    ckv = ckv_ref[...]
    ckvn = (ckv * _rms(ckv, KV_LORA_RANK) * gkva_ref[...]).astype(BF16)
    kn = jnp.dot(ckvn, wuk_ref[...], preferred_element_type=F32)
    v_ref[...] = jnp.dot(ckvn, wuv_ref[...], preferred_element_type=F32).astype(v_ref.dtype)
    gq = gq_ref[...]
    gk = gk_ref[...]
    kpe = kpe_ref[...]
    ss_pe = jnp.sum(kpe * kpe, axis=-1, keepdims=True)
    kpe_rot = _rope(kpe * gk[:, LANES:], tab)
    scale = QK_HEAD_DIM ** -0.5 * LOG2_E
    for h in range(MLA_HEADS):
        q0 = qf[:, h * HEAD_PAD:h * HEAD_PAD + LANES]
        q1 = qf[:, h * HEAD_PAD + LANES:(h + 1) * HEAD_PAD]
        ss = jnp.sum(q0 * q0, axis=-1, keepdims=True) + jnp.sum(q1 * q1, axis=-1, keepdims=True)
        rq = lax.rsqrt(ss * (1.0 / QK_HEAD_DIM) + NORM_EPS) * scale
        q_ref[:, h * HEAD_PAD:h * HEAD_PAD + LANES] = (q0 * gq[:, :LANES] * rq).astype(q_ref.dtype)
        q_ref[:, h * HEAD_PAD + LANES:(h + 1) * HEAD_PAD] = (_rope(q1 * gq[:, LANES:], tab) * rq).astype(q_ref.dtype)
        k0 = kn[:, h * LANES:(h + 1) * LANES]
        ssk = jnp.sum(k0 * k0, axis=-1, keepdims=True) + ss_pe
        rk = lax.rsqrt(ssk * (1.0 / QK_HEAD_DIM) + NORM_EPS)
        k_ref[:, h * HEAD_PAD:h * HEAD_PAD + LANES] = (k0 * gk[:, :LANES] * rk).astype(k_ref.dtype)
        k_ref[:, h * HEAD_PAD + LANES:(h + 1) * HEAD_PAD] = (kpe_rot * rk).astype(k_ref.dtype)


def mla_prep(za, tab, wuq, wuk, wuv, gqa, gkva, gq, gk, tm=512):
    T = za.shape[0]
    HP = MLA_HEADS * HEAD_PAD
    full = lambda shape: pl.BlockSpec(shape, lambda i: (0, 0))
    return pl.pallas_call(
        _mla_prep_kernel,
        grid=(T // tm,),
        in_specs=[pl.BlockSpec((tm, Q_LORA_RANK), lambda i: (i, ZA_CQ // Q_LORA_RANK)),
                  pl.BlockSpec((tm, KV_LORA_RANK), lambda i: (i, ZA_CKV // KV_LORA_RANK)),
                  pl.BlockSpec((tm, LANES), lambda i: (i, ZA_KPE // LANES)),
                  pl.BlockSpec((tm, 2 * LANES), lambda i: (i, 0)),
                  full((Q_LORA_RANK, HP)), full((KV_LORA_RANK, MLA_WIDTH)), full((KV_LORA_RANK, MLA_WIDTH)),
                  full((1, Q_LORA_RANK)), full((1, KV_LORA_RANK)), full((1, HEAD_PAD)), full((1, HEAD_PAD))],
        out_specs=[pl.BlockSpec((tm, HP), lambda i: (i, 0)),
                   pl.BlockSpec((tm, HP), lambda i: (i, 0)),
                   pl.BlockSpec((tm, MLA_WIDTH), lambda i: (i, 0))],
        out_shape=[jax.ShapeDtypeStruct((T, HP), BF16),
                   jax.ShapeDtypeStruct((T, HP), BF16),
                   jax.ShapeDtypeStruct((T, MLA_WIDTH), BF16)],
        compiler_params=_cp("parallel"),
        name="mla_prep",
    )(za, za, za, tab, wuq, wuk, wuv, gqa, gkva, gq, gk)


def _causal_attn_kernel(q_ref, k_ref, v_ref, *rest, tq, n_cast):
    cast_src = rest[:n_cast]
    o_ref = rest[n_cast]
    cast_dst = rest[n_cast + 1:]
    for s_ref, d_ref in zip(cast_src, cast_dst):
        d_ref[...] = s_ref[...].astype(d_ref.dtype)
    S = q_ref.shape[0]
    row = lax.broadcasted_iota(jnp.int32, (tq, tq), 0)
    col = lax.broadcasted_iota(jnp.int32, (tq, tq), 1)
    tri = col <= row
    for qi in range(S // tq):
        kl = (qi + 1) * tq
        q = q_ref[qi * tq:kl, :]
        s = lax.dot_general(q, k_ref[0:kl, :], (((1,), (1,)), ((), ())), preferred_element_type=F32)
        s_diag = jnp.where(tri, s[:, kl - tq:kl], -jnp.inf)
        if qi > 0:
            s = jnp.concatenate([s[:, :kl - tq], s_diag], axis=1)
        else:
            s = s_diag
        m = jnp.max(s, axis=-1, keepdims=True)
        p = jnp.exp2(s - m)
        l = jnp.sum(p, axis=-1, keepdims=True)
        o = jnp.dot(p.astype(BF16), v_ref[0:kl, :], preferred_element_type=F32)
        o_ref[qi * tq:kl, :] = (o / l).astype(o_ref.dtype)


def causal_attention(q, k, v, B, S, casts=(), tq=256):
    T = q.shape[0]
    n_steps = B * MLA_HEADS
    in_specs = [pl.BlockSpec((S, HEAD_PAD), lambda b, h: (b, h)),
                pl.BlockSpec((S, HEAD_PAD), lambda b, h: (b, h)),
                pl.BlockSpec((S, V_HEAD_DIM), lambda b, h: (b, h))]
    out_specs = [pl.BlockSpec((S, V_HEAD_DIM), lambda b, h: (b, h))]
    out_shape = [jax.ShapeDtypeStruct((T, MLA_WIDTH), BF16)]
    for src, part, n_parts in casts:
        E, rows, cols = src.shape
        spe = n_steps // E
        rb, cb = rows // spe, cols // n_parts
        assert spe * E == n_steps and rb * spe == rows and rb % 16 == 0
        assert cb * n_parts == cols and cb % LANES == 0
        step = lambda b, h: b * MLA_HEADS + h
        in_specs.append(pl.BlockSpec((1, rb, cb), lambda b, h, part=part, spe=spe:
                                     (step(b, h) // spe, step(b, h) % spe, part)))
        out_specs.append(pl.BlockSpec((1, rb, cb), lambda b, h, spe=spe:
                                      (step(b, h) // spe, step(b, h) % spe, 0)))
        out_shape.append(jax.ShapeDtypeStruct((E, rows, cb), BF16))
    return pl.pallas_call(
        functools.partial(_causal_attn_kernel, tq=tq, n_cast=len(casts)),
        grid=(B, MLA_HEADS),
        in_specs=in_specs,
        out_specs=out_specs,
        out_shape=out_shape,
        compiler_params=_cp("parallel", "parallel"),
        name="causal_attention",
    )(q, k, v, *(c[0] for c in casts))


def _pool_kernel(u_ref, w_ref, sc_ref, o_ref, pad_ref, *, rc):
    S = u_ref.shape[0]
    C = POOL_GROUP_DIM
    t1 = lax.broadcasted_iota(jnp.int32, (rc, C), 0).astype(F32) + 1.0
    for g, win in enumerate(POOL_WINDOWS):
        pad_ref[0:POOL_HALO, :] = jnp.zeros((POOL_HALO, C), F32)
        pad_ref[POOL_HALO:POOL_HALO + S, :] = u_ref[:, g * C:(g + 1) * C]
        for r0 in range(0, S, rc):
            acc = pad_ref[POOL_HALO + r0:POOL_HALO + r0 + rc, :]
            tok = acc
            for kk in range(1, win):
                acc = acc + pad_ref[POOL_HALO + r0 - kk:POOL_HALO + r0 - kk + rc, :]
            cnt = jnp.minimum(t1 + float(r0), float(win))
            pooled = (acc / cnt - tok).astype(BF16)
            mixed = jnp.dot(pooled, w_ref[g], preferred_element_type=F32) * sc_ref[:, g * C:(g + 1) * C]
            o_ref[r0:r0 + rc, g * C:(g + 1) * C] = mixed.astype(o_ref.dtype)


def pool_mixer(za, pool_w, pool_scale, B, S, rc=512):
    T = za.shape[0]
    rc = min(rc, S)
    return pl.pallas_call(
        functools.partial(_pool_kernel, rc=rc),
        grid=(B,),
        in_specs=[pl.BlockSpec((S, POOL_WIDTH), lambda b: (b, ZA_POOL // POOL_WIDTH)),
                  pl.BlockSpec((POOL_GROUPS, POOL_GROUP_DIM, POOL_GROUP_DIM), lambda b: (0, 0, 0)),
                  pl.BlockSpec((1, POOL_WIDTH), lambda b: (0, 0))],
        out_specs=pl.BlockSpec((S, POOL_WIDTH), lambda b: (b, 0)),
        out_shape=jax.ShapeDtypeStruct((T, POOL_WIDTH), BF16),
        scratch_shapes=[pltpu.VMEM((POOL_HALO + S, POOL_GROUP_DIM), F32)],
        compiler_params=_cp("parallel"),
        name="pool_mixer",
    )(za, pool_w, pool_scale)


def _mem_kv_kernel(m_ref, g_ref, w_ref, gk_ref, k_ref, v_ref):
    x = m_ref[...]
    xn = (x * _rms(x, x.shape[-1]) * g_ref[...]).astype(BF16)
    kv = jnp.dot(xn, w_ref[...], preferred_element_type=F32)
    for h in range(MEM_HEADS):
        kh = kv[:, h * MEM_HEAD_DIM:(h + 1) * MEM_HEAD_DIM]
        k_ref[:, h * MEM_HEAD_DIM:(h + 1) * MEM_HEAD_DIM] = (
            kh * _rms(kh, MEM_HEAD_DIM) * gk_ref[...]).astype(k_ref.dtype)
    v_ref[...] = kv[:, MEM_WIDTH:].astype(v_ref.dtype)


def mem_kv(mem2d, g, w_kv, gk, B, M):
    D = mem2d.shape[1]
    return pl.pallas_call(
        _mem_kv_kernel,
        grid=(B,),
        in_specs=[pl.BlockSpec((M, D), lambda b: (b, 0)),
                  pl.BlockSpec((1, D), lambda b: (0, 0)),
                  pl.BlockSpec((D, 2 * MEM_WIDTH), lambda b: (0, 0)),
                  pl.BlockSpec((1, MEM_HEAD_DIM), lambda b: (0, 0))],
        out_specs=[pl.BlockSpec((M, MEM_WIDTH), lambda b: (b, 0)),
                   pl.BlockSpec((M, MEM_WIDTH), lambda b: (b, 0))],
        out_shape=[jax.ShapeDtypeStruct((B * M, MEM_WIDTH), BF16),
                   jax.ShapeDtypeStruct((B * M, MEM_WIDTH), BF16)],
        compiler_params=_cp("parallel"),
        name="mem_kv",
    )(mem2d, g, w_kv, gk)


def _mem_attn_kernel(q_ref, k_ref, v_ref, gq_ref, o_ref):
    scale = MEM_HEAD_DIM ** -0.5 * LOG2_E
    for h in range(MEM_HEADS):
        sl = slice(h * MEM_HEAD_DIM, (h + 1) * MEM_HEAD_DIM)
        qh = q_ref[:, sl]
        qn = (qh * (_rms(qh, MEM_HEAD_DIM) * scale) * gq_ref[...]).astype(BF16)
        s = lax.dot_general(qn, k_ref[:, sl], (((1,), (1,)), ((), ())), preferred_element_type=F32)
        m = jnp.max(s, axis=-1, keepdims=True)
        p = jnp.exp2(s - m)
        l = jnp.sum(p, axis=-1, keepdims=True)
        o = jnp.dot(p.astype(BF16), v_ref[:, sl], preferred_element_type=F32)
        o_ref[:, sl] = (o / l).astype(o_ref.dtype)


def mem_attention(za, k_m, v_m, gq, B, S, M, tq=512):
    T = za.shape[0]
    tq = min(tq, S)
    nq = S // tq
    return pl.pallas_call(
        _mem_attn_kernel,
        grid=(B, nq),
        in_specs=[pl.BlockSpec((tq, MEM_WIDTH), lambda b, i: (b * nq + i, ZA_QMEM // MEM_WIDTH)),
                  pl.BlockSpec((M, MEM_WIDTH), lambda b, i: (b, 0)),
                  pl.BlockSpec((M, MEM_WIDTH), lambda b, i: (b, 0)),
                  pl.BlockSpec((1, MEM_HEAD_DIM), lambda b, i: (0, 0))],
        out_specs=pl.BlockSpec((tq, MEM_WIDTH), lambda b, i: (b * nq + i, 0)),
        out_shape=jax.ShapeDtypeStruct((T, MEM_WIDTH), BF16),
        compiler_params=_cp("parallel", "arbitrary"),
        name="mem_attention",
    )(za, k_m, v_m, gq)


def _merge_kernel(h_ref, a_ref, p_ref, m_ref, g0_ref, g1_ref, g2_ref, w0_ref, w1_ref, w2_ref, o_ref):
    h = h_ref[...]
    acc = None
    for x_ref, g_ref, w_ref in ((a_ref, g0_ref, w0_ref), (p_ref, g1_ref, w1_ref), (m_ref, g2_ref, w2_ref)):
        logit = jnp.dot(h, g_ref[0], preferred_element_type=F32)
        gate = 1.0 / (1.0 + jnp.exp(-logit))
        br = jnp.dot(x_ref[...], w_ref[0], preferred_element_type=F32)
        acc = gate * br if acc is None else acc + gate * br
    o_ref[...] = acc.astype(o_ref.dtype)


def merge_branches(h, a, p, m, w_all, w_a, w_p, w_m, l, tm=1024, tn=512):
    T, D = h.shape
    W = a.shape[1]
    nj = D // tn
    g0 = ZA_WIDTH // tn
    row = lambda width: pl.BlockSpec((tm, width), lambda i, j: (i, 0))
    gate_spec = lambda b: pl.BlockSpec((1, D, tn), lambda i, j, b=b: (l, 0, g0 + b * nj + j))
    out_w = pl.BlockSpec((1, W, tn), lambda i, j: (l, 0, j))
    return pl.pallas_call(
        _merge_kernel,
        grid=(T // tm, nj),
        in_specs=[row(D), row(W), row(W), row(W),
                  gate_spec(0), gate_spec(1), gate_spec(2), out_w, out_w, out_w],
        out_specs=pl.BlockSpec((tm, tn), lambda i, j: (i, j)),
        out_shape=jax.ShapeDtypeStruct((T, D), BF16),
        compiler_params=_cp("parallel", "arbitrary"),
        name="merge_branches",
    )(h, a, p, m, w_all, w_all, w_all, w_a, w_p, w_m)


def _w_in_prep_kernel(w_ref, o_ref):
    s1 = Q_LORA_RANK + KV_LORA_RANK + QK_ROPE_DIM
    s2 = s1 + POOL_WIDTH
    s3 = s2 + MEM_WIDTH
    tk = w_ref.shape[2]
    s1_up = -(-s1 // LANES) * LANES
    o_ref[0, :, ZA_POOL:ZA_POOL + POOL_WIDTH] = w_ref[0, s1:s2, :].T.astype(BF16)
    o_ref[0, :, ZA_QMEM:ZA_QMEM + MEM_WIDTH] = w_ref[0, s2:s3, :].T.astype(BF16)
    low = w_ref[0, 0:s1_up, :].T.astype(BF16)
    n_c = Q_LORA_RANK + KV_LORA_RANK
    o_ref[0, :, ZA_CQ:ZA_CQ + n_c] = low[:, 0:n_c]
    o_ref[0, :, ZA_KPE:ZA_WIDTH] = jnp.zeros((tk, ZA_WIDTH - ZA_KPE), BF16)
    o_ref[0, :, ZA_KPE:ZA_KPE + ROPE_HALF] = low[:, n_c:n_c + ROPE_HALF]
    o_ref[0, :, ZA_KPE + LANES // 2:ZA_KPE + LANES // 2 + ROPE_HALF] = low[:, n_c + ROPE_HALF:s1]
    o_ref[0, :, ZA_WIDTH:] = w_ref[0, s3:, :].T.astype(BF16)


def prep_w_in(w_in, tk=256):
    L, D, win = w_in.shape
    wout = ZA_WIDTH + N_BRANCHES * D
    return pl.pallas_call(
        _w_in_prep_kernel,
        grid=(L, D // tk),
        in_specs=[pl.BlockSpec((1, win, tk), lambda l, i: (l, 0, i))],
        out_specs=pl.BlockSpec((1, tk, wout), lambda l, i: (l, i, 0)),
        out_shape=jax.ShapeDtypeStruct((L, D, wout), BF16),
        compiler_params=_cp("parallel", "parallel"),
        name="prep_w_in",
    )(jnp.swapaxes(w_in, 1, 2))


def _swiglu_accumulate(xb, wg_ref, wu_ref, wd_refs, o_ref):
    a = jnp.dot(xb, wg_ref[0], preferred_element_type=F32)
    b = jnp.dot(xb, wu_ref[0], preferred_element_type=F32)
    act = (a / (1.0 + jnp.exp(-a)) * b).astype(BF16)
    width = o_ref.shape[1] // len(wd_refs)
    for n, wd_ref in enumerate(wd_refs):
        o_ref[:, n * width:(n + 1) * width] += jnp.dot(act, wd_ref[0], preferred_element_type=F32)


def _dense_swiglu_kernel(x_ref, wg_ref, wu_ref, wd_ref, r_ref, o_ref):
    @pl.when(pl.program_id(1) == 0)
    def _():
        o_ref[...] = r_ref[...]

    _swiglu_accumulate(x_ref[...], wg_ref, wu_ref, (wd_ref,), o_ref)


def dense_swiglu(x, w_gate, w_up, w_down, residual, tm=1024, tf=512):
    T, D = x.shape
    F = w_gate.shape[2]
    tm = min(tm, T)
    return pl.pallas_call(
        _dense_swiglu_kernel,
        grid=(T // tm, F // tf),
        in_specs=[pl.BlockSpec((tm, D), lambda i, f: (i, 0)),
                  pl.BlockSpec((1, D, tf), lambda i, f: (0, 0, f)),
                  pl.BlockSpec((1, D, tf), lambda i, f: (0, 0, f)),
                  pl.BlockSpec((1, tf, D), lambda i, f: (0, f, 0)),
                  pl.BlockSpec((tm, D), lambda i, f: (i, 0), pipeline_mode=pl.Buffered(1))],
        out_specs=pl.BlockSpec((tm, D), lambda i, f: (i, 0)),
        out_shape=jax.ShapeDtypeStruct((T, D), F32),
        compiler_params=_cp("parallel", "arbitrary"),
        name="dense_swiglu",
    )(x, w_gate, w_up, w_down, residual)


def _row_copy(src_hbm, dst_ref, src_row, dst_row, sem):
    return pltpu.make_async_copy(src_hbm.at[pl.ds(src_row, 1)], dst_ref.at[pl.ds(dst_row, 1)], sem)


def _moe_rows_per_step(tm, nf):
    return -(-tm // (nf * SUBLANES)) * SUBLANES


def _moe_swiglu_kernel(tok_ref, be_ref, nu_ref, h_hbm, wg_ref, wu_ref, wd0_ref, wd1_ref, o_ref,
                       xg_ref, xb_ref, sem, *, tm, nf):
    del be_ref
    i = pl.program_id(0)
    f = pl.program_id(1)
    n_used = nu_ref[0]
    rows_per_step = _moe_rows_per_step(tm, nf)
    n_fetch = rows_per_step * nf

    def fetch(block, r, slot):
        tok = tok_ref[block * tm + jnp.minimum(r, tm - 1)]
        return _row_copy(h_hbm, xg_ref.at[slot], tok, r, sem.at[slot])

    def wait_rows(slot, n_rows):
        rows = xg_ref.at[slot, pl.ds(0, n_rows)]
        pltpu.make_async_copy(rows, rows, sem.at[slot]).wait()

    @pl.when((i >= n_used) & (f == 0))
    def _():
        o_ref[...] = jnp.zeros_like(o_ref)

    @pl.when(i < n_used)
    def _():
        slot = i % 2

        @pl.when((i == 0) & (f == 0))
        def _():
            def start(r, c):
                fetch(0, r, 0).start()
                return c
            lax.fori_loop(0, n_fetch, start, 0)
            for j in range(rows_per_step):
                fetch(jnp.minimum(1, n_used - 1), j, 1).start()

        @pl.when(f == 0)
        def _():
            wait_rows(slot, n_fetch)
            xb_ref[...] = xg_ref[slot, 0:tm, :].astype(BF16)
            o_ref[...] = jnp.zeros_like(o_ref)

        wrap = (f == nf - 1).astype(jnp.int32)
        chunk = (f + 1) * (1 - wrap)
        target = jnp.minimum(i + 1 + wrap, n_used - 1)
        target_slot = (i + 1 + wrap) % 2
        for j in range(rows_per_step):
            fetch(target, chunk * rows_per_step + j, target_slot).start()

        _swiglu_accumulate(xb_ref[...], wg_ref, wu_ref, (wd0_ref, wd1_ref), o_ref)

        @pl.when((i == n_used - 1) & (f == nf - 1))
        def _():
            wait_rows(1 - slot, n_fetch)
            wait_rows(slot, rows_per_step)


def moe_swiglu(h, slot_tok, block_e, n_used, w_gate, w_up, w_down_lo, w_down_hi, tm, tf=1024):
    n_slots = slot_tok.shape[0]
    D = h.shape[1]
    F = w_gate.shape[2]
    while F % tf:
        tf //= 2
    nf = F // tf
    xg_rows = _moe_rows_per_step(tm, nf) * nf
    blk = lambda i, nu: jnp.minimum(i, nu[0] - 1)
    chunk = lambda i, f, nu: jnp.where(i < nu[0], f, nf - 1)
    up_spec = pl.BlockSpec((1, D, tf), lambda i, f, tok, be, nu: (be[blk(i, nu)], 0, chunk(i, f, nu)))
    down_spec = pl.BlockSpec((1, tf, D // 2), lambda i, f, tok, be, nu: (be[blk(i, nu)], chunk(i, f, nu), 0))
    return pl.pallas_call(
        functools.partial(_moe_swiglu_kernel, tm=tm, nf=nf),
        grid_spec=pltpu.PrefetchScalarGridSpec(
            num_scalar_prefetch=3,
            grid=(n_slots // tm, nf),
            in_specs=[pl.BlockSpec(memory_space=pl.ANY), up_spec, up_spec, down_spec, down_spec],
            out_specs=pl.BlockSpec((tm, D), lambda i, f, tok, be, nu: (i, 0)),
            scratch_shapes=[pltpu.VMEM((2, xg_rows, D), F32),
                            pltpu.VMEM((tm, D), BF16),
                            pltpu.SemaphoreType.DMA((2,))]),
        out_shape=jax.ShapeDtypeStruct((n_slots, D), F32),
        compiler_params=_cp("arbitrary", "arbitrary"),
        name="moe_swiglu",
    )(slot_tok, block_e, n_used, h, w_gate, w_up, w_down_lo, w_down_hi)


def _combine_kernel(dest_ref, y_ref, gate_ref, ys_hbm, o_ref, buf_ref, sem, *, tb):
    i = pl.program_id(0)

    def fetch(step, slot):
        base = step * tb

        def start(r, c):
            for k in range(TOP_K):
                _row_copy(ys_hbm, buf_ref.at[slot, k], dest_ref[TOP_K * (base + r) + k], r, sem.at[slot]).start()
            return c

        lax.fori_loop(0, tb, start, 0, unroll=8)

    @pl.when(i == 0)
    def _():
        fetch(0, 0)

    @pl.when(i + 1 < pl.num_programs(0))
    def _():
        fetch(i + 1, (i + 1) % 2)

    slot = i % 2
    pltpu.make_async_copy(buf_ref.at[slot], buf_ref.at[slot], sem.at[slot]).wait()
    g = gate_ref[...]
    o_ref[...] = y_ref[...] + g[:, 0:1] * buf_ref[slot, 0] + g[:, 1:2] * buf_ref[slot, 1]


def combine_rows(y, gates, ys, dest, tb=256):
    T, D = y.shape
    return pl.pallas_call(
        functools.partial(_combine_kernel, tb=tb),
        grid_spec=pltpu.PrefetchScalarGridSpec(
            num_scalar_prefetch=1,
            grid=(T // tb,),
            in_specs=[pl.BlockSpec((tb, D), lambda i, d: (i, 0)),
                      pl.BlockSpec((tb, LANES), lambda i, d: (i, 0)),
                      pl.BlockSpec(memory_space=pl.ANY)],
            out_specs=pl.BlockSpec((tb, D), lambda i, d: (i, 0)),
            scratch_shapes=[pltpu.VMEM((2, TOP_K, tb, D), F32),
                            pltpu.SemaphoreType.DMA((2,))]),
        out_shape=jax.ShapeDtypeStruct((T, D), F32),
        compiler_params=_cp("arbitrary"),
        name="moe_combine",
    )(dest, y, gates, ys)


def _routing_tables(top_idx, T, block):
    A = T * TOP_K
    E = N_EXPERTS
    flat_e = top_idx.reshape(A)
    onehot = (flat_e[:, None] == jnp.arange(E, dtype=jnp.int32)[None, :]).astype(jnp.int32)
    csum = jnp.cumsum(onehot, axis=0)
    counts = csum[-1]
    rank = jnp.sum((csum - 1) * onehot, axis=1)
    padded = (counts + block - 1) // block * block
    pends = jnp.cumsum(padded)
    pstarts = pends - padded
    starts = jnp.cumsum(counts) - counts
    dest = (pstarts[flat_e] + rank).astype(jnp.int32)
    n_blocks = A // block + E
    n_slots = n_blocks * block
    order = jnp.argsort(flat_e, stable=True).astype(jnp.int32)
    slot = jnp.arange(n_slots, dtype=jnp.int32)
    slot_e = jnp.clip(jnp.searchsorted(pends, slot, side='right'), 0, E - 1).astype(jnp.int32)
    r = slot - pstarts[slot_e]
    valid = r < counts[slot_e]
    src = jnp.clip(starts[slot_e] + r, 0, A - 1)
    slot_tok = jnp.where(valid, order[src] // TOP_K, 0).astype(jnp.int32)
    block_e = slot_e[::block]
    n_used = (pends[-1:] // block).astype(jnp.int32)
    return slot_tok, dest, block_e, n_used


def _token_mixer(h, mem2d, tab, B, S, M, l, casts, w_in, mla_q_a_norm_g, mla_w_uq, mla_kv_a_norm_g, mla_w_ukv,
                 mla_q_norm_g, mla_k_norm_g, mla_w_out, pool_w, pool_scale, pool_w_out,
                 mem_norm_g, mem_w_kv, mem_q_norm_g, mem_k_norm_g, mem_w_out):
    za = matmul(h, w_in, l, ZA_WIDTH, F32, tn=ZA_WIDTH // 2)

    head_pad = lambda v: jnp.concatenate([v[..., :QK_NOPE_DIM], _rope_lanes(v[..., QK_NOPE_DIM:])], axis=-1)
    wuq = head_pad(mla_w_uq[l].reshape(Q_LORA_RANK, MLA_HEADS, QK_HEAD_DIM)).reshape(
        Q_LORA_RANK, MLA_HEADS * HEAD_PAD)
    wukv = mla_w_ukv[l].reshape(KV_LORA_RANK, MLA_HEADS, QK_NOPE_DIM + V_HEAD_DIM)
    wuk = wukv[:, :, :QK_NOPE_DIM].reshape(KV_LORA_RANK, MLA_WIDTH)
    wuv = wukv[:, :, QK_NOPE_DIM:].reshape(KV_LORA_RANK, MLA_WIDTH)
    pad_g = lambda g: head_pad(g).reshape(1, HEAD_PAD)
    q, k, v = mla_prep(za, tab, wuq.astype(BF16), wuk.astype(BF16), wuv.astype(BF16),
                       mla_q_a_norm_g[l].reshape(1, -1), mla_kv_a_norm_g[l].reshape(1, -1),
                       pad_g(mla_q_norm_g[l]), pad_g(mla_k_norm_g[l]))
    attn, *cast_out = causal_attention(q, k, v, B, S, casts)

    mixed = pool_mixer(za, pool_w[l].astype(BF16), pool_scale[l].reshape(1, -1), B, S)

    k_m, v_m = mem_kv(mem2d, mem_norm_g[l].reshape(1, -1), mem_w_kv[l].astype(BF16),
                      mem_k_norm_g[l].reshape(1, -1), B, M)
    o_mem = mem_attention(za, k_m, v_m, mem_q_norm_g[l].reshape(1, -1), B, S, M)

    merged = merge_branches(h, attn, mixed, o_mem, w_in, mla_w_out, pool_w_out, mem_w_out, l)
    return merged, cast_out


def kernel(x, mem, positions, attn_norm_g, w_in, mla_q_a_norm_g, mla_w_uq, mla_kv_a_norm_g, mla_w_ukv, mla_q_norm_g, mla_k_norm_g, mla_w_out, pool_w, pool_scale, pool_w_out, mem_norm_g, mem_w_kv, mem_q_norm_g, mem_k_norm_g, mem_w_out, w_o, ffn_norm_g, dense_w_gate, dense_w_up, dense_w_down, router_w, router_b, moe_w_gate, moe_w_up, moe_w_down):
    B, S, D = x.shape
    M = mem.shape[1]
    T = B * S
    depth = attn_norm_g.shape[0]
    moe_block = 512
    y = x.reshape(T, D)
    mem2d = mem.reshape(B * M, D)
    tab = rope_tables(positions)
    w_in = prep_w_in(w_in)
    mla_w_out, pool_w_out, mem_w_out, w_o = (w.astype(BF16) for w in (mla_w_out, pool_w_out, mem_w_out, w_o))
    moe_bf16 = {}
    for l in range(depth):
        h = rmsnorm(y, attn_norm_g[l], BF16)
        i = l // 2
        casts = ()
        if l % 2 == 0 and l + 1 < depth:
            casts = ((moe_w_gate[i], 0, 1), (moe_w_down[i], 0, 2))
        elif l % 2 == 1:
            casts = ((moe_w_up[i], 0, 1), (moe_w_down[i], 1, 2))
        merged, cast_out = _token_mixer(h, mem2d, tab, B, S, M, l, casts, w_in, mla_q_a_norm_g, mla_w_uq,
                                        mla_kv_a_norm_g, mla_w_ukv, mla_q_norm_g, mla_k_norm_g, mla_w_out,
                                        pool_w, pool_scale, pool_w_out, mem_norm_g, mem_w_kv, mem_q_norm_g,
                                        mem_k_norm_g, mem_w_out)
        if casts:
            names = ("gate", "down_lo") if l % 2 == 0 else ("up", "down_hi")
            moe_bf16.update(zip(names, cast_out))
        if l % 2 == 0:
            y, h = matmul_residual_norm(merged, w_o, l, y, ffn_norm_g[l])
            y = dense_swiglu(h, dense_w_gate[i:i + 1].astype(BF16), dense_w_up[i:i + 1].astype(BF16),
                             dense_w_down[i:i + 1].astype(BF16), y)
        else:
            y = matmul(merged, w_o, l, D, F32, residual=y, tm=512, tn=D)
            hf, idx, gates = norm_router(y, ffn_norm_g[l], router_w[i], router_b[i])
            slot_tok, dest, block_e, n_used = _routing_tables(idx[:, :TOP_K], T, moe_block)
            ys = moe_swiglu(hf, slot_tok, block_e, n_used, moe_bf16.pop("gate"), moe_bf16.pop("up"),
                            moe_bf16.pop("down_lo"), moe_bf16.pop("down_hi"), tm=moe_block)
            y = combine_rows(y, gates, ys, dest)
    return y.reshape(B, S, D)
```

```python
import functools

import numpy as np
import jax
import jax.numpy as jnp
from jax import lax
from jax.experimental import pallas as pl
from jax.experimental.pallas import tpu as pltpu

F32 = jnp.float32
BF16 = jnp.bfloat16

MEM_HEADS = 4
MEM_HEAD_DIM = 256
MEM_WIDTH = MEM_HEADS * MEM_HEAD_DIM
MLA_HEADS = 8
Q_LORA_RANK = 512
KV_LORA_RANK = 256
QK_NOPE_DIM = 128
QK_ROPE_DIM = 64
QK_HEAD_DIM = QK_NOPE_DIM + QK_ROPE_DIM
V_HEAD_DIM = 128
MLA_WIDTH = MLA_HEADS * V_HEAD_DIM
ROPE_THETA = 10000.0
POOL_WINDOWS = (2, 4, 8, 16)
POOL_GROUPS = 4
POOL_GROUP_DIM = 256
POOL_WIDTH = POOL_GROUPS * POOL_GROUP_DIM
N_BRANCHES = 3
N_EXPERTS = 8
TOP_K = 2
NORM_EPS = 1e-6
LOG2_E = 1.4426950408889634

LANES = 128
SUBLANES = 8
HEAD_PAD = 2 * LANES
ROPE_HALF = QK_ROPE_DIM // 2
POOL_HALO = 16

ZA_POOL = 0
ZA_QMEM = POOL_WIDTH
ZA_CQ = ZA_QMEM + MEM_WIDTH
ZA_CKV = ZA_CQ + Q_LORA_RANK
ZA_KPE = ZA_CKV + KV_LORA_RANK
ZA_WIDTH = 3072

DENSE_TF = 1024
VMEM_LIMIT = 52 * 1024 * 1024


def _cp(*sem):
    return pltpu.CompilerParams(dimension_semantics=sem, vmem_limit_bytes=VMEM_LIMIT)


def _rms(x, eps_dim):
    return lax.rsqrt(jnp.sum(x * x, axis=-1, keepdims=True) * (1.0 / eps_dim) + NORM_EPS)


def _rmsnorm_kernel(x_ref, g_ref, o_ref):
    x = x_ref[...]
    o_ref[...] = (x * _rms(x, x.shape[-1]) * g_ref[...]).astype(o_ref.dtype)


def rmsnorm(x, g, out_dtype, tm=512):
    T, D = x.shape
    return pl.pallas_call(
        _rmsnorm_kernel,
        grid=(T // tm,),
        in_specs=[pl.BlockSpec((tm, D), lambda i: (i, 0)),
                  pl.BlockSpec((1, D), lambda i: (0, 0))],
        out_specs=pl.BlockSpec((tm, D), lambda i: (i, 0)),
        out_shape=jax.ShapeDtypeStruct((T, D), out_dtype),
        compiler_params=_cp("parallel"),
        name="rmsnorm",
    )(x, g.reshape(1, D))


def _norm_router_kernel(x_ref, g_ref, rwh_ref, rwl_ref, rb_ref, h_ref, idx_ref, gate_ref):
    x = x_ref[...]
    h = x * _rms(x, x.shape[-1]) * g_ref[...]
    h_ref[...] = h
    h_hi = h.astype(BF16)
    h_lo = (h - h_hi.astype(F32)).astype(BF16)
    logits = (jnp.dot(h_hi, rwh_ref[...], preferred_element_type=F32)
              + jnp.dot(h_lo, rwh_ref[...], preferred_element_type=F32)
              + jnp.dot(h_hi, rwl_ref[...], preferred_element_type=F32)) + rb_ref[...]
    lane = lax.broadcasted_iota(jnp.int32, logits.shape, 1)
    neg = -jnp.inf
    l1 = jnp.where(lane < N_EXPERTS, logits, neg)
    m1 = jnp.max(l1, axis=-1, keepdims=True)
    i1 = jnp.min(jnp.where(l1 == m1, lane, LANES), axis=-1, keepdims=True)
    l2 = jnp.where(lane == i1, neg, l1)
    m2 = jnp.max(l2, axis=-1, keepdims=True)
    i2 = jnp.min(jnp.where(l2 == m2, lane, LANES), axis=-1, keepdims=True)
    e = jnp.exp(m2 - m1)
    g1 = 1.0 / (1.0 + e)
    g2 = e / (1.0 + e)
    idx_ref[...] = jnp.where(lane == 0, i1, jnp.where(lane == 1, i2, 0))
    gate_ref[...] = jnp.where(lane == 0, g1, jnp.where(lane == 1, g2, 0.0))


def norm_router(x, g, router_w, router_b, tm=512):
    T, D = x.shape
    E = router_w.shape[1]
    rw = jnp.zeros((D, LANES), F32).at[:, :E].set(router_w)
    rw_hi = rw.astype(BF16)
    rw_lo = (rw - rw_hi.astype(F32)).astype(BF16)
    rb = jnp.zeros((1, LANES), F32).at[0, :E].set(router_b)
    return pl.pallas_call(
        _norm_router_kernel,
        grid=(T // tm,),
        in_specs=[pl.BlockSpec((tm, D), lambda i: (i, 0)),
                  pl.BlockSpec((1, D), lambda i: (0, 0)),
                  pl.BlockSpec((D, LANES), lambda i: (0, 0)),
                  pl.BlockSpec((D, LANES), lambda i: (0, 0)),
                  pl.BlockSpec((1, LANES), lambda i: (0, 0))],
        out_specs=[pl.BlockSpec((tm, D), lambda i: (i, 0)),
                   pl.BlockSpec((tm, LANES), lambda i: (i, 0)),
                   pl.BlockSpec((tm, LANES), lambda i: (i, 0))],
        out_shape=[jax.ShapeDtypeStruct((T, D), F32),
                   jax.ShapeDtypeStruct((T, LANES), jnp.int32),
                   jax.ShapeDtypeStruct((T, LANES), F32)],
        compiler_params=_cp("parallel"),
        name="norm_router",
    )(x, g.reshape(1, D), rw_hi, rw_lo, rb)


def _mm_kernel(a_ref, w_ref, o_ref):
    o_ref[...] = jnp.dot(a_ref[...], w_ref[0], preferred_element_type=F32).astype(o_ref.dtype)


def _mm_res_kernel(a_ref, w_ref, r_ref, o_ref):
    o_ref[...] = (r_ref[...] + jnp.dot(a_ref[...], w_ref[0], preferred_element_type=F32)).astype(o_ref.dtype)


def matmul(a, w, l, N, out_dtype, residual=None, tm=1024, tn=1024):
    M, K = a.shape
    tm, tn = min(tm, M), min(tn, N)
    in_specs = [pl.BlockSpec((tm, K), lambda i, j: (i, 0)),
                pl.BlockSpec((1, K, tn), lambda i, j: (l, 0, j))]
    args = [a, w]
    body = _mm_kernel
    if residual is not None:
        in_specs.append(pl.BlockSpec((tm, tn), lambda i, j: (i, j)))
        args.append(residual)
        body = _mm_res_kernel
    return pl.pallas_call(
        body,
        grid=(M // tm, N // tn),
        in_specs=in_specs,
        out_specs=pl.BlockSpec((tm, tn), lambda i, j: (i, j)),
        out_shape=jax.ShapeDtypeStruct((M, N), out_dtype),
        compiler_params=_cp("parallel", "arbitrary"),
        name="matmul",
    )(*args)


def _mm_res_norm_kernel(a_ref, w_ref, r_ref, g_ref, o_ref, h_ref):
    y = r_ref[...] + jnp.dot(a_ref[...], w_ref[0], preferred_element_type=F32)
    o_ref[...] = y
    h_ref[...] = (y * _rms(y, y.shape[-1]) * g_ref[...]).astype(h_ref.dtype)


def matmul_residual_norm(a, w, l, residual, g, tm=512):
    M, K = a.shape
    N = w.shape[2]
    rows = lambda width: pl.BlockSpec((tm, width), lambda i: (i, 0))
    return pl.pallas_call(
        _mm_res_norm_kernel,
        grid=(M // tm,),
        in_specs=[rows(K), pl.BlockSpec((1, K, N), lambda i: (l, 0, 0)), rows(N),
                  pl.BlockSpec((1, N), lambda i: (0, 0))],
        out_specs=[rows(N), rows(N)],
        out_shape=[jax.ShapeDtypeStruct((M, N), F32), jax.ShapeDtypeStruct((M, N), BF16)],
        compiler_params=_cp("parallel"),
        name="matmul_residual_norm",
    )(a, w, residual, g.reshape(1, N))


def _rope_lanes(v):
    z = jnp.zeros(v.shape[:-1] + (LANES // 2 - ROPE_HALF,), v.dtype)
    return jnp.concatenate([v[..., :ROPE_HALF], z, v[..., ROPE_HALF:], z], axis=-1)


def _rope_table_kernel(pos_ref, invf_ref, o_ref):
    ang = pos_ref[...].astype(F32) * invf_ref[...]
    lane = lax.broadcasted_iota(jnp.int32, ang.shape, 1)
    first = lane < ROPE_HALF
    second = (lane >= LANES // 2) & (lane < LANES // 2 + ROPE_HALF)
    c = jnp.cos(ang)
    s = jnp.sin(ang)
    o_ref[:, 0:LANES] = jnp.where(first | second, c, 0.0)
    o_ref[:, LANES:2 * LANES] = jnp.where(first, -s, jnp.where(second, s, 0.0))


def rope_tables(positions, tm=512):
    T = positions.size
    inv_freq = ROPE_THETA ** (-np.arange(0, QK_ROPE_DIM, 2, dtype=np.float32) / QK_ROPE_DIM)
    invf = np.zeros((1, LANES), np.float32)
    invf[0, :ROPE_HALF] = inv_freq
    invf[0, LANES // 2:LANES // 2 + ROPE_HALF] = inv_freq
    return pl.pallas_call(
        _rope_table_kernel,
        grid=(T // tm,),
        in_specs=[pl.BlockSpec((tm, 1), lambda i: (i, 0)),
                  pl.BlockSpec((1, LANES), lambda i: (0, 0))],
        out_specs=pl.BlockSpec((tm, 2 * LANES), lambda i: (i, 0)),
        out_shape=jax.ShapeDtypeStruct((T, 2 * LANES), F32),
        compiler_params=_cp("parallel"),
        name="rope_tables",
    )(positions.reshape(T, 1), jnp.asarray(invf))


def _rope(x, tab):
    return x * tab[:, 0:LANES] + pltpu.roll(x, LANES // 2, 1) * tab[:, LANES:2 * LANES]


def _mla_prep_kernel(cq_ref, ckv_ref, kpe_ref, tab_ref, wuq_ref, wuk_ref, wuv_ref,
                     gqa_ref, gkva_ref, gq_ref, gk_ref, q_ref, k_ref, v_ref):
    tab = tab_ref[...]
    cq = cq_ref[...]
    cqn = (cq * _rms(cq, Q_LORA_RANK) * gqa_ref[...]).astype(BF16)
    qf = jnp.dot(cqn, wuq_ref[...], preferred_element_type=F32)
    ckv = ckv_ref[...]
    ckvn = (ckv * _rms(ckv, KV_LORA_RANK) * gkva_ref[...]).astype(BF16)
    kn = jnp.dot(ckvn, wuk_ref[...], preferred_element_type=F32)
    v_ref[...] = jnp.dot(ckvn, wuv_ref[...], preferred_element_type=F32).astype(v_ref.dtype)
    gq = gq_ref[...]
    gk = gk_ref[...]
    kpe = kpe_ref[...]
    ss_pe = jnp.sum(kpe * kpe, axis=-1, keepdims=True)
    kpe_rot = _rope(kpe * gk[:, LANES:], tab)
    scale = QK_HEAD_DIM ** -0.5 * LOG2_E
    for h in range(MLA_HEADS):
        q0 = qf[:, h * HEAD_PAD:h * HEAD_PAD + LANES]
        q1 = qf[:, h * HEAD_PAD + LANES:(h + 1) * HEAD_PAD]
        ss = jnp.sum(q0 * q0, axis=-1, keepdims=True) + jnp.sum(q1 * q1, axis=-1, keepdims=True)
        rq = lax.rsqrt(ss * (1.0 / QK_HEAD_DIM) + NORM_EPS) * scale
        q_ref[:, h * HEAD_PAD:h * HEAD_PAD + LANES] = (q0 * gq[:, :LANES] * rq).astype(q_ref.dtype)
        q_ref[:, h * HEAD_PAD + LANES:(h + 1) * HEAD_PAD] = (_rope(q1 * gq[:, LANES:], tab) * rq).astype(q_ref.dtype)
        k0 = kn[:, h * LANES:(h + 1) * LANES]
        ssk = jnp.sum(k0 * k0, axis=-1, keepdims=True) + ss_pe
        rk = lax.rsqrt(ssk * (1.0 / QK_HEAD_DIM) + NORM_EPS)
        k_ref[:, h * HEAD_PAD:h * HEAD_PAD + LANES] = (k0 * gk[:, :LANES] * rk).astype(k_ref.dtype)
        k_ref[:, h * HEAD_PAD + LANES:(h + 1) * HEAD_PAD] = (kpe_rot * rk).astype(k_ref.dtype)


def mla_prep(za, tab, wuq, wuk, wuv, gqa, gkva, gq, gk, tm=512):
    T = za.shape[0]
    HP = MLA_HEADS * HEAD_PAD
    full = lambda shape: pl.BlockSpec(shape, lambda i: (0, 0))
    return pl.pallas_call(
        _mla_prep_kernel,
        grid=(T // tm,),
        in_specs=[pl.BlockSpec((tm, Q_LORA_RANK), lambda i: (i, ZA_CQ // Q_LORA_RANK)),
                  pl.BlockSpec((tm, KV_LORA_RANK), lambda i: (i, ZA_CKV // KV_LORA_RANK)),
                  pl.BlockSpec((tm, LANES), lambda i: (i, ZA_KPE // LANES)),
                  pl.BlockSpec((tm, 2 * LANES), lambda i: (i, 0)),
                  full((Q_LORA_RANK, HP)), full((KV_LORA_RANK, MLA_WIDTH)), full((KV_LORA_RANK, MLA_WIDTH)),
                  full((1, Q_LORA_RANK)), full((1, KV_LORA_RANK)), full((1, HEAD_PAD)), full((1, HEAD_PAD))],
        out_specs=[pl.BlockSpec((tm, HP), lambda i: (i, 0)),
                   pl.BlockSpec((tm, HP), lambda i: (i, 0)),
                   pl.BlockSpec((tm, MLA_WIDTH), lambda i: (i, 0))],
        out_shape=[jax.ShapeDtypeStruct((T, HP), BF16),
                   jax.ShapeDtypeStruct((T, HP), BF16),
                   jax.ShapeDtypeStruct((T, MLA_WIDTH), BF16)],
        compiler_params=_cp("parallel"),
        name="mla_prep",
    )(za, za, za, tab, wuq, wuk, wuv, gqa, gkva, gq, gk)


def _causal_attn_kernel(q_ref, k_ref, v_ref, *rest, tq, n_cast):
    cast_src = rest[:n_cast]
    o_ref = rest[n_cast]
    cast_dst = rest[n_cast + 1:]
    for s_ref, d_ref in zip(cast_src, cast_dst):
        d_ref[...] = s_ref[...].astype(d_ref.dtype)
    S = q_ref.shape[0]
    row = lax.broadcasted_iota(jnp.int32, (tq, tq), 0)
    col = lax.broadcasted_iota(jnp.int32, (tq, tq), 1)
    tri = col <= row
    for qi in range(S // tq):
        kl = (qi + 1) * tq
        q = q_ref[qi * tq:kl, :]
        s = lax.dot_general(q, k_ref[0:kl, :], (((1,), (1,)), ((), ())), preferred_element_type=F32)
        s_diag = jnp.where(tri, s[:, kl - tq:kl], -jnp.inf)
        if qi > 0:
            s = jnp.concatenate([s[:, :kl - tq], s_diag], axis=1)
        else:
            s = s_diag
        m = jnp.max(s, axis=-1, keepdims=True)
        p = jnp.exp2(s - m)
        l = jnp.sum(p, axis=-1, keepdims=True)
        o = jnp.dot(p.astype(BF16), v_ref[0:kl, :], preferred_element_type=F32)
        o_ref[qi * tq:kl, :] = (o / l).astype(o_ref.dtype)


def causal_attention(q, k, v, B, S, casts=(), tq=256):
    T = q.shape[0]
    n_steps = B * MLA_HEADS
    in_specs = [pl.BlockSpec((S, HEAD_PAD), lambda b, h: (b, h)),
                pl.BlockSpec((S, HEAD_PAD), lambda b, h: (b, h)),
                pl.BlockSpec((S, V_HEAD_DIM), lambda b, h: (b, h))]
    out_specs = [pl.BlockSpec((S, V_HEAD_DIM), lambda b, h: (b, h))]
    out_shape = [jax.ShapeDtypeStruct((T, MLA_WIDTH), BF16)]
    for src, part, n_parts in casts:
        E, rows, cols = src.shape
        spe = n_steps // E
        rb, cb = rows // spe, cols // n_parts
        assert spe * E == n_steps and rb * spe == rows and rb % 16 == 0
        assert cb * n_parts == cols and cb % LANES == 0
        step = lambda b, h: b * MLA_HEADS + h
        in_specs.append(pl.BlockSpec((1, rb, cb), lambda b, h, part=part, spe=spe:
                                     (step(b, h) // spe, step(b, h) % spe, part)))
        out_specs.append(pl.BlockSpec((1, rb, cb), lambda b, h, spe=spe:
                                      (step(b, h) // spe, step(b, h) % spe, 0)))
        out_shape.append(jax.ShapeDtypeStruct((E, rows, cb), BF16))
    return pl.pallas_call(
        functools.partial(_causal_attn_kernel, tq=tq, n_cast=len(casts)),
        grid=(B, MLA_HEADS),
        in_specs=in_specs,
        out_specs=out_specs,
        out_shape=out_shape,
        compiler_params=_cp("parallel", "parallel"),
        name="causal_attention",
    )(q, k, v, *(c[0] for c in casts))


def _pool_kernel(u_ref, w_ref, sc_ref, o_ref, pad_ref, *, rc):
    S = u_ref.shape[0]
    C = POOL_GROUP_DIM
    t1 = lax.broadcasted_iota(jnp.int32, (rc, C), 0).astype(F32) + 1.0
    for g, win in enumerate(POOL_WINDOWS):
        pad_ref[0:POOL_HALO, :] = jnp.zeros((POOL_HALO, C), F32)
        pad_ref[POOL_HALO:POOL_HALO + S, :] = u_ref[:, g * C:(g + 1) * C]
        for r0 in range(0, S, rc):
            acc = pad_ref[POOL_HALO + r0:POOL_HALO + r0 + rc, :]
            tok = acc
            for kk in range(1, win):
                acc = acc + pad_ref[POOL_HALO + r0 - kk:POOL_HALO + r0 - kk + rc, :]
            cnt = jnp.minimum(t1 + float(r0), float(win))
            pooled = (acc / cnt - tok).astype(BF16)
            mixed = jnp.dot(pooled, w_ref[g], preferred_element_type=F32) * sc_ref[:, g * C:(g + 1) * C]
            o_ref[r0:r0 + rc, g * C:(g + 1) * C] = mixed.astype(o_ref.dtype)


def pool_mixer(za, pool_w, pool_scale, B, S, rc=512):
    T = za.shape[0]
    rc = min(rc, S)
    return pl.pallas_call(
        functools.partial(_pool_kernel, rc=rc),
        grid=(B,),
        in_specs=[pl.BlockSpec((S, POOL_WIDTH), lambda b: (b, ZA_POOL // POOL_WIDTH)),
                  pl.BlockSpec((POOL_GROUPS, POOL_GROUP_DIM, POOL_GROUP_DIM), lambda b: (0, 0, 0)),
                  pl.BlockSpec((1, POOL_WIDTH), lambda b: (0, 0))],
        out_specs=pl.BlockSpec((S, POOL_WIDTH), lambda b: (b, 0)),
        out_shape=jax.ShapeDtypeStruct((T, POOL_WIDTH), BF16),
        scratch_shapes=[pltpu.VMEM((POOL_HALO + S, POOL_GROUP_DIM), F32)],
        compiler_params=_cp("parallel"),
        name="pool_mixer",
    )(za, pool_w, pool_scale)


def _mem_kv_kernel(m_ref, g_ref, w_ref, gk_ref, k_ref, v_ref):
    x = m_ref[...]
    xn = (x * _rms(x, x.shape[-1]) * g_ref[...]).astype(BF16)
    kv = jnp.dot(xn, w_ref[...], preferred_element_type=F32)
    for h in range(MEM_HEADS):
        kh = kv[:, h * MEM_HEAD_DIM:(h + 1) * MEM_HEAD_DIM]
        k_ref[:, h * MEM_HEAD_DIM:(h + 1) * MEM_HEAD_DIM] = (
            kh * _rms(kh, MEM_HEAD_DIM) * gk_ref[...]).astype(k_ref.dtype)
    v_ref[...] = kv[:, MEM_WIDTH:].astype(v_ref.dtype)


def mem_kv(mem2d, g, w_kv, gk, B, M):
    D = mem2d.shape[1]
    return pl.pallas_call(
        _mem_kv_kernel,
        grid=(B,),
        in_specs=[pl.BlockSpec((M, D), lambda b: (b, 0)),
                  pl.BlockSpec((1, D), lambda b: (0, 0)),
                  pl.BlockSpec((D, 2 * MEM_WIDTH), lambda b: (0, 0)),
                  pl.BlockSpec((1, MEM_HEAD_DIM), lambda b: (0, 0))],
        out_specs=[pl.BlockSpec((M, MEM_WIDTH), lambda b: (b, 0)),
                   pl.BlockSpec((M, MEM_WIDTH), lambda b: (b, 0))],
        out_shape=[jax.ShapeDtypeStruct((B * M, MEM_WIDTH), BF16),
                   jax.ShapeDtypeStruct((B * M, MEM_WIDTH), BF16)],
        compiler_params=_cp("parallel"),
        name="mem_kv",
    )(mem2d, g, w_kv, gk)


def _mem_attn_kernel(q_ref, k_ref, v_ref, gq_ref, o_ref):
    scale = MEM_HEAD_DIM ** -0.5 * LOG2_E
    for h in range(MEM_HEADS):
        sl = slice(h * MEM_HEAD_DIM, (h + 1) * MEM_HEAD_DIM)
        qh = q_ref[:, sl]
        qn = (qh * (_rms(qh, MEM_HEAD_DIM) * scale) * gq_ref[...]).astype(BF16)
        s = lax.dot_general(qn, k_ref[:, sl], (((1,), (1,)), ((), ())), preferred_element_type=F32)
        m = jnp.max(s, axis=-1, keepdims=True)
        p = jnp.exp2(s - m)
        l = jnp.sum(p, axis=-1, keepdims=True)
        o = jnp.dot(p.astype(BF16), v_ref[:, sl], preferred_element_type=F32)
        o_ref[:, sl] = (o / l).astype(o_ref.dtype)


def mem_attention(za, k_m, v_m, gq, B, S, M, tq=512):
    T = za.shape[0]
    tq = min(tq, S)
    nq = S // tq
    return pl.pallas_call(
        _mem_attn_kernel,
        grid=(B, nq),
        in_specs=[pl.BlockSpec((tq, MEM_WIDTH), lambda b, i: (b * nq + i, ZA_QMEM // MEM_WIDTH)),
                  pl.BlockSpec((M, MEM_WIDTH), lambda b, i: (b, 0)),
                  pl.BlockSpec((M, MEM_WIDTH), lambda b, i: (b, 0)),
                  pl.BlockSpec((1, MEM_HEAD_DIM), lambda b, i: (0, 0))],
        out_specs=pl.BlockSpec((tq, MEM_WIDTH), lambda b, i: (b * nq + i, 0)),
        out_shape=jax.ShapeDtypeStruct((T, MEM_WIDTH), BF16),
        compiler_params=_cp("parallel", "arbitrary"),
        name="mem_attention",
    )(za, k_m, v_m, gq)


def _merge_kernel(h_ref, a_ref, p_ref, m_ref, g0_ref, g1_ref, g2_ref, w0_ref, w1_ref, w2_ref, o_ref):
    h = h_ref[...]
    acc = None
    for x_ref, g_ref, w_ref in ((a_ref, g0_ref, w0_ref), (p_ref, g1_ref, w1_ref), (m_ref, g2_ref, w2_ref)):
        logit = jnp.dot(h, g_ref[0], preferred_element_type=F32)
        gate = 1.0 / (1.0 + jnp.exp(-logit))
        br = jnp.dot(x_ref[...], w_ref[0], preferred_element_type=F32)
        acc = gate * br if acc is None else acc + gate * br
    o_ref[...] = acc.astype(o_ref.dtype)


def merge_branches(h, a, p, m, w_all, w_a, w_p, w_m, l, tm=1024, tn=512):
    T, D = h.shape
    W = a.shape[1]
    nj = D // tn
    g0 = ZA_WIDTH // tn
    row = lambda width: pl.BlockSpec((tm, width), lambda i, j: (i, 0))
    gate_spec = lambda b: pl.BlockSpec((1, D, tn), lambda i, j, b=b: (l, 0, g0 + b * nj + j))
    out_w = pl.BlockSpec((1, W, tn), lambda i, j: (l, 0, j))
    return pl.pallas_call(
        _merge_kernel,
        grid=(T // tm, nj),
        in_specs=[row(D), row(W), row(W), row(W),
                  gate_spec(0), gate_spec(1), gate_spec(2), out_w, out_w, out_w],
        out_specs=pl.BlockSpec((tm, tn), lambda i, j: (i, j)),
        out_shape=jax.ShapeDtypeStruct((T, D), BF16),
        compiler_params=_cp("parallel", "arbitrary"),
        name="merge_branches",
    )(h, a, p, m, w_all, w_all, w_all, w_a, w_p, w_m)


def _w_in_prep_kernel(w_ref, o_ref):
    s1 = Q_LORA_RANK + KV_LORA_RANK + QK_ROPE_DIM
    s2 = s1 + POOL_WIDTH
    s3 = s2 + MEM_WIDTH
    tk = w_ref.shape[2]
    s1_up = -(-s1 // LANES) * LANES
    o_ref[0, :, ZA_POOL:ZA_POOL + POOL_WIDTH] = w_ref[0, s1:s2, :].T.astype(BF16)
    o_ref[0, :, ZA_QMEM:ZA_QMEM + MEM_WIDTH] = w_ref[0, s2:s3, :].T.astype(BF16)
    low = w_ref[0, 0:s1_up, :].T.astype(BF16)
    n_c = Q_LORA_RANK + KV_LORA_RANK
    o_ref[0, :, ZA_CQ:ZA_CQ + n_c] = low[:, 0:n_c]
    o_ref[0, :, ZA_KPE:ZA_WIDTH] = jnp.zeros((tk, ZA_WIDTH - ZA_KPE), BF16)
    o_ref[0, :, ZA_KPE:ZA_KPE + ROPE_HALF] = low[:, n_c:n_c + ROPE_HALF]
    o_ref[0, :, ZA_KPE + LANES // 2:ZA_KPE + LANES // 2 + ROPE_HALF] = low[:, n_c + ROPE_HALF:s1]
    o_ref[0, :, ZA_WIDTH:] = w_ref[0, s3:, :].T.astype(BF16)


def prep_w_in(w_in, tk=256):
    L, D, win = w_in.shape
    wout = ZA_WIDTH + N_BRANCHES * D
    return pl.pallas_call(
        _w_in_prep_kernel,
        grid=(L, D // tk),
        in_specs=[pl.BlockSpec((1, win, tk), lambda l, i: (l, 0, i))],
        out_specs=pl.BlockSpec((1, tk, wout), lambda l, i: (l, i, 0)),
        out_shape=jax.ShapeDtypeStruct((L, D, wout), BF16),
        compiler_params=_cp("parallel", "parallel"),
        name="prep_w_in",
    )(jnp.swapaxes(w_in, 1, 2))


def _swiglu_accumulate(xb, wg_ref, wu_ref, wd_refs, o_ref):
    a = jnp.dot(xb, wg_ref[0], preferred_element_type=F32)
    b = jnp.dot(xb, wu_ref[0], preferred_element_type=F32)
    act = (a / (1.0 + jnp.exp(-a)) * b).astype(BF16)
    width = o_ref.shape[1] // len(wd_refs)
    for n, wd_ref in enumerate(wd_refs):
        o_ref[:, n * width:(n + 1) * width] += jnp.dot(act, wd_ref[0], preferred_element_type=F32)


def _dense_swiglu_kernel(x_ref, wg_ref, wu_ref, wd_ref, r_ref, o_ref):
    @pl.when(pl.program_id(1) == 0)
    def _():
        o_ref[...] = r_ref[...]

    _swiglu_accumulate(x_ref[...], wg_ref, wu_ref, (wd_ref,), o_ref)


def dense_swiglu(x, w_gate, w_up, w_down, residual, tf, chunk0, n_chunks, tm=512):
    T, D = x.shape
    tm = min(tm, T)
    return pl.pallas_call(
        _dense_swiglu_kernel,
        grid=(T // tm, n_chunks),
        in_specs=[pl.BlockSpec((tm, D), lambda i, f: (i, 0)),
                  pl.BlockSpec((1, D, tf), lambda i, f: (0, 0, chunk0 + f)),
                  pl.BlockSpec((1, D, tf), lambda i, f: (0, 0, chunk0 + f)),
                  pl.BlockSpec((1, tf, D), lambda i, f: (0, chunk0 + f, 0)),
                  pl.BlockSpec((tm, D), lambda i, f: (i, 0), pipeline_mode=pl.Buffered(1))],
        out_specs=pl.BlockSpec((tm, D), lambda i, f: (i, 0)),
        out_shape=jax.ShapeDtypeStruct((T, D), F32),
        compiler_params=_cp("parallel", "arbitrary"),
        name="dense_swiglu",
    )(x, w_gate, w_up, w_down, residual)


def _row_copy(src_hbm, dst_ref, src_row, dst_row, sem):
    return pltpu.make_async_copy(src_hbm.at[pl.ds(src_row, 1)], dst_ref.at[pl.ds(dst_row, 1)], sem)


def _moe_rows_per_step(tm, nf):
    return -(-tm // (nf * SUBLANES)) * SUBLANES


def _moe_swiglu_kernel(tok_ref, be_ref, nu_ref, h_hbm, wg_ref, wu_ref, wd0_ref, wd1_ref, o_ref,
                       xg_ref, xb_ref, sem, *, tm, nf):
    del be_ref
    i = pl.program_id(0)
    f = pl.program_id(1)
    n_used = nu_ref[0]
    rows_per_step = _moe_rows_per_step(tm, nf)
    n_fetch = rows_per_step * nf

    def fetch(block, r, slot):
        tok = tok_ref[block * tm + jnp.minimum(r, tm - 1)]
        return _row_copy(h_hbm, xg_ref.at[slot], tok, r, sem.at[slot])

    def wait_rows(slot, n_rows):
        rows = xg_ref.at[slot, pl.ds(0, n_rows)]
        pltpu.make_async_copy(rows, rows, sem.at[slot]).wait()

    @pl.when((i >= n_used) & (f == 0))
    def _():
        o_ref[...] = jnp.zeros_like(o_ref)

    @pl.when(i < n_used)
    def _():
        slot = i % 2

        @pl.when((i == 0) & (f == 0))
        def _():
            def start(r, c):
                fetch(0, r, 0).start()
                return c
            lax.fori_loop(0, n_fetch, start, 0)
            for j in range(rows_per_step):
                fetch(jnp.minimum(1, n_used - 1), j, 1).start()

        @pl.when(f == 0)
        def _():
            wait_rows(slot, n_fetch)
            xb_ref[...] = xg_ref[slot, 0:tm, :].astype(BF16)
            o_ref[...] = jnp.zeros_like(o_ref)

        wrap = (f == nf - 1).astype(jnp.int32)
        chunk = (f + 1) * (1 - wrap)
        target = jnp.minimum(i + 1 + wrap, n_used - 1)
        target_slot = (i + 1 + wrap) % 2
        for j in range(rows_per_step):
            fetch(target, chunk * rows_per_step + j, target_slot).start()

        _swiglu_accumulate(xb_ref[...], wg_ref, wu_ref, (wd0_ref, wd1_ref), o_ref)

        @pl.when((i == n_used - 1) & (f == nf - 1))
        def _():
            wait_rows(1 - slot, n_fetch)
            wait_rows(slot, rows_per_step)


def moe_swiglu(h, slot_tok, block_e, n_used, w_gate, w_up, w_down_lo, w_down_hi, tm, tf=1024):
    n_slots = slot_tok.shape[0]
    D = h.shape[1]
    F = w_gate.shape[2]
    while F % tf:
        tf //= 2
    nf = F // tf
    xg_rows = _moe_rows_per_step(tm, nf) * nf
    blk = lambda i, nu: jnp.minimum(i, nu[0] - 1)
    chunk = lambda i, f, nu: jnp.where(i < nu[0], f, nf - 1)
    up_spec = pl.BlockSpec((1, D, tf), lambda i, f, tok, be, nu: (be[blk(i, nu)], 0, chunk(i, f, nu)))
    down_spec = pl.BlockSpec((1, tf, D // 2), lambda i, f, tok, be, nu: (be[blk(i, nu)], chunk(i, f, nu), 0))
    return pl.pallas_call(
        functools.partial(_moe_swiglu_kernel, tm=tm, nf=nf),
        grid_spec=pltpu.PrefetchScalarGridSpec(
            num_scalar_prefetch=3,
            grid=(n_slots // tm, nf),
            in_specs=[pl.BlockSpec(memory_space=pl.ANY), up_spec, up_spec, down_spec, down_spec],
            out_specs=pl.BlockSpec((tm, D), lambda i, f, tok, be, nu: (i, 0)),
            scratch_shapes=[pltpu.VMEM((2, xg_rows, D), F32),
                            pltpu.VMEM((tm, D), BF16),
                            pltpu.SemaphoreType.DMA((2,))]),
        out_shape=jax.ShapeDtypeStruct((n_slots, D), F32),
        compiler_params=_cp("arbitrary", "arbitrary"),
        name="moe_swiglu",
    )(slot_tok, block_e, n_used, h, w_gate, w_up, w_down_lo, w_down_hi)


def _combine_kernel(dest_ref, y_ref, gate_ref, ys_hbm, o_ref, buf_ref, sem, *, tb):
    i = pl.program_id(0)

    def fetch(step, slot):
        base = step * tb

        def start(r, c):
            for k in range(TOP_K):
                _row_copy(ys_hbm, buf_ref.at[slot, k], dest_ref[TOP_K * (base + r) + k], r, sem.at[slot]).start()
            return c

        lax.fori_loop(0, tb, start, 0, unroll=8)

    @pl.when(i == 0)
    def _():
        fetch(0, 0)

    @pl.when(i + 1 < pl.num_programs(0))
    def _():
        fetch(i + 1, (i + 1) % 2)

    slot = i % 2
    pltpu.make_async_copy(buf_ref.at[slot], buf_ref.at[slot], sem.at[slot]).wait()
    g = gate_ref[...]
    o_ref[...] = y_ref[...] + g[:, 0:1] * buf_ref[slot, 0] + g[:, 1:2] * buf_ref[slot, 1]


def combine_rows(y, gates, ys, dest, tb=256):
    T, D = y.shape
    return pl.pallas_call(
        functools.partial(_combine_kernel, tb=tb),
        grid_spec=pltpu.PrefetchScalarGridSpec(
            num_scalar_prefetch=1,
            grid=(T // tb,),
            in_specs=[pl.BlockSpec((tb, D), lambda i, d: (i, 0)),
                      pl.BlockSpec((tb, LANES), lambda i, d: (i, 0)),
                      pl.BlockSpec(memory_space=pl.ANY)],
            out_specs=pl.BlockSpec((tb, D), lambda i, d: (i, 0)),
            scratch_shapes=[pltpu.VMEM((2, TOP_K, tb, D), F32),
                            pltpu.SemaphoreType.DMA((2,))]),
        out_shape=jax.ShapeDtypeStruct((T, D), F32),
        compiler_params=_cp("arbitrary"),
        name="moe_combine",
    )(dest, y, gates, ys)


def _routing_tables(top_idx, T, block):
    A = T * TOP_K
    E = N_EXPERTS
    flat_e = top_idx.reshape(A)
    onehot = (flat_e[:, None] == jnp.arange(E, dtype=jnp.int32)[None, :]).astype(jnp.int32)
    csum = jnp.cumsum(onehot, axis=0)
    counts = csum[-1]
    rank = jnp.sum((csum - 1) * onehot, axis=1)
    padded = (counts + block - 1) // block * block
    pends = jnp.cumsum(padded)
    pstarts = pends - padded
    starts = jnp.cumsum(counts) - counts
    dest = (pstarts[flat_e] + rank).astype(jnp.int32)
    n_blocks = A // block + E
    n_slots = n_blocks * block
    order = jnp.argsort(flat_e, stable=True).astype(jnp.int32)
    slot = jnp.arange(n_slots, dtype=jnp.int32)
    slot_e = jnp.clip(jnp.searchsorted(pends, slot, side='right'), 0, E - 1).astype(jnp.int32)
    r = slot - pstarts[slot_e]
    valid = r < counts[slot_e]
    src = jnp.clip(starts[slot_e] + r, 0, A - 1)
    slot_tok = jnp.where(valid, order[src] // TOP_K, 0).astype(jnp.int32)
    block_e = slot_e[::block]
    n_used = (pends[-1:] // block).astype(jnp.int32)
    return slot_tok, dest, block_e, n_used


def _token_mixer(h, mem2d, tab, B, S, M, l, casts, w_in, mla_q_a_norm_g, mla_w_uq, mla_kv_a_norm_g, mla_w_ukv,
                 mla_q_norm_g, mla_k_norm_g, mla_w_out, pool_w, pool_scale, pool_w_out,
                 mem_norm_g, mem_w_kv, mem_q_norm_g, mem_k_norm_g, mem_w_out):
    za = matmul(h, w_in, l, ZA_WIDTH, F32, tn=ZA_WIDTH // 2)

    head_pad = lambda v: jnp.concatenate([v[..., :QK_NOPE_DIM], _rope_lanes(v[..., QK_NOPE_DIM:])], axis=-1)
    wuq = head_pad(mla_w_uq[l].reshape(Q_LORA_RANK, MLA_HEADS, QK_HEAD_DIM)).reshape(
        Q_LORA_RANK, MLA_HEADS * HEAD_PAD)
    wukv = mla_w_ukv[l].reshape(KV_LORA_RANK, MLA_HEADS, QK_NOPE_DIM + V_HEAD_DIM)
    wuk = wukv[:, :, :QK_NOPE_DIM].reshape(KV_LORA_RANK, MLA_WIDTH)
    wuv = wukv[:, :, QK_NOPE_DIM:].reshape(KV_LORA_RANK, MLA_WIDTH)
    pad_g = lambda g: head_pad(g).reshape(1, HEAD_PAD)
    q, k, v = mla_prep(za, tab, wuq.astype(BF16), wuk.astype(BF16), wuv.astype(BF16),
                       mla_q_a_norm_g[l].reshape(1, -1), mla_kv_a_norm_g[l].reshape(1, -1),
                       pad_g(mla_q_norm_g[l]), pad_g(mla_k_norm_g[l]))
    attn, *cast_out = causal_attention(q, k, v, B, S, casts)

    mixed = pool_mixer(za, pool_w[l].astype(BF16), pool_scale[l].reshape(1, -1), B, S)

    k_m, v_m = mem_kv(mem2d, mem_norm_g[l].reshape(1, -1), mem_w_kv[l].astype(BF16),
                      mem_k_norm_g[l].reshape(1, -1), B, M)
    o_mem = mem_attention(za, k_m, v_m, mem_q_norm_g[l].reshape(1, -1), B, S, M)

    merged = merge_branches(h, attn, mixed, o_mem, w_in, mla_w_out, pool_w_out, mem_w_out, l)
    return merged, cast_out


def kernel(x, mem, positions, attn_norm_g, w_in, mla_q_a_norm_g, mla_w_uq, mla_kv_a_norm_g, mla_w_ukv, mla_q_norm_g, mla_k_norm_g, mla_w_out, pool_w, pool_scale, pool_w_out, mem_norm_g, mem_w_kv, mem_q_norm_g, mem_k_norm_g, mem_w_out, w_o, ffn_norm_g, dense_w_gate, dense_w_up, dense_w_down, router_w, router_b, moe_w_gate, moe_w_up, moe_w_down):
    B, S, D = x.shape
    M = mem.shape[1]
    T = B * S
    depth = attn_norm_g.shape[0]
    moe_block = 512
    y = x.reshape(T, D)
    mem2d = mem.reshape(B * M, D)
    tab = rope_tables(positions)
    w_in = prep_w_in(w_in)
    mla_w_out, pool_w_out, mem_w_out, w_o = (w.astype(BF16) for w in (mla_w_out, pool_w_out, mem_w_out, w_o))
    moe_bf16 = {}
    for l in range(depth):
        h = rmsnorm(y, attn_norm_g[l], BF16)
        i = l // 2
        casts = ()
        if l % 2 == 0 and l + 1 < depth:
            casts = ((moe_w_gate[i], 0, 1), (moe_w_down[i], 0, 2))
        elif l % 2 == 1:
            casts = ((moe_w_up[i], 0, 1), (moe_w_down[i], 1, 2))
        merged, cast_out = _token_mixer(h, mem2d, tab, B, S, M, l, casts, w_in, mla_q_a_norm_g, mla_w_uq,
                                        mla_kv_a_norm_g, mla_w_ukv, mla_q_norm_g, mla_k_norm_g, mla_w_out,
                                        pool_w, pool_scale, pool_w_out, mem_norm_g, mem_w_kv, mem_q_norm_g,
                                        mem_k_norm_g, mem_w_out)
        if casts:
            names = ("gate", "down_lo") if l % 2 == 0 else ("up", "down_hi")
            moe_bf16.update(zip(names, cast_out))
        if l % 2 == 0:
            y, h = matmul_residual_norm(merged, w_o, l, y, ffn_norm_g[l])
            dense_w = (dense_w_gate[i:i + 1].astype(BF16), dense_w_up[i:i + 1].astype(BF16),
                       dense_w_down[i:i + 1].astype(BF16))
            F_dense = dense_w_gate.shape[2]
            n_main, rem = divmod(F_dense, DENSE_TF)
            y = dense_swiglu(h, *dense_w, y, DENSE_TF, 0, n_main)
            if rem:
                assert (n_main * DENSE_TF) % rem == 0
                y = dense_swiglu(h, *dense_w, y, rem, n_main * DENSE_TF // rem, 1)
        else:
            y = matmul(merged, w_o, l, D, F32, residual=y, tm=512, tn=D)
            hf, idx, gates = norm_router(y, ffn_norm_g[l], router_w[i], router_b[i])
            slot_tok, dest, block_e, n_used = _routing_tables(idx[:, :TOP_K], T, moe_block)
            ys = moe_swiglu(hf, slot_tok, block_e, n_used, moe_bf16.pop("gate"), moe_bf16.pop("up"),
                            moe_bf16.pop("down_lo"), moe_bf16.pop("down_hi"), tm=moe_block)
            y = combine_rows(y, gates, ys, dest)
    return y.reshape(B, S, D)
```

```python
import functools

import numpy as np
import jax
import jax.numpy as jnp
from jax import lax
from jax.experimental import pallas as pl
from jax.experimental.pallas import tpu as pltpu

F32 = jnp.float32
BF16 = jnp.bfloat16

MEM_HEADS = 4
MEM_HEAD_DIM = 256
MEM_WIDTH = MEM_HEADS * MEM_HEAD_DIM
MLA_HEADS = 8
Q_LORA_RANK = 512
KV_LORA_RANK = 256
QK_NOPE_DIM = 128
QK_ROPE_DIM = 64
QK_HEAD_DIM = QK_NOPE_DIM + QK_ROPE_DIM
V_HEAD_DIM = 128
MLA_WIDTH = MLA_HEADS * V_HEAD_DIM
ROPE_THETA = 10000.0
POOL_WINDOWS = (2, 4, 8, 16)
POOL_GROUPS = 4
POOL_GROUP_DIM = 256
POOL_WIDTH = POOL_GROUPS * POOL_GROUP_DIM
N_BRANCHES = 3
N_EXPERTS = 8
TOP_K = 2
NORM_EPS = 1e-6
LOG2_E = 1.4426950408889634

LANES = 128
SUBLANES = 8
HEAD_PAD = 2 * LANES
ROPE_HALF = QK_ROPE_DIM // 2
POOL_HALO = 16

ZA_POOL = 0
ZA_QMEM = POOL_WIDTH
ZA_CQ = ZA_QMEM + MEM_WIDTH
ZA_CKV = ZA_CQ + Q_LORA_RANK
ZA_KPE = ZA_CKV + KV_LORA_RANK
ZA_WIDTH = 3072

VMEM_LIMIT = 52 * 1024 * 1024


def _cp(*sem):
    return pltpu.CompilerParams(dimension_semantics=sem, vmem_limit_bytes=VMEM_LIMIT)


def _rms(x, eps_dim):
    return lax.rsqrt(jnp.sum(x * x, axis=-1, keepdims=True) * (1.0 / eps_dim) + NORM_EPS)


def _norm_router_kernel(x_ref, g_ref, rwh_ref, rwl_ref, rb_ref, h_ref, idx_ref, gate_ref):
    x = x_ref[...]
    h = x * _rms(x, x.shape[-1]) * g_ref[...]
    h_ref[...] = h
    h_hi = h.astype(BF16)
    h_lo = (h - h_hi.astype(F32)).astype(BF16)
    logits = (jnp.dot(h_hi, rwh_ref[...], preferred_element_type=F32)
              + jnp.dot(h_lo, rwh_ref[...], preferred_element_type=F32)
              + jnp.dot(h_hi, rwl_ref[...], preferred_element_type=F32)) + rb_ref[...]
    lane = lax.broadcasted_iota(jnp.int32, logits.shape, 1)
    neg = -jnp.inf
    l1 = jnp.where(lane < N_EXPERTS, logits, neg)
    m1 = jnp.max(l1, axis=-1, keepdims=True)
    i1 = jnp.min(jnp.where(l1 == m1, lane, LANES), axis=-1, keepdims=True)
    l2 = jnp.where(lane == i1, neg, l1)
    m2 = jnp.max(l2, axis=-1, keepdims=True)
    i2 = jnp.min(jnp.where(l2 == m2, lane, LANES), axis=-1, keepdims=True)
    e = jnp.exp(m2 - m1)
    g1 = 1.0 / (1.0 + e)
    g2 = e / (1.0 + e)
    idx_ref[...] = jnp.where(lane == 0, i1, jnp.where(lane == 1, i2, 0))
    gate_ref[...] = jnp.where(lane == 0, g1, jnp.where(lane == 1, g2, 0.0))


def norm_router(x, g, router_w, router_b, tm=512):
    T, D = x.shape
    E = router_w.shape[1]
    rw = jnp.zeros((D, LANES), F32).at[:, :E].set(router_w)
    rw_hi = rw.astype(BF16)
    rw_lo = (rw - rw_hi.astype(F32)).astype(BF16)
    rb = jnp.zeros((1, LANES), F32).at[0, :E].set(router_b)
    return pl.pallas_call(
        _norm_router_kernel,
        grid=(T // tm,),
        in_specs=[pl.BlockSpec((tm, D), lambda i: (i, 0)),
                  pl.BlockSpec((1, D), lambda i: (0, 0)),
                  pl.BlockSpec((D, LANES), lambda i: (0, 0)),
                  pl.BlockSpec((D, LANES), lambda i: (0, 0)),
                  pl.BlockSpec((1, LANES), lambda i: (0, 0))],
        out_specs=[pl.BlockSpec((tm, D), lambda i: (i, 0)),
                   pl.BlockSpec((tm, LANES), lambda i: (i, 0)),
                   pl.BlockSpec((tm, LANES), lambda i: (i, 0))],
        out_shape=[jax.ShapeDtypeStruct((T, D), F32),
                   jax.ShapeDtypeStruct((T, LANES), jnp.int32),
                   jax.ShapeDtypeStruct((T, LANES), F32)],
        compiler_params=_cp("parallel"),
        name="norm_router",
    )(x, g.reshape(1, D), rw_hi, rw_lo, rb)


def _mm_kernel(a_ref, w_ref, o_ref):
    o_ref[...] = jnp.dot(a_ref[...], w_ref[0], preferred_element_type=F32).astype(o_ref.dtype)


def _mm_res_kernel(a_ref, w_ref, r_ref, o_ref):
    o_ref[...] = (r_ref[...] + jnp.dot(a_ref[...], w_ref[0], preferred_element_type=F32)).astype(o_ref.dtype)


def matmul(a, w, l, N, out_dtype, residual=None, tm=1024, tn=1024):
    M, K = a.shape
    tm, tn = min(tm, M), min(tn, N)
    in_specs = [pl.BlockSpec((tm, K), lambda i, j: (i, 0)),
                pl.BlockSpec((1, K, tn), lambda i, j: (l, 0, j))]
    args = [a, w]
    body = _mm_kernel
    if residual is not None:
        in_specs.append(pl.BlockSpec((tm, tn), lambda i, j: (i, j)))
        args.append(residual)
        body = _mm_res_kernel
    return pl.pallas_call(
        body,
        grid=(M // tm, N // tn),
        in_specs=in_specs,
        out_specs=pl.BlockSpec((tm, tn), lambda i, j: (i, j)),
        out_shape=jax.ShapeDtypeStruct((M, N), out_dtype),
        compiler_params=_cp("parallel", "arbitrary"),
        name="matmul",
    )(*args)


def _norm_mm_kernel(y_ref, g_ref, w_ref, h_ref, o_ref):
    @pl.when(pl.program_id(1) == 0)
    def _():
        y = y_ref[...]
        h_ref[...] = (y * _rms(y, y.shape[-1]) * g_ref[...]).astype(h_ref.dtype)

    o_ref[...] = jnp.dot(h_ref[...], w_ref[0], preferred_element_type=F32)


def norm_matmul(y, g, w, l, N, tm=1024, tn=1024):
    M, K = y.shape
    tm = min(tm, M)
    return pl.pallas_call(
        _norm_mm_kernel,
        grid=(M // tm, N // tn),
        in_specs=[pl.BlockSpec((tm, K), lambda i, j: (i, 0)),
                  pl.BlockSpec((1, K), lambda i, j: (0, 0)),
                  pl.BlockSpec((1, K, tn), lambda i, j: (l, 0, j))],
        out_specs=[pl.BlockSpec((tm, K), lambda i, j: (i, 0)),
                   pl.BlockSpec((tm, tn), lambda i, j: (i, j))],
        out_shape=[jax.ShapeDtypeStruct((M, K), BF16), jax.ShapeDtypeStruct((M, N), F32)],
        compiler_params=_cp("parallel", "arbitrary"),
        name="norm_matmul",
    )(y, g.reshape(1, K), w)


def _mm_res_norm_kernel(a_ref, w_ref, r_ref, g_ref, o_ref, h_ref):
    y = r_ref[...] + jnp.dot(a_ref[...], w_ref[0], preferred_element_type=F32)
    o_ref[...] = y
    h_ref[...] = (y * _rms(y, y.shape[-1]) * g_ref[...]).astype(h_ref.dtype)


def matmul_residual_norm(a, w, l, residual, g, tm=512):
    M, K = a.shape
    N = w.shape[2]
    rows = lambda width: pl.BlockSpec((tm, width), lambda i: (i, 0))
    return pl.pallas_call(
        _mm_res_norm_kernel,
        grid=(M // tm,),
        in_specs=[rows(K), pl.BlockSpec((1, K, N), lambda i: (l, 0, 0)), rows(N),
                  pl.BlockSpec((1, N), lambda i: (0, 0))],
        out_specs=[rows(N), rows(N)],
        out_shape=[jax.ShapeDtypeStruct((M, N), F32), jax.ShapeDtypeStruct((M, N), BF16)],
        compiler_params=_cp("parallel"),
        name="matmul_residual_norm",
    )(a, w, residual, g.reshape(1, N))


def _rope_lanes(v):
    z = jnp.zeros(v.shape[:-1] + (LANES // 2 - ROPE_HALF,), v.dtype)
    return jnp.concatenate([v[..., :ROPE_HALF], z, v[..., ROPE_HALF:], z], axis=-1)


def _rope_table_kernel(pos_ref, invf_ref, o_ref):
    ang = pos_ref[...].astype(F32) * invf_ref[...]
    lane = lax.broadcasted_iota(jnp.int32, ang.shape, 1)
    first = lane < ROPE_HALF
    second = (lane >= LANES // 2) & (lane < LANES // 2 + ROPE_HALF)
    c = jnp.cos(ang)
    s = jnp.sin(ang)
    o_ref[:, 0:LANES] = jnp.where(first | second, c, 0.0)
    o_ref[:, LANES:2 * LANES] = jnp.where(first, -s, jnp.where(second, s, 0.0))


def rope_tables(positions, tm=512):
    T = positions.size
    inv_freq = ROPE_THETA ** (-np.arange(0, QK_ROPE_DIM, 2, dtype=np.float32) / QK_ROPE_DIM)
    invf = np.zeros((1, LANES), np.float32)
    invf[0, :ROPE_HALF] = inv_freq
    invf[0, LANES // 2:LANES // 2 + ROPE_HALF] = inv_freq
    return pl.pallas_call(
        _rope_table_kernel,
        grid=(T // tm,),
        in_specs=[pl.BlockSpec((tm, 1), lambda i: (i, 0)),
                  pl.BlockSpec((1, LANES), lambda i: (0, 0))],
        out_specs=pl.BlockSpec((tm, 2 * LANES), lambda i: (i, 0)),
        out_shape=jax.ShapeDtypeStruct((T, 2 * LANES), F32),
        compiler_params=_cp("parallel"),
        name="rope_tables",
    )(positions.reshape(T, 1), jnp.asarray(invf))


def _rope(x, tab):
    return x * tab[:, 0:LANES] + pltpu.roll(x, LANES // 2, 1) * tab[:, LANES:2 * LANES]


def _mla_prep_kernel(cq_ref, ckv_ref, kpe_ref, tab_ref, wuq_ref, wuk_ref, wuv_ref,
                     gqa_ref, gkva_ref, gq_ref, gk_ref, q_ref, k_ref, v_ref):
    tab = tab_ref[...]
    cq = cq_ref[...]
    cqn = (cq * _rms(cq, Q_LORA_RANK) * gqa_ref[...]).astype(BF16)
    qf = jnp.dot(cqn, wuq_ref[...], preferred_element_type=F32)
    ckv = ckv_ref[...]
    ckvn = (ckv * _rms(ckv, KV_LORA_RANK) * gkva_ref[...]).astype(BF16)
    kn = jnp.dot(ckvn, wuk_ref[...], preferred_element_type=F32)
    v_ref[...] = jnp.dot(ckvn, wuv_ref[...], preferred_element_type=F32).astype(v_ref.dtype)
    gq = gq_ref[...]
    gk = gk_ref[...]
    kpe = kpe_ref[...]
    ss_pe = jnp.sum(kpe * kpe, axis=-1, keepdims=True)
    kpe_rot = _rope(kpe * gk[:, LANES:], tab)
    scale = QK_HEAD_DIM ** -0.5 * LOG2_E
    for h in range(MLA_HEADS):
        q0 = qf[:, h * HEAD_PAD:h * HEAD_PAD + LANES]
        q1 = qf[:, h * HEAD_PAD + LANES:(h + 1) * HEAD_PAD]
        ss = jnp.sum(q0 * q0, axis=-1, keepdims=True) + jnp.sum(q1 * q1, axis=-1, keepdims=True)
        rq = lax.rsqrt(ss * (1.0 / QK_HEAD_DIM) + NORM_EPS) * scale
        q_ref[:, h * HEAD_PAD:h * HEAD_PAD + LANES] = (q0 * gq[:, :LANES] * rq).astype(q_ref.dtype)
        q_ref[:, h * HEAD_PAD + LANES:(h + 1) * HEAD_PAD] = (_rope(q1 * gq[:, LANES:], tab) * rq).astype(q_ref.dtype)
        k0 = kn[:, h * LANES:(h + 1) * LANES]
        ssk = jnp.sum(k0 * k0, axis=-1, keepdims=True) + ss_pe
        rk = lax.rsqrt(ssk * (1.0 / QK_HEAD_DIM) + NORM_EPS)
        k_ref[:, h * HEAD_PAD:h * HEAD_PAD + LANES] = (k0 * gk[:, :LANES] * rk).astype(k_ref.dtype)
        k_ref[:, h * HEAD_PAD + LANES:(h + 1) * HEAD_PAD] = (kpe_rot * rk).astype(k_ref.dtype)


def mla_prep(za, tab, wuq, wuk, wuv, gqa, gkva, gq, gk, tm=512):
    T = za.shape[0]
    HP = MLA_HEADS * HEAD_PAD
    full = lambda shape: pl.BlockSpec(shape, lambda i: (0, 0))
    return pl.pallas_call(
        _mla_prep_kernel,
        grid=(T // tm,),
        in_specs=[pl.BlockSpec((tm, Q_LORA_RANK), lambda i: (i, ZA_CQ // Q_LORA_RANK)),
                  pl.BlockSpec((tm, KV_LORA_RANK), lambda i: (i, ZA_CKV // KV_LORA_RANK)),
                  pl.BlockSpec((tm, LANES), lambda i: (i, ZA_KPE // LANES)),
                  pl.BlockSpec((tm, 2 * LANES), lambda i: (i, 0)),
                  full((Q_LORA_RANK, HP)), full((KV_LORA_RANK, MLA_WIDTH)), full((KV_LORA_RANK, MLA_WIDTH)),
                  full((1, Q_LORA_RANK)), full((1, KV_LORA_RANK)), full((1, HEAD_PAD)), full((1, HEAD_PAD))],
        out_specs=[pl.BlockSpec((tm, HP), lambda i: (i, 0)),
                   pl.BlockSpec((tm, HP), lambda i: (i, 0)),
                   pl.BlockSpec((tm, MLA_WIDTH), lambda i: (i, 0))],
        out_shape=[jax.ShapeDtypeStruct((T, HP), BF16),
                   jax.ShapeDtypeStruct((T, HP), BF16),
                   jax.ShapeDtypeStruct((T, MLA_WIDTH), BF16)],
        compiler_params=_cp("parallel"),
        name="mla_prep",
    )(za, za, za, tab, wuq, wuk, wuv, gqa, gkva, gq, gk)


def _causal_attn_kernel(q_ref, k_ref, v_ref, *rest, tq, n_cast):
    cast_src = rest[:n_cast]
    o_ref = rest[n_cast]
    cast_dst = rest[n_cast + 1:]
    S = q_ref.shape[0]
    nq = S // tq
    row = lax.broadcasted_iota(jnp.int32, (tq, tq), 0)
    col = lax.broadcasted_iota(jnp.int32, (tq, tq), 1)
    tri = col <= row
    for qi in range(nq):
        for s_ref, d_ref in zip(cast_src, cast_dst):
            rc = s_ref.shape[1] // nq
            d_ref[0, qi * rc:(qi + 1) * rc, :] = s_ref[0, qi * rc:(qi + 1) * rc, :].astype(d_ref.dtype)
        kl = (qi + 1) * tq
        q = q_ref[qi * tq:kl, :]
        s = lax.dot_general(q, k_ref[0:kl, :], (((1,), (1,)), ((), ())), preferred_element_type=F32)
        s_diag = jnp.where(tri, s[:, kl - tq:kl], -jnp.inf)
        if qi > 0:
            s = jnp.concatenate([s[:, :kl - tq], s_diag], axis=1)
        else:
            s = s_diag
        m = jnp.max(s, axis=-1, keepdims=True)
        p = jnp.exp2(s - m)
        l = jnp.sum(p, axis=-1, keepdims=True)
        o = jnp.dot(p.astype(BF16), v_ref[0:kl, :], preferred_element_type=F32)
        o_ref[qi * tq:kl, :] = (o / l).astype(o_ref.dtype)


def causal_attention(q, k, v, B, S, casts=(), tq=256):
    T = q.shape[0]
    n_steps = B * MLA_HEADS
    in_specs = [pl.BlockSpec((S, HEAD_PAD), lambda b, h: (b, h)),
                pl.BlockSpec((S, HEAD_PAD), lambda b, h: (b, h)),
                pl.BlockSpec((S, V_HEAD_DIM), lambda b, h: (b, h))]
    out_specs = [pl.BlockSpec((S, V_HEAD_DIM), lambda b, h: (b, h))]
    out_shape = [jax.ShapeDtypeStruct((T, MLA_WIDTH), BF16)]
    for src, part, n_parts in casts:
        E, rows, cols = src.shape
        spe = n_steps // E
        rb, cb = rows // spe, cols // n_parts
        assert spe * E == n_steps and rb * spe == rows and rb % (16 * (S // tq)) == 0
        assert cb * n_parts == cols and cb % LANES == 0
        step = lambda b, h: b * MLA_HEADS + h
        in_specs.append(pl.BlockSpec((1, rb, cb), lambda b, h, part=part, spe=spe:
                                     (step(b, h) // spe, step(b, h) % spe, part)))
        out_specs.append(pl.BlockSpec((1, rb, cb), lambda b, h, spe=spe:
                                      (step(b, h) // spe, step(b, h) % spe, 0)))
        out_shape.append(jax.ShapeDtypeStruct((E, rows, cb), BF16))
    return pl.pallas_call(
        functools.partial(_causal_attn_kernel, tq=tq, n_cast=len(casts)),
        grid=(B, MLA_HEADS),
        in_specs=in_specs,
        out_specs=out_specs,
        out_shape=out_shape,
        compiler_params=_cp("parallel", "parallel"),
        name="causal_attention",
    )(q, k, v, *(c[0] for c in casts))


def _pool_kernel(u_ref, w_ref, sc_ref, o_ref, pad_ref, *, rc):
    S = u_ref.shape[0]
    C = POOL_GROUP_DIM
    t1 = lax.broadcasted_iota(jnp.int32, (rc, C), 0).astype(F32) + 1.0
    for g, win in enumerate(POOL_WINDOWS):
        pad_ref[0:POOL_HALO, :] = jnp.zeros((POOL_HALO, C), F32)
        pad_ref[POOL_HALO:POOL_HALO + S, :] = u_ref[:, g * C:(g + 1) * C]
        for r0 in range(0, S, rc):
            acc = pad_ref[POOL_HALO + r0:POOL_HALO + r0 + rc, :]
            tok = acc
            for kk in range(1, win):
                acc = acc + pad_ref[POOL_HALO + r0 - kk:POOL_HALO + r0 - kk + rc, :]
            cnt = jnp.minimum(t1 + float(r0), float(win))
            pooled = (acc / cnt - tok).astype(BF16)
            mixed = jnp.dot(pooled, w_ref[g], preferred_element_type=F32) * sc_ref[:, g * C:(g + 1) * C]
            o_ref[r0:r0 + rc, g * C:(g + 1) * C] = mixed.astype(o_ref.dtype)


def pool_mixer(za, pool_w, pool_scale, B, S, rc=512):
    T = za.shape[0]
    rc = min(rc, S)
    return pl.pallas_call(
        functools.partial(_pool_kernel, rc=rc),
        grid=(B,),
        in_specs=[pl.BlockSpec((S, POOL_WIDTH), lambda b: (b, ZA_POOL // POOL_WIDTH)),
                  pl.BlockSpec((POOL_GROUPS, POOL_GROUP_DIM, POOL_GROUP_DIM), lambda b: (0, 0, 0)),
                  pl.BlockSpec((1, POOL_WIDTH), lambda b: (0, 0))],
        out_specs=pl.BlockSpec((S, POOL_WIDTH), lambda b: (b, 0)),
        out_shape=jax.ShapeDtypeStruct((T, POOL_WIDTH), BF16),
        scratch_shapes=[pltpu.VMEM((POOL_HALO + S, POOL_GROUP_DIM), F32)],
        compiler_params=_cp("parallel"),
        name="pool_mixer",
    )(za, pool_w, pool_scale)


def _mem_kv_kernel(m_ref, g_ref, w_ref, gk_ref, k_ref, v_ref):
    x = m_ref[...]
    xn = (x * _rms(x, x.shape[-1]) * g_ref[...]).astype(BF16)
    kv = jnp.dot(xn, w_ref[...], preferred_element_type=F32)
    for h in range(MEM_HEADS):
        kh = kv[:, h * MEM_HEAD_DIM:(h + 1) * MEM_HEAD_DIM]
        k_ref[:, h * MEM_HEAD_DIM:(h + 1) * MEM_HEAD_DIM] = (
            kh * _rms(kh, MEM_HEAD_DIM) * gk_ref[...]).astype(k_ref.dtype)
    v_ref[...] = kv[:, MEM_WIDTH:].astype(v_ref.dtype)


def mem_kv(mem2d, g, w_kv, gk, B, M):
    D = mem2d.shape[1]
    return pl.pallas_call(
        _mem_kv_kernel,
        grid=(B,),
        in_specs=[pl.BlockSpec((M, D), lambda b: (b, 0)),
                  pl.BlockSpec((1, D), lambda b: (0, 0)),
                  pl.BlockSpec((D, 2 * MEM_WIDTH), lambda b: (0, 0)),
                  pl.BlockSpec((1, MEM_HEAD_DIM), lambda b: (0, 0))],
        out_specs=[pl.BlockSpec((M, MEM_WIDTH), lambda b: (b, 0)),
                   pl.BlockSpec((M, MEM_WIDTH), lambda b: (b, 0))],
        out_shape=[jax.ShapeDtypeStruct((B * M, MEM_WIDTH), BF16),
                   jax.ShapeDtypeStruct((B * M, MEM_WIDTH), BF16)],
        compiler_params=_cp("parallel"),
        name="mem_kv",
    )(mem2d, g, w_kv, gk)


def _mem_attn_kernel(q_ref, k_ref, v_ref, gq_ref, o_ref):
    scale = MEM_HEAD_DIM ** -0.5 * LOG2_E
    for h in range(MEM_HEADS):
        sl = slice(h * MEM_HEAD_DIM, (h + 1) * MEM_HEAD_DIM)
        qh = q_ref[:, sl]
        qn = (qh * (_rms(qh, MEM_HEAD_DIM) * scale) * gq_ref[...]).astype(BF16)
        s = lax.dot_general(qn, k_ref[:, sl], (((1,), (1,)), ((), ())), preferred_element_type=F32)
        m = jnp.max(s, axis=-1, keepdims=True)
        p = jnp.exp2(s - m)
        l = jnp.sum(p, axis=-1, keepdims=True)
        o = jnp.dot(p.astype(BF16), v_ref[:, sl], preferred_element_type=F32)
        o_ref[:, sl] = (o / l).astype(o_ref.dtype)


def mem_attention(za, k_m, v_m, gq, B, S, M, tq=512):
    T = za.shape[0]
    tq = min(tq, S)
    nq = S // tq
    return pl.pallas_call(
        _mem_attn_kernel,
        grid=(B, nq),
        in_specs=[pl.BlockSpec((tq, MEM_WIDTH), lambda b, i: (b * nq + i, ZA_QMEM // MEM_WIDTH)),
                  pl.BlockSpec((M, MEM_WIDTH), lambda b, i: (b, 0)),
                  pl.BlockSpec((M, MEM_WIDTH), lambda b, i: (b, 0)),
                  pl.BlockSpec((1, MEM_HEAD_DIM), lambda b, i: (0, 0))],
        out_specs=pl.BlockSpec((tq, MEM_WIDTH), lambda b, i: (b * nq + i, 0)),
        out_shape=jax.ShapeDtypeStruct((T, MEM_WIDTH), BF16),
        compiler_params=_cp("parallel", "arbitrary"),
        name="mem_attention",
    )(za, k_m, v_m, gq)


def _merge_kernel(h_ref, a_ref, p_ref, m_ref, g0_ref, g1_ref, g2_ref, w0_ref, w1_ref, w2_ref, o_ref):
    h = h_ref[...]
    acc = None
    for x_ref, g_ref, w_ref in ((a_ref, g0_ref, w0_ref), (p_ref, g1_ref, w1_ref), (m_ref, g2_ref, w2_ref)):
        logit = jnp.dot(h, g_ref[0], preferred_element_type=F32)
        gate = 1.0 / (1.0 + jnp.exp(-logit))
        br = jnp.dot(x_ref[...], w_ref[0], preferred_element_type=F32)
        acc = gate * br if acc is None else acc + gate * br
    o_ref[...] = acc.astype(o_ref.dtype)


def merge_branches(h, a, p, m, w_all, w_a, w_p, w_m, l, tm=1024, tn=512):
    T, D = h.shape
    W = a.shape[1]
    nj = D // tn
    g0 = ZA_WIDTH // tn
    row = lambda width: pl.BlockSpec((tm, width), lambda i, j: (i, 0))
    gate_spec = lambda b: pl.BlockSpec((1, D, tn), lambda i, j, b=b: (l, 0, g0 + b * nj + j))
    out_w = pl.BlockSpec((1, W, tn), lambda i, j: (l, 0, j))
    return pl.pallas_call(
        _merge_kernel,
        grid=(T // tm, nj),
        in_specs=[row(D), row(W), row(W), row(W),
                  gate_spec(0), gate_spec(1), gate_spec(2), out_w, out_w, out_w],
        out_specs=pl.BlockSpec((tm, tn), lambda i, j: (i, j)),
        out_shape=jax.ShapeDtypeStruct((T, D), BF16),
        compiler_params=_cp("parallel", "arbitrary"),
        name="merge_branches",
    )(h, a, p, m, w_all, w_all, w_all, w_a, w_p, w_m)


def _w_in_prep_kernel(w_ref, o_ref):
    s1 = Q_LORA_RANK + KV_LORA_RANK + QK_ROPE_DIM
    s2 = s1 + POOL_WIDTH
    s3 = s2 + MEM_WIDTH
    tk = w_ref.shape[2]
    s1_up = -(-s1 // LANES) * LANES
    o_ref[0, :, ZA_POOL:ZA_POOL + POOL_WIDTH] = w_ref[0, s1:s2, :].T.astype(BF16)
    o_ref[0, :, ZA_QMEM:ZA_QMEM + MEM_WIDTH] = w_ref[0, s2:s3, :].T.astype(BF16)
    low = w_ref[0, 0:s1_up, :].T.astype(BF16)
    n_c = Q_LORA_RANK + KV_LORA_RANK
    o_ref[0, :, ZA_CQ:ZA_CQ + n_c] = low[:, 0:n_c]
    o_ref[0, :, ZA_KPE:ZA_WIDTH] = jnp.zeros((tk, ZA_WIDTH - ZA_KPE), BF16)
    o_ref[0, :, ZA_KPE:ZA_KPE + ROPE_HALF] = low[:, n_c:n_c + ROPE_HALF]
    o_ref[0, :, ZA_KPE + LANES // 2:ZA_KPE + LANES // 2 + ROPE_HALF] = low[:, n_c + ROPE_HALF:s1]
    o_ref[0, :, ZA_WIDTH:] = w_ref[0, s3:, :].T.astype(BF16)


def prep_w_in(w_in, tk=256):
    L, D, win = w_in.shape
    wout = ZA_WIDTH + N_BRANCHES * D
    return pl.pallas_call(
        _w_in_prep_kernel,
        grid=(L, D // tk),
        in_specs=[pl.BlockSpec((1, win, tk), lambda l, i: (l, 0, i))],
        out_specs=pl.BlockSpec((1, tk, wout), lambda l, i: (l, i, 0)),
        out_shape=jax.ShapeDtypeStruct((L, D, wout), BF16),
        compiler_params=_cp("parallel", "parallel"),
        name="prep_w_in",
    )(jnp.swapaxes(w_in, 1, 2))


def _swiglu_accumulate(xb, wg_ref, wu_ref, wd_refs, o_ref):
    a = jnp.dot(xb, wg_ref[0], preferred_element_type=F32)
    b = jnp.dot(xb, wu_ref[0], preferred_element_type=F32)
    act = (a / (1.0 + jnp.exp(-a)) * b).astype(BF16)
    width = o_ref.shape[1] // len(wd_refs)
    for n, wd_ref in enumerate(wd_refs):
        o_ref[:, n * width:(n + 1) * width] += jnp.dot(act, wd_ref[0], preferred_element_type=F32)


def _dense_swiglu_kernel(x_ref, wg_ref, wu_ref, wd_ref, r_ref, o_ref):
    @pl.when(pl.program_id(1) == 0)
    def _():
        o_ref[...] = r_ref[...]

    _swiglu_accumulate(x_ref[...], wg_ref, wu_ref, (wd_ref,), o_ref)


def dense_swiglu(x, w_gate, w_up, w_down, residual, tm=1024, tf=512):
    T, D = x.shape
    F = w_gate.shape[2]
    tm = min(tm, T)
    return pl.pallas_call(
        _dense_swiglu_kernel,
        grid=(T // tm, F // tf),
        in_specs=[pl.BlockSpec((tm, D), lambda i, f: (i, 0)),
                  pl.BlockSpec((1, D, tf), lambda i, f: (0, 0, f)),
                  pl.BlockSpec((1, D, tf), lambda i, f: (0, 0, f)),
                  pl.BlockSpec((1, tf, D), lambda i, f: (0, f, 0)),
                  pl.BlockSpec((tm, D), lambda i, f: (i, 0), pipeline_mode=pl.Buffered(1))],
        out_specs=pl.BlockSpec((tm, D), lambda i, f: (i, 0)),
        out_shape=jax.ShapeDtypeStruct((T, D), F32),
        compiler_params=_cp("parallel", "arbitrary"),
        name="dense_swiglu",
    )(x, w_gate, w_up, w_down, residual)


def _row_copy(src_hbm, dst_ref, src_row, dst_row, sem):
    return pltpu.make_async_copy(src_hbm.at[pl.ds(src_row, 1)], dst_ref.at[pl.ds(dst_row, 1)], sem)


def _moe_rows_per_step(tm, nf):
    return -(-tm // (nf * SUBLANES)) * SUBLANES


def _moe_swiglu_kernel(tok_ref, be_ref, nu_ref, h_hbm, wg_ref, wu_ref, wd0_ref, wd1_ref, o_ref,
                       xg_ref, xb_ref, sem, *, tm, nf):
    del be_ref
    i = pl.program_id(0)
    f = pl.program_id(1)
    n_used = nu_ref[0]
    rows_per_step = _moe_rows_per_step(tm, nf)
    n_fetch = rows_per_step * nf

    def fetch(block, r, slot):
        tok = tok_ref[block * tm + jnp.minimum(r, tm - 1)]
        return _row_copy(h_hbm, xg_ref.at[slot], tok, r, sem.at[slot])

    def wait_rows(slot, n_rows):
        rows = xg_ref.at[slot, pl.ds(0, n_rows)]
        pltpu.make_async_copy(rows, rows, sem.at[slot]).wait()

    @pl.when((i >= n_used) & (f == 0))
    def _():
        o_ref[...] = jnp.zeros_like(o_ref)

    @pl.when(i < n_used)
    def _():
        slot = i % 2

        @pl.when((i == 0) & (f == 0))
        def _():
            def start(r, c):
                fetch(0, r, 0).start()
                return c
            lax.fori_loop(0, n_fetch, start, 0)
            for j in range(rows_per_step):
                fetch(jnp.minimum(1, n_used - 1), j, 1).start()

        @pl.when(f == 0)
        def _():
            wait_rows(slot, n_fetch)
            xb_ref[...] = xg_ref[slot, 0:tm, :].astype(BF16)
            o_ref[...] = jnp.zeros_like(o_ref)

        wrap = (f == nf - 1).astype(jnp.int32)
        chunk = (f + 1) * (1 - wrap)
        target = jnp.minimum(i + 1 + wrap, n_used - 1)
        target_slot = (i + 1 + wrap) % 2
        for j in range(rows_per_step):
            fetch(target, chunk * rows_per_step + j, target_slot).start()

        _swiglu_accumulate(xb_ref[...], wg_ref, wu_ref, (wd0_ref, wd1_ref), o_ref)

        @pl.when((i == n_used - 1) & (f == nf - 1))
        def _():
            wait_rows(1 - slot, n_fetch)
            wait_rows(slot, rows_per_step)


def moe_swiglu(h, slot_tok, block_e, n_used, w_gate, w_up, w_down_lo, w_down_hi, tm, tf=1024):
    n_slots = slot_tok.shape[0]
    D = h.shape[1]
    F = w_gate.shape[2]
    while F % tf:
        tf //= 2
    nf = F // tf
    xg_rows = _moe_rows_per_step(tm, nf) * nf
    blk = lambda i, nu: jnp.minimum(i, nu[0] - 1)
    chunk = lambda i, f, nu: jnp.where(i < nu[0], f, nf - 1)
    up_spec = pl.BlockSpec((1, D, tf), lambda i, f, tok, be, nu: (be[blk(i, nu)], 0, chunk(i, f, nu)))
    down_spec = pl.BlockSpec((1, tf, D // 2), lambda i, f, tok, be, nu: (be[blk(i, nu)], chunk(i, f, nu), 0))
    return pl.pallas_call(
        functools.partial(_moe_swiglu_kernel, tm=tm, nf=nf),
        grid_spec=pltpu.PrefetchScalarGridSpec(
            num_scalar_prefetch=3,
            grid=(n_slots // tm, nf),
            in_specs=[pl.BlockSpec(memory_space=pl.ANY), up_spec, up_spec, down_spec, down_spec],
            out_specs=pl.BlockSpec((tm, D), lambda i, f, tok, be, nu: (i, 0)),
            scratch_shapes=[pltpu.VMEM((2, xg_rows, D), F32),
                            pltpu.VMEM((tm, D), BF16),
                            pltpu.SemaphoreType.DMA((2,))]),
        out_shape=jax.ShapeDtypeStruct((n_slots, D), F32),
        compiler_params=_cp("arbitrary", "arbitrary"),
        name="moe_swiglu",
    )(slot_tok, block_e, n_used, h, w_gate, w_up, w_down_lo, w_down_hi)


def _combine_kernel(dest_ref, y_ref, gate_ref, ys_hbm, o_ref, buf_ref, sem, *, tb):
    i = pl.program_id(0)

    def fetch(step, slot):
        base = step * tb

        def start(r, c):
            for k in range(TOP_K):
                _row_copy(ys_hbm, buf_ref.at[slot, k], dest_ref[TOP_K * (base + r) + k], r, sem.at[slot]).start()
            return c

        lax.fori_loop(0, tb, start, 0, unroll=8)

    @pl.when(i == 0)
    def _():
        fetch(0, 0)

    @pl.when(i + 1 < pl.num_programs(0))
    def _():
        fetch(i + 1, (i + 1) % 2)

    slot = i % 2
    pltpu.make_async_copy(buf_ref.at[slot], buf_ref.at[slot], sem.at[slot]).wait()
    g = gate_ref[...]
    o_ref[...] = y_ref[...] + g[:, 0:1] * buf_ref[slot, 0] + g[:, 1:2] * buf_ref[slot, 1]


def combine_rows(y, gates, ys, dest, tb=256):
    T, D = y.shape
    return pl.pallas_call(
        functools.partial(_combine_kernel, tb=tb),
        grid_spec=pltpu.PrefetchScalarGridSpec(
            num_scalar_prefetch=1,
            grid=(T // tb,),
            in_specs=[pl.BlockSpec((tb, D), lambda i, d: (i, 0)),
                      pl.BlockSpec((tb, LANES), lambda i, d: (i, 0)),
                      pl.BlockSpec(memory_space=pl.ANY)],
            out_specs=pl.BlockSpec((tb, D), lambda i, d: (i, 0)),
            scratch_shapes=[pltpu.VMEM((2, TOP_K, tb, D), F32),
                            pltpu.SemaphoreType.DMA((2,))]),
        out_shape=jax.ShapeDtypeStruct((T, D), F32),
        compiler_params=_cp("arbitrary"),
        name="moe_combine",
    )(dest, y, gates, ys)


def _routing_tables(top_idx, T, block):
    A = T * TOP_K
    E = N_EXPERTS
    flat_e = top_idx.reshape(A)
    onehot = (flat_e[:, None] == jnp.arange(E, dtype=jnp.int32)[None, :]).astype(jnp.int32)
    csum = jnp.cumsum(onehot, axis=0)
    counts = csum[-1]
    rank = jnp.sum((csum - 1) * onehot, axis=1)
    padded = (counts + block - 1) // block * block
    pends = jnp.cumsum(padded)
    pstarts = pends - padded
    starts = jnp.cumsum(counts) - counts
    dest = (pstarts[flat_e] + rank).astype(jnp.int32)
    n_blocks = A // block + E
    n_slots = n_blocks * block
    order = jnp.argsort(flat_e, stable=True).astype(jnp.int32)
    slot = jnp.arange(n_slots, dtype=jnp.int32)
    slot_e = jnp.clip(jnp.searchsorted(pends, slot, side='right'), 0, E - 1).astype(jnp.int32)
    r = slot - pstarts[slot_e]
    valid = r < counts[slot_e]
    src = jnp.clip(starts[slot_e] + r, 0, A - 1)
    slot_tok = jnp.where(valid, order[src] // TOP_K, 0).astype(jnp.int32)
    block_e = slot_e[::block]
    n_used = (pends[-1:] // block).astype(jnp.int32)
    return slot_tok, dest, block_e, n_used


def _token_mixer(y, attn_g, mem2d, tab, B, S, M, l, casts, w_in, mla_q_a_norm_g, mla_w_uq, mla_kv_a_norm_g,
                 mla_w_ukv, mla_q_norm_g, mla_k_norm_g, mla_w_out, pool_w, pool_scale, pool_w_out,
                 mem_norm_g, mem_w_kv, mem_q_norm_g, mem_k_norm_g, mem_w_out):
    h, za = norm_matmul(y, attn_g, w_in, l, ZA_WIDTH)

    head_pad = lambda v: jnp.concatenate([v[..., :QK_NOPE_DIM], _rope_lanes(v[..., QK_NOPE_DIM:])], axis=-1)
    wuq = head_pad(mla_w_uq[l].reshape(Q_LORA_RANK, MLA_HEADS, QK_HEAD_DIM)).reshape(
        Q_LORA_RANK, MLA_HEADS * HEAD_PAD)
    wukv = mla_w_ukv[l].reshape(KV_LORA_RANK, MLA_HEADS, QK_NOPE_DIM + V_HEAD_DIM)
    wuk = wukv[:, :, :QK_NOPE_DIM].reshape(KV_LORA_RANK, MLA_WIDTH)
    wuv = wukv[:, :, QK_NOPE_DIM:].reshape(KV_LORA_RANK, MLA_WIDTH)
    pad_g = lambda g: head_pad(g).reshape(1, HEAD_PAD)
    q, k, v = mla_prep(za, tab, wuq.astype(BF16), wuk.astype(BF16), wuv.astype(BF16),
                       mla_q_a_norm_g[l].reshape(1, -1), mla_kv_a_norm_g[l].reshape(1, -1),
                       pad_g(mla_q_norm_g[l]), pad_g(mla_k_norm_g[l]))
    attn, *cast_out = causal_attention(q, k, v, B, S, casts)

    mixed = pool_mixer(za, pool_w[l].astype(BF16), pool_scale[l].reshape(1, -1), B, S)

    k_m, v_m = mem_kv(mem2d, mem_norm_g[l].reshape(1, -1), mem_w_kv[l].astype(BF16),
                      mem_k_norm_g[l].reshape(1, -1), B, M)
    o_mem = mem_attention(za, k_m, v_m, mem_q_norm_g[l].reshape(1, -1), B, S, M)

    merged = merge_branches(h, attn, mixed, o_mem, w_in, mla_w_out, pool_w_out, mem_w_out, l)
    return merged, cast_out


def kernel(x, mem, positions, attn_norm_g, w_in, mla_q_a_norm_g, mla_w_uq, mla_kv_a_norm_g, mla_w_ukv, mla_q_norm_g, mla_k_norm_g, mla_w_out, pool_w, pool_scale, pool_w_out, mem_norm_g, mem_w_kv, mem_q_norm_g, mem_k_norm_g, mem_w_out, w_o, ffn_norm_g, dense_w_gate, dense_w_up, dense_w_down, router_w, router_b, moe_w_gate, moe_w_up, moe_w_down):
    B, S, D = x.shape
    M = mem.shape[1]
    T = B * S
    depth = attn_norm_g.shape[0]
    moe_block = 512
    y = x.reshape(T, D)
    mem2d = mem.reshape(B * M, D)
    tab = rope_tables(positions)
    w_in = prep_w_in(w_in)
    mla_w_out, pool_w_out, mem_w_out, w_o = (w.astype(BF16) for w in (mla_w_out, pool_w_out, mem_w_out, w_o))
    moe_bf16 = {}
    for l in range(depth):
        i = l // 2
        casts = ()
        if l % 2 == 0 and l + 1 < depth:
            casts = ((moe_w_gate[i], 0, 1), (moe_w_down[i], 0, 2))
        elif l % 2 == 1:
            casts = ((moe_w_up[i], 0, 1), (moe_w_down[i], 1, 2))
        merged, cast_out = _token_mixer(y, attn_norm_g[l], mem2d, tab, B, S, M, l, casts, w_in, mla_q_a_norm_g,
                                        mla_w_uq, mla_kv_a_norm_g, mla_w_ukv, mla_q_norm_g, mla_k_norm_g,
                                        mla_w_out, pool_w, pool_scale, pool_w_out, mem_norm_g, mem_w_kv,
                                        mem_q_norm_g, mem_k_norm_g, mem_w_out)
        if casts:
            names = ("gate", "down_lo") if l % 2 == 0 else ("up", "down_hi")
            moe_bf16.update(zip(names, cast_out))
        if l % 2 == 0:
            y, h = matmul_residual_norm(merged, w_o, l, y, ffn_norm_g[l])
            y = dense_swiglu(h, dense_w_gate[i:i + 1].astype(BF16), dense_w_up[i:i + 1].astype(BF16),
                             dense_w_down[i:i + 1].astype(BF16), y)
        else:
            y = matmul(merged, w_o, l, D, F32, residual=y, tm=512, tn=D)
            hf, idx, gates = norm_router(y, ffn_norm_g[l], router_w[i], router_b[i])
            slot_tok, dest, block_e, n_used = _routing_tables(idx[:, :TOP_K], T, moe_block)
            ys = moe_swiglu(hf, slot_tok, block_e, n_used, moe_bf16.pop("gate"), moe_bf16.pop("up"),
                            moe_bf16.pop("down_lo"), moe_bf16.pop("down_hi"), tm=moe_block)
            y = combine_rows(y, gates, ys, dest)
    return y.reshape(B, S, D)
```

```python
import functools

import numpy as np
import jax
import jax.numpy as jnp
from jax import lax
from jax.experimental import pallas as pl
from jax.experimental.pallas import tpu as pltpu

F32 = jnp.float32
BF16 = jnp.bfloat16

MEM_HEADS = 4
MEM_HEAD_DIM = 256
MEM_WIDTH = MEM_HEADS * MEM_HEAD_DIM
MLA_HEADS = 8
Q_LORA_RANK = 512
KV_LORA_RANK = 256
QK_NOPE_DIM = 128
QK_ROPE_DIM = 64
QK_HEAD_DIM = QK_NOPE_DIM + QK_ROPE_DIM
V_HEAD_DIM = 128
MLA_WIDTH = MLA_HEADS * V_HEAD_DIM
ROPE_THETA = 10000.0
POOL_WINDOWS = (2, 4, 8, 16)
POOL_GROUPS = 4
POOL_GROUP_DIM = 256
POOL_WIDTH = POOL_GROUPS * POOL_GROUP_DIM
N_BRANCHES = 3
N_EXPERTS = 8
TOP_K = 2
NORM_EPS = 1e-6
LOG2_E = 1.4426950408889634

LANES = 128
SUBLANES = 8
HEAD_PAD = 2 * LANES
ROPE_HALF = QK_ROPE_DIM // 2
POOL_HALO = 16

ZA_POOL = 0
ZA_QMEM = POOL_WIDTH
ZA_CQ = ZA_QMEM + MEM_WIDTH
ZA_CKV = ZA_CQ + Q_LORA_RANK
ZA_KPE = ZA_CKV + KV_LORA_RANK
ZA_WIDTH = 3072

VMEM_LIMIT = 52 * 1024 * 1024


def _cp(*sem):
    return pltpu.CompilerParams(dimension_semantics=sem, vmem_limit_bytes=VMEM_LIMIT)


def _rms(x, eps_dim):
    return lax.rsqrt(jnp.sum(x * x, axis=-1, keepdims=True) * (1.0 / eps_dim) + NORM_EPS)


def _rmsnorm_kernel(x_ref, g_ref, o_ref):
    x = x_ref[...]
    o_ref[...] = (x * _rms(x, x.shape[-1]) * g_ref[...]).astype(o_ref.dtype)


def rmsnorm(x, g, out_dtype, tm=512):
    T, D = x.shape
    return pl.pallas_call(
        _rmsnorm_kernel,
        grid=(T // tm,),
        in_specs=[pl.BlockSpec((tm, D), lambda i: (i, 0)),
                  pl.BlockSpec((1, D), lambda i: (0, 0))],
        out_specs=pl.BlockSpec((tm, D), lambda i: (i, 0)),
        out_shape=jax.ShapeDtypeStruct((T, D), out_dtype),
        compiler_params=_cp("parallel"),
        name="rmsnorm",
    )(x, g.reshape(1, D))


def _norm_router_kernel(x_ref, g_ref, rwh_ref, rwl_ref, rb_ref, h_ref, idx_ref, gate_ref):
    x = x_ref[...]
    h = x * _rms(x, x.shape[-1]) * g_ref[...]
    h_ref[...] = h
    h_hi = h.astype(BF16)
    h_lo = (h - h_hi.astype(F32)).astype(BF16)
    logits = (jnp.dot(h_hi, rwh_ref[...], preferred_element_type=F32)
              + jnp.dot(h_lo, rwh_ref[...], preferred_element_type=F32)
              + jnp.dot(h_hi, rwl_ref[...], preferred_element_type=F32)) + rb_ref[...]
    lane = lax.broadcasted_iota(jnp.int32, logits.shape, 1)
    neg = -jnp.inf
    l1 = jnp.where(lane < N_EXPERTS, logits, neg)
    m1 = jnp.max(l1, axis=-1, keepdims=True)
    i1 = jnp.min(jnp.where(l1 == m1, lane, LANES), axis=-1, keepdims=True)
    l2 = jnp.where(lane == i1, neg, l1)
    m2 = jnp.max(l2, axis=-1, keepdims=True)
    i2 = jnp.min(jnp.where(l2 == m2, lane, LANES), axis=-1, keepdims=True)
    e = jnp.exp(m2 - m1)
    g1 = 1.0 / (1.0 + e)
    g2 = e / (1.0 + e)
    idx_ref[...] = jnp.where(lane == 0, i1, jnp.where(lane == 1, i2, 0))
    gate_ref[...] = jnp.where(lane == 0, g1, jnp.where(lane == 1, g2, 0.0))


def norm_router(x, g, router_w, router_b, tm=512):
    T, D = x.shape
    E = router_w.shape[1]
    rw = jnp.zeros((D, LANES), F32).at[:, :E].set(router_w)
    rw_hi = rw.astype(BF16)
    rw_lo = (rw - rw_hi.astype(F32)).astype(BF16)
    rb = jnp.zeros((1, LANES), F32).at[0, :E].set(router_b)
    return pl.pallas_call(
        _norm_router_kernel,
        grid=(T // tm,),
        in_specs=[pl.BlockSpec((tm, D), lambda i: (i, 0)),
                  pl.BlockSpec((1, D), lambda i: (0, 0)),
                  pl.BlockSpec((D, LANES), lambda i: (0, 0)),
                  pl.BlockSpec((D, LANES), lambda i: (0, 0)),
                  pl.BlockSpec((1, LANES), lambda i: (0, 0))],
        out_specs=[pl.BlockSpec((tm, D), lambda i: (i, 0)),
                   pl.BlockSpec((tm, LANES), lambda i: (i, 0)),
                   pl.BlockSpec((tm, LANES), lambda i: (i, 0))],
        out_shape=[jax.ShapeDtypeStruct((T, D), F32),
                   jax.ShapeDtypeStruct((T, LANES), jnp.int32),
                   jax.ShapeDtypeStruct((T, LANES), F32)],
        compiler_params=_cp("parallel"),
        name="norm_router",
    )(x, g.reshape(1, D), rw_hi, rw_lo, rb)


def _mm_kernel(a_ref, w_ref, o_ref):
    o_ref[...] = jnp.dot(a_ref[...], w_ref[0], preferred_element_type=F32).astype(o_ref.dtype)


def _mm_res_kernel(a_ref, w_ref, r_ref, o_ref):
    o_ref[...] = (r_ref[...] + jnp.dot(a_ref[...], w_ref[0], preferred_element_type=F32)).astype(o_ref.dtype)


def matmul(a, w, l, N, out_dtype, residual=None, tm=1024, tn=1024):
    M, K = a.shape
    tm, tn = min(tm, M), min(tn, N)
    in_specs = [pl.BlockSpec((tm, K), lambda i, j: (i, 0)),
                pl.BlockSpec((1, K, tn), lambda i, j: (l, 0, j))]
    args = [a, w]
    body = _mm_kernel
    if residual is not None:
        in_specs.append(pl.BlockSpec((tm, tn), lambda i, j: (i, j)))
        args.append(residual)
        body = _mm_res_kernel
    return pl.pallas_call(
        body,
        grid=(M // tm, N // tn),
        in_specs=in_specs,
        out_specs=pl.BlockSpec((tm, tn), lambda i, j: (i, j)),
        out_shape=jax.ShapeDtypeStruct((M, N), out_dtype),
        compiler_params=_cp("parallel", "arbitrary"),
        name="matmul",
    )(*args)


def _mm_res_norm_kernel(a_ref, w_ref, r_ref, g_ref, o_ref, h_ref):
    y = r_ref[...] + jnp.dot(a_ref[...], w_ref[0], preferred_element_type=F32)
    o_ref[...] = y
    h_ref[...] = (y * _rms(y, y.shape[-1]) * g_ref[...]).astype(h_ref.dtype)


def matmul_residual_norm(a, w, l, residual, g, tm=512):
    M, K = a.shape
    N = w.shape[2]
    rows = lambda width: pl.BlockSpec((tm, width), lambda i: (i, 0))
    return pl.pallas_call(
        _mm_res_norm_kernel,
        grid=(M // tm,),
        in_specs=[rows(K), pl.BlockSpec((1, K, N), lambda i: (l, 0, 0)), rows(N),
                  pl.BlockSpec((1, N), lambda i: (0, 0))],
        out_specs=[rows(N), rows(N)],
        out_shape=[jax.ShapeDtypeStruct((M, N), F32), jax.ShapeDtypeStruct((M, N), BF16)],
        compiler_params=_cp("parallel"),
        name="matmul_residual_norm",
    )(a, w, residual, g.reshape(1, N))


def _rope_lanes(v):
    z = jnp.zeros(v.shape[:-1] + (LANES // 2 - ROPE_HALF,), v.dtype)
    return jnp.concatenate([v[..., :ROPE_HALF], z, v[..., ROPE_HALF:], z], axis=-1)


def _rope_table_kernel(pos_ref, invf_ref, o_ref):
    ang = pos_ref[...].astype(F32) * invf_ref[...]
    lane = lax.broadcasted_iota(jnp.int32, ang.shape, 1)
    first = lane < ROPE_HALF
    second = (lane >= LANES // 2) & (lane < LANES // 2 + ROPE_HALF)
    c = jnp.cos(ang)
    s = jnp.sin(ang)
    o_ref[:, 0:LANES] = jnp.where(first | second, c, 0.0)
    o_ref[:, LANES:2 * LANES] = jnp.where(first, -s, jnp.where(second, s, 0.0))


def rope_tables(positions, tm=512):
    T = positions.size
    inv_freq = ROPE_THETA ** (-np.arange(0, QK_ROPE_DIM, 2, dtype=np.float32) / QK_ROPE_DIM)
    invf = np.zeros((1, LANES), np.float32)
    invf[0, :ROPE_HALF] = inv_freq
    invf[0, LANES // 2:LANES // 2 + ROPE_HALF] = inv_freq
    return pl.pallas_call(
        _rope_table_kernel,
        grid=(T // tm,),
        in_specs=[pl.BlockSpec((tm, 1), lambda i: (i, 0)),
                  pl.BlockSpec((1, LANES), lambda i: (0, 0))],
        out_specs=pl.BlockSpec((tm, 2 * LANES), lambda i: (i, 0)),
        out_shape=jax.ShapeDtypeStruct((T, 2 * LANES), F32),
        compiler_params=_cp("parallel"),
        name="rope_tables",
    )(positions.reshape(T, 1), jnp.asarray(invf))


def _rope(x, tab):
    return x * tab[:, 0:LANES] + pltpu.roll(x, LANES // 2, 1) * tab[:, LANES:2 * LANES]


def _mla_prep_kernel(cq_ref, ckv_ref, kpe_ref, tab_ref, wuq_ref, wuk_ref, wuv_ref,
                     gqa_ref, gkva_ref, gq_ref, gk_ref, q_ref, k_ref, v_ref):
    tab = tab_ref[...]
    cq = cq_ref[...]
    cqn = (cq * _rms(cq, Q_LORA_RANK) * gqa_ref[...]).astype(BF16)
    qf = jnp.dot(cqn, wuq_ref[...], preferred_element_type=F32)
    ckv = ckv_ref[...]
    ckvn = (ckv * _rms(ckv, KV_LORA_RANK) * gkva_ref[...]).astype(BF16)
    kn = jnp.dot(ckvn, wuk_ref[...], preferred_element_type=F32)
    v_ref[...] = jnp.dot(ckvn, wuv_ref[...], preferred_element_type=F32).astype(v_ref.dtype)
    gq = gq_ref[...]
    gk = gk_ref[...]
    kpe = kpe_ref[...]
    ss_pe = jnp.sum(kpe * kpe, axis=-1, keepdims=True)
    kpe_rot = _rope(kpe * gk[:, LANES:], tab)
    scale = QK_HEAD_DIM ** -0.5 * LOG2_E
    for h in range(MLA_HEADS):
        q0 = qf[:, h * HEAD_PAD:h * HEAD_PAD + LANES]
        q1 = qf[:, h * HEAD_PAD + LANES:(h + 1) * HEAD_PAD]
        ss = jnp.sum(q0 * q0, axis=-1, keepdims=True) + jnp.sum(q1 * q1, axis=-1, keepdims=True)
        rq = lax.rsqrt(ss * (1.0 / QK_HEAD_DIM) + NORM_EPS) * scale
        q_ref[:, h * HEAD_PAD:h * HEAD_PAD + LANES] = (q0 * gq[:, :LANES] * rq).astype(q_ref.dtype)
        q_ref[:, h * HEAD_PAD + LANES:(h + 1) * HEAD_PAD] = (_rope(q1 * gq[:, LANES:], tab) * rq).astype(q_ref.dtype)
        k0 = kn[:, h * LANES:(h + 1) * LANES]
        ssk = jnp.sum(k0 * k0, axis=-1, keepdims=True) + ss_pe
        rk = lax.rsqrt(ssk * (1.0 / QK_HEAD_DIM) + NORM_EPS)
        k_ref[:, h * HEAD_PAD:h * HEAD_PAD + LANES] = (k0 * gk[:, :LANES] * rk).astype(k_ref.dtype)
        k_ref[:, h * HEAD_PAD + LANES:(h + 1) * HEAD_PAD] = (kpe_rot * rk).astype(k_ref.dtype)


def mla_prep(za, tab, wuq, wuk, wuv, gqa, gkva, gq, gk, tm=512):
    T = za.shape[0]
    HP = MLA_HEADS * HEAD_PAD
    full = lambda shape: pl.BlockSpec(shape, lambda i: (0, 0))
    return pl.pallas_call(
        _mla_prep_kernel,
        grid=(T // tm,),
        in_specs=[pl.BlockSpec((tm, Q_LORA_RANK), lambda i: (i, ZA_CQ // Q_LORA_RANK)),
                  pl.BlockSpec((tm, KV_LORA_RANK), lambda i: (i, ZA_CKV // KV_LORA_RANK)),
                  pl.BlockSpec((tm, LANES), lambda i: (i, ZA_KPE // LANES)),
                  pl.BlockSpec((tm, 2 * LANES), lambda i: (i, 0)),
                  full((Q_LORA_RANK, HP)), full((KV_LORA_RANK, MLA_WIDTH)), full((KV_LORA_RANK, MLA_WIDTH)),
                  full((1, Q_LORA_RANK)), full((1, KV_LORA_RANK)), full((1, HEAD_PAD)), full((1, HEAD_PAD))],
        out_specs=[pl.BlockSpec((tm, HP), lambda i: (i, 0)),
                   pl.BlockSpec((tm, HP), lambda i: (i, 0)),
                   pl.BlockSpec((tm, MLA_WIDTH), lambda i: (i, 0))],
        out_shape=[jax.ShapeDtypeStruct((T, HP), BF16),
                   jax.ShapeDtypeStruct((T, HP), BF16),
                   jax.ShapeDtypeStruct((T, MLA_WIDTH), BF16)],
        compiler_params=_cp("parallel"),
        name="mla_prep",
    )(za, za, za, tab, wuq, wuk, wuv, gqa, gkva, gq, gk)


def _causal_attn_kernel(q_ref, k_ref, v_ref, *rest, tq, n_cast):
    cast_src = rest[:n_cast]
    o_ref = rest[n_cast]
    cast_dst = rest[n_cast + 1:]
    for s_ref, d_ref in zip(cast_src, cast_dst):
        d_ref[...] = s_ref[...].astype(d_ref.dtype)
    S = q_ref.shape[0]
    row = lax.broadcasted_iota(jnp.int32, (tq, tq), 0)
    col = lax.broadcasted_iota(jnp.int32, (tq, tq), 1)
    tri = col <= row
    for qi in range(S // tq):
        kl = (qi + 1) * tq
        q = q_ref[qi * tq:kl, :]
        s = lax.dot_general(q, k_ref[0:kl, :], (((1,), (1,)), ((), ())), preferred_element_type=F32)
        s_diag = jnp.where(tri, s[:, kl - tq:kl], -jnp.inf)
        if qi > 0:
            s = jnp.concatenate([s[:, :kl - tq], s_diag], axis=1)
        else:
            s = s_diag
        m = jnp.max(s, axis=-1, keepdims=True)
        p = jnp.exp2(s - m)
        l = jnp.sum(p, axis=-1, keepdims=True)
        o = jnp.dot(p.astype(BF16), v_ref[0:kl, :], preferred_element_type=F32)
        o_ref[qi * tq:kl, :] = (o / l).astype(o_ref.dtype)


def causal_attention(q, k, v, B, S, casts=(), tq=256):
    T = q.shape[0]
    n_steps = B * MLA_HEADS
    in_specs = [pl.BlockSpec((S, HEAD_PAD), lambda b, h: (b, h)),
                pl.BlockSpec((S, HEAD_PAD), lambda b, h: (b, h)),
                pl.BlockSpec((S, V_HEAD_DIM), lambda b, h: (b, h))]
    out_specs = [pl.BlockSpec((S, V_HEAD_DIM), lambda b, h: (b, h))]
    out_shape = [jax.ShapeDtypeStruct((T, MLA_WIDTH), BF16)]
    for src, part, n_parts in casts:
        E, rows, cols = src.shape
        spe = n_steps // E
        rb, cb = rows // spe, cols // n_parts
        assert spe * E == n_steps and rb * spe == rows and rb % 16 == 0
        assert cb * n_parts == cols and cb % LANES == 0
        step = lambda b, h: b * MLA_HEADS + h
        in_specs.append(pl.BlockSpec((1, rb, cb), lambda b, h, part=part, spe=spe:
                                     (step(b, h) // spe, step(b, h) % spe, part)))
        out_specs.append(pl.BlockSpec((1, rb, cb), lambda b, h, spe=spe:
                                      (step(b, h) // spe, step(b, h) % spe, 0)))
        out_shape.append(jax.ShapeDtypeStruct((E, rows, cb), BF16))
    return pl.pallas_call(
        functools.partial(_causal_attn_kernel, tq=tq, n_cast=len(casts)),
        grid=(B, MLA_HEADS),
        in_specs=in_specs,
        out_specs=out_specs,
        out_shape=out_shape,
        compiler_params=_cp("parallel", "parallel"),
        name="causal_attention",
    )(q, k, v, *(c[0] for c in casts))


def _pool_kernel(u_ref, w_ref, sc_ref, o_ref, pad_ref, *, rc):
    S = u_ref.shape[0]
    C = POOL_GROUP_DIM
    t1 = lax.broadcasted_iota(jnp.int32, (rc, C), 0).astype(F32) + 1.0
    for g, win in enumerate(POOL_WINDOWS):
        pad_ref[0:POOL_HALO, :] = jnp.zeros((POOL_HALO, C), F32)
        pad_ref[POOL_HALO:POOL_HALO + S, :] = u_ref[:, g * C:(g + 1) * C]
        for r0 in range(0, S, rc):
            acc = pad_ref[POOL_HALO + r0:POOL_HALO + r0 + rc, :]
            tok = acc
            for kk in range(1, win):
                acc = acc + pad_ref[POOL_HALO + r0 - kk:POOL_HALO + r0 - kk + rc, :]
            cnt = jnp.minimum(t1 + float(r0), float(win))
            pooled = (acc / cnt - tok).astype(BF16)
            mixed = jnp.dot(pooled, w_ref[g], preferred_element_type=F32) * sc_ref[:, g * C:(g + 1) * C]
            o_ref[r0:r0 + rc, g * C:(g + 1) * C] = mixed.astype(o_ref.dtype)


def pool_mixer(za, pool_w, pool_scale, B, S, rc=512):
    T = za.shape[0]
    rc = min(rc, S)
    return pl.pallas_call(
        functools.partial(_pool_kernel, rc=rc),
        grid=(B,),
        in_specs=[pl.BlockSpec((S, POOL_WIDTH), lambda b: (b, ZA_POOL // POOL_WIDTH)),
                  pl.BlockSpec((POOL_GROUPS, POOL_GROUP_DIM, POOL_GROUP_DIM), lambda b: (0, 0, 0)),
                  pl.BlockSpec((1, POOL_WIDTH), lambda b: (0, 0))],
        out_specs=pl.BlockSpec((S, POOL_WIDTH), lambda b: (b, 0)),
        out_shape=jax.ShapeDtypeStruct((T, POOL_WIDTH), BF16),
        scratch_shapes=[pltpu.VMEM((POOL_HALO + S, POOL_GROUP_DIM), F32)],
        compiler_params=_cp("parallel"),
        name="pool_mixer",
    )(za, pool_w, pool_scale)


def _mem_kv_kernel(m_ref, g_ref, w_ref, gk_ref, k_ref, v_ref):
    x = m_ref[...]
    xn = (x * _rms(x, x.shape[-1]) * g_ref[...]).astype(BF16)
    kv = jnp.dot(xn, w_ref[...], preferred_element_type=F32)
    for h in range(MEM_HEADS):
        kh = kv[:, h * MEM_HEAD_DIM:(h + 1) * MEM_HEAD_DIM]
        k_ref[:, h * MEM_HEAD_DIM:(h + 1) * MEM_HEAD_DIM] = (
            kh * _rms(kh, MEM_HEAD_DIM) * gk_ref[...]).astype(k_ref.dtype)
    v_ref[...] = kv[:, MEM_WIDTH:].astype(v_ref.dtype)


def mem_kv(mem2d, g, w_kv, gk, B, M):
    D = mem2d.shape[1]
    return pl.pallas_call(
        _mem_kv_kernel,
        grid=(B,),
        in_specs=[pl.BlockSpec((M, D), lambda b: (b, 0)),
                  pl.BlockSpec((1, D), lambda b: (0, 0)),
                  pl.BlockSpec((D, 2 * MEM_WIDTH), lambda b: (0, 0)),
                  pl.BlockSpec((1, MEM_HEAD_DIM), lambda b: (0, 0))],
        out_specs=[pl.BlockSpec((M, MEM_WIDTH), lambda b: (b, 0)),
                   pl.BlockSpec((M, MEM_WIDTH), lambda b: (b, 0))],
        out_shape=[jax.ShapeDtypeStruct((B * M, MEM_WIDTH), BF16),
                   jax.ShapeDtypeStruct((B * M, MEM_WIDTH), BF16)],
        compiler_params=_cp("parallel"),
        name="mem_kv",
    )(mem2d, g, w_kv, gk)


def _mem_attn_kernel(q_ref, k_ref, v_ref, gq_ref, o_ref):
    scale = MEM_HEAD_DIM ** -0.5 * LOG2_E
    for h in range(MEM_HEADS):
        sl = slice(h * MEM_HEAD_DIM, (h + 1) * MEM_HEAD_DIM)
        qh = q_ref[:, sl]
        qn = (qh * (_rms(qh, MEM_HEAD_DIM) * scale) * gq_ref[...]).astype(BF16)
        s = lax.dot_general(qn, k_ref[:, sl], (((1,), (1,)), ((), ())), preferred_element_type=F32)
        m = jnp.max(s, axis=-1, keepdims=True)
        p = jnp.exp2(s - m)
        l = jnp.sum(p, axis=-1, keepdims=True)
        o = jnp.dot(p.astype(BF16), v_ref[:, sl], preferred_element_type=F32)
        o_ref[:, sl] = (o / l).astype(o_ref.dtype)


def mem_attention(za, k_m, v_m, gq, B, S, M, tq=512):
    T = za.shape[0]
    tq = min(tq, S)
    nq = S // tq
    return pl.pallas_call(
        _mem_attn_kernel,
        grid=(B, nq),
        in_specs=[pl.BlockSpec((tq, MEM_WIDTH), lambda b, i: (b * nq + i, ZA_QMEM // MEM_WIDTH)),
                  pl.BlockSpec((M, MEM_WIDTH), lambda b, i: (b, 0)),
                  pl.BlockSpec((M, MEM_WIDTH), lambda b, i: (b, 0)),
                  pl.BlockSpec((1, MEM_HEAD_DIM), lambda b, i: (0, 0))],
        out_specs=pl.BlockSpec((tq, MEM_WIDTH), lambda b, i: (b * nq + i, 0)),
        out_shape=jax.ShapeDtypeStruct((T, MEM_WIDTH), BF16),
        compiler_params=_cp("parallel", "arbitrary"),
        name="mem_attention",
    )(za, k_m, v_m, gq)


def _merge_kernel(h_ref, a_ref, p_ref, m_ref, g0_ref, g1_ref, g2_ref, w0_ref, w1_ref, w2_ref, o_ref):
    h = h_ref[...]
    acc = None
    for x_ref, g_ref, w_ref in ((a_ref, g0_ref, w0_ref), (p_ref, g1_ref, w1_ref), (m_ref, g2_ref, w2_ref)):
        logit = jnp.dot(h, g_ref[0], preferred_element_type=F32)
        gate = 1.0 / (1.0 + jnp.exp(-logit))
        br = jnp.dot(x_ref[...], w_ref[0], preferred_element_type=F32)
        acc = gate * br if acc is None else acc + gate * br
    o_ref[...] = acc.astype(o_ref.dtype)


def merge_branches(h, a, p, m, w_all, w_a, w_p, w_m, l, tm=1024, tn=512):
    T, D = h.shape
    W = a.shape[1]
    nj = D // tn
    g0 = ZA_WIDTH // tn
    row = lambda width: pl.BlockSpec((tm, width), lambda i, j: (i, 0))
    gate_spec = lambda b: pl.BlockSpec((1, D, tn), lambda i, j, b=b: (l, 0, g0 + b * nj + j))
    out_w = pl.BlockSpec((1, W, tn), lambda i, j: (l, 0, j))
    return pl.pallas_call(
        _merge_kernel,
        grid=(T // tm, nj),
        in_specs=[row(D), row(W), row(W), row(W),
                  gate_spec(0), gate_spec(1), gate_spec(2), out_w, out_w, out_w],
        out_specs=pl.BlockSpec((tm, tn), lambda i, j: (i, j)),
        out_shape=jax.ShapeDtypeStruct((T, D), BF16),
        compiler_params=_cp("parallel", "arbitrary"),
        name="merge_branches",
    )(h, a, p, m, w_all, w_all, w_all, w_a, w_p, w_m)


def _w_in_prep_kernel(w_ref, o_ref):
    s1 = Q_LORA_RANK + KV_LORA_RANK + QK_ROPE_DIM
    s2 = s1 + POOL_WIDTH
    s3 = s2 + MEM_WIDTH
    tk = w_ref.shape[2]
    s1_up = -(-s1 // LANES) * LANES
    o_ref[0, :, ZA_POOL:ZA_POOL + POOL_WIDTH] = w_ref[0, s1:s2, :].T.astype(BF16)
    o_ref[0, :, ZA_QMEM:ZA_QMEM + MEM_WIDTH] = w_ref[0, s2:s3, :].T.astype(BF16)
    low = w_ref[0, 0:s1_up, :].T.astype(BF16)
    n_c = Q_LORA_RANK + KV_LORA_RANK
    o_ref[0, :, ZA_CQ:ZA_CQ + n_c] = low[:, 0:n_c]
    o_ref[0, :, ZA_KPE:ZA_WIDTH] = jnp.zeros((tk, ZA_WIDTH - ZA_KPE), BF16)
    o_ref[0, :, ZA_KPE:ZA_KPE + ROPE_HALF] = low[:, n_c:n_c + ROPE_HALF]
    o_ref[0, :, ZA_KPE + LANES // 2:ZA_KPE + LANES // 2 + ROPE_HALF] = low[:, n_c + ROPE_HALF:s1]
    o_ref[0, :, ZA_WIDTH:] = w_ref[0, s3:, :].T.astype(BF16)


def prep_w_in(w_in, tk=256):
    L, D, win = w_in.shape
    wout = ZA_WIDTH + N_BRANCHES * D
    return pl.pallas_call(
        _w_in_prep_kernel,
        grid=(L, D // tk),
        in_specs=[pl.BlockSpec((1, win, tk), lambda l, i: (l, 0, i))],
        out_specs=pl.BlockSpec((1, tk, wout), lambda l, i: (l, i, 0)),
        out_shape=jax.ShapeDtypeStruct((L, D, wout), BF16),
        compiler_params=_cp("parallel", "parallel"),
        name="prep_w_in",
    )(jnp.swapaxes(w_in, 1, 2))


def _swiglu_accumulate(xb, wg_ref, wu_ref, wd_refs, o_ref):
    a = jnp.dot(xb, wg_ref[0], preferred_element_type=F32)
    b = jnp.dot(xb, wu_ref[0], preferred_element_type=F32)
    act = (a / (1.0 + jnp.exp(-a)) * b).astype(BF16)
    width = o_ref.shape[1] // len(wd_refs)
    for n, wd_ref in enumerate(wd_refs):
        o_ref[:, n * width:(n + 1) * width] += jnp.dot(act, wd_ref[0], preferred_element_type=F32)


def _dense_swiglu_kernel(x_ref, wg_ref, wu_ref, wd_ref, r_ref, o_ref):
    @pl.when(pl.program_id(1) == 0)
    def _():
        o_ref[...] = r_ref[...]

    _swiglu_accumulate(x_ref[...], wg_ref, wu_ref, (wd_ref,), o_ref)


def dense_swiglu(x, w_gate, w_up, w_down, residual, tm=1024, tf=512):
    T, D = x.shape
    F = w_gate.shape[2]
    tm = min(tm, T)
    return pl.pallas_call(
        _dense_swiglu_kernel,
        grid=(T // tm, F // tf),
        in_specs=[pl.BlockSpec((tm, D), lambda i, f: (i, 0)),
                  pl.BlockSpec((1, D, tf), lambda i, f: (0, 0, f)),
                  pl.BlockSpec((1, D, tf), lambda i, f: (0, 0, f)),
                  pl.BlockSpec((1, tf, D), lambda i, f: (0, f, 0)),
                  pl.BlockSpec((tm, D), lambda i, f: (i, 0), pipeline_mode=pl.Buffered(1))],
        out_specs=pl.BlockSpec((tm, D), lambda i, f: (i, 0)),
        out_shape=jax.ShapeDtypeStruct((T, D), F32),
        compiler_params=_cp("parallel", "arbitrary"),
        name="dense_swiglu",
    )(x, w_gate, w_up, w_down, residual)


def _row_copy(src_hbm, dst_ref, src_row, dst_row, sem):
    return pltpu.make_async_copy(src_hbm.at[pl.ds(src_row, 1)], dst_ref.at[pl.ds(dst_row, 1)], sem)


def _moe_rows_per_step(tm, nf):
    return -(-tm // (nf * SUBLANES)) * SUBLANES


def _moe_swiglu_kernel(tok_ref, be_ref, nu_ref, half_ref, h_hbm, wg_ref, wu_ref, wd0_ref, wd1_ref, o_ref,
                       xg_ref, xb_ref, sem, *, tm, nf):
    del be_ref
    i = pl.program_id(0)
    f = pl.program_id(1)
    n_used = nu_ref[0]
    rows_per_step = _moe_rows_per_step(tm, nf)
    n_fetch = rows_per_step * nf

    def fetch(block, r, slot):
        tok = tok_ref[block * tm + jnp.minimum(r, tm - 1)]
        return _row_copy(h_hbm, xg_ref.at[slot], tok, r, sem.at[slot])

    def wait_rows(slot, n_rows):
        rows = xg_ref.at[slot, pl.ds(0, n_rows)]
        pltpu.make_async_copy(rows, rows, sem.at[slot]).wait()

    @pl.when((i >= n_used) & (f == 0))
    def _():
        o_ref[...] = jnp.zeros_like(o_ref)

    @pl.when(i < n_used)
    def _():
        slot = i % 2

        @pl.when((i == 0) & (f == 0))
        def _():
            def start(r, c):
                fetch(0, r, 0).start()
                return c
            lax.fori_loop(0, n_fetch, start, 0)
            for j in range(rows_per_step):
                fetch(jnp.minimum(1, n_used - 1), j, 1).start()

        @pl.when(f == 0)
        def _():
            wait_rows(slot, n_fetch)
            xb_ref[...] = xg_ref[slot, 0:tm, :].astype(BF16)
            o_ref[...] = jnp.zeros_like(o_ref)

        def fetch_next_rows():
            wrap = (f == nf - 1).astype(jnp.int32)
            chunk = (f + 1) * (1 - wrap)
            target = jnp.minimum(i + 1 + wrap, n_used - 1)
            target_slot = (i + 1 + wrap) % 2
            for j in range(rows_per_step):
                fetch(target, chunk * rows_per_step + j, target_slot).start()

        half = half_ref[i] == 1
        hm = tm // 2

        @pl.when(jnp.logical_not(half))
        def _():
            fetch_next_rows()
            _swiglu_accumulate(xb_ref[...], wg_ref, wu_ref, (wd0_ref, wd1_ref), o_ref)

        @pl.when(half)
        def _():
            fetch_next_rows()
            _swiglu_accumulate(xb_ref[0:hm, :], wg_ref, wu_ref, (wd0_ref, wd1_ref), o_ref.at[pl.ds(0, hm)])

        @pl.when((i == n_used - 1) & (f == nf - 1))
        def _():
            wait_rows(1 - slot, n_fetch)
            wait_rows(slot, rows_per_step)


def moe_swiglu(h, slot_tok, block_e, n_used, block_half, w_gate, w_up, w_down_lo, w_down_hi, tm, tf=1024):
    n_slots = slot_tok.shape[0]
    D = h.shape[1]
    F = w_gate.shape[2]
    while F % tf:
        tf //= 2
    nf = F // tf
    xg_rows = _moe_rows_per_step(tm, nf) * nf
    blk = lambda i, nu: jnp.minimum(i, nu[0] - 1)
    chunk = lambda i, f, nu: jnp.where(i < nu[0], f, nf - 1)
    up_spec = pl.BlockSpec((1, D, tf), lambda i, f, tok, be, nu, hf: (be[blk(i, nu)], 0, chunk(i, f, nu)))
    down_spec = pl.BlockSpec((1, tf, D // 2),
                             lambda i, f, tok, be, nu, hf: (be[blk(i, nu)], chunk(i, f, nu), 0))
    return pl.pallas_call(
        functools.partial(_moe_swiglu_kernel, tm=tm, nf=nf),
        grid_spec=pltpu.PrefetchScalarGridSpec(
            num_scalar_prefetch=4,
            grid=(n_slots // tm, nf),
            in_specs=[pl.BlockSpec(memory_space=pl.ANY), up_spec, up_spec, down_spec, down_spec],
            out_specs=pl.BlockSpec((tm, D), lambda i, f, tok, be, nu, hf: (i, 0)),
            scratch_shapes=[pltpu.VMEM((2, xg_rows, D), F32),
                            pltpu.VMEM((tm, D), BF16),
                            pltpu.SemaphoreType.DMA((2,))]),
        out_shape=jax.ShapeDtypeStruct((n_slots, D), F32),
        compiler_params=_cp("arbitrary", "arbitrary"),
        name="moe_swiglu",
    )(slot_tok, block_e, n_used, block_half, h, w_gate, w_up, w_down_lo, w_down_hi)


def _combine_kernel(dest_ref, y_ref, gate_ref, ys_hbm, o_ref, buf_ref, sem, *, tb):
    i = pl.program_id(0)

    def fetch(step, slot):
        base = step * tb

        def start(r, c):
            for k in range(TOP_K):
                _row_copy(ys_hbm, buf_ref.at[slot, k], dest_ref[TOP_K * (base + r) + k], r, sem.at[slot]).start()
            return c

        lax.fori_loop(0, tb, start, 0, unroll=8)

    @pl.when(i == 0)
    def _():
        fetch(0, 0)

    @pl.when(i + 1 < pl.num_programs(0))
    def _():
        fetch(i + 1, (i + 1) % 2)

    slot = i % 2
    pltpu.make_async_copy(buf_ref.at[slot], buf_ref.at[slot], sem.at[slot]).wait()
    g = gate_ref[...]
    o_ref[...] = y_ref[...] + g[:, 0:1] * buf_ref[slot, 0] + g[:, 1:2] * buf_ref[slot, 1]


def combine_rows(y, gates, ys, dest, tb=256):
    T, D = y.shape
    return pl.pallas_call(
        functools.partial(_combine_kernel, tb=tb),
        grid_spec=pltpu.PrefetchScalarGridSpec(
            num_scalar_prefetch=1,
            grid=(T // tb,),
            in_specs=[pl.BlockSpec((tb, D), lambda i, d: (i, 0)),
                      pl.BlockSpec((tb, LANES), lambda i, d: (i, 0)),
                      pl.BlockSpec(memory_space=pl.ANY)],
            out_specs=pl.BlockSpec((tb, D), lambda i, d: (i, 0)),
            scratch_shapes=[pltpu.VMEM((2, TOP_K, tb, D), F32),
                            pltpu.SemaphoreType.DMA((2,))]),
        out_shape=jax.ShapeDtypeStruct((T, D), F32),
        compiler_params=_cp("arbitrary"),
        name="moe_combine",
    )(dest, y, gates, ys)


def _routing_tables(top_idx, T, block):
    A = T * TOP_K
    E = N_EXPERTS
    flat_e = top_idx.reshape(A)
    onehot = (flat_e[:, None] == jnp.arange(E, dtype=jnp.int32)[None, :]).astype(jnp.int32)
    csum = jnp.cumsum(onehot, axis=0)
    counts = csum[-1]
    rank = jnp.sum((csum - 1) * onehot, axis=1)
    padded = (counts + block - 1) // block * block
    pends = jnp.cumsum(padded)
    pstarts = pends - padded
    starts = jnp.cumsum(counts) - counts
    dest = (pstarts[flat_e] + rank).astype(jnp.int32)
    n_blocks = A // block + E
    n_slots = n_blocks * block
    order = jnp.argsort(flat_e, stable=True).astype(jnp.int32)
    slot = jnp.arange(n_slots, dtype=jnp.int32)
    slot_e = jnp.clip(jnp.searchsorted(pends, slot, side='right'), 0, E - 1).astype(jnp.int32)
    r = slot - pstarts[slot_e]
    valid = r < counts[slot_e]
    src = jnp.clip(starts[slot_e] + r, 0, A - 1)
    slot_tok = jnp.where(valid, order[src] // TOP_K, 0).astype(jnp.int32)
    block_e = slot_e[::block]
    n_used = (pends[-1:] // block).astype(jnp.int32)
    in_block = jnp.clip(counts[block_e] - (slot[::block] - pstarts[block_e]), 0, block)
    block_half = (in_block <= block // 2).astype(jnp.int32)
    return slot_tok, dest, block_e, n_used, block_half


def _token_mixer(h, mem2d, tab, B, S, M, l, casts, w_in, mla_q_a_norm_g, mla_w_uq, mla_kv_a_norm_g, mla_w_ukv,
                 mla_q_norm_g, mla_k_norm_g, mla_w_out, pool_w, pool_scale, pool_w_out,
                 mem_norm_g, mem_w_kv, mem_q_norm_g, mem_k_norm_g, mem_w_out):
    za = matmul(h, w_in, l, ZA_WIDTH, F32, tn=ZA_WIDTH // 2)

    head_pad = lambda v: jnp.concatenate([v[..., :QK_NOPE_DIM], _rope_lanes(v[..., QK_NOPE_DIM:])], axis=-1)
    wuq = head_pad(mla_w_uq[l].reshape(Q_LORA_RANK, MLA_HEADS, QK_HEAD_DIM)).reshape(
        Q_LORA_RANK, MLA_HEADS * HEAD_PAD)
    wukv = mla_w_ukv[l].reshape(KV_LORA_RANK, MLA_HEADS, QK_NOPE_DIM + V_HEAD_DIM)
    wuk = wukv[:, :, :QK_NOPE_DIM].reshape(KV_LORA_RANK, MLA_WIDTH)
    wuv = wukv[:, :, QK_NOPE_DIM:].reshape(KV_LORA_RANK, MLA_WIDTH)
    pad_g = lambda g: head_pad(g).reshape(1, HEAD_PAD)
    q, k, v = mla_prep(za, tab, wuq.astype(BF16), wuk.astype(BF16), wuv.astype(BF16),
                       mla_q_a_norm_g[l].reshape(1, -1), mla_kv_a_norm_g[l].reshape(1, -1),
                       pad_g(mla_q_norm_g[l]), pad_g(mla_k_norm_g[l]))
    attn, *cast_out = causal_attention(q, k, v, B, S, casts)

    mixed = pool_mixer(za, pool_w[l].astype(BF16), pool_scale[l].reshape(1, -1), B, S)

    k_m, v_m = mem_kv(mem2d, mem_norm_g[l].reshape(1, -1), mem_w_kv[l].astype(BF16),
                      mem_k_norm_g[l].reshape(1, -1), B, M)
    o_mem = mem_attention(za, k_m, v_m, mem_q_norm_g[l].reshape(1, -1), B, S, M)

    merged = merge_branches(h, attn, mixed, o_mem, w_in, mla_w_out, pool_w_out, mem_w_out, l)
    return merged, cast_out


def kernel(x, mem, positions, attn_norm_g, w_in, mla_q_a_norm_g, mla_w_uq, mla_kv_a_norm_g, mla_w_ukv, mla_q_norm_g, mla_k_norm_g, mla_w_out, pool_w, pool_scale, pool_w_out, mem_norm_g, mem_w_kv, mem_q_norm_g, mem_k_norm_g, mem_w_out, w_o, ffn_norm_g, dense_w_gate, dense_w_up, dense_w_down, router_w, router_b, moe_w_gate, moe_w_up, moe_w_down):
    B, S, D = x.shape
    M = mem.shape[1]
    T = B * S
    depth = attn_norm_g.shape[0]
    moe_block = 512
    y = x.reshape(T, D)
    mem2d = mem.reshape(B * M, D)
    tab = rope_tables(positions)
    w_in = prep_w_in(w_in)
    mla_w_out, pool_w_out, mem_w_out, w_o = (w.astype(BF16) for w in (mla_w_out, pool_w_out, mem_w_out, w_o))
    moe_bf16 = {}
    for l in range(depth):
        h = rmsnorm(y, attn_norm_g[l], BF16)
        i = l // 2
        casts = ()
        if l % 2 == 0 and l + 1 < depth:
            casts = ((moe_w_gate[i], 0, 1), (moe_w_down[i], 0, 2))
        elif l % 2 == 1:
            casts = ((moe_w_up[i], 0, 1), (moe_w_down[i], 1, 2))
        merged, cast_out = _token_mixer(h, mem2d, tab, B, S, M, l, casts, w_in, mla_q_a_norm_g, mla_w_uq,
                                        mla_kv_a_norm_g, mla_w_ukv, mla_q_norm_g, mla_k_norm_g, mla_w_out,
                                        pool_w, pool_scale, pool_w_out, mem_norm_g, mem_w_kv, mem_q_norm_g,
                                        mem_k_norm_g, mem_w_out)
        if casts:
            names = ("gate", "down_lo") if l % 2 == 0 else ("up", "down_hi")
            moe_bf16.update(zip(names, cast_out))
        if l % 2 == 0:
            y, h = matmul_residual_norm(merged, w_o, l, y, ffn_norm_g[l])
            y = dense_swiglu(h, dense_w_gate[i:i + 1].astype(BF16), dense_w_up[i:i + 1].astype(BF16),
                             dense_w_down[i:i + 1].astype(BF16), y)
        else:
            y = matmul(merged, w_o, l, D, F32, residual=y, tm=512, tn=D)
            hf, idx, gates = norm_router(y, ffn_norm_g[l], router_w[i], router_b[i])
            slot_tok, dest, block_e, n_used, block_half = _routing_tables(idx[:, :TOP_K], T, moe_block)
            ys = moe_swiglu(hf, slot_tok, block_e, n_used, block_half, moe_bf16.pop("gate"), moe_bf16.pop("up"),
                            moe_bf16.pop("down_lo"), moe_bf16.pop("down_hi"), tm=moe_block)
            y = combine_rows(y, gates, ys, dest)
    return y.reshape(B, S, D)
```

```python
import functools

import numpy as np
import jax
import jax.numpy as jnp
from jax import lax
from jax.experimental import pallas as pl
from jax.experimental.pallas import tpu as pltpu

F32 = jnp.float32
BF16 = jnp.bfloat16

MEM_HEADS = 4
MEM_HEAD_DIM = 256
MEM_WIDTH = MEM_HEADS * MEM_HEAD_DIM
MLA_HEADS = 8
Q_LORA_RANK = 512
KV_LORA_RANK = 256
QK_NOPE_DIM = 128
QK_ROPE_DIM = 64
QK_HEAD_DIM = QK_NOPE_DIM + QK_ROPE_DIM
V_HEAD_DIM = 128
MLA_WIDTH = MLA_HEADS * V_HEAD_DIM
ROPE_THETA = 10000.0
POOL_WINDOWS = (2, 4, 8, 16)
POOL_GROUPS = 4
POOL_GROUP_DIM = 256
POOL_WIDTH = POOL_GROUPS * POOL_GROUP_DIM
N_BRANCHES = 3
N_EXPERTS = 8
TOP_K = 2
NORM_EPS = 1e-6
LOG2_E = 1.4426950408889634

LANES = 128
SUBLANES = 8
HEAD_PAD = 2 * LANES
ROPE_HALF = QK_ROPE_DIM // 2
POOL_HALO = 16

ZA_POOL = 0
ZA_QMEM = POOL_WIDTH
ZA_CQ = ZA_QMEM + MEM_WIDTH
ZA_CKV = ZA_CQ + Q_LORA_RANK
ZA_KPE = ZA_CKV + KV_LORA_RANK
ZA_WIDTH = 3072

MOE_ROW_PARTS = 4
VMEM_LIMIT = 52 * 1024 * 1024


def _cp(*sem):
    return pltpu.CompilerParams(dimension_semantics=sem, vmem_limit_bytes=VMEM_LIMIT)


def _rms(x, eps_dim):
    return lax.rsqrt(jnp.sum(x * x, axis=-1, keepdims=True) * (1.0 / eps_dim) + NORM_EPS)


def _rmsnorm_kernel(x_ref, g_ref, o_ref):
    x = x_ref[...]
    o_ref[...] = (x * _rms(x, x.shape[-1]) * g_ref[...]).astype(o_ref.dtype)


def rmsnorm(x, g, out_dtype, tm=512):
    T, D = x.shape
    return pl.pallas_call(
        _rmsnorm_kernel,
        grid=(T // tm,),
        in_specs=[pl.BlockSpec((tm, D), lambda i: (i, 0)),
                  pl.BlockSpec((1, D), lambda i: (0, 0))],
        out_specs=pl.BlockSpec((tm, D), lambda i: (i, 0)),
        out_shape=jax.ShapeDtypeStruct((T, D), out_dtype),
        compiler_params=_cp("parallel"),
        name="rmsnorm",
    )(x, g.reshape(1, D))


def _norm_router_kernel(x_ref, g_ref, rwh_ref, rwl_ref, rb_ref, h_ref, idx_ref, gate_ref):
    x = x_ref[...]
    h = x * _rms(x, x.shape[-1]) * g_ref[...]
    h_ref[...] = h
    h_hi = h.astype(BF16)
    h_lo = (h - h_hi.astype(F32)).astype(BF16)
    logits = (jnp.dot(h_hi, rwh_ref[...], preferred_element_type=F32)
              + jnp.dot(h_lo, rwh_ref[...], preferred_element_type=F32)
              + jnp.dot(h_hi, rwl_ref[...], preferred_element_type=F32)) + rb_ref[...]
    lane = lax.broadcasted_iota(jnp.int32, logits.shape, 1)
    neg = -jnp.inf
    l1 = jnp.where(lane < N_EXPERTS, logits, neg)
    m1 = jnp.max(l1, axis=-1, keepdims=True)
    i1 = jnp.min(jnp.where(l1 == m1, lane, LANES), axis=-1, keepdims=True)
    l2 = jnp.where(lane == i1, neg, l1)
    m2 = jnp.max(l2, axis=-1, keepdims=True)
    i2 = jnp.min(jnp.where(l2 == m2, lane, LANES), axis=-1, keepdims=True)
    e = jnp.exp(m2 - m1)
    g1 = 1.0 / (1.0 + e)
    g2 = e / (1.0 + e)
    idx_ref[...] = jnp.where(lane == 0, i1, jnp.where(lane == 1, i2, 0))
    gate_ref[...] = jnp.where(lane == 0, g1, jnp.where(lane == 1, g2, 0.0))


def norm_router(x, g, router_w, router_b, tm=512):
    T, D = x.shape
    E = router_w.shape[1]
    rw = jnp.zeros((D, LANES), F32).at[:, :E].set(router_w)
    rw_hi = rw.astype(BF16)
    rw_lo = (rw - rw_hi.astype(F32)).astype(BF16)
    rb = jnp.zeros((1, LANES), F32).at[0, :E].set(router_b)
    return pl.pallas_call(
        _norm_router_kernel,
        grid=(T // tm,),
        in_specs=[pl.BlockSpec((tm, D), lambda i: (i, 0)),
                  pl.BlockSpec((1, D), lambda i: (0, 0)),
                  pl.BlockSpec((D, LANES), lambda i: (0, 0)),
                  pl.BlockSpec((D, LANES), lambda i: (0, 0)),
                  pl.BlockSpec((1, LANES), lambda i: (0, 0))],
        out_specs=[pl.BlockSpec((tm, D), lambda i: (i, 0)),
                   pl.BlockSpec((tm, LANES), lambda i: (i, 0)),
                   pl.BlockSpec((tm, LANES), lambda i: (i, 0))],
        out_shape=[jax.ShapeDtypeStruct((T, D), F32),
                   jax.ShapeDtypeStruct((T, LANES), jnp.int32),
                   jax.ShapeDtypeStruct((T, LANES), F32)],
        compiler_params=_cp("parallel"),
        name="norm_router",
    )(x, g.reshape(1, D), rw_hi, rw_lo, rb)


def _mm_kernel(a_ref, w_ref, o_ref):
    o_ref[...] = jnp.dot(a_ref[...], w_ref[0], preferred_element_type=F32).astype(o_ref.dtype)


def _mm_res_kernel(a_ref, w_ref, r_ref, o_ref):
    o_ref[...] = (r_ref[...] + jnp.dot(a_ref[...], w_ref[0], preferred_element_type=F32)).astype(o_ref.dtype)


def matmul(a, w, l, N, out_dtype, residual=None, tm=1024, tn=1024):
    M, K = a.shape
    tm, tn = min(tm, M), min(tn, N)
    in_specs = [pl.BlockSpec((tm, K), lambda i, j: (i, 0)),
                pl.BlockSpec((1, K, tn), lambda i, j: (l, 0, j))]
    args = [a, w]
    body = _mm_kernel
    if residual is not None:
        in_specs.append(pl.BlockSpec((tm, tn), lambda i, j: (i, j)))
        args.append(residual)
        body = _mm_res_kernel
    return pl.pallas_call(
        body,
        grid=(M // tm, N // tn),
        in_specs=in_specs,
        out_specs=pl.BlockSpec((tm, tn), lambda i, j: (i, j)),
        out_shape=jax.ShapeDtypeStruct((M, N), out_dtype),
        compiler_params=_cp("parallel", "arbitrary"),
        name="matmul",
    )(*args)


def _mm_res_norm_kernel(a_ref, w_ref, r_ref, g_ref, o_ref, h_ref):
    y = r_ref[...] + jnp.dot(a_ref[...], w_ref[0], preferred_element_type=F32)
    o_ref[...] = y
    h_ref[...] = (y * _rms(y, y.shape[-1]) * g_ref[...]).astype(h_ref.dtype)


def matmul_residual_norm(a, w, l, residual, g, tm=512):
    M, K = a.shape
    N = w.shape[2]
    rows = lambda width: pl.BlockSpec((tm, width), lambda i: (i, 0))
    return pl.pallas_call(
        _mm_res_norm_kernel,
        grid=(M // tm,),
        in_specs=[rows(K), pl.BlockSpec((1, K, N), lambda i: (l, 0, 0)), rows(N),
                  pl.BlockSpec((1, N), lambda i: (0, 0))],
        out_specs=[rows(N), rows(N)],
        out_shape=[jax.ShapeDtypeStruct((M, N), F32), jax.ShapeDtypeStruct((M, N), BF16)],
        compiler_params=_cp("parallel"),
        name="matmul_residual_norm",
    )(a, w, residual, g.reshape(1, N))


def _rope_lanes(v):
    z = jnp.zeros(v.shape[:-1] + (LANES // 2 - ROPE_HALF,), v.dtype)
    return jnp.concatenate([v[..., :ROPE_HALF], z, v[..., ROPE_HALF:], z], axis=-1)


def _rope_table_kernel(pos_ref, invf_ref, o_ref):
    ang = pos_ref[...].astype(F32) * invf_ref[...]
    lane = lax.broadcasted_iota(jnp.int32, ang.shape, 1)
    first = lane < ROPE_HALF
    second = (lane >= LANES // 2) & (lane < LANES // 2 + ROPE_HALF)
    c = jnp.cos(ang)
    s = jnp.sin(ang)
    o_ref[:, 0:LANES] = jnp.where(first | second, c, 0.0)
    o_ref[:, LANES:2 * LANES] = jnp.where(first, -s, jnp.where(second, s, 0.0))


def rope_tables(positions, tm=512):
    T = positions.size
    inv_freq = ROPE_THETA ** (-np.arange(0, QK_ROPE_DIM, 2, dtype=np.float32) / QK_ROPE_DIM)
    invf = np.zeros((1, LANES), np.float32)
    invf[0, :ROPE_HALF] = inv_freq
    invf[0, LANES // 2:LANES // 2 + ROPE_HALF] = inv_freq
    return pl.pallas_call(
        _rope_table_kernel,
        grid=(T // tm,),
        in_specs=[pl.BlockSpec((tm, 1), lambda i: (i, 0)),
                  pl.BlockSpec((1, LANES), lambda i: (0, 0))],
        out_specs=pl.BlockSpec((tm, 2 * LANES), lambda i: (i, 0)),
        out_shape=jax.ShapeDtypeStruct((T, 2 * LANES), F32),
        compiler_params=_cp("parallel"),
        name="rope_tables",
    )(positions.reshape(T, 1), jnp.asarray(invf))


def _rope(x, tab):
    return x * tab[:, 0:LANES] + pltpu.roll(x, LANES // 2, 1) * tab[:, LANES:2 * LANES]


def _mla_prep_kernel(cq_ref, ckv_ref, kpe_ref, tab_ref, wuq_ref, wuk_ref, wuv_ref,
                     gqa_ref, gkva_ref, gq_ref, gk_ref, q_ref, k_ref, v_ref):
    tab = tab_ref[...]
    cq = cq_ref[...]
    cqn = (cq * _rms(cq, Q_LORA_RANK) * gqa_ref[...]).astype(BF16)
    qf = jnp.dot(cqn, wuq_ref[...], preferred_element_type=F32)
    ckv = ckv_ref[...]
    ckvn = (ckv * _rms(ckv, KV_LORA_RANK) * gkva_ref[...]).astype(BF16)
    kn = jnp.dot(ckvn, wuk_ref[...], preferred_element_type=F32)
    v_ref[...] = jnp.dot(ckvn, wuv_ref[...], preferred_element_type=F32).astype(v_ref.dtype)
    gq = gq_ref[...]
    gk = gk_ref[...]
    kpe = kpe_ref[...]
    ss_pe = jnp.sum(kpe * kpe, axis=-1, keepdims=True)
    kpe_rot = _rope(kpe * gk[:, LANES:], tab)
    scale = QK_HEAD_DIM ** -0.5 * LOG2_E
    for h in range(MLA_HEADS):
        q0 = qf[:, h * HEAD_PAD:h * HEAD_PAD + LANES]
        q1 = qf[:, h * HEAD_PAD + LANES:(h + 1) * HEAD_PAD]
        ss = jnp.sum(q0 * q0, axis=-1, keepdims=True) + jnp.sum(q1 * q1, axis=-1, keepdims=True)
        rq = lax.rsqrt(ss * (1.0 / QK_HEAD_DIM) + NORM_EPS) * scale
        q_ref[:, h * HEAD_PAD:h * HEAD_PAD + LANES] = (q0 * gq[:, :LANES] * rq).astype(q_ref.dtype)
        q_ref[:, h * HEAD_PAD + LANES:(h + 1) * HEAD_PAD] = (_rope(q1 * gq[:, LANES:], tab) * rq).astype(q_ref.dtype)
        k0 = kn[:, h * LANES:(h + 1) * LANES]
        ssk = jnp.sum(k0 * k0, axis=-1, keepdims=True) + ss_pe
        rk = lax.rsqrt(ssk * (1.0 / QK_HEAD_DIM) + NORM_EPS)
        k_ref[:, h * HEAD_PAD:h * HEAD_PAD + LANES] = (k0 * gk[:, :LANES] * rk).astype(k_ref.dtype)
        k_ref[:, h * HEAD_PAD + LANES:(h + 1) * HEAD_PAD] = (kpe_rot * rk).astype(k_ref.dtype)


def mla_prep(za, tab, wuq, wuk, wuv, gqa, gkva, gq, gk, tm=512):
    T = za.shape[0]
    HP = MLA_HEADS * HEAD_PAD
    full = lambda shape: pl.BlockSpec(shape, lambda i: (0, 0))
    return pl.pallas_call(
        _mla_prep_kernel,
        grid=(T // tm,),
        in_specs=[pl.BlockSpec((tm, Q_LORA_RANK), lambda i: (i, ZA_CQ // Q_LORA_RANK)),
                  pl.BlockSpec((tm, KV_LORA_RANK), lambda i: (i, ZA_CKV // KV_LORA_RANK)),
                  pl.BlockSpec((tm, LANES), lambda i: (i, ZA_KPE // LANES)),
                  pl.BlockSpec((tm, 2 * LANES), lambda i: (i, 0)),
                  full((Q_LORA_RANK, HP)), full((KV_LORA_RANK, MLA_WIDTH)), full((KV_LORA_RANK, MLA_WIDTH)),
                  full((1, Q_LORA_RANK)), full((1, KV_LORA_RANK)), full((1, HEAD_PAD)), full((1, HEAD_PAD))],
        out_specs=[pl.BlockSpec((tm, HP), lambda i: (i, 0)),
                   pl.BlockSpec((tm, HP), lambda i: (i, 0)),
                   pl.BlockSpec((tm, MLA_WIDTH), lambda i: (i, 0))],
        out_shape=[jax.ShapeDtypeStruct((T, HP), BF16),
                   jax.ShapeDtypeStruct((T, HP), BF16),
                   jax.ShapeDtypeStruct((T, MLA_WIDTH), BF16)],
        compiler_params=_cp("parallel"),
        name="mla_prep",
    )(za, za, za, tab, wuq, wuk, wuv, gqa, gkva, gq, gk)


def _causal_attn_kernel(q_ref, k_ref, v_ref, *rest, tq, n_cast):
    cast_src = rest[:n_cast]
    o_ref = rest[n_cast]
    cast_dst = rest[n_cast + 1:]
    for s_ref, d_ref in zip(cast_src, cast_dst):
        d_ref[...] = s_ref[...].astype(d_ref.dtype)
    S = q_ref.shape[0]
    row = lax.broadcasted_iota(jnp.int32, (tq, tq), 0)
    col = lax.broadcasted_iota(jnp.int32, (tq, tq), 1)
    tri = col <= row
    for qi in range(S // tq):
        kl = (qi + 1) * tq
        q = q_ref[qi * tq:kl, :]
        s = lax.dot_general(q, k_ref[0:kl, :], (((1,), (1,)), ((), ())), preferred_element_type=F32)
        s_diag = jnp.where(tri, s[:, kl - tq:kl], -jnp.inf)
        if qi > 0:
            s = jnp.concatenate([s[:, :kl - tq], s_diag], axis=1)
        else:
            s = s_diag
        m = jnp.max(s, axis=-1, keepdims=True)
        p = jnp.exp2(s - m)
        l = jnp.sum(p, axis=-1, keepdims=True)
        o = jnp.dot(p.astype(BF16), v_ref[0:kl, :], preferred_element_type=F32)
        o_ref[qi * tq:kl, :] = (o / l).astype(o_ref.dtype)


def causal_attention(q, k, v, B, S, casts=(), tq=256):
    T = q.shape[0]
    n_steps = B * MLA_HEADS
    in_specs = [pl.BlockSpec((S, HEAD_PAD), lambda b, h: (b, h)),
                pl.BlockSpec((S, HEAD_PAD), lambda b, h: (b, h)),
                pl.BlockSpec((S, V_HEAD_DIM), lambda b, h: (b, h))]
    out_specs = [pl.BlockSpec((S, V_HEAD_DIM), lambda b, h: (b, h))]
    out_shape = [jax.ShapeDtypeStruct((T, MLA_WIDTH), BF16)]
    for src, part, n_parts in casts:
        E, rows, cols = src.shape
        spe = n_steps // E
        rb, cb = rows // spe, cols // n_parts
        assert spe * E == n_steps and rb * spe == rows and rb % 16 == 0
        assert cb * n_parts == cols and cb % LANES == 0
        step = lambda b, h: b * MLA_HEADS + h
        in_specs.append(pl.BlockSpec((1, rb, cb), lambda b, h, part=part, spe=spe:
                                     (step(b, h) // spe, step(b, h) % spe, part)))
        out_specs.append(pl.BlockSpec((1, rb, cb), lambda b, h, spe=spe:
                                      (step(b, h) // spe, step(b, h) % spe, 0)))
        out_shape.append(jax.ShapeDtypeStruct((E, rows, cb), BF16))
    return pl.pallas_call(
        functools.partial(_causal_attn_kernel, tq=tq, n_cast=len(casts)),
        grid=(B, MLA_HEADS),
        in_specs=in_specs,
        out_specs=out_specs,
        out_shape=out_shape,
        compiler_params=_cp("parallel", "parallel"),
        name="causal_attention",
    )(q, k, v, *(c[0] for c in casts))


def _pool_kernel(u_ref, w_ref, sc_ref, o_ref, pad_ref, *, rc):
    S = u_ref.shape[0]
    C = POOL_GROUP_DIM
    t1 = lax.broadcasted_iota(jnp.int32, (rc, C), 0).astype(F32) + 1.0
    for g, win in enumerate(POOL_WINDOWS):
        pad_ref[0:POOL_HALO, :] = jnp.zeros((POOL_HALO, C), F32)
        pad_ref[POOL_HALO:POOL_HALO + S, :] = u_ref[:, g * C:(g + 1) * C]
        for r0 in range(0, S, rc):
            acc = pad_ref[POOL_HALO + r0:POOL_HALO + r0 + rc, :]
            tok = acc
            for kk in range(1, win):
                acc = acc + pad_ref[POOL_HALO + r0 - kk:POOL_HALO + r0 - kk + rc, :]
            cnt = jnp.minimum(t1 + float(r0), float(win))
            pooled = (acc / cnt - tok).astype(BF16)
            mixed = jnp.dot(pooled, w_ref[g], preferred_element_type=F32) * sc_ref[:, g * C:(g + 1) * C]
            o_ref[r0:r0 + rc, g * C:(g + 1) * C] = mixed.astype(o_ref.dtype)


def pool_mixer(za, pool_w, pool_scale, B, S, rc=512):
    T = za.shape[0]
    rc = min(rc, S)
    return pl.pallas_call(
        functools.partial(_pool_kernel, rc=rc),
        grid=(B,),
        in_specs=[pl.BlockSpec((S, POOL_WIDTH), lambda b: (b, ZA_POOL // POOL_WIDTH)),
                  pl.BlockSpec((POOL_GROUPS, POOL_GROUP_DIM, POOL_GROUP_DIM), lambda b: (0, 0, 0)),
                  pl.BlockSpec((1, POOL_WIDTH), lambda b: (0, 0))],
        out_specs=pl.BlockSpec((S, POOL_WIDTH), lambda b: (b, 0)),
        out_shape=jax.ShapeDtypeStruct((T, POOL_WIDTH), BF16),
        scratch_shapes=[pltpu.VMEM((POOL_HALO + S, POOL_GROUP_DIM), F32)],
        compiler_params=_cp("parallel"),
        name="pool_mixer",
    )(za, pool_w, pool_scale)


def _mem_kv_kernel(m_ref, g_ref, w_ref, gk_ref, k_ref, v_ref):
    x = m_ref[...]
    xn = (x * _rms(x, x.shape[-1]) * g_ref[...]).astype(BF16)
    kv = jnp.dot(xn, w_ref[...], preferred_element_type=F32)
    for h in range(MEM_HEADS):
        kh = kv[:, h * MEM_HEAD_DIM:(h + 1) * MEM_HEAD_DIM]
        k_ref[:, h * MEM_HEAD_DIM:(h + 1) * MEM_HEAD_DIM] = (
            kh * _rms(kh, MEM_HEAD_DIM) * gk_ref[...]).astype(k_ref.dtype)
    v_ref[...] = kv[:, MEM_WIDTH:].astype(v_ref.dtype)


def mem_kv(mem2d, g, w_kv, gk, B, M):
    D = mem2d.shape[1]
    return pl.pallas_call(
        _mem_kv_kernel,
        grid=(B,),
        in_specs=[pl.BlockSpec((M, D), lambda b: (b, 0)),
                  pl.BlockSpec((1, D), lambda b: (0, 0)),
                  pl.BlockSpec((D, 2 * MEM_WIDTH), lambda b: (0, 0)),
                  pl.BlockSpec((1, MEM_HEAD_DIM), lambda b: (0, 0))],
        out_specs=[pl.BlockSpec((M, MEM_WIDTH), lambda b: (b, 0)),
                   pl.BlockSpec((M, MEM_WIDTH), lambda b: (b, 0))],
        out_shape=[jax.ShapeDtypeStruct((B * M, MEM_WIDTH), BF16),
                   jax.ShapeDtypeStruct((B * M, MEM_WIDTH), BF16)],
        compiler_params=_cp("parallel"),
        name="mem_kv",
    )(mem2d, g, w_kv, gk)


def _mem_attn_kernel(q_ref, k_ref, v_ref, gq_ref, o_ref):
    scale = MEM_HEAD_DIM ** -0.5 * LOG2_E
    for h in range(MEM_HEADS):
        sl = slice(h * MEM_HEAD_DIM, (h + 1) * MEM_HEAD_DIM)
        qh = q_ref[:, sl]
        qn = (qh * (_rms(qh, MEM_HEAD_DIM) * scale) * gq_ref[...]).astype(BF16)
        s = lax.dot_general(qn, k_ref[:, sl], (((1,), (1,)), ((), ())), preferred_element_type=F32)
        m = jnp.max(s, axis=-1, keepdims=True)
        p = jnp.exp2(s - m)
        l = jnp.sum(p, axis=-1, keepdims=True)
        o = jnp.dot(p.astype(BF16), v_ref[:, sl], preferred_element_type=F32)
        o_ref[:, sl] = (o / l).astype(o_ref.dtype)


def mem_attention(za, k_m, v_m, gq, B, S, M, tq=512):
    T = za.shape[0]
    tq = min(tq, S)
    nq = S // tq
    return pl.pallas_call(
        _mem_attn_kernel,
        grid=(B, nq),
        in_specs=[pl.BlockSpec((tq, MEM_WIDTH), lambda b, i: (b * nq + i, ZA_QMEM // MEM_WIDTH)),
                  pl.BlockSpec((M, MEM_WIDTH), lambda b, i: (b, 0)),
                  pl.BlockSpec((M, MEM_WIDTH), lambda b, i: (b, 0)),
                  pl.BlockSpec((1, MEM_HEAD_DIM), lambda b, i: (0, 0))],
        out_specs=pl.BlockSpec((tq, MEM_WIDTH), lambda b, i: (b * nq + i, 0)),
        out_shape=jax.ShapeDtypeStruct((T, MEM_WIDTH), BF16),
        compiler_params=_cp("parallel", "arbitrary"),
        name="mem_attention",
    )(za, k_m, v_m, gq)


def _merge_kernel(h_ref, a_ref, p_ref, m_ref, g0_ref, g1_ref, g2_ref, w0_ref, w1_ref, w2_ref, o_ref):
    h = h_ref[...]
    acc = None
    for x_ref, g_ref, w_ref in ((a_ref, g0_ref, w0_ref), (p_ref, g1_ref, w1_ref), (m_ref, g2_ref, w2_ref)):
        logit = jnp.dot(h, g_ref[0], preferred_element_type=F32)
        gate = 1.0 / (1.0 + jnp.exp(-logit))
        br = jnp.dot(x_ref[...], w_ref[0], preferred_element_type=F32)
        acc = gate * br if acc is None else acc + gate * br
    o_ref[...] = acc.astype(o_ref.dtype)


def merge_branches(h, a, p, m, w_all, w_a, w_p, w_m, l, tm=1024, tn=512):
    T, D = h.shape
    W = a.shape[1]
    nj = D // tn
    g0 = ZA_WIDTH // tn
    row = lambda width: pl.BlockSpec((tm, width), lambda i, j: (i, 0))
    gate_spec = lambda b: pl.BlockSpec((1, D, tn), lambda i, j, b=b: (l, 0, g0 + b * nj + j))
    out_w = pl.BlockSpec((1, W, tn), lambda i, j: (l, 0, j))
    return pl.pallas_call(
        _merge_kernel,
        grid=(T // tm, nj),
        in_specs=[row(D), row(W), row(W), row(W),
                  gate_spec(0), gate_spec(1), gate_spec(2), out_w, out_w, out_w],
        out_specs=pl.BlockSpec((tm, tn), lambda i, j: (i, j)),
        out_shape=jax.ShapeDtypeStruct((T, D), BF16),
        compiler_params=_cp("parallel", "arbitrary"),
        name="merge_branches",
    )(h, a, p, m, w_all, w_all, w_all, w_a, w_p, w_m)


def _w_in_prep_kernel(w_ref, o_ref):
    s1 = Q_LORA_RANK + KV_LORA_RANK + QK_ROPE_DIM
    s2 = s1 + POOL_WIDTH
    s3 = s2 + MEM_WIDTH
    tk = w_ref.shape[2]
    s1_up = -(-s1 // LANES) * LANES
    o_ref[0, :, ZA_POOL:ZA_POOL + POOL_WIDTH] = w_ref[0, s1:s2, :].T.astype(BF16)
    o_ref[0, :, ZA_QMEM:ZA_QMEM + MEM_WIDTH] = w_ref[0, s2:s3, :].T.astype(BF16)
    low = w_ref[0, 0:s1_up, :].T.astype(BF16)
    n_c = Q_LORA_RANK + KV_LORA_RANK
    o_ref[0, :, ZA_CQ:ZA_CQ + n_c] = low[:, 0:n_c]
    o_ref[0, :, ZA_KPE:ZA_WIDTH] = jnp.zeros((tk, ZA_WIDTH - ZA_KPE), BF16)
    o_ref[0, :, ZA_KPE:ZA_KPE + ROPE_HALF] = low[:, n_c:n_c + ROPE_HALF]
    o_ref[0, :, ZA_KPE + LANES // 2:ZA_KPE + LANES // 2 + ROPE_HALF] = low[:, n_c + ROPE_HALF:s1]
    o_ref[0, :, ZA_WIDTH:] = w_ref[0, s3:, :].T.astype(BF16)


def prep_w_in(w_in, tk=256):
    L, D, win = w_in.shape
    wout = ZA_WIDTH + N_BRANCHES * D
    return pl.pallas_call(
        _w_in_prep_kernel,
        grid=(L, D // tk),
        in_specs=[pl.BlockSpec((1, win, tk), lambda l, i: (l, 0, i))],
        out_specs=pl.BlockSpec((1, tk, wout), lambda l, i: (l, i, 0)),
        out_shape=jax.ShapeDtypeStruct((L, D, wout), BF16),
        compiler_params=_cp("parallel", "parallel"),
        name="prep_w_in",
    )(jnp.swapaxes(w_in, 1, 2))


def _swiglu_accumulate(xb, wg_ref, wu_ref, wd_refs, o_ref):
    a = jnp.dot(xb, wg_ref[0], preferred_element_type=F32)
    b = jnp.dot(xb, wu_ref[0], preferred_element_type=F32)
    act = (a / (1.0 + jnp.exp(-a)) * b).astype(BF16)
    width = o_ref.shape[1] // len(wd_refs)
    for n, wd_ref in enumerate(wd_refs):
        o_ref[:, n * width:(n + 1) * width] += jnp.dot(act, wd_ref[0], preferred_element_type=F32)


def _dense_swiglu_kernel(x_ref, wg_ref, wu_ref, wd_ref, r_ref, o_ref):
    @pl.when(pl.program_id(1) == 0)
    def _():
        o_ref[...] = r_ref[...]

    _swiglu_accumulate(x_ref[...], wg_ref, wu_ref, (wd_ref,), o_ref)


def dense_swiglu(x, w_gate, w_up, w_down, residual, tm=1024, tf=512):
    T, D = x.shape
    F = w_gate.shape[2]
    tm = min(tm, T)
    return pl.pallas_call(
        _dense_swiglu_kernel,
        grid=(T // tm, F // tf),
        in_specs=[pl.BlockSpec((tm, D), lambda i, f: (i, 0)),
                  pl.BlockSpec((1, D, tf), lambda i, f: (0, 0, f)),
                  pl.BlockSpec((1, D, tf), lambda i, f: (0, 0, f)),
                  pl.BlockSpec((1, tf, D), lambda i, f: (0, f, 0)),
                  pl.BlockSpec((tm, D), lambda i, f: (i, 0), pipeline_mode=pl.Buffered(1))],
        out_specs=pl.BlockSpec((tm, D), lambda i, f: (i, 0)),
        out_shape=jax.ShapeDtypeStruct((T, D), F32),
        compiler_params=_cp("parallel", "arbitrary"),
        name="dense_swiglu",
    )(x, w_gate, w_up, w_down, residual)


def _row_copy(src_hbm, dst_ref, src_row, dst_row, sem):
    return pltpu.make_async_copy(src_hbm.at[pl.ds(src_row, 1)], dst_ref.at[pl.ds(dst_row, 1)], sem)


def _moe_rows_per_step(tm, nf):
    return -(-tm // (nf * SUBLANES)) * SUBLANES


def _moe_swiglu_kernel(tok_ref, be_ref, nu_ref, parts_ref, h_hbm, wg_ref, wu_ref, wd0_ref, wd1_ref, o_ref,
                       xg_ref, xb_ref, sem, *, tm, nf):
    del be_ref
    i = pl.program_id(0)
    f = pl.program_id(1)
    n_used = nu_ref[0]
    rows_per_step = _moe_rows_per_step(tm, nf)
    n_fetch = rows_per_step * nf

    def fetch(block, r, slot):
        tok = tok_ref[block * tm + jnp.minimum(r, tm - 1)]
        return _row_copy(h_hbm, xg_ref.at[slot], tok, r, sem.at[slot])

    def wait_rows(slot, n_rows):
        rows = xg_ref.at[slot, pl.ds(0, n_rows)]
        pltpu.make_async_copy(rows, rows, sem.at[slot]).wait()

    @pl.when((i >= n_used) & (f == 0))
    def _():
        o_ref[...] = jnp.zeros_like(o_ref)

    @pl.when(i < n_used)
    def _():
        slot = i % 2

        @pl.when((i == 0) & (f == 0))
        def _():
            def start(r, c):
                fetch(0, r, 0).start()
                return c
            lax.fori_loop(0, n_fetch, start, 0)
            for j in range(rows_per_step):
                fetch(jnp.minimum(1, n_used - 1), j, 1).start()

        @pl.when(f == 0)
        def _():
            wait_rows(slot, n_fetch)
            xb_ref[...] = xg_ref[slot, 0:tm, :].astype(BF16)
            o_ref[...] = jnp.zeros_like(o_ref)

        def fetch_next_rows():
            wrap = (f == nf - 1).astype(jnp.int32)
            chunk = (f + 1) * (1 - wrap)
            target = jnp.minimum(i + 1 + wrap, n_used - 1)
            target_slot = (i + 1 + wrap) % 2
            for j in range(rows_per_step):
                fetch(target, chunk * rows_per_step + j, target_slot).start()

        quarters = parts_ref[i]
        for nq in range(MOE_ROW_PARTS, 0, -1):
            rows = nq * tm // MOE_ROW_PARTS

            @pl.when(quarters == nq)
            def _(rows=rows):
                fetch_next_rows()
                _swiglu_accumulate(xb_ref[0:rows, :], wg_ref, wu_ref, (wd0_ref, wd1_ref),
                                   o_ref.at[pl.ds(0, rows)])

        @pl.when((i == n_used - 1) & (f == nf - 1))
        def _():
            wait_rows(1 - slot, n_fetch)
            wait_rows(slot, rows_per_step)


def moe_swiglu(h, slot_tok, block_e, n_used, block_parts, w_gate, w_up, w_down_lo, w_down_hi, tm, tf=1024):
    n_slots = slot_tok.shape[0]
    D = h.shape[1]
    F = w_gate.shape[2]
    while F % tf:
        tf //= 2
    nf = F // tf
    xg_rows = _moe_rows_per_step(tm, nf) * nf
    blk = lambda i, nu: jnp.minimum(i, nu[0] - 1)
    chunk = lambda i, f, nu: jnp.where(i < nu[0], f, nf - 1)
    up_spec = pl.BlockSpec((1, D, tf), lambda i, f, tok, be, nu, hf: (be[blk(i, nu)], 0, chunk(i, f, nu)))
    down_spec = pl.BlockSpec((1, tf, D // 2),
                             lambda i, f, tok, be, nu, hf: (be[blk(i, nu)], chunk(i, f, nu), 0))
    return pl.pallas_call(
        functools.partial(_moe_swiglu_kernel, tm=tm, nf=nf),
        grid_spec=pltpu.PrefetchScalarGridSpec(
            num_scalar_prefetch=4,
            grid=(n_slots // tm, nf),
            in_specs=[pl.BlockSpec(memory_space=pl.ANY), up_spec, up_spec, down_spec, down_spec],
            out_specs=pl.BlockSpec((tm, D), lambda i, f, tok, be, nu, hf: (i, 0)),
            scratch_shapes=[pltpu.VMEM((2, xg_rows, D), F32),
                            pltpu.VMEM((tm, D), BF16),
                            pltpu.SemaphoreType.DMA((2,))]),
        out_shape=jax.ShapeDtypeStruct((n_slots, D), F32),
        compiler_params=_cp("arbitrary", "arbitrary"),
        name="moe_swiglu",
    )(slot_tok, block_e, n_used, block_parts, h, w_gate, w_up, w_down_lo, w_down_hi)


def _combine_kernel(dest_ref, y_ref, gate_ref, ys_hbm, o_ref, buf_ref, sem, *, tb):
    i = pl.program_id(0)

    def fetch(step, slot):
        base = step * tb

        def start(r, c):
            for k in range(TOP_K):
                _row_copy(ys_hbm, buf_ref.at[slot, k], dest_ref[TOP_K * (base + r) + k], r, sem.at[slot]).start()
            return c

        lax.fori_loop(0, tb, start, 0, unroll=8)

    @pl.when(i == 0)
    def _():
        fetch(0, 0)

    @pl.when(i + 1 < pl.num_programs(0))
    def _():
        fetch(i + 1, (i + 1) % 2)

    slot = i % 2
    pltpu.make_async_copy(buf_ref.at[slot], buf_ref.at[slot], sem.at[slot]).wait()
    g = gate_ref[...]
    o_ref[...] = y_ref[...] + g[:, 0:1] * buf_ref[slot, 0] + g[:, 1:2] * buf_ref[slot, 1]


def combine_rows(y, gates, ys, dest, tb=256):
    T, D = y.shape
    return pl.pallas_call(
        functools.partial(_combine_kernel, tb=tb),
        grid_spec=pltpu.PrefetchScalarGridSpec(
            num_scalar_prefetch=1,
            grid=(T // tb,),
            in_specs=[pl.BlockSpec((tb, D), lambda i, d: (i, 0)),
                      pl.BlockSpec((tb, LANES), lambda i, d: (i, 0)),
                      pl.BlockSpec(memory_space=pl.ANY)],
            out_specs=pl.BlockSpec((tb, D), lambda i, d: (i, 0)),
            scratch_shapes=[pltpu.VMEM((2, TOP_K, tb, D), F32),
                            pltpu.SemaphoreType.DMA((2,))]),
        out_shape=jax.ShapeDtypeStruct((T, D), F32),
        compiler_params=_cp("arbitrary"),
        name="moe_combine",
    )(dest, y, gates, ys)


def _routing_tables(top_idx, T, block):
    A = T * TOP_K
    E = N_EXPERTS
    flat_e = top_idx.reshape(A)
    onehot = (flat_e[:, None] == jnp.arange(E, dtype=jnp.int32)[None, :]).astype(jnp.int32)
    csum = jnp.cumsum(onehot, axis=0)
    counts = csum[-1]
    rank = jnp.sum((csum - 1) * onehot, axis=1)
    padded = (counts + block - 1) // block * block
    pends = jnp.cumsum(padded)
    pstarts = pends - padded
    starts = jnp.cumsum(counts) - counts
    dest = (pstarts[flat_e] + rank).astype(jnp.int32)
    n_blocks = A // block + E
    n_slots = n_blocks * block
    order = jnp.argsort(flat_e, stable=True).astype(jnp.int32)
    slot = jnp.arange(n_slots, dtype=jnp.int32)
    slot_e = jnp.clip(jnp.searchsorted(pends, slot, side='right'), 0, E - 1).astype(jnp.int32)
    r = slot - pstarts[slot_e]
    valid = r < counts[slot_e]
    src = jnp.clip(starts[slot_e] + r, 0, A - 1)
    slot_tok = jnp.where(valid, order[src] // TOP_K, 0).astype(jnp.int32)
    block_e = slot_e[::block]
    n_used = (pends[-1:] // block).astype(jnp.int32)
    in_block = jnp.clip(counts[block_e] - (slot[::block] - pstarts[block_e]), 1, block)
    part = block // MOE_ROW_PARTS
    block_parts = ((in_block + part - 1) // part).astype(jnp.int32)
    return slot_tok, dest, block_e, n_used, block_parts


def _token_mixer(h, mem2d, tab, B, S, M, l, casts, w_in, mla_q_a_norm_g, mla_w_uq, mla_kv_a_norm_g, mla_w_ukv,
                 mla_q_norm_g, mla_k_norm_g, mla_w_out, pool_w, pool_scale, pool_w_out,
                 mem_norm_g, mem_w_kv, mem_q_norm_g, mem_k_norm_g, mem_w_out):
    za = matmul(h, w_in, l, ZA_WIDTH, F32, tn=ZA_WIDTH // 2)

    head_pad = lambda v: jnp.concatenate([v[..., :QK_NOPE_DIM], _rope_lanes(v[..., QK_NOPE_DIM:])], axis=-1)
    wuq = head_pad(mla_w_uq[l].reshape(Q_LORA_RANK, MLA_HEADS, QK_HEAD_DIM)).reshape(
        Q_LORA_RANK, MLA_HEADS * HEAD_PAD)
    wukv = mla_w_ukv[l].reshape(KV_LORA_RANK, MLA_HEADS, QK_NOPE_DIM + V_HEAD_DIM)
    wuk = wukv[:, :, :QK_NOPE_DIM].reshape(KV_LORA_RANK, MLA_WIDTH)
    wuv = wukv[:, :, QK_NOPE_DIM:].reshape(KV_LORA_RANK, MLA_WIDTH)
    pad_g = lambda g: head_pad(g).reshape(1, HEAD_PAD)
    q, k, v = mla_prep(za, tab, wuq.astype(BF16), wuk.astype(BF16), wuv.astype(BF16),
                       mla_q_a_norm_g[l].reshape(1, -1), mla_kv_a_norm_g[l].reshape(1, -1),
                       pad_g(mla_q_norm_g[l]), pad_g(mla_k_norm_g[l]))
    attn, *cast_out = causal_attention(q, k, v, B, S, casts)

    mixed = pool_mixer(za, pool_w[l].astype(BF16), pool_scale[l].reshape(1, -1), B, S)

    k_m, v_m = mem_kv(mem2d, mem_norm_g[l].reshape(1, -1), mem_w_kv[l].astype(BF16),
                      mem_k_norm_g[l].reshape(1, -1), B, M)
    o_mem = mem_attention(za, k_m, v_m, mem_q_norm_g[l].reshape(1, -1), B, S, M)

    merged = merge_branches(h, attn, mixed, o_mem, w_in, mla_w_out, pool_w_out, mem_w_out, l)
    return merged, cast_out


def kernel(x, mem, positions, attn_norm_g, w_in, mla_q_a_norm_g, mla_w_uq, mla_kv_a_norm_g, mla_w_ukv, mla_q_norm_g, mla_k_norm_g, mla_w_out, pool_w, pool_scale, pool_w_out, mem_norm_g, mem_w_kv, mem_q_norm_g, mem_k_norm_g, mem_w_out, w_o, ffn_norm_g, dense_w_gate, dense_w_up, dense_w_down, router_w, router_b, moe_w_gate, moe_w_up, moe_w_down):
    B, S, D = x.shape
    M = mem.shape[1]
    T = B * S
    depth = attn_norm_g.shape[0]
    moe_block = 512
    y = x.reshape(T, D)
    mem2d = mem.reshape(B * M, D)
    tab = rope_tables(positions)
    w_in = prep_w_in(w_in)
    mla_w_out, pool_w_out, mem_w_out, w_o = (w.astype(BF16) for w in (mla_w_out, pool_w_out, mem_w_out, w_o))
    moe_bf16 = {}
    for l in range(depth):
        h = rmsnorm(y, attn_norm_g[l], BF16)
        i = l // 2
        casts = ()
        if l % 2 == 0 and l + 1 < depth:
            casts = ((moe_w_gate[i], 0, 1), (moe_w_down[i], 0, 2))
        elif l % 2 == 1:
            casts = ((moe_w_up[i], 0, 1), (moe_w_down[i], 1, 2))
        merged, cast_out = _token_mixer(h, mem2d, tab, B, S, M, l, casts, w_in, mla_q_a_norm_g, mla_w_uq,
                                        mla_kv_a_norm_g, mla_w_ukv, mla_q_norm_g, mla_k_norm_g, mla_w_out,
                                        pool_w, pool_scale, pool_w_out, mem_norm_g, mem_w_kv, mem_q_norm_g,
                                        mem_k_norm_g, mem_w_out)
        if casts:
            names = ("gate", "down_lo") if l % 2 == 0 else ("up", "down_hi")
            moe_bf16.update(zip(names, cast_out))
        if l % 2 == 0:
            y, h = matmul_residual_norm(merged, w_o, l, y, ffn_norm_g[l])
            y = dense_swiglu(h, dense_w_gate[i:i + 1].astype(BF16), dense_w_up[i:i + 1].astype(BF16),
                             dense_w_down[i:i + 1].astype(BF16), y)
        else:
            y = matmul(merged, w_o, l, D, F32, residual=y, tm=512, tn=D)
            hf, idx, gates = norm_router(y, ffn_norm_g[l], router_w[i], router_b[i])
            slot_tok, dest, block_e, n_used, block_parts = _routing_tables(idx[:, :TOP_K], T, moe_block)
            ys = moe_swiglu(hf, slot_tok, block_e, n_used, block_parts, moe_bf16.pop("gate"), moe_bf16.pop("up"),
                            moe_bf16.pop("down_lo"), moe_bf16.pop("down_hi"), tm=moe_block)
            y = combine_rows(y, gates, ys, dest)
    return y.reshape(B, S, D)
```

```python
import functools

import numpy as np
import jax
import jax.numpy as jnp
from jax import lax
from jax.experimental import pallas as pl
from jax.experimental.pallas import tpu as pltpu

F32 = jnp.float32
BF16 = jnp.bfloat16

MEM_HEADS = 4
MEM_HEAD_DIM = 256
MEM_WIDTH = MEM_HEADS * MEM_HEAD_DIM
MLA_HEADS = 8
Q_LORA_RANK = 512
KV_LORA_RANK = 256
QK_NOPE_DIM = 128
QK_ROPE_DIM = 64
QK_HEAD_DIM = QK_NOPE_DIM + QK_ROPE_DIM
V_HEAD_DIM = 128
MLA_WIDTH = MLA_HEADS * V_HEAD_DIM
ROPE_THETA = 10000.0
POOL_WINDOWS = (2, 4, 8, 16)
POOL_GROUPS = 4
POOL_GROUP_DIM = 256
POOL_WIDTH = POOL_GROUPS * POOL_GROUP_DIM
N_BRANCHES = 3
N_EXPERTS = 8
TOP_K = 2
NORM_EPS = 1e-6
LOG2_E = 1.4426950408889634

LANES = 128
SUBLANES = 8
HEAD_PAD = 2 * LANES
ROPE_HALF = QK_ROPE_DIM // 2
POOL_HALO = 16

ZA_POOL = 0
ZA_QMEM = POOL_WIDTH
ZA_CQ = ZA_QMEM + MEM_WIDTH
ZA_CKV = ZA_CQ + Q_LORA_RANK
ZA_KPE = ZA_CKV + KV_LORA_RANK
ZA_WIDTH = 3072

MOE_ROW_PARTS = 4
VMEM_LIMIT = 52 * 1024 * 1024


def _cp(*sem):
    return pltpu.CompilerParams(dimension_semantics=sem, vmem_limit_bytes=VMEM_LIMIT)


def _rms(x, eps_dim):
    return lax.rsqrt(jnp.sum(x * x, axis=-1, keepdims=True) * (1.0 / eps_dim) + NORM_EPS)


def _rmsnorm_kernel(x_ref, g_ref, o_ref):
    x = x_ref[...]
    o_ref[...] = (x * _rms(x, x.shape[-1]) * g_ref[...]).astype(o_ref.dtype)


def rmsnorm(x, g, out_dtype, tm=512):
    T, D = x.shape
    return pl.pallas_call(
        _rmsnorm_kernel,
        grid=(T // tm,),
        in_specs=[pl.BlockSpec((tm, D), lambda i: (i, 0)),
                  pl.BlockSpec((1, D), lambda i: (0, 0))],
        out_specs=pl.BlockSpec((tm, D), lambda i: (i, 0)),
        out_shape=jax.ShapeDtypeStruct((T, D), out_dtype),
        compiler_params=_cp("parallel"),
        name="rmsnorm",
    )(x, g.reshape(1, D))


def _norm_router_kernel(x_ref, g_ref, rwh_ref, rwl_ref, rb_ref, h_ref, idx_ref, gate_ref):
    x = x_ref[...]
    h = x * _rms(x, x.shape[-1]) * g_ref[...]
    h_ref[...] = h
    h_hi = h.astype(BF16)
    h_lo = (h - h_hi.astype(F32)).astype(BF16)
    logits = (jnp.dot(h_hi, rwh_ref[...], preferred_element_type=F32)
              + jnp.dot(h_lo, rwh_ref[...], preferred_element_type=F32)
              + jnp.dot(h_hi, rwl_ref[...], preferred_element_type=F32)) + rb_ref[...]
    lane = lax.broadcasted_iota(jnp.int32, logits.shape, 1)
    neg = -jnp.inf
    l1 = jnp.where(lane < N_EXPERTS, logits, neg)
    m1 = jnp.max(l1, axis=-1, keepdims=True)
    i1 = jnp.min(jnp.where(l1 == m1, lane, LANES), axis=-1, keepdims=True)
    l2 = jnp.where(lane == i1, neg, l1)
    m2 = jnp.max(l2, axis=-1, keepdims=True)
    i2 = jnp.min(jnp.where(l2 == m2, lane, LANES), axis=-1, keepdims=True)
    e = jnp.exp(m2 - m1)
    g1 = 1.0 / (1.0 + e)
    g2 = e / (1.0 + e)
    idx_ref[...] = jnp.where(lane == 0, i1, jnp.where(lane == 1, i2, 0))
    gate_ref[...] = jnp.where(lane == 0, g1, jnp.where(lane == 1, g2, 0.0))


def norm_router(x, g, router_w, router_b, tm=512):
    T, D = x.shape
    E = router_w.shape[1]
    rw = jnp.zeros((D, LANES), F32).at[:, :E].set(router_w)
    rw_hi = rw.astype(BF16)
    rw_lo = (rw - rw_hi.astype(F32)).astype(BF16)
    rb = jnp.zeros((1, LANES), F32).at[0, :E].set(router_b)
    return pl.pallas_call(
        _norm_router_kernel,
        grid=(T // tm,),
        in_specs=[pl.BlockSpec((tm, D), lambda i: (i, 0)),
                  pl.BlockSpec((1, D), lambda i: (0, 0)),
                  pl.BlockSpec((D, LANES), lambda i: (0, 0)),
                  pl.BlockSpec((D, LANES), lambda i: (0, 0)),
                  pl.BlockSpec((1, LANES), lambda i: (0, 0))],
        out_specs=[pl.BlockSpec((tm, D), lambda i: (i, 0)),
                   pl.BlockSpec((tm, LANES), lambda i: (i, 0)),
                   pl.BlockSpec((tm, LANES), lambda i: (i, 0))],
        out_shape=[jax.ShapeDtypeStruct((T, D), F32),
                   jax.ShapeDtypeStruct((T, LANES), jnp.int32),
                   jax.ShapeDtypeStruct((T, LANES), F32)],
        compiler_params=_cp("parallel"),
        name="norm_router",
    )(x, g.reshape(1, D), rw_hi, rw_lo, rb)


def _mm_kernel(a_ref, w_ref, o_ref):
    o_ref[...] = jnp.dot(a_ref[...], w_ref[0], preferred_element_type=F32).astype(o_ref.dtype)


def _mm_res_kernel(a_ref, w_ref, r_ref, o_ref):
    o_ref[...] = (r_ref[...] + jnp.dot(a_ref[...], w_ref[0], preferred_element_type=F32)).astype(o_ref.dtype)


def matmul(a, w, l, N, out_dtype, residual=None, tm=1024, tn=1024):
    M, K = a.shape
    tm, tn = min(tm, M), min(tn, N)
    in_specs = [pl.BlockSpec((tm, K), lambda i, j: (i, 0)),
                pl.BlockSpec((1, K, tn), lambda i, j: (l, 0, j))]
    args = [a, w]
    body = _mm_kernel
    if residual is not None:
        in_specs.append(pl.BlockSpec((tm, tn), lambda i, j: (i, j)))
        args.append(residual)
        body = _mm_res_kernel
    return pl.pallas_call(
        body,
        grid=(M // tm, N // tn),
        in_specs=in_specs,
        out_specs=pl.BlockSpec((tm, tn), lambda i, j: (i, j)),
        out_shape=jax.ShapeDtypeStruct((M, N), out_dtype),
        compiler_params=_cp("parallel", "arbitrary"),
        name="matmul",
    )(*args)


def _mm_res_norm_kernel(a_ref, w_ref, r_ref, g_ref, o_ref, h_ref):
    y = r_ref[...] + jnp.dot(a_ref[...], w_ref[0], preferred_element_type=F32)
    o_ref[...] = y
    h_ref[...] = (y * _rms(y, y.shape[-1]) * g_ref[...]).astype(h_ref.dtype)


def matmul_residual_norm(a, w, l, residual, g, tm=512):
    M, K = a.shape
    N = w.shape[2]
    rows = lambda width: pl.BlockSpec((tm, width), lambda i: (i, 0))
    return pl.pallas_call(
        _mm_res_norm_kernel,
        grid=(M // tm,),
        in_specs=[rows(K), pl.BlockSpec((1, K, N), lambda i: (l, 0, 0)), rows(N),
                  pl.BlockSpec((1, N), lambda i: (0, 0))],
        out_specs=[rows(N), rows(N)],
        out_shape=[jax.ShapeDtypeStruct((M, N), F32), jax.ShapeDtypeStruct((M, N), BF16)],
        compiler_params=_cp("parallel"),
        name="matmul_residual_norm",
    )(a, w, residual, g.reshape(1, N))


def _rope_lanes(v):
    z = jnp.zeros(v.shape[:-1] + (LANES // 2 - ROPE_HALF,), v.dtype)
    return jnp.concatenate([v[..., :ROPE_HALF], z, v[..., ROPE_HALF:], z], axis=-1)


def _rope_table_kernel(pos_ref, invf_ref, o_ref):
    ang = pos_ref[...].astype(F32) * invf_ref[...]
    lane = lax.broadcasted_iota(jnp.int32, ang.shape, 1)
    first = lane < ROPE_HALF
    second = (lane >= LANES // 2) & (lane < LANES // 2 + ROPE_HALF)
    c = jnp.cos(ang)
    s = jnp.sin(ang)
    o_ref[:, 0:LANES] = jnp.where(first | second, c, 0.0)
    o_ref[:, LANES:2 * LANES] = jnp.where(first, -s, jnp.where(second, s, 0.0))


def rope_tables(positions, tm=512):
    T = positions.size
    inv_freq = ROPE_THETA ** (-np.arange(0, QK_ROPE_DIM, 2, dtype=np.float32) / QK_ROPE_DIM)
    invf = np.zeros((1, LANES), np.float32)
    invf[0, :ROPE_HALF] = inv_freq
    invf[0, LANES // 2:LANES // 2 + ROPE_HALF] = inv_freq
    return pl.pallas_call(
        _rope_table_kernel,
        grid=(T // tm,),
        in_specs=[pl.BlockSpec((tm, 1), lambda i: (i, 0)),
                  pl.BlockSpec((1, LANES), lambda i: (0, 0))],
        out_specs=pl.BlockSpec((tm, 2 * LANES), lambda i: (i, 0)),
        out_shape=jax.ShapeDtypeStruct((T, 2 * LANES), F32),
        compiler_params=_cp("parallel"),
        name="rope_tables",
    )(positions.reshape(T, 1), jnp.asarray(invf))


def _rope(x, tab):
    return x * tab[:, 0:LANES] + pltpu.roll(x, LANES // 2, 1) * tab[:, LANES:2 * LANES]


def _mla_prep_kernel(cq_ref, ckv_ref, kpe_ref, tab_ref, wuq_ref, wuk_ref, wuv_ref,
                     gqa_ref, gkva_ref, gq_ref, gk_ref, q_ref, k_ref, v_ref):
    tab = tab_ref[...]
    cq = cq_ref[...]
    cqn = (cq * _rms(cq, Q_LORA_RANK) * gqa_ref[...]).astype(BF16)
    qf = jnp.dot(cqn, wuq_ref[...], preferred_element_type=F32)
    ckv = ckv_ref[...]
    ckvn = (ckv * _rms(ckv, KV_LORA_RANK) * gkva_ref[...]).astype(BF16)
    kn = jnp.dot(ckvn, wuk_ref[...], preferred_element_type=F32)
    v_ref[...] = jnp.dot(ckvn, wuv_ref[...], preferred_element_type=F32).astype(v_ref.dtype)
    gq = gq_ref[...]
    gk = gk_ref[...]
    kpe = kpe_ref[...]
    ss_pe = jnp.sum(kpe * kpe, axis=-1, keepdims=True)
    kpe_rot = _rope(kpe * gk[:, LANES:], tab)
    scale = QK_HEAD_DIM ** -0.5 * LOG2_E
    for h in range(MLA_HEADS):
        q0 = qf[:, h * HEAD_PAD:h * HEAD_PAD + LANES]
        q1 = qf[:, h * HEAD_PAD + LANES:(h + 1) * HEAD_PAD]
        ss = jnp.sum(q0 * q0, axis=-1, keepdims=True) + jnp.sum(q1 * q1, axis=-1, keepdims=True)
        rq = lax.rsqrt(ss * (1.0 / QK_HEAD_DIM) + NORM_EPS) * scale
        q_ref[:, h * HEAD_PAD:h * HEAD_PAD + LANES] = (q0 * gq[:, :LANES] * rq).astype(q_ref.dtype)
        q_ref[:, h * HEAD_PAD + LANES:(h + 1) * HEAD_PAD] = (_rope(q1 * gq[:, LANES:], tab) * rq).astype(q_ref.dtype)
        k0 = kn[:, h * LANES:(h + 1) * LANES]
        ssk = jnp.sum(k0 * k0, axis=-1, keepdims=True) + ss_pe
        rk = lax.rsqrt(ssk * (1.0 / QK_HEAD_DIM) + NORM_EPS)
        k_ref[:, h * HEAD_PAD:h * HEAD_PAD + LANES] = (k0 * gk[:, :LANES] * rk).astype(k_ref.dtype)
        k_ref[:, h * HEAD_PAD + LANES:(h + 1) * HEAD_PAD] = (kpe_rot * rk).astype(k_ref.dtype)


def mla_prep(za, tab, wuq, wuk, wuv, gqa, gkva, gq, gk, tm=512):
    T = za.shape[0]
    HP = MLA_HEADS * HEAD_PAD
    full = lambda shape: pl.BlockSpec(shape, lambda i: (0, 0))
    return pl.pallas_call(
        _mla_prep_kernel,
        grid=(T // tm,),
        in_specs=[pl.BlockSpec((tm, Q_LORA_RANK), lambda i: (i, ZA_CQ // Q_LORA_RANK)),
                  pl.BlockSpec((tm, KV_LORA_RANK), lambda i: (i, ZA_CKV // KV_LORA_RANK)),
                  pl.BlockSpec((tm, LANES), lambda i: (i, ZA_KPE // LANES)),
                  pl.BlockSpec((tm, 2 * LANES), lambda i: (i, 0)),
                  full((Q_LORA_RANK, HP)), full((KV_LORA_RANK, MLA_WIDTH)), full((KV_LORA_RANK, MLA_WIDTH)),
                  full((1, Q_LORA_RANK)), full((1, KV_LORA_RANK)), full((1, HEAD_PAD)), full((1, HEAD_PAD))],
        out_specs=[pl.BlockSpec((tm, HP), lambda i: (i, 0)),
                   pl.BlockSpec((tm, HP), lambda i: (i, 0)),
                   pl.BlockSpec((tm, MLA_WIDTH), lambda i: (i, 0))],
        out_shape=[jax.ShapeDtypeStruct((T, HP), BF16),
                   jax.ShapeDtypeStruct((T, HP), BF16),
                   jax.ShapeDtypeStruct((T, MLA_WIDTH), BF16)],
        compiler_params=_cp("parallel"),
        name="mla_prep",
    )(za, za, za, tab, wuq, wuk, wuv, gqa, gkva, gq, gk)


def _causal_attn_kernel(q_ref, k_ref, v_ref, *rest, tq, n_cast):
    cast_src = rest[:n_cast]
    o_ref = rest[n_cast]
    cast_dst = rest[n_cast + 1:]
    for s_ref, d_ref in zip(cast_src, cast_dst):
        d_ref[...] = s_ref[...].astype(d_ref.dtype)
    S = q_ref.shape[0]
    row = lax.broadcasted_iota(jnp.int32, (tq, tq), 0)
    col = lax.broadcasted_iota(jnp.int32, (tq, tq), 1)
    tri = col <= row
    for qi in range(S // tq):
        kl = (qi + 1) * tq
        q = q_ref[qi * tq:kl, :]
        s = lax.dot_general(q, k_ref[0:kl, :], (((1,), (1,)), ((), ())), preferred_element_type=F32)
        s_diag = jnp.where(tri, s[:, kl - tq:kl], -jnp.inf)
        if qi > 0:
            s = jnp.concatenate([s[:, :kl - tq], s_diag], axis=1)
        else:
            s = s_diag
        m = jnp.max(s, axis=-1, keepdims=True)
        p = jnp.exp2(s - m)
        l = jnp.sum(p, axis=-1, keepdims=True)
        o = jnp.dot(p.astype(BF16), v_ref[0:kl, :], preferred_element_type=F32)
        o_ref[qi * tq:kl, :] = (o / l).astype(o_ref.dtype)


def causal_attention(q, k, v, B, S, casts=(), tq=256):
    T = q.shape[0]
    n_steps = B * MLA_HEADS
    in_specs = [pl.BlockSpec((S, HEAD_PAD), lambda b, h: (b, h)),
                pl.BlockSpec((S, HEAD_PAD), lambda b, h: (b, h)),
                pl.BlockSpec((S, V_HEAD_DIM), lambda b, h: (b, h))]
    out_specs = [pl.BlockSpec((S, V_HEAD_DIM), lambda b, h: (b, h))]
    out_shape = [jax.ShapeDtypeStruct((T, MLA_WIDTH), BF16)]
    for src, part, n_parts in casts:
        E, rows, cols = src.shape
        spe = n_steps // E
        rb, cb = rows // spe, cols // n_parts
        assert spe * E == n_steps and rb * spe == rows and rb % 16 == 0
        assert cb * n_parts == cols and cb % LANES == 0
        step = lambda b, h: b * MLA_HEADS + h
        in_specs.append(pl.BlockSpec((1, rb, cb), lambda b, h, part=part, spe=spe:
                                     (step(b, h) // spe, step(b, h) % spe, part)))
        out_specs.append(pl.BlockSpec((1, rb, cb), lambda b, h, spe=spe:
                                      (step(b, h) // spe, step(b, h) % spe, 0)))
        out_shape.append(jax.ShapeDtypeStruct((E, rows, cb), BF16))
    return pl.pallas_call(
        functools.partial(_causal_attn_kernel, tq=tq, n_cast=len(casts)),
        grid=(B, MLA_HEADS),
        in_specs=in_specs,
        out_specs=out_specs,
        out_shape=out_shape,
        compiler_params=_cp("parallel", "parallel"),
        name="causal_attention",
    )(q, k, v, *(c[0] for c in casts))


def _pool_kernel(u_ref, w_ref, sc_ref, o_ref, pad_ref, *, rc):
    S = u_ref.shape[0]
    C = POOL_GROUP_DIM
    t1 = lax.broadcasted_iota(jnp.int32, (rc, C), 0).astype(F32) + 1.0
    for g, win in enumerate(POOL_WINDOWS):
        pad_ref[0:POOL_HALO, :] = jnp.zeros((POOL_HALO, C), F32)
        pad_ref[POOL_HALO:POOL_HALO + S, :] = u_ref[:, g * C:(g + 1) * C]
        for r0 in range(0, S, rc):
            acc = pad_ref[POOL_HALO + r0:POOL_HALO + r0 + rc, :]
            tok = acc
            for kk in range(1, win):
                acc = acc + pad_ref[POOL_HALO + r0 - kk:POOL_HALO + r0 - kk + rc, :]
            cnt = jnp.minimum(t1 + float(r0), float(win))
            pooled = (acc / cnt - tok).astype(BF16)
            mixed = jnp.dot(pooled, w_ref[g], preferred_element_type=F32) * sc_ref[:, g * C:(g + 1) * C]
            o_ref[r0:r0 + rc, g * C:(g + 1) * C] = mixed.astype(o_ref.dtype)


def pool_mixer(za, pool_w, pool_scale, B, S, rc=512):
    T = za.shape[0]
    rc = min(rc, S)
    return pl.pallas_call(
        functools.partial(_pool_kernel, rc=rc),
        grid=(B,),
        in_specs=[pl.BlockSpec((S, POOL_WIDTH), lambda b: (b, ZA_POOL // POOL_WIDTH)),
                  pl.BlockSpec((POOL_GROUPS, POOL_GROUP_DIM, POOL_GROUP_DIM), lambda b: (0, 0, 0)),
                  pl.BlockSpec((1, POOL_WIDTH), lambda b: (0, 0))],
        out_specs=pl.BlockSpec((S, POOL_WIDTH), lambda b: (b, 0)),
        out_shape=jax.ShapeDtypeStruct((T, POOL_WIDTH), BF16),
        scratch_shapes=[pltpu.VMEM((POOL_HALO + S, POOL_GROUP_DIM), F32)],
        compiler_params=_cp("parallel"),
        name="pool_mixer",
    )(za, pool_w, pool_scale)


def _mem_kv_kernel(m_ref, g_ref, w_ref, gk_ref, k_ref, v_ref):
    x = m_ref[...]
    xn = (x * _rms(x, x.shape[-1]) * g_ref[...]).astype(BF16)
    kv = jnp.dot(xn, w_ref[...], preferred_element_type=F32)
    for h in range(MEM_HEADS):
        kh = kv[:, h * MEM_HEAD_DIM:(h + 1) * MEM_HEAD_DIM]
        k_ref[:, h * MEM_HEAD_DIM:(h + 1) * MEM_HEAD_DIM] = (
            kh * _rms(kh, MEM_HEAD_DIM) * gk_ref[...]).astype(k_ref.dtype)
    v_ref[...] = kv[:, MEM_WIDTH:].astype(v_ref.dtype)


def mem_kv(mem2d, g, w_kv, gk, B, M):
    D = mem2d.shape[1]
    return pl.pallas_call(
        _mem_kv_kernel,
        grid=(B,),
        in_specs=[pl.BlockSpec((M, D), lambda b: (b, 0)),
                  pl.BlockSpec((1, D), lambda b: (0, 0)),
                  pl.BlockSpec((D, 2 * MEM_WIDTH), lambda b: (0, 0)),
                  pl.BlockSpec((1, MEM_HEAD_DIM), lambda b: (0, 0))],
        out_specs=[pl.BlockSpec((M, MEM_WIDTH), lambda b: (b, 0)),
                   pl.BlockSpec((M, MEM_WIDTH), lambda b: (b, 0))],
        out_shape=[jax.ShapeDtypeStruct((B * M, MEM_WIDTH), BF16),
                   jax.ShapeDtypeStruct((B * M, MEM_WIDTH), BF16)],
        compiler_params=_cp("parallel"),
        name="mem_kv",
    )(mem2d, g, w_kv, gk)


def _mem_attn_kernel(q_ref, k_ref, v_ref, gq_ref, o_ref):
    scale = MEM_HEAD_DIM ** -0.5 * LOG2_E
    for h in range(MEM_HEADS):
        sl = slice(h * MEM_HEAD_DIM, (h + 1) * MEM_HEAD_DIM)
        qh = q_ref[:, sl]
        qn = (qh * (_rms(qh, MEM_HEAD_DIM) * scale) * gq_ref[...]).astype(BF16)
        s = lax.dot_general(qn, k_ref[:, sl], (((1,), (1,)), ((), ())), preferred_element_type=F32)
        m = jnp.max(s, axis=-1, keepdims=True)
        p = jnp.exp2(s - m)
        l = jnp.sum(p, axis=-1, keepdims=True)
        o = jnp.dot(p.astype(BF16), v_ref[:, sl], preferred_element_type=F32)
        o_ref[:, sl] = (o / l).astype(o_ref.dtype)


def mem_attention(za, k_m, v_m, gq, B, S, M, tq=1024):
    T = za.shape[0]
    tq = min(tq, S)
    nq = S // tq
    return pl.pallas_call(
        _mem_attn_kernel,
        grid=(B, nq),
        in_specs=[pl.BlockSpec((tq, MEM_WIDTH), lambda b, i: (b * nq + i, ZA_QMEM // MEM_WIDTH)),
                  pl.BlockSpec((M, MEM_WIDTH), lambda b, i: (b, 0)),
                  pl.BlockSpec((M, MEM_WIDTH), lambda b, i: (b, 0)),
                  pl.BlockSpec((1, MEM_HEAD_DIM), lambda b, i: (0, 0))],
        out_specs=pl.BlockSpec((tq, MEM_WIDTH), lambda b, i: (b * nq + i, 0)),
        out_shape=jax.ShapeDtypeStruct((T, MEM_WIDTH), BF16),
        compiler_params=_cp("parallel", "arbitrary"),
        name="mem_attention",
    )(za, k_m, v_m, gq)


def _merge_kernel(h_ref, a_ref, p_ref, m_ref, g0_ref, g1_ref, g2_ref, w0_ref, w1_ref, w2_ref, o_ref):
    h = h_ref[...]
    acc = None
    for x_ref, g_ref, w_ref in ((a_ref, g0_ref, w0_ref), (p_ref, g1_ref, w1_ref), (m_ref, g2_ref, w2_ref)):
        logit = jnp.dot(h, g_ref[0], preferred_element_type=F32)
        gate = 1.0 / (1.0 + jnp.exp(-logit))
        br = jnp.dot(x_ref[...], w_ref[0], preferred_element_type=F32)
        acc = gate * br if acc is None else acc + gate * br
    o_ref[...] = acc.astype(o_ref.dtype)


def merge_branches(h, a, p, m, w_all, w_a, w_p, w_m, l, tm=1024, tn=512):
    T, D = h.shape
    W = a.shape[1]
    nj = D // tn
    g0 = ZA_WIDTH // tn
    row = lambda width: pl.BlockSpec((tm, width), lambda i, j: (i, 0))
    gate_spec = lambda b: pl.BlockSpec((1, D, tn), lambda i, j, b=b: (l, 0, g0 + b * nj + j))
    out_w = pl.BlockSpec((1, W, tn), lambda i, j: (l, 0, j))
    return pl.pallas_call(
        _merge_kernel,
        grid=(T // tm, nj),
        in_specs=[row(D), row(W), row(W), row(W),
                  gate_spec(0), gate_spec(1), gate_spec(2), out_w, out_w, out_w],
        out_specs=pl.BlockSpec((tm, tn), lambda i, j: (i, j)),
        out_shape=jax.ShapeDtypeStruct((T, D), BF16),
        compiler_params=_cp("parallel", "arbitrary"),
        name="merge_branches",
    )(h, a, p, m, w_all, w_all, w_all, w_a, w_p, w_m)


def _w_in_prep_kernel(w_ref, o_ref):
    s1 = Q_LORA_RANK + KV_LORA_RANK + QK_ROPE_DIM
    s2 = s1 + POOL_WIDTH
    s3 = s2 + MEM_WIDTH
    tk = w_ref.shape[2]
    s1_up = -(-s1 // LANES) * LANES
    o_ref[0, :, ZA_POOL:ZA_POOL + POOL_WIDTH] = w_ref[0, s1:s2, :].T.astype(BF16)
    o_ref[0, :, ZA_QMEM:ZA_QMEM + MEM_WIDTH] = w_ref[0, s2:s3, :].T.astype(BF16)
    low = w_ref[0, 0:s1_up, :].T.astype(BF16)
    n_c = Q_LORA_RANK + KV_LORA_RANK
    o_ref[0, :, ZA_CQ:ZA_CQ + n_c] = low[:, 0:n_c]
    o_ref[0, :, ZA_KPE:ZA_WIDTH] = jnp.zeros((tk, ZA_WIDTH - ZA_KPE), BF16)
    o_ref[0, :, ZA_KPE:ZA_KPE + ROPE_HALF] = low[:, n_c:n_c + ROPE_HALF]
    o_ref[0, :, ZA_KPE + LANES // 2:ZA_KPE + LANES // 2 + ROPE_HALF] = low[:, n_c + ROPE_HALF:s1]
    o_ref[0, :, ZA_WIDTH:] = w_ref[0, s3:, :].T.astype(BF16)


def prep_w_in(w_in, tk=256):
    L, D, win = w_in.shape
    wout = ZA_WIDTH + N_BRANCHES * D
    return pl.pallas_call(
        _w_in_prep_kernel,
        grid=(L, D // tk),
        in_specs=[pl.BlockSpec((1, win, tk), lambda l, i: (l, 0, i))],
        out_specs=pl.BlockSpec((1, tk, wout), lambda l, i: (l, i, 0)),
        out_shape=jax.ShapeDtypeStruct((L, D, wout), BF16),
        compiler_params=_cp("parallel", "parallel"),
        name="prep_w_in",
    )(jnp.swapaxes(w_in, 1, 2))


def _swiglu_accumulate(xb, wg_ref, wu_ref, wd_refs, o_ref):
    a = jnp.dot(xb, wg_ref[0], preferred_element_type=F32)
    b = jnp.dot(xb, wu_ref[0], preferred_element_type=F32)
    act = (a / (1.0 + jnp.exp(-a)) * b).astype(BF16)
    width = o_ref.shape[1] // len(wd_refs)
    for n, wd_ref in enumerate(wd_refs):
        o_ref[:, n * width:(n + 1) * width] += jnp.dot(act, wd_ref[0], preferred_element_type=F32)


def _dense_swiglu_kernel(x_ref, wg_ref, wu_ref, wd_ref, r_ref, o_ref):
    @pl.when(pl.program_id(1) == 0)
    def _():
        o_ref[...] = r_ref[...]

    _swiglu_accumulate(x_ref[...], wg_ref, wu_ref, (wd_ref,), o_ref)


def dense_swiglu(x, w_gate, w_up, w_down, residual, tm=1024, tf=512):
    T, D = x.shape
    F = w_gate.shape[2]
    tm = min(tm, T)
    return pl.pallas_call(
        _dense_swiglu_kernel,
        grid=(T // tm, F // tf),
        in_specs=[pl.BlockSpec((tm, D), lambda i, f: (i, 0)),
                  pl.BlockSpec((1, D, tf), lambda i, f: (0, 0, f)),
                  pl.BlockSpec((1, D, tf), lambda i, f: (0, 0, f)),
                  pl.BlockSpec((1, tf, D), lambda i, f: (0, f, 0)),
                  pl.BlockSpec((tm, D), lambda i, f: (i, 0), pipeline_mode=pl.Buffered(1))],
        out_specs=pl.BlockSpec((tm, D), lambda i, f: (i, 0)),
        out_shape=jax.ShapeDtypeStruct((T, D), F32),
        compiler_params=_cp("parallel", "arbitrary"),
        name="dense_swiglu",
    )(x, w_gate, w_up, w_down, residual)


def _row_copy(src_hbm, dst_ref, src_row, dst_row, sem):
    return pltpu.make_async_copy(src_hbm.at[pl.ds(src_row, 1)], dst_ref.at[pl.ds(dst_row, 1)], sem)


def _moe_rows_per_step(tm, nf):
    return -(-tm // (nf * SUBLANES)) * SUBLANES


def _moe_swiglu_kernel(tok_ref, be_ref, nu_ref, parts_ref, h_hbm, wg_ref, wu_ref, wd0_ref, wd1_ref, o_ref,
                       xg_ref, xb_ref, sem, *, tm, nf):
    del be_ref
    i = pl.program_id(0)
    f = pl.program_id(1)
    n_used = nu_ref[0]
    rows_per_step = _moe_rows_per_step(tm, nf)
    n_fetch = rows_per_step * nf

    def fetch(block, r, slot):
        tok = tok_ref[block * tm + jnp.minimum(r, tm - 1)]
        return _row_copy(h_hbm, xg_ref.at[slot], tok, r, sem.at[slot])

    def wait_rows(slot, n_rows):
        rows = xg_ref.at[slot, pl.ds(0, n_rows)]
        pltpu.make_async_copy(rows, rows, sem.at[slot]).wait()

    @pl.when((i >= n_used) & (f == 0))
    def _():
        o_ref[...] = jnp.zeros_like(o_ref)

    @pl.when(i < n_used)
    def _():
        slot = i % 2

        @pl.when((i == 0) & (f == 0))
        def _():
            def start(r, c):
                fetch(0, r, 0).start()
                return c
            lax.fori_loop(0, n_fetch, start, 0)
            for j in range(rows_per_step):
                fetch(jnp.minimum(1, n_used - 1), j, 1).start()

        @pl.when(f == 0)
        def _():
            wait_rows(slot, n_fetch)
            xb_ref[...] = xg_ref[slot, 0:tm, :].astype(BF16)
            o_ref[...] = jnp.zeros_like(o_ref)

        def fetch_next_rows():
            wrap = (f == nf - 1).astype(jnp.int32)
            chunk = (f + 1) * (1 - wrap)
            target = jnp.minimum(i + 1 + wrap, n_used - 1)
            target_slot = (i + 1 + wrap) % 2
            for j in range(rows_per_step):
                fetch(target, chunk * rows_per_step + j, target_slot).start()

        quarters = parts_ref[i]
        for nq in range(MOE_ROW_PARTS, 0, -1):
            rows = nq * tm // MOE_ROW_PARTS

            @pl.when(quarters == nq)
            def _(rows=rows):
                fetch_next_rows()
                _swiglu_accumulate(xb_ref[0:rows, :], wg_ref, wu_ref, (wd0_ref, wd1_ref),
                                   o_ref.at[pl.ds(0, rows)])

        @pl.when((i == n_used - 1) & (f == nf - 1))
        def _():
            wait_rows(1 - slot, n_fetch)
            wait_rows(slot, rows_per_step)


def moe_swiglu(h, slot_tok, block_e, n_used, block_parts, w_gate, w_up, w_down_lo, w_down_hi, tm, tf=1024):
    n_slots = slot_tok.shape[0]
    D = h.shape[1]
    F = w_gate.shape[2]
    while F % tf:
        tf //= 2
    nf = F // tf
    xg_rows = _moe_rows_per_step(tm, nf) * nf
    blk = lambda i, nu: jnp.minimum(i, nu[0] - 1)
    chunk = lambda i, f, nu: jnp.where(i < nu[0], f, nf - 1)
    up_spec = pl.BlockSpec((1, D, tf), lambda i, f, tok, be, nu, hf: (be[blk(i, nu)], 0, chunk(i, f, nu)))
    down_spec = pl.BlockSpec((1, tf, D // 2),
                             lambda i, f, tok, be, nu, hf: (be[blk(i, nu)], chunk(i, f, nu), 0))
    return pl.pallas_call(
        functools.partial(_moe_swiglu_kernel, tm=tm, nf=nf),
        grid_spec=pltpu.PrefetchScalarGridSpec(
            num_scalar_prefetch=4,
            grid=(n_slots // tm, nf),
            in_specs=[pl.BlockSpec(memory_space=pl.ANY), up_spec, up_spec, down_spec, down_spec],
            out_specs=pl.BlockSpec((tm, D), lambda i, f, tok, be, nu, hf: (i, 0)),
            scratch_shapes=[pltpu.VMEM((2, xg_rows, D), F32),
                            pltpu.VMEM((tm, D), BF16),
                            pltpu.SemaphoreType.DMA((2,))]),
        out_shape=jax.ShapeDtypeStruct((n_slots, D), F32),
        compiler_params=_cp("arbitrary", "arbitrary"),
        name="moe_swiglu",
    )(slot_tok, block_e, n_used, block_parts, h, w_gate, w_up, w_down_lo, w_down_hi)


def _combine_kernel(dest_ref, y_ref, gate_ref, ys_hbm, o_ref, buf_ref, sem, *, tb):
    i = pl.program_id(0)

    def fetch(step, slot):
        base = step * tb

        def start(r, c):
            for k in range(TOP_K):
                _row_copy(ys_hbm, buf_ref.at[slot, k], dest_ref[TOP_K * (base + r) + k], r, sem.at[slot]).start()
            return c

        lax.fori_loop(0, tb, start, 0, unroll=16)

    @pl.when(i == 0)
    def _():
        fetch(0, 0)

    @pl.when(i + 1 < pl.num_programs(0))
    def _():
        fetch(i + 1, (i + 1) % 2)

    slot = i % 2
    pltpu.make_async_copy(buf_ref.at[slot], buf_ref.at[slot], sem.at[slot]).wait()
    g = gate_ref[...]
    o_ref[...] = y_ref[...] + g[:, 0:1] * buf_ref[slot, 0] + g[:, 1:2] * buf_ref[slot, 1]


def combine_rows(y, gates, ys, dest, tb=256):
    T, D = y.shape
    return pl.pallas_call(
        functools.partial(_combine_kernel, tb=tb),
        grid_spec=pltpu.PrefetchScalarGridSpec(
            num_scalar_prefetch=1,
            grid=(T // tb,),
            in_specs=[pl.BlockSpec((tb, D), lambda i, d: (i, 0)),
                      pl.BlockSpec((tb, LANES), lambda i, d: (i, 0)),
                      pl.BlockSpec(memory_space=pl.ANY)],
            out_specs=pl.BlockSpec((tb, D), lambda i, d: (i, 0)),
            scratch_shapes=[pltpu.VMEM((2, TOP_K, tb, D), F32),
                            pltpu.SemaphoreType.DMA((2,))]),
        out_shape=jax.ShapeDtypeStruct((T, D), F32),
        compiler_params=_cp("arbitrary"),
        name="moe_combine",
    )(dest, y, gates, ys)


def _routing_tables(top_idx, T, block):
    A = T * TOP_K
    E = N_EXPERTS
    flat_e = top_idx.reshape(A)
    onehot = (flat_e[:, None] == jnp.arange(E, dtype=jnp.int32)[None, :]).astype(jnp.int32)
    csum = jnp.cumsum(onehot, axis=0)
    counts = csum[-1]
    rank = jnp.sum((csum - 1) * onehot, axis=1)
    padded = (counts + block - 1) // block * block
    pends = jnp.cumsum(padded)
    pstarts = pends - padded
    starts = jnp.cumsum(counts) - counts
    dest = (pstarts[flat_e] + rank).astype(jnp.int32)
    n_blocks = A // block + E
    n_slots = n_blocks * block
    order = jnp.argsort(flat_e, stable=True).astype(jnp.int32)
    slot = jnp.arange(n_slots, dtype=jnp.int32)
    slot_e = jnp.clip(jnp.searchsorted(pends, slot, side='right'), 0, E - 1).astype(jnp.int32)
    r = slot - pstarts[slot_e]
    valid = r < counts[slot_e]
    src = jnp.clip(starts[slot_e] + r, 0, A - 1)
    slot_tok = jnp.where(valid, order[src] // TOP_K, 0).astype(jnp.int32)
    block_e = slot_e[::block]
    n_used = (pends[-1:] // block).astype(jnp.int32)
    in_block = jnp.clip(counts[block_e] - (slot[::block] - pstarts[block_e]), 1, block)
    part = block // MOE_ROW_PARTS
    block_parts = ((in_block + part - 1) // part).astype(jnp.int32)
    return slot_tok, dest, block_e, n_used, block_parts


def _token_mixer(h, mem2d, tab, B, S, M, l, casts, w_in, mla_q_a_norm_g, mla_w_uq, mla_kv_a_norm_g, mla_w_ukv,
                 mla_q_norm_g, mla_k_norm_g, mla_w_out, pool_w, pool_scale, pool_w_out,
                 mem_norm_g, mem_w_kv, mem_q_norm_g, mem_k_norm_g, mem_w_out):
    za = matmul(h, w_in, l, ZA_WIDTH, F32, tn=ZA_WIDTH // 2)

    head_pad = lambda v: jnp.concatenate([v[..., :QK_NOPE_DIM], _rope_lanes(v[..., QK_NOPE_DIM:])], axis=-1)
    wuq = head_pad(mla_w_uq[l].reshape(Q_LORA_RANK, MLA_HEADS, QK_HEAD_DIM)).reshape(
        Q_LORA_RANK, MLA_HEADS * HEAD_PAD)
    wukv = mla_w_ukv[l].reshape(KV_LORA_RANK, MLA_HEADS, QK_NOPE_DIM + V_HEAD_DIM)
    wuk = wukv[:, :, :QK_NOPE_DIM].reshape(KV_LORA_RANK, MLA_WIDTH)
    wuv = wukv[:, :, QK_NOPE_DIM:].reshape(KV_LORA_RANK, MLA_WIDTH)
    pad_g = lambda g: head_pad(g).reshape(1, HEAD_PAD)
    q, k, v = mla_prep(za, tab, wuq.astype(BF16), wuk.astype(BF16), wuv.astype(BF16),
                       mla_q_a_norm_g[l].reshape(1, -1), mla_kv_a_norm_g[l].reshape(1, -1),
                       pad_g(mla_q_norm_g[l]), pad_g(mla_k_norm_g[l]))
    attn, *cast_out = causal_attention(q, k, v, B, S, casts)

    mixed = pool_mixer(za, pool_w[l].astype(BF16), pool_scale[l].reshape(1, -1), B, S)

    k_m, v_m = mem_kv(mem2d, mem_norm_g[l].reshape(1, -1), mem_w_kv[l].astype(BF16),
                      mem_k_norm_g[l].reshape(1, -1), B, M)
    o_mem = mem_attention(za, k_m, v_m, mem_q_norm_g[l].reshape(1, -1), B, S, M)

    merged = merge_branches(h, attn, mixed, o_mem, w_in, mla_w_out, pool_w_out, mem_w_out, l)
    return merged, cast_out


def kernel(x, mem, positions, attn_norm_g, w_in, mla_q_a_norm_g, mla_w_uq, mla_kv_a_norm_g, mla_w_ukv, mla_q_norm_g, mla_k_norm_g, mla_w_out, pool_w, pool_scale, pool_w_out, mem_norm_g, mem_w_kv, mem_q_norm_g, mem_k_norm_g, mem_w_out, w_o, ffn_norm_g, dense_w_gate, dense_w_up, dense_w_down, router_w, router_b, moe_w_gate, moe_w_up, moe_w_down):
    B, S, D = x.shape
    M = mem.shape[1]
    T = B * S
    depth = attn_norm_g.shape[0]
    moe_block = 512
    y = x.reshape(T, D)
    mem2d = mem.reshape(B * M, D)
    tab = rope_tables(positions)
    w_in = prep_w_in(w_in)
    mla_w_out, pool_w_out, mem_w_out, w_o = (w.astype(BF16) for w in (mla_w_out, pool_w_out, mem_w_out, w_o))
    moe_bf16 = {}
    for l in range(depth):
        h = rmsnorm(y, attn_norm_g[l], BF16)
        i = l // 2
        casts = ()
        if l % 2 == 0 and l + 1 < depth:
            casts = ((moe_w_gate[i], 0, 1), (moe_w_down[i], 0, 2))
        elif l % 2 == 1:
            casts = ((moe_w_up[i], 0, 1), (moe_w_down[i], 1, 2))
        merged, cast_out = _token_mixer(h, mem2d, tab, B, S, M, l, casts, w_in, mla_q_a_norm_g, mla_w_uq,
                                        mla_kv_a_norm_g, mla_w_ukv, mla_q_norm_g, mla_k_norm_g, mla_w_out,
                                        pool_w, pool_scale, pool_w_out, mem_norm_g, mem_w_kv, mem_q_norm_g,
                                        mem_k_norm_g, mem_w_out)
        if casts:
            names = ("gate", "down_lo") if l % 2 == 0 else ("up", "down_hi")
            moe_bf16.update(zip(names, cast_out))
        if l % 2 == 0:
            y, h = matmul_residual_norm(merged, w_o, l, y, ffn_norm_g[l])
            y = dense_swiglu(h, dense_w_gate[i:i + 1].astype(BF16), dense_w_up[i:i + 1].astype(BF16),
                             dense_w_down[i:i + 1].astype(BF16), y)
        else:
            y = matmul(merged, w_o, l, D, F32, residual=y, tm=512, tn=D)
            hf, idx, gates = norm_router(y, ffn_norm_g[l], router_w[i], router_b[i])
            slot_tok, dest, block_e, n_used, block_parts = _routing_tables(idx[:, :TOP_K], T, moe_block)
            ys = moe_swiglu(hf, slot_tok, block_e, n_used, block_parts, moe_bf16.pop("gate"), moe_bf16.pop("up"),
                            moe_bf16.pop("down_lo"), moe_bf16.pop("down_hi"), tm=moe_block)
            y = combine_rows(y, gates, ys, dest)
    return y.reshape(B, S, D)
```

```python
import functools

import numpy as np
import jax
import jax.numpy as jnp
from jax import lax
from jax.experimental import pallas as pl
from jax.experimental.pallas import tpu as pltpu

F32 = jnp.float32
BF16 = jnp.bfloat16

MEM_HEADS = 4
MEM_HEAD_DIM = 256
MEM_WIDTH = MEM_HEADS * MEM_HEAD_DIM
MLA_HEADS = 8
Q_LORA_RANK = 512
KV_LORA_RANK = 256
QK_NOPE_DIM = 128
QK_ROPE_DIM = 64
QK_HEAD_DIM = QK_NOPE_DIM + QK_ROPE_DIM
V_HEAD_DIM = 128
MLA_WIDTH = MLA_HEADS * V_HEAD_DIM
ROPE_THETA = 10000.0
POOL_WINDOWS = (2, 4, 8, 16)
POOL_GROUPS = 4
POOL_GROUP_DIM = 256
POOL_WIDTH = POOL_GROUPS * POOL_GROUP_DIM
N_BRANCHES = 3
N_EXPERTS = 8
TOP_K = 2
NORM_EPS = 1e-6
LOG2_E = 1.4426950408889634

LANES = 128
SUBLANES = 8
HEAD_PAD = 2 * LANES
ROPE_HALF = QK_ROPE_DIM // 2
POOL_HALO = 16

ZA_POOL = 0
ZA_QMEM = POOL_WIDTH
ZA_CQ = ZA_QMEM + MEM_WIDTH
ZA_CKV = ZA_CQ + Q_LORA_RANK
ZA_KPE = ZA_CKV + KV_LORA_RANK
ZA_WIDTH = 3072

MOE_ROW_PARTS = 4
VMEM_LIMIT = 52 * 1024 * 1024


def _cp(*sem):
    return pltpu.CompilerParams(dimension_semantics=sem, vmem_limit_bytes=VMEM_LIMIT)


def _rms(x, eps_dim):
    return lax.rsqrt(jnp.sum(x * x, axis=-1, keepdims=True) * (1.0 / eps_dim) + NORM_EPS)


def _rmsnorm_kernel(x_ref, g_ref, o_ref):
    x = x_ref[...]
    o_ref[...] = (x * _rms(x, x.shape[-1]) * g_ref[...]).astype(o_ref.dtype)


def rmsnorm(x, g, out_dtype, tm=512):
    T, D = x.shape
    return pl.pallas_call(
        _rmsnorm_kernel,
        grid=(T // tm,),
        in_specs=[pl.BlockSpec((tm, D), lambda i: (i, 0)),
                  pl.BlockSpec((1, D), lambda i: (0, 0))],
        out_specs=pl.BlockSpec((tm, D), lambda i: (i, 0)),
        out_shape=jax.ShapeDtypeStruct((T, D), out_dtype),
        compiler_params=_cp("parallel"),
        name="rmsnorm",
    )(x, g.reshape(1, D))


def _norm_router_kernel(x_ref, g_ref, rwh_ref, rwl_ref, rb_ref, h_ref, idx_ref, gate_ref):
    x = x_ref[...]
    h = x * _rms(x, x.shape[-1]) * g_ref[...]
    h_ref[...] = h
    h_hi = h.astype(BF16)
    h_lo = (h - h_hi.astype(F32)).astype(BF16)
    logits = (jnp.dot(h_hi, rwh_ref[...], preferred_element_type=F32)
              + jnp.dot(h_lo, rwh_ref[...], preferred_element_type=F32)
              + jnp.dot(h_hi, rwl_ref[...], preferred_element_type=F32)) + rb_ref[...]
    lane = lax.broadcasted_iota(jnp.int32, logits.shape, 1)
    neg = -jnp.inf
    l1 = jnp.where(lane < N_EXPERTS, logits, neg)
    m1 = jnp.max(l1, axis=-1, keepdims=True)
    i1 = jnp.min(jnp.where(l1 == m1, lane, LANES), axis=-1, keepdims=True)
    l2 = jnp.where(lane == i1, neg, l1)
    m2 = jnp.max(l2, axis=-1, keepdims=True)
    i2 = jnp.min(jnp.where(l2 == m2, lane, LANES), axis=-1, keepdims=True)
    e = jnp.exp(m2 - m1)
    g1 = 1.0 / (1.0 + e)
    g2 = e / (1.0 + e)
    idx_ref[...] = jnp.where(lane == 0, i1, jnp.where(lane == 1, i2, 0))
    gate_ref[...] = jnp.where(lane == 0, g1, jnp.where(lane == 1, g2, 0.0))


def norm_router(x, g, router_w, router_b, tm=1024):
    T, D = x.shape
    tm = min(tm, T)
    E = router_w.shape[1]
    rw = jnp.zeros((D, LANES), F32).at[:, :E].set(router_w)
    rw_hi = rw.astype(BF16)
    rw_lo = (rw - rw_hi.astype(F32)).astype(BF16)
    rb = jnp.zeros((1, LANES), F32).at[0, :E].set(router_b)
    return pl.pallas_call(
        _norm_router_kernel,
        grid=(T // tm,),
        in_specs=[pl.BlockSpec((tm, D), lambda i: (i, 0)),
                  pl.BlockSpec((1, D), lambda i: (0, 0)),
                  pl.BlockSpec((D, LANES), lambda i: (0, 0)),
                  pl.BlockSpec((D, LANES), lambda i: (0, 0)),
                  pl.BlockSpec((1, LANES), lambda i: (0, 0))],
        out_specs=[pl.BlockSpec((tm, D), lambda i: (i, 0)),
                   pl.BlockSpec((tm, LANES), lambda i: (i, 0)),
                   pl.BlockSpec((tm, LANES), lambda i: (i, 0))],
        out_shape=[jax.ShapeDtypeStruct((T, D), F32),
                   jax.ShapeDtypeStruct((T, LANES), jnp.int32),
                   jax.ShapeDtypeStruct((T, LANES), F32)],
        compiler_params=_cp("parallel"),
        name="norm_router",
    )(x, g.reshape(1, D), rw_hi, rw_lo, rb)


def _mm_kernel(a_ref, w_ref, o_ref):
    o_ref[...] = jnp.dot(a_ref[...], w_ref[0], preferred_element_type=F32).astype(o_ref.dtype)


def _mm_res_kernel(a_ref, w_ref, r_ref, o_ref):
    o_ref[...] = (r_ref[...] + jnp.dot(a_ref[...], w_ref[0], preferred_element_type=F32)).astype(o_ref.dtype)


def matmul(a, w, l, N, out_dtype, residual=None, tm=1024, tn=1024):
    M, K = a.shape
    tm, tn = min(tm, M), min(tn, N)
    in_specs = [pl.BlockSpec((tm, K), lambda i, j: (i, 0)),
                pl.BlockSpec((1, K, tn), lambda i, j: (l, 0, j))]
    args = [a, w]
    body = _mm_kernel
    if residual is not None:
        in_specs.append(pl.BlockSpec((tm, tn), lambda i, j: (i, j)))
        args.append(residual)
        body = _mm_res_kernel
    return pl.pallas_call(
        body,
        grid=(M // tm, N // tn),
        in_specs=in_specs,
        out_specs=pl.BlockSpec((tm, tn), lambda i, j: (i, j)),
        out_shape=jax.ShapeDtypeStruct((M, N), out_dtype),
        compiler_params=_cp("parallel", "arbitrary"),
        name="matmul",
    )(*args)


def _mm_res_norm_kernel(a_ref, w_ref, r_ref, g_ref, o_ref, h_ref):
    y = r_ref[...] + jnp.dot(a_ref[...], w_ref[0], preferred_element_type=F32)
    o_ref[...] = y
    h_ref[...] = (y * _rms(y, y.shape[-1]) * g_ref[...]).astype(h_ref.dtype)


def matmul_residual_norm(a, w, l, residual, g, tm=512):
    M, K = a.shape
    N = w.shape[2]
    rows = lambda width: pl.BlockSpec((tm, width), lambda i: (i, 0))
    return pl.pallas_call(
        _mm_res_norm_kernel,
        grid=(M // tm,),
        in_specs=[rows(K), pl.BlockSpec((1, K, N), lambda i: (l, 0, 0)), rows(N),
                  pl.BlockSpec((1, N), lambda i: (0, 0))],
        out_specs=[rows(N), rows(N)],
        out_shape=[jax.ShapeDtypeStruct((M, N), F32), jax.ShapeDtypeStruct((M, N), BF16)],
        compiler_params=_cp("parallel"),
        name="matmul_residual_norm",
    )(a, w, residual, g.reshape(1, N))


def _rope_lanes(v):
    z = jnp.zeros(v.shape[:-1] + (LANES // 2 - ROPE_HALF,), v.dtype)
    return jnp.concatenate([v[..., :ROPE_HALF], z, v[..., ROPE_HALF:], z], axis=-1)


def _rope_table_kernel(pos_ref, invf_ref, o_ref):
    ang = pos_ref[...].astype(F32) * invf_ref[...]
    lane = lax.broadcasted_iota(jnp.int32, ang.shape, 1)
    first = lane < ROPE_HALF
    second = (lane >= LANES // 2) & (lane < LANES // 2 + ROPE_HALF)
    c = jnp.cos(ang)
    s = jnp.sin(ang)
    o_ref[:, 0:LANES] = jnp.where(first | second, c, 0.0)
    o_ref[:, LANES:2 * LANES] = jnp.where(first, -s, jnp.where(second, s, 0.0))


def rope_tables(positions, tm=2048):
    T = positions.size
    tm = min(tm, T)
    inv_freq = ROPE_THETA ** (-np.arange(0, QK_ROPE_DIM, 2, dtype=np.float32) / QK_ROPE_DIM)
    invf = np.zeros((1, LANES), np.float32)
    invf[0, :ROPE_HALF] = inv_freq
    invf[0, LANES // 2:LANES // 2 + ROPE_HALF] = inv_freq
    return pl.pallas_call(
        _rope_table_kernel,
        grid=(T // tm,),
        in_specs=[pl.BlockSpec((tm, 1), lambda i: (i, 0)),
                  pl.BlockSpec((1, LANES), lambda i: (0, 0))],
        out_specs=pl.BlockSpec((tm, 2 * LANES), lambda i: (i, 0)),
        out_shape=jax.ShapeDtypeStruct((T, 2 * LANES), F32),
        compiler_params=_cp("parallel"),
        name="rope_tables",
    )(positions.reshape(T, 1), jnp.asarray(invf))


def _rope(x, tab):
    return x * tab[:, 0:LANES] + pltpu.roll(x, LANES // 2, 1) * tab[:, LANES:2 * LANES]


def _mla_prep_kernel(cq_ref, ckv_ref, kpe_ref, tab_ref, wuq_ref, wuk_ref, wuv_ref,
                     gqa_ref, gkva_ref, gq_ref, gk_ref, q_ref, k_ref, v_ref):
    tab = tab_ref[...]
    cq = cq_ref[...]
    cqn = (cq * _rms(cq, Q_LORA_RANK) * gqa_ref[...]).astype(BF16)
    qf = jnp.dot(cqn, wuq_ref[...], preferred_element_type=F32)
    ckv = ckv_ref[...]
    ckvn = (ckv * _rms(ckv, KV_LORA_RANK) * gkva_ref[...]).astype(BF16)
    kn = jnp.dot(ckvn, wuk_ref[...], preferred_element_type=F32)
    v_ref[...] = jnp.dot(ckvn, wuv_ref[...], preferred_element_type=F32).astype(v_ref.dtype)
    gq = gq_ref[...]
    gk = gk_ref[...]
    kpe = kpe_ref[...]
    ss_pe = jnp.sum(kpe * kpe, axis=-1, keepdims=True)
    kpe_rot = _rope(kpe * gk[:, LANES:], tab)
    scale = QK_HEAD_DIM ** -0.5 * LOG2_E
    for h in range(MLA_HEADS):
        q0 = qf[:, h * HEAD_PAD:h * HEAD_PAD + LANES]
        q1 = qf[:, h * HEAD_PAD + LANES:(h + 1) * HEAD_PAD]
        ss = jnp.sum(q0 * q0, axis=-1, keepdims=True) + jnp.sum(q1 * q1, axis=-1, keepdims=True)
        rq = lax.rsqrt(ss * (1.0 / QK_HEAD_DIM) + NORM_EPS) * scale
        q_ref[:, h * HEAD_PAD:h * HEAD_PAD + LANES] = (q0 * gq[:, :LANES] * rq).astype(q_ref.dtype)
        q_ref[:, h * HEAD_PAD + LANES:(h + 1) * HEAD_PAD] = (_rope(q1 * gq[:, LANES:], tab) * rq).astype(q_ref.dtype)
        k0 = kn[:, h * LANES:(h + 1) * LANES]
        ssk = jnp.sum(k0 * k0, axis=-1, keepdims=True) + ss_pe
        rk = lax.rsqrt(ssk * (1.0 / QK_HEAD_DIM) + NORM_EPS)
        k_ref[:, h * HEAD_PAD:h * HEAD_PAD + LANES] = (k0 * gk[:, :LANES] * rk).astype(k_ref.dtype)
        k_ref[:, h * HEAD_PAD + LANES:(h + 1) * HEAD_PAD] = (kpe_rot * rk).astype(k_ref.dtype)


def mla_prep(za, tab, wuq, wuk, wuv, gqa, gkva, gq, gk, tm=1024):
    T = za.shape[0]
    tm = min(tm, T)
    HP = MLA_HEADS * HEAD_PAD
    full = lambda shape: pl.BlockSpec(shape, lambda i: (0, 0))
    return pl.pallas_call(
        _mla_prep_kernel,
        grid=(T // tm,),
        in_specs=[pl.BlockSpec((tm, Q_LORA_RANK), lambda i: (i, ZA_CQ // Q_LORA_RANK)),
                  pl.BlockSpec((tm, KV_LORA_RANK), lambda i: (i, ZA_CKV // KV_LORA_RANK)),
                  pl.BlockSpec((tm, LANES), lambda i: (i, ZA_KPE // LANES)),
                  pl.BlockSpec((tm, 2 * LANES), lambda i: (i, 0)),
                  full((Q_LORA_RANK, HP)), full((KV_LORA_RANK, MLA_WIDTH)), full((KV_LORA_RANK, MLA_WIDTH)),
                  full((1, Q_LORA_RANK)), full((1, KV_LORA_RANK)), full((1, HEAD_PAD)), full((1, HEAD_PAD))],
        out_specs=[pl.BlockSpec((tm, HP), lambda i: (i, 0)),
                   pl.BlockSpec((tm, HP), lambda i: (i, 0)),
                   pl.BlockSpec((tm, MLA_WIDTH), lambda i: (i, 0))],
        out_shape=[jax.ShapeDtypeStruct((T, HP), BF16),
                   jax.ShapeDtypeStruct((T, HP), BF16),
                   jax.ShapeDtypeStruct((T, MLA_WIDTH), BF16)],
        compiler_params=_cp("parallel"),
        name="mla_prep",
    )(za, za, za, tab, wuq, wuk, wuv, gqa, gkva, gq, gk)


def _causal_attn_kernel(q_ref, k_ref, v_ref, *rest, tq, n_cast):
    cast_src = rest[:n_cast]
    o_ref = rest[n_cast]
    cast_dst = rest[n_cast + 1:]
    for s_ref, d_ref in zip(cast_src, cast_dst):
        d_ref[...] = s_ref[...].astype(d_ref.dtype)
    S = q_ref.shape[0]
    row = lax.broadcasted_iota(jnp.int32, (tq, tq), 0)
    col = lax.broadcasted_iota(jnp.int32, (tq, tq), 1)
    tri = col <= row
    for qi in range(S // tq):
        kl = (qi + 1) * tq
        q = q_ref[qi * tq:kl, :]
        s = lax.dot_general(q, k_ref[0:kl, :], (((1,), (1,)), ((), ())), preferred_element_type=F32)
        s_diag = jnp.where(tri, s[:, kl - tq:kl], -jnp.inf)
        if qi > 0:
            s = jnp.concatenate([s[:, :kl - tq], s_diag], axis=1)
        else:
            s = s_diag
        m = jnp.max(s, axis=-1, keepdims=True)
        p = jnp.exp2(s - m)
        l = jnp.sum(p, axis=-1, keepdims=True)
        o = jnp.dot(p.astype(BF16), v_ref[0:kl, :], preferred_element_type=F32)
        o_ref[qi * tq:kl, :] = (o / l).astype(o_ref.dtype)


def causal_attention(q, k, v, B, S, casts=(), tq=256):
    T = q.shape[0]
    n_steps = B * MLA_HEADS
    in_specs = [pl.BlockSpec((S, HEAD_PAD), lambda b, h: (b, h)),
                pl.BlockSpec((S, HEAD_PAD), lambda b, h: (b, h)),
                pl.BlockSpec((S, V_HEAD_DIM), lambda b, h: (b, h))]
    out_specs = [pl.BlockSpec((S, V_HEAD_DIM), lambda b, h: (b, h))]
    out_shape = [jax.ShapeDtypeStruct((T, MLA_WIDTH), BF16)]
    for src, part, n_parts in casts:
        E, rows, cols = src.shape
        spe = n_steps // E
        rb, cb = rows // spe, cols // n_parts
        assert spe * E == n_steps and rb * spe == rows and rb % 16 == 0
        assert cb * n_parts == cols and cb % LANES == 0
        step = lambda b, h: b * MLA_HEADS + h
        in_specs.append(pl.BlockSpec((1, rb, cb), lambda b, h, part=part, spe=spe:
                                     (step(b, h) // spe, step(b, h) % spe, part)))
        out_specs.append(pl.BlockSpec((1, rb, cb), lambda b, h, spe=spe:
                                      (step(b, h) // spe, step(b, h) % spe, 0)))
        out_shape.append(jax.ShapeDtypeStruct((E, rows, cb), BF16))
    return pl.pallas_call(
        functools.partial(_causal_attn_kernel, tq=tq, n_cast=len(casts)),
        grid=(B, MLA_HEADS),
        in_specs=in_specs,
        out_specs=out_specs,
        out_shape=out_shape,
        compiler_params=_cp("parallel", "parallel"),
        name="causal_attention",
    )(q, k, v, *(c[0] for c in casts))


def _pool_kernel(u_ref, w_ref, sc_ref, o_ref, pad_ref, *, rc):
    S = u_ref.shape[0]
    C = POOL_GROUP_DIM
    t1 = lax.broadcasted_iota(jnp.int32, (rc, C), 0).astype(F32) + 1.0
    for g, win in enumerate(POOL_WINDOWS):
        pad_ref[0:POOL_HALO, :] = jnp.zeros((POOL_HALO, C), F32)
        pad_ref[POOL_HALO:POOL_HALO + S, :] = u_ref[:, g * C:(g + 1) * C]
        for r0 in range(0, S, rc):
            acc = pad_ref[POOL_HALO + r0:POOL_HALO + r0 + rc, :]
            tok = acc
            for kk in range(1, win):
                acc = acc + pad_ref[POOL_HALO + r0 - kk:POOL_HALO + r0 - kk + rc, :]
            cnt = jnp.minimum(t1 + float(r0), float(win))
            pooled = (acc / cnt - tok).astype(BF16)
            mixed = jnp.dot(pooled, w_ref[g], preferred_element_type=F32) * sc_ref[:, g * C:(g + 1) * C]
            o_ref[r0:r0 + rc, g * C:(g + 1) * C] = mixed.astype(o_ref.dtype)


def pool_mixer(za, pool_w, pool_scale, B, S, rc=512):
    T = za.shape[0]
    rc = min(rc, S)
    return pl.pallas_call(
        functools.partial(_pool_kernel, rc=rc),
        grid=(B,),
        in_specs=[pl.BlockSpec((S, POOL_WIDTH), lambda b: (b, ZA_POOL // POOL_WIDTH)),
                  pl.BlockSpec((POOL_GROUPS, POOL_GROUP_DIM, POOL_GROUP_DIM), lambda b: (0, 0, 0)),
                  pl.BlockSpec((1, POOL_WIDTH), lambda b: (0, 0))],
        out_specs=pl.BlockSpec((S, POOL_WIDTH), lambda b: (b, 0)),
        out_shape=jax.ShapeDtypeStruct((T, POOL_WIDTH), BF16),
        scratch_shapes=[pltpu.VMEM((POOL_HALO + S, POOL_GROUP_DIM), F32)],
        compiler_params=_cp("parallel"),
        name="pool_mixer",
    )(za, pool_w, pool_scale)


def _mem_kv_kernel(m_ref, g_ref, w_ref, gk_ref, k_ref, v_ref):
    x = m_ref[...]
    xn = (x * _rms(x, x.shape[-1]) * g_ref[...]).astype(BF16)
    kv = jnp.dot(xn, w_ref[...], preferred_element_type=F32)
    for h in range(MEM_HEADS):
        kh = kv[:, h * MEM_HEAD_DIM:(h + 1) * MEM_HEAD_DIM]
        k_ref[:, h * MEM_HEAD_DIM:(h + 1) * MEM_HEAD_DIM] = (
            kh * _rms(kh, MEM_HEAD_DIM) * gk_ref[...]).astype(k_ref.dtype)
    v_ref[...] = kv[:, MEM_WIDTH:].astype(v_ref.dtype)


def mem_kv(mem2d, g, w_kv, gk, B, M):
    D = mem2d.shape[1]
    return pl.pallas_call(
        _mem_kv_kernel,
        grid=(B,),
        in_specs=[pl.BlockSpec((M, D), lambda b: (b, 0)),
                  pl.BlockSpec((1, D), lambda b: (0, 0)),
                  pl.BlockSpec((D, 2 * MEM_WIDTH), lambda b: (0, 0)),
                  pl.BlockSpec((1, MEM_HEAD_DIM), lambda b: (0, 0))],
        out_specs=[pl.BlockSpec((M, MEM_WIDTH), lambda b: (b, 0)),
                   pl.BlockSpec((M, MEM_WIDTH), lambda b: (b, 0))],
        out_shape=[jax.ShapeDtypeStruct((B * M, MEM_WIDTH), BF16),
                   jax.ShapeDtypeStruct((B * M, MEM_WIDTH), BF16)],
        compiler_params=_cp("parallel"),
        name="mem_kv",
    )(mem2d, g, w_kv, gk)


def _mem_attn_kernel(q_ref, k_ref, v_ref, gq_ref, o_ref):
    scale = MEM_HEAD_DIM ** -0.5 * LOG2_E
    for h in range(MEM_HEADS):
        sl = slice(h * MEM_HEAD_DIM, (h + 1) * MEM_HEAD_DIM)
        qh = q_ref[:, sl]
        qn = (qh * (_rms(qh, MEM_HEAD_DIM) * scale) * gq_ref[...]).astype(BF16)
        s = lax.dot_general(qn, k_ref[:, sl], (((1,), (1,)), ((), ())), preferred_element_type=F32)
        m = jnp.max(s, axis=-1, keepdims=True)
        p = jnp.exp2(s - m)
        l = jnp.sum(p, axis=-1, keepdims=True)
        o = jnp.dot(p.astype(BF16), v_ref[:, sl], preferred_element_type=F32)
        o_ref[:, sl] = (o / l).astype(o_ref.dtype)


def mem_attention(za, k_m, v_m, gq, B, S, M, tq=1024):
    T = za.shape[0]
    tq = min(tq, S)
    nq = S // tq
    return pl.pallas_call(
        _mem_attn_kernel,
        grid=(B, nq),
        in_specs=[pl.BlockSpec((tq, MEM_WIDTH), lambda b, i: (b * nq + i, ZA_QMEM // MEM_WIDTH)),
                  pl.BlockSpec((M, MEM_WIDTH), lambda b, i: (b, 0)),
                  pl.BlockSpec((M, MEM_WIDTH), lambda b, i: (b, 0)),
                  pl.BlockSpec((1, MEM_HEAD_DIM), lambda b, i: (0, 0))],
        out_specs=pl.BlockSpec((tq, MEM_WIDTH), lambda b, i: (b * nq + i, 0)),
        out_shape=jax.ShapeDtypeStruct((T, MEM_WIDTH), BF16),
        compiler_params=_cp("parallel", "arbitrary"),
        name="mem_attention",
    )(za, k_m, v_m, gq)


def _merge_kernel(h_ref, a_ref, p_ref, m_ref, g0_ref, g1_ref, g2_ref, w0_ref, w1_ref, w2_ref, o_ref):
    h = h_ref[...]
    acc = None
    for x_ref, g_ref, w_ref in ((a_ref, g0_ref, w0_ref), (p_ref, g1_ref, w1_ref), (m_ref, g2_ref, w2_ref)):
        logit = jnp.dot(h, g_ref[0], preferred_element_type=F32)
        gate = 1.0 / (1.0 + jnp.exp(-logit))
        br = jnp.dot(x_ref[...], w_ref[0], preferred_element_type=F32)
        acc = gate * br if acc is None else acc + gate * br
    o_ref[...] = acc.astype(o_ref.dtype)


def merge_branches(h, a, p, m, w_all, w_a, w_p, w_m, l, tm=1024, tn=512):
    T, D = h.shape
    W = a.shape[1]
    nj = D // tn
    g0 = ZA_WIDTH // tn
    row = lambda width: pl.BlockSpec((tm, width), lambda i, j: (i, 0))
    gate_spec = lambda b: pl.BlockSpec((1, D, tn), lambda i, j, b=b: (l, 0, g0 + b * nj + j))
    out_w = pl.BlockSpec((1, W, tn), lambda i, j: (l, 0, j))
    return pl.pallas_call(
        _merge_kernel,
        grid=(T // tm, nj),
        in_specs=[row(D), row(W), row(W), row(W),
                  gate_spec(0), gate_spec(1), gate_spec(2), out_w, out_w, out_w],
        out_specs=pl.BlockSpec((tm, tn), lambda i, j: (i, j)),
        out_shape=jax.ShapeDtypeStruct((T, D), BF16),
        compiler_params=_cp("parallel", "arbitrary"),
        name="merge_branches",
    )(h, a, p, m, w_all, w_all, w_all, w_a, w_p, w_m)


def _w_in_prep_kernel(w_ref, o_ref):
    s1 = Q_LORA_RANK + KV_LORA_RANK + QK_ROPE_DIM
    s2 = s1 + POOL_WIDTH
    s3 = s2 + MEM_WIDTH
    tk = w_ref.shape[2]
    s1_up = -(-s1 // LANES) * LANES
    o_ref[0, :, ZA_POOL:ZA_POOL + POOL_WIDTH] = w_ref[0, s1:s2, :].T.astype(BF16)
    o_ref[0, :, ZA_QMEM:ZA_QMEM + MEM_WIDTH] = w_ref[0, s2:s3, :].T.astype(BF16)
    low = w_ref[0, 0:s1_up, :].T.astype(BF16)
    n_c = Q_LORA_RANK + KV_LORA_RANK
    o_ref[0, :, ZA_CQ:ZA_CQ + n_c] = low[:, 0:n_c]
    o_ref[0, :, ZA_KPE:ZA_WIDTH] = jnp.zeros((tk, ZA_WIDTH - ZA_KPE), BF16)
    o_ref[0, :, ZA_KPE:ZA_KPE + ROPE_HALF] = low[:, n_c:n_c + ROPE_HALF]
    o_ref[0, :, ZA_KPE + LANES // 2:ZA_KPE + LANES // 2 + ROPE_HALF] = low[:, n_c + ROPE_HALF:s1]
    o_ref[0, :, ZA_WIDTH:] = w_ref[0, s3:, :].T.astype(BF16)


def prep_w_in(w_in, tk=256):
    L, D, win = w_in.shape
    wout = ZA_WIDTH + N_BRANCHES * D
    return pl.pallas_call(
        _w_in_prep_kernel,
        grid=(L, D // tk),
        in_specs=[pl.BlockSpec((1, win, tk), lambda l, i: (l, 0, i))],
        out_specs=pl.BlockSpec((1, tk, wout), lambda l, i: (l, i, 0)),
        out_shape=jax.ShapeDtypeStruct((L, D, wout), BF16),
        compiler_params=_cp("parallel", "parallel"),
        name="prep_w_in",
    )(jnp.swapaxes(w_in, 1, 2))


def _swiglu_accumulate(xb, wg_ref, wu_ref, wd_refs, o_ref):
    a = jnp.dot(xb, wg_ref[0], preferred_element_type=F32)
    b = jnp.dot(xb, wu_ref[0], preferred_element_type=F32)
    act = (a / (1.0 + jnp.exp(-a)) * b).astype(BF16)
    width = o_ref.shape[1] // len(wd_refs)
    for n, wd_ref in enumerate(wd_refs):
        o_ref[:, n * width:(n + 1) * width] += jnp.dot(act, wd_ref[0], preferred_element_type=F32)


def _dense_swiglu_kernel(x_ref, wg_ref, wu_ref, wd_ref, r_ref, o_ref):
    @pl.when(pl.program_id(1) == 0)
    def _():
        o_ref[...] = r_ref[...]

    _swiglu_accumulate(x_ref[...], wg_ref, wu_ref, (wd_ref,), o_ref)


def dense_swiglu(x, w_gate, w_up, w_down, residual, tm=1024, tf=512):
    T, D = x.shape
    F = w_gate.shape[2]
    tm = min(tm, T)
    return pl.pallas_call(
        _dense_swiglu_kernel,
        grid=(T // tm, F // tf),
        in_specs=[pl.BlockSpec((tm, D), lambda i, f: (i, 0)),
                  pl.BlockSpec((1, D, tf), lambda i, f: (0, 0, f)),
                  pl.BlockSpec((1, D, tf), lambda i, f: (0, 0, f)),
                  pl.BlockSpec((1, tf, D), lambda i, f: (0, f, 0)),
                  pl.BlockSpec((tm, D), lambda i, f: (i, 0), pipeline_mode=pl.Buffered(1))],
        out_specs=pl.BlockSpec((tm, D), lambda i, f: (i, 0)),
        out_shape=jax.ShapeDtypeStruct((T, D), F32),
        compiler_params=_cp("parallel", "arbitrary"),
        name="dense_swiglu",
    )(x, w_gate, w_up, w_down, residual)


def _row_copy(src_hbm, dst_ref, src_row, dst_row, sem):
    return pltpu.make_async_copy(src_hbm.at[pl.ds(src_row, 1)], dst_ref.at[pl.ds(dst_row, 1)], sem)


def _moe_rows_per_step(tm, nf):
    return -(-tm // (nf * SUBLANES)) * SUBLANES


def _moe_swiglu_kernel(tok_ref, be_ref, nu_ref, parts_ref, h_hbm, wg_ref, wu_ref, wd0_ref, wd1_ref, o_ref,
                       xg_ref, xb_ref, sem, *, tm, nf):
    del be_ref
    i = pl.program_id(0)
    f = pl.program_id(1)
    n_used = nu_ref[0]
    rows_per_step = _moe_rows_per_step(tm, nf)
    n_fetch = rows_per_step * nf

    def fetch(block, r, slot):
        tok = tok_ref[block * tm + jnp.minimum(r, tm - 1)]
        return _row_copy(h_hbm, xg_ref.at[slot], tok, r, sem.at[slot])

    def wait_rows(slot, n_rows):
        rows = xg_ref.at[slot, pl.ds(0, n_rows)]
        pltpu.make_async_copy(rows, rows, sem.at[slot]).wait()

    @pl.when((i >= n_used) & (f == 0))
    def _():
        o_ref[...] = jnp.zeros_like(o_ref)

    @pl.when(i < n_used)
    def _():
        slot = i % 2

        @pl.when((i == 0) & (f == 0))
        def _():
            def start(r, c):
                fetch(0, r, 0).start()
                return c
            lax.fori_loop(0, n_fetch, start, 0)
            for j in range(rows_per_step):
                fetch(jnp.minimum(1, n_used - 1), j, 1).start()

        @pl.when(f == 0)
        def _():
            wait_rows(slot, n_fetch)
            xb_ref[...] = xg_ref[slot, 0:tm, :].astype(BF16)
            o_ref[...] = jnp.zeros_like(o_ref)

        def fetch_next_rows():
            wrap = (f == nf - 1).astype(jnp.int32)
            chunk = (f + 1) * (1 - wrap)
            target = jnp.minimum(i + 1 + wrap, n_used - 1)
            target_slot = (i + 1 + wrap) % 2
            for j in range(rows_per_step):
                fetch(target, chunk * rows_per_step + j, target_slot).start()

        quarters = parts_ref[i]
        for nq in range(MOE_ROW_PARTS, 0, -1):
            rows = nq * tm // MOE_ROW_PARTS

            @pl.when(quarters == nq)
            def _(rows=rows):
                fetch_next_rows()
                _swiglu_accumulate(xb_ref[0:rows, :], wg_ref, wu_ref, (wd0_ref, wd1_ref),
                                   o_ref.at[pl.ds(0, rows)])

        @pl.when((i == n_used - 1) & (f == nf - 1))
        def _():
            wait_rows(1 - slot, n_fetch)
            wait_rows(slot, rows_per_step)


def moe_swiglu(h, slot_tok, block_e, n_used, block_parts, w_gate, w_up, w_down_lo, w_down_hi, tm, tf=1024):
    n_slots = slot_tok.shape[0]
    D = h.shape[1]
    F = w_gate.shape[2]
    while F % tf:
        tf //= 2
    nf = F // tf
    xg_rows = _moe_rows_per_step(tm, nf) * nf
    blk = lambda i, nu: jnp.minimum(i, nu[0] - 1)
    chunk = lambda i, f, nu: jnp.where(i < nu[0], f, nf - 1)
    up_spec = pl.BlockSpec((1, D, tf), lambda i, f, tok, be, nu, hf: (be[blk(i, nu)], 0, chunk(i, f, nu)))
    down_spec = pl.BlockSpec((1, tf, D // 2),
                             lambda i, f, tok, be, nu, hf: (be[blk(i, nu)], chunk(i, f, nu), 0))
    return pl.pallas_call(
        functools.partial(_moe_swiglu_kernel, tm=tm, nf=nf),
        grid_spec=pltpu.PrefetchScalarGridSpec(
            num_scalar_prefetch=4,
            grid=(n_slots // tm, nf),
            in_specs=[pl.BlockSpec(memory_space=pl.ANY), up_spec, up_spec, down_spec, down_spec],
            out_specs=pl.BlockSpec((tm, D), lambda i, f, tok, be, nu, hf: (i, 0)),
            scratch_shapes=[pltpu.VMEM((2, xg_rows, D), F32),
                            pltpu.VMEM((tm, D), BF16),
                            pltpu.SemaphoreType.DMA((2,))]),
        out_shape=jax.ShapeDtypeStruct((n_slots, D), F32),
        compiler_params=_cp("arbitrary", "arbitrary"),
        name="moe_swiglu",
    )(slot_tok, block_e, n_used, block_parts, h, w_gate, w_up, w_down_lo, w_down_hi)


def _combine_kernel(dest_ref, y_ref, gate_ref, ys_hbm, o_ref, buf_ref, sem, *, tb):
    i = pl.program_id(0)

    def fetch(step, slot):
        base = step * tb

        def start(r, c):
            for k in range(TOP_K):
                _row_copy(ys_hbm, buf_ref.at[slot, k], dest_ref[TOP_K * (base + r) + k], r, sem.at[slot]).start()
            return c

        lax.fori_loop(0, tb, start, 0, unroll=16)

    @pl.when(i == 0)
    def _():
        fetch(0, 0)

    @pl.when(i + 1 < pl.num_programs(0))
    def _():
        fetch(i + 1, (i + 1) % 2)

    slot = i % 2
    pltpu.make_async_copy(buf_ref.at[slot], buf_ref.at[slot], sem.at[slot]).wait()
    g = gate_ref[...]
    o_ref[...] = y_ref[...] + g[:, 0:1] * buf_ref[slot, 0] + g[:, 1:2] * buf_ref[slot, 1]


def combine_rows(y, gates, ys, dest, tb=512):
    T, D = y.shape
    tb = min(tb, T)
    return pl.pallas_call(
        functools.partial(_combine_kernel, tb=tb),
        grid_spec=pltpu.PrefetchScalarGridSpec(
            num_scalar_prefetch=1,
            grid=(T // tb,),
            in_specs=[pl.BlockSpec((tb, D), lambda i, d: (i, 0)),
                      pl.BlockSpec((tb, LANES), lambda i, d: (i, 0)),
                      pl.BlockSpec(memory_space=pl.ANY)],
            out_specs=pl.BlockSpec((tb, D), lambda i, d: (i, 0)),
            scratch_shapes=[pltpu.VMEM((2, TOP_K, tb, D), F32),
                            pltpu.SemaphoreType.DMA((2,))]),
        out_shape=jax.ShapeDtypeStruct((T, D), F32),
        compiler_params=_cp("arbitrary"),
        name="moe_combine",
    )(dest, y, gates, ys)


def _routing_tables(top_idx, T, block):
    A = T * TOP_K
    E = N_EXPERTS
    flat_e = top_idx.reshape(A)
    onehot = (flat_e[:, None] == jnp.arange(E, dtype=jnp.int32)[None, :]).astype(jnp.int32)
    csum = jnp.cumsum(onehot, axis=0)
    counts = csum[-1]
    rank = jnp.sum((csum - 1) * onehot, axis=1)
    padded = (counts + block - 1) // block * block
    pends = jnp.cumsum(padded)
    pstarts = pends - padded
    starts = jnp.cumsum(counts) - counts
    dest = (pstarts[flat_e] + rank).astype(jnp.int32)
    n_blocks = A // block + E
    n_slots = n_blocks * block
    order = jnp.argsort(flat_e, stable=True).astype(jnp.int32)
    slot = jnp.arange(n_slots, dtype=jnp.int32)
    slot_e = jnp.clip(jnp.searchsorted(pends, slot, side='right'), 0, E - 1).astype(jnp.int32)
    r = slot - pstarts[slot_e]
    valid = r < counts[slot_e]
    src = jnp.clip(starts[slot_e] + r, 0, A - 1)
    slot_tok = jnp.where(valid, order[src] // TOP_K, 0).astype(jnp.int32)
    block_e = slot_e[::block]
    n_used = (pends[-1:] // block).astype(jnp.int32)
    in_block = jnp.clip(counts[block_e] - (slot[::block] - pstarts[block_e]), 1, block)
    part = block // MOE_ROW_PARTS
    block_parts = ((in_block + part - 1) // part).astype(jnp.int32)
    return slot_tok, dest, block_e, n_used, block_parts


def _token_mixer(h, mem2d, tab, B, S, M, l, casts, w_in, mla_q_a_norm_g, mla_w_uq, mla_kv_a_norm_g, mla_w_ukv,
                 mla_q_norm_g, mla_k_norm_g, mla_w_out, pool_w, pool_scale, pool_w_out,
                 mem_norm_g, mem_w_kv, mem_q_norm_g, mem_k_norm_g, mem_w_out):
    za = matmul(h, w_in, l, ZA_WIDTH, F32, tn=ZA_WIDTH // 2)

    head_pad = lambda v: jnp.concatenate([v[..., :QK_NOPE_DIM], _rope_lanes(v[..., QK_NOPE_DIM:])], axis=-1)
    wuq = head_pad(mla_w_uq[l].reshape(Q_LORA_RANK, MLA_HEADS, QK_HEAD_DIM)).reshape(
        Q_LORA_RANK, MLA_HEADS * HEAD_PAD)
    wukv = mla_w_ukv[l].reshape(KV_LORA_RANK, MLA_HEADS, QK_NOPE_DIM + V_HEAD_DIM)
    wuk = wukv[:, :, :QK_NOPE_DIM].reshape(KV_LORA_RANK, MLA_WIDTH)
    wuv = wukv[:, :, QK_NOPE_DIM:].reshape(KV_LORA_RANK, MLA_WIDTH)
    pad_g = lambda g: head_pad(g).reshape(1, HEAD_PAD)
    q, k, v = mla_prep(za, tab, wuq.astype(BF16), wuk.astype(BF16), wuv.astype(BF16),
                       mla_q_a_norm_g[l].reshape(1, -1), mla_kv_a_norm_g[l].reshape(1, -1),
                       pad_g(mla_q_norm_g[l]), pad_g(mla_k_norm_g[l]))
    attn, *cast_out = causal_attention(q, k, v, B, S, casts)

    mixed = pool_mixer(za, pool_w[l].astype(BF16), pool_scale[l].reshape(1, -1), B, S)

    k_m, v_m = mem_kv(mem2d, mem_norm_g[l].reshape(1, -1), mem_w_kv[l].astype(BF16),
                      mem_k_norm_g[l].reshape(1, -1), B, M)
    o_mem = mem_attention(za, k_m, v_m, mem_q_norm_g[l].reshape(1, -1), B, S, M)

    merged = merge_branches(h, attn, mixed, o_mem, w_in, mla_w_out, pool_w_out, mem_w_out, l)
    return merged, cast_out


def kernel(x, mem, positions, attn_norm_g, w_in, mla_q_a_norm_g, mla_w_uq, mla_kv_a_norm_g, mla_w_ukv, mla_q_norm_g, mla_k_norm_g, mla_w_out, pool_w, pool_scale, pool_w_out, mem_norm_g, mem_w_kv, mem_q_norm_g, mem_k_norm_g, mem_w_out, w_o, ffn_norm_g, dense_w_gate, dense_w_up, dense_w_down, router_w, router_b, moe_w_gate, moe_w_up, moe_w_down):
    B, S, D = x.shape
    M = mem.shape[1]
    T = B * S
    depth = attn_norm_g.shape[0]
    moe_block = 512
    y = x.reshape(T, D)
    mem2d = mem.reshape(B * M, D)
    tab = rope_tables(positions)
    w_in = prep_w_in(w_in)
    mla_w_out, pool_w_out, mem_w_out, w_o = (w.astype(BF16) for w in (mla_w_out, pool_w_out, mem_w_out, w_o))
    moe_bf16 = {}
    for l in range(depth):
        h = rmsnorm(y, attn_norm_g[l], BF16)
        i = l // 2
        casts = ()
        if l % 2 == 0 and l + 1 < depth:
            casts = ((moe_w_gate[i], 0, 1), (moe_w_down[i], 0, 2))
        elif l % 2 == 1:
            casts = ((moe_w_up[i], 0, 1), (moe_w_down[i], 1, 2))
        merged, cast_out = _token_mixer(h, mem2d, tab, B, S, M, l, casts, w_in, mla_q_a_norm_g, mla_w_uq,
                                        mla_kv_a_norm_g, mla_w_ukv, mla_q_norm_g, mla_k_norm_g, mla_w_out,
                                        pool_w, pool_scale, pool_w_out, mem_norm_g, mem_w_kv, mem_q_norm_g,
                                        mem_k_norm_g, mem_w_out)
        if casts:
            names = ("gate", "down_lo") if l % 2 == 0 else ("up", "down_hi")
            moe_bf16.update(zip(names, cast_out))
        if l % 2 == 0:
            y, h = matmul_residual_norm(merged, w_o, l, y, ffn_norm_g[l])
            y = dense_swiglu(h, dense_w_gate[i:i + 1].astype(BF16), dense_w_up[i:i + 1].astype(BF16),
                             dense_w_down[i:i + 1].astype(BF16), y)
        else:
            y = matmul(merged, w_o, l, D, F32, residual=y, tm=512, tn=D)
            hf, idx, gates = norm_router(y, ffn_norm_g[l], router_w[i], router_b[i])
            slot_tok, dest, block_e, n_used, block_parts = _routing_tables(idx[:, :TOP_K], T, moe_block)
            ys = moe_swiglu(hf, slot_tok, block_e, n_used, block_parts, moe_bf16.pop("gate"), moe_bf16.pop("up"),
                            moe_bf16.pop("down_lo"), moe_bf16.pop("down_hi"), tm=moe_block)
            y = combine_rows(y, gates, ys, dest)
    return y.reshape(B, S, D)
```

```python
import functools

import numpy as np
import jax
import jax.numpy as jnp
from jax import lax
from jax.experimental import pallas as pl
from jax.experimental.pallas import tpu as pltpu

F32 = jnp.float32
BF16 = jnp.bfloat16

MEM_HEADS = 4
MEM_HEAD_DIM = 256
MEM_WIDTH = MEM_HEADS * MEM_HEAD_DIM
MLA_HEADS = 8
Q_LORA_RANK = 512
KV_LORA_RANK = 256
QK_NOPE_DIM = 128
QK_ROPE_DIM = 64
QK_HEAD_DIM = QK_NOPE_DIM + QK_ROPE_DIM
V_HEAD_DIM = 128
MLA_WIDTH = MLA_HEADS * V_HEAD_DIM
ROPE_THETA = 10000.0
POOL_WINDOWS = (2, 4, 8, 16)
POOL_GROUPS = 4
POOL_GROUP_DIM = 256
POOL_WIDTH = POOL_GROUPS * POOL_GROUP_DIM
N_BRANCHES = 3
N_EXPERTS = 8
TOP_K = 2
NORM_EPS = 1e-6
LOG2_E = 1.4426950408889634

LANES = 128
SUBLANES = 8
HEAD_PAD = 2 * LANES
ROPE_HALF = QK_ROPE_DIM // 2
POOL_HALO = 16

ZA_POOL = 0
ZA_QMEM = POOL_WIDTH
ZA_CQ = ZA_QMEM + MEM_WIDTH
ZA_CKV = ZA_CQ + Q_LORA_RANK
ZA_KPE = ZA_CKV + KV_LORA_RANK
ZA_WIDTH = 3072

MOE_ROW_PARTS = 4
VMEM_LIMIT = 52 * 1024 * 1024


def _cp(*sem):
    return pltpu.CompilerParams(dimension_semantics=sem, vmem_limit_bytes=VMEM_LIMIT)


def _rms(x, eps_dim):
    return lax.rsqrt(jnp.sum(x * x, axis=-1, keepdims=True) * (1.0 / eps_dim) + NORM_EPS)


def _rmsnorm_kernel(x_ref, g_ref, o_ref):
    x = x_ref[...]
    o_ref[...] = (x * _rms(x, x.shape[-1]) * g_ref[...]).astype(o_ref.dtype)


def rmsnorm(x, g, out_dtype, tm=512):
    T, D = x.shape
    return pl.pallas_call(
        _rmsnorm_kernel,
        grid=(T // tm,),
        in_specs=[pl.BlockSpec((tm, D), lambda i: (i, 0)),
                  pl.BlockSpec((1, D), lambda i: (0, 0))],
        out_specs=pl.BlockSpec((tm, D), lambda i: (i, 0)),
        out_shape=jax.ShapeDtypeStruct((T, D), out_dtype),
        compiler_params=_cp("parallel"),
        name="rmsnorm",
    )(x, g.reshape(1, D))


def _norm_router_kernel(x_ref, g_ref, rwh_ref, rwl_ref, rb_ref, h_ref, idx_ref, gate_ref):
    x = x_ref[...]
    h = x * _rms(x, x.shape[-1]) * g_ref[...]
    h_ref[...] = h
    h_hi = h.astype(BF16)
    h_lo = (h - h_hi.astype(F32)).astype(BF16)
    logits = (jnp.dot(h_hi, rwh_ref[...], preferred_element_type=F32)
              + jnp.dot(h_lo, rwh_ref[...], preferred_element_type=F32)
              + jnp.dot(h_hi, rwl_ref[...], preferred_element_type=F32)) + rb_ref[...]
    lane = lax.broadcasted_iota(jnp.int32, logits.shape, 1)
    neg = -jnp.inf
    l1 = jnp.where(lane < N_EXPERTS, logits, neg)
    m1 = jnp.max(l1, axis=-1, keepdims=True)
    i1 = jnp.min(jnp.where(l1 == m1, lane, LANES), axis=-1, keepdims=True)
    l2 = jnp.where(lane == i1, neg, l1)
    m2 = jnp.max(l2, axis=-1, keepdims=True)
    i2 = jnp.min(jnp.where(l2 == m2, lane, LANES), axis=-1, keepdims=True)
    e = jnp.exp(m2 - m1)
    g1 = 1.0 / (1.0 + e)
    g2 = e / (1.0 + e)
    idx_ref[...] = jnp.where(lane == 0, i1, jnp.where(lane == 1, i2, 0))
    gate_ref[...] = jnp.where(lane == 0, g1, jnp.where(lane == 1, g2, 0.0))


def norm_router(x, g, router_w, router_b, tm=1024):
    T, D = x.shape
    tm = min(tm, T)
    E = router_w.shape[1]
    rw = jnp.zeros((D, LANES), F32).at[:, :E].set(router_w)
    rw_hi = rw.astype(BF16)
    rw_lo = (rw - rw_hi.astype(F32)).astype(BF16)
    rb = jnp.zeros((1, LANES), F32).at[0, :E].set(router_b)
    return pl.pallas_call(
        _norm_router_kernel,
        grid=(T // tm,),
        in_specs=[pl.BlockSpec((tm, D), lambda i: (i, 0)),
                  pl.BlockSpec((1, D), lambda i: (0, 0)),
                  pl.BlockSpec((D, LANES), lambda i: (0, 0)),
                  pl.BlockSpec((D, LANES), lambda i: (0, 0)),
                  pl.BlockSpec((1, LANES), lambda i: (0, 0))],
        out_specs=[pl.BlockSpec((tm, D), lambda i: (i, 0)),
                   pl.BlockSpec((tm, LANES), lambda i: (i, 0)),
                   pl.BlockSpec((tm, LANES), lambda i: (i, 0))],
        out_shape=[jax.ShapeDtypeStruct((T, D), F32),
                   jax.ShapeDtypeStruct((T, LANES), jnp.int32),
                   jax.ShapeDtypeStruct((T, LANES), F32)],
        compiler_params=_cp("parallel"),
        name="norm_router",
    )(x, g.reshape(1, D), rw_hi, rw_lo, rb)


def _mm_kernel(a_ref, w_ref, o_ref):
    o_ref[...] = jnp.dot(a_ref[...], w_ref[0], preferred_element_type=F32).astype(o_ref.dtype)


def _mm_res_kernel(a_ref, w_ref, r_ref, o_ref):
    o_ref[...] = (r_ref[...] + jnp.dot(a_ref[...], w_ref[0], preferred_element_type=F32)).astype(o_ref.dtype)


def matmul(a, w, l, N, out_dtype, residual=None, tm=1024, tn=1024):
    M, K = a.shape
    tm, tn = min(tm, M), min(tn, N)
    in_specs = [pl.BlockSpec((tm, K), lambda i, j: (i, 0)),
                pl.BlockSpec((1, K, tn), lambda i, j: (l, 0, j))]
    args = [a, w]
    body = _mm_kernel
    if residual is not None:
        in_specs.append(pl.BlockSpec((tm, tn), lambda i, j: (i, j)))
        args.append(residual)
        body = _mm_res_kernel
    return pl.pallas_call(
        body,
        grid=(M // tm, N // tn),
        in_specs=in_specs,
        out_specs=pl.BlockSpec((tm, tn), lambda i, j: (i, j)),
        out_shape=jax.ShapeDtypeStruct((M, N), out_dtype),
        compiler_params=_cp("parallel", "arbitrary"),
        name="matmul",
    )(*args)


def _mm_res_norm_kernel(a_ref, w_ref, r_ref, g_ref, o_ref, h_ref):
    y = r_ref[...] + jnp.dot(a_ref[...], w_ref[0], preferred_element_type=F32)
    o_ref[...] = y
    h_ref[...] = (y * _rms(y, y.shape[-1]) * g_ref[...]).astype(h_ref.dtype)


def matmul_residual_norm(a, w, l, residual, g, tm=512):
    M, K = a.shape
    N = w.shape[2]
    rows = lambda width: pl.BlockSpec((tm, width), lambda i: (i, 0))
    return pl.pallas_call(
        _mm_res_norm_kernel,
        grid=(M // tm,),
        in_specs=[rows(K), pl.BlockSpec((1, K, N), lambda i: (l, 0, 0)), rows(N),
                  pl.BlockSpec((1, N), lambda i: (0, 0))],
        out_specs=[rows(N), rows(N)],
        out_shape=[jax.ShapeDtypeStruct((M, N), F32), jax.ShapeDtypeStruct((M, N), BF16)],
        compiler_params=_cp("parallel"),
        name="matmul_residual_norm",
    )(a, w, residual, g.reshape(1, N))


def _rope_lanes(v):
    z = jnp.zeros(v.shape[:-1] + (LANES // 2 - ROPE_HALF,), v.dtype)
    return jnp.concatenate([v[..., :ROPE_HALF], z, v[..., ROPE_HALF:], z], axis=-1)


def _rope_table_kernel(pos_ref, invf_ref, o_ref):
    ang = pos_ref[...].astype(F32) * invf_ref[...]
    lane = lax.broadcasted_iota(jnp.int32, ang.shape, 1)
    first = lane < ROPE_HALF
    second = (lane >= LANES // 2) & (lane < LANES // 2 + ROPE_HALF)
    c = jnp.cos(ang)
    s = jnp.sin(ang)
    o_ref[:, 0:LANES] = jnp.where(first | second, c, 0.0)
    o_ref[:, LANES:2 * LANES] = jnp.where(first, -s, jnp.where(second, s, 0.0))


def rope_tables(positions, tm=2048):
    T = positions.size
    tm = min(tm, T)
    inv_freq = ROPE_THETA ** (-np.arange(0, QK_ROPE_DIM, 2, dtype=np.float32) / QK_ROPE_DIM)
    invf = np.zeros((1, LANES), np.float32)
    invf[0, :ROPE_HALF] = inv_freq
    invf[0, LANES // 2:LANES // 2 + ROPE_HALF] = inv_freq
    return pl.pallas_call(
        _rope_table_kernel,
        grid=(T // tm,),
        in_specs=[pl.BlockSpec((tm, 1), lambda i: (i, 0)),
                  pl.BlockSpec((1, LANES), lambda i: (0, 0))],
        out_specs=pl.BlockSpec((tm, 2 * LANES), lambda i: (i, 0)),
        out_shape=jax.ShapeDtypeStruct((T, 2 * LANES), F32),
        compiler_params=_cp("parallel"),
        name="rope_tables",
    )(positions.reshape(T, 1), jnp.asarray(invf))


def _rope(x, tab):
    return x * tab[:, 0:LANES] + pltpu.roll(x, LANES // 2, 1) * tab[:, LANES:2 * LANES]


def _mla_prep_kernel(cq_ref, ckv_ref, kpe_ref, tab_ref, wuq_ref, wuk_ref, wuv_ref,
                     gqa_ref, gkva_ref, gq_ref, gk_ref, q_ref, k_ref, v_ref):
    tab = tab_ref[...]
    cq = cq_ref[...]
    cqn = (cq * _rms(cq, Q_LORA_RANK) * gqa_ref[...]).astype(BF16)
    qf = jnp.dot(cqn, wuq_ref[...], preferred_element_type=F32)
    ckv = ckv_ref[...]
    ckvn = (ckv * _rms(ckv, KV_LORA_RANK) * gkva_ref[...]).astype(BF16)
    kn = jnp.dot(ckvn, wuk_ref[...], preferred_element_type=F32)
    v_ref[...] = jnp.dot(ckvn, wuv_ref[...], preferred_element_type=F32).astype(v_ref.dtype)
    gq = gq_ref[...]
    gk = gk_ref[...]
    kpe = kpe_ref[...]
    ss_pe = jnp.sum(kpe * kpe, axis=-1, keepdims=True)
    kpe_rot = _rope(kpe * gk[:, LANES:], tab)
    scale = QK_HEAD_DIM ** -0.5 * LOG2_E
    for h in range(MLA_HEADS):
        q0 = qf[:, h * HEAD_PAD:h * HEAD_PAD + LANES]
        q1 = qf[:, h * HEAD_PAD + LANES:(h + 1) * HEAD_PAD]
        ss = jnp.sum(q0 * q0, axis=-1, keepdims=True) + jnp.sum(q1 * q1, axis=-1, keepdims=True)
        rq = lax.rsqrt(ss * (1.0 / QK_HEAD_DIM) + NORM_EPS) * scale
        q_ref[:, h * HEAD_PAD:h * HEAD_PAD + LANES] = (q0 * gq[:, :LANES] * rq).astype(q_ref.dtype)
        q_ref[:, h * HEAD_PAD + LANES:(h + 1) * HEAD_PAD] = (_rope(q1 * gq[:, LANES:], tab) * rq).astype(q_ref.dtype)
        k0 = kn[:, h * LANES:(h + 1) * LANES]
        ssk = jnp.sum(k0 * k0, axis=-1, keepdims=True) + ss_pe
        rk = lax.rsqrt(ssk * (1.0 / QK_HEAD_DIM) + NORM_EPS)
        k_ref[:, h * HEAD_PAD:h * HEAD_PAD + LANES] = (k0 * gk[:, :LANES] * rk).astype(k_ref.dtype)
        k_ref[:, h * HEAD_PAD + LANES:(h + 1) * HEAD_PAD] = (kpe_rot * rk).astype(k_ref.dtype)


def mla_prep(za, tab, wuq, wuk, wuv, gqa, gkva, gq, gk, tm=1024):
    T = za.shape[0]
    tm = min(tm, T)
    HP = MLA_HEADS * HEAD_PAD
    full = lambda shape: pl.BlockSpec(shape, lambda i: (0, 0))
    return pl.pallas_call(
        _mla_prep_kernel,
        grid=(T // tm,),
        in_specs=[pl.BlockSpec((tm, Q_LORA_RANK), lambda i: (i, ZA_CQ // Q_LORA_RANK)),
                  pl.BlockSpec((tm, KV_LORA_RANK), lambda i: (i, ZA_CKV // KV_LORA_RANK)),
                  pl.BlockSpec((tm, LANES), lambda i: (i, ZA_KPE // LANES)),
                  pl.BlockSpec((tm, 2 * LANES), lambda i: (i, 0)),
                  full((Q_LORA_RANK, HP)), full((KV_LORA_RANK, MLA_WIDTH)), full((KV_LORA_RANK, MLA_WIDTH)),
                  full((1, Q_LORA_RANK)), full((1, KV_LORA_RANK)), full((1, HEAD_PAD)), full((1, HEAD_PAD))],
        out_specs=[pl.BlockSpec((tm, HP), lambda i: (i, 0)),
                   pl.BlockSpec((tm, HP), lambda i: (i, 0)),
                   pl.BlockSpec((tm, MLA_WIDTH), lambda i: (i, 0))],
        out_shape=[jax.ShapeDtypeStruct((T, HP), BF16),
                   jax.ShapeDtypeStruct((T, HP), BF16),
                   jax.ShapeDtypeStruct((T, MLA_WIDTH), BF16)],
        compiler_params=_cp("parallel"),
        name="mla_prep",
    )(za, za, za, tab, wuq, wuk, wuv, gqa, gkva, gq, gk)


def _causal_attn_kernel(q_ref, k_ref, v_ref, *rest, tq, n_cast):
    cast_src = rest[:n_cast]
    o_ref = rest[n_cast]
    cast_dst = rest[n_cast + 1:]
    for s_ref, d_ref in zip(cast_src, cast_dst):
        d_ref[...] = s_ref[...].astype(d_ref.dtype)
    S = q_ref.shape[0]
    row = lax.broadcasted_iota(jnp.int32, (tq, tq), 0)
    col = lax.broadcasted_iota(jnp.int32, (tq, tq), 1)
    tri = col <= row
    for qi in range(S // tq):
        kl = (qi + 1) * tq
        q = q_ref[qi * tq:kl, :]
        s = lax.dot_general(q, k_ref[0:kl, :], (((1,), (1,)), ((), ())), preferred_element_type=F32)
        s_diag = jnp.where(tri, s[:, kl - tq:kl], -jnp.inf)
        if qi > 0:
            s = jnp.concatenate([s[:, :kl - tq], s_diag], axis=1)
        else:
            s = s_diag
        m = jnp.max(s, axis=-1, keepdims=True)
        p = jnp.exp2(s - m)
        l = jnp.sum(p, axis=-1, keepdims=True)
        o = jnp.dot(p.astype(BF16), v_ref[0:kl, :], preferred_element_type=F32)
        o_ref[qi * tq:kl, :] = (o / l).astype(o_ref.dtype)


def causal_attention(q, k, v, B, S, casts=(), tq=256):
    T = q.shape[0]
    n_steps = B * MLA_HEADS
    in_specs = [pl.BlockSpec((S, HEAD_PAD), lambda b, h: (b, h)),
                pl.BlockSpec((S, HEAD_PAD), lambda b, h: (b, h)),
                pl.BlockSpec((S, V_HEAD_DIM), lambda b, h: (b, h))]
    out_specs = [pl.BlockSpec((S, V_HEAD_DIM), lambda b, h: (b, h))]
    out_shape = [jax.ShapeDtypeStruct((T, MLA_WIDTH), BF16)]
    for src, part, n_parts in casts:
        E, rows, cols = src.shape
        spe = n_steps // E
        rb, cb = rows // spe, cols // n_parts
        assert spe * E == n_steps and rb * spe == rows and rb % 16 == 0
        assert cb * n_parts == cols and cb % LANES == 0
        step = lambda b, h: b * MLA_HEADS + h
        in_specs.append(pl.BlockSpec((1, rb, cb), lambda b, h, part=part, spe=spe:
                                     (step(b, h) // spe, step(b, h) % spe, part)))
        out_specs.append(pl.BlockSpec((1, rb, cb), lambda b, h, spe=spe:
                                      (step(b, h) // spe, step(b, h) % spe, 0)))
        out_shape.append(jax.ShapeDtypeStruct((E, rows, cb), BF16))
    return pl.pallas_call(
        functools.partial(_causal_attn_kernel, tq=tq, n_cast=len(casts)),
        grid=(B, MLA_HEADS),
        in_specs=in_specs,
        out_specs=out_specs,
        out_shape=out_shape,
        compiler_params=_cp("parallel", "parallel"),
        name="causal_attention",
    )(q, k, v, *(c[0] for c in casts))


def _pool_kernel(u_ref, w_ref, sc_ref, o_ref, pad_ref, *, rc):
    S = u_ref.shape[0]
    C = POOL_GROUP_DIM
    H = POOL_HALO
    t1 = lax.broadcasted_iota(jnp.int32, (rc, C), 0).astype(F32) + 1.0
    rows = lambda buf, r0, shift: pad_ref[buf, H + r0 - shift:H + r0 - shift + rc, :]
    for buf in range(2):
        pad_ref[buf, 0:H, :] = jnp.zeros((H, C), F32)
    for g, win in enumerate(POOL_WINDOWS):
        assert win & (win - 1) == 0 and win // 2 <= H
        pad_ref[0, H:H + S, :] = u_ref[:, g * C:(g + 1) * C]
        src, m = 0, 1
        while 2 * m < win:
            for r0 in range(0, S, rc):
                pad_ref[1 - src, H + r0:H + r0 + rc, :] = rows(src, r0, 0) + rows(src, r0, m)
            src, m = 1 - src, 2 * m
        for r0 in range(0, S, rc):
            acc = rows(src, r0, 0) + rows(src, r0, m)
            tok = u_ref[r0:r0 + rc, g * C:(g + 1) * C]
            cnt = jnp.minimum(t1 + float(r0), float(win))
            pooled = (acc / cnt - tok).astype(BF16)
            mixed = jnp.dot(pooled, w_ref[g], preferred_element_type=F32) * sc_ref[:, g * C:(g + 1) * C]
            o_ref[r0:r0 + rc, g * C:(g + 1) * C] = mixed.astype(o_ref.dtype)


def pool_mixer(za, pool_w, pool_scale, B, S, rc=512):
    T = za.shape[0]
    rc = min(rc, S)
    return pl.pallas_call(
        functools.partial(_pool_kernel, rc=rc),
        grid=(B,),
        in_specs=[pl.BlockSpec((S, POOL_WIDTH), lambda b: (b, ZA_POOL // POOL_WIDTH)),
                  pl.BlockSpec((POOL_GROUPS, POOL_GROUP_DIM, POOL_GROUP_DIM), lambda b: (0, 0, 0)),
                  pl.BlockSpec((1, POOL_WIDTH), lambda b: (0, 0))],
        out_specs=pl.BlockSpec((S, POOL_WIDTH), lambda b: (b, 0)),
        out_shape=jax.ShapeDtypeStruct((T, POOL_WIDTH), BF16),
        scratch_shapes=[pltpu.VMEM((2, POOL_HALO + S, POOL_GROUP_DIM), F32)],
        compiler_params=_cp("parallel"),
        name="pool_mixer",
    )(za, pool_w, pool_scale)


def _mem_kv_kernel(m_ref, g_ref, w_ref, gk_ref, k_ref, v_ref):
    x = m_ref[...]
    xn = (x * _rms(x, x.shape[-1]) * g_ref[...]).astype(BF16)
    kv = jnp.dot(xn, w_ref[...], preferred_element_type=F32)
    for h in range(MEM_HEADS):
        kh = kv[:, h * MEM_HEAD_DIM:(h + 1) * MEM_HEAD_DIM]
        k_ref[:, h * MEM_HEAD_DIM:(h + 1) * MEM_HEAD_DIM] = (
            kh * _rms(kh, MEM_HEAD_DIM) * gk_ref[...]).astype(k_ref.dtype)
    v_ref[...] = kv[:, MEM_WIDTH:].astype(v_ref.dtype)


def mem_kv(mem2d, g, w_kv, gk, B, M):
    D = mem2d.shape[1]
    return pl.pallas_call(
        _mem_kv_kernel,
        grid=(B,),
        in_specs=[pl.BlockSpec((M, D), lambda b: (b, 0)),
                  pl.BlockSpec((1, D), lambda b: (0, 0)),
                  pl.BlockSpec((D, 2 * MEM_WIDTH), lambda b: (0, 0)),
                  pl.BlockSpec((1, MEM_HEAD_DIM), lambda b: (0, 0))],
        out_specs=[pl.BlockSpec((M, MEM_WIDTH), lambda b: (b, 0)),
                   pl.BlockSpec((M, MEM_WIDTH), lambda b: (b, 0))],
        out_shape=[jax.ShapeDtypeStruct((B * M, MEM_WIDTH), BF16),
                   jax.ShapeDtypeStruct((B * M, MEM_WIDTH), BF16)],
        compiler_params=_cp("parallel"),
        name="mem_kv",
    )(mem2d, g, w_kv, gk)


def _mem_attn_kernel(q_ref, k_ref, v_ref, gq_ref, o_ref):
    scale = MEM_HEAD_DIM ** -0.5 * LOG2_E
    for h in range(MEM_HEADS):
        sl = slice(h * MEM_HEAD_DIM, (h + 1) * MEM_HEAD_DIM)
        qh = q_ref[:, sl]
        qn = (qh * (_rms(qh, MEM_HEAD_DIM) * scale) * gq_ref[...]).astype(BF16)
        s = lax.dot_general(qn, k_ref[:, sl], (((1,), (1,)), ((), ())), preferred_element_type=F32)
        m = jnp.max(s, axis=-1, keepdims=True)
        p = jnp.exp2(s - m)
        l = jnp.sum(p, axis=-1, keepdims=True)
        o = jnp.dot(p.astype(BF16), v_ref[:, sl], preferred_element_type=F32)
        o_ref[:, sl] = (o / l).astype(o_ref.dtype)


def mem_attention(za, k_m, v_m, gq, B, S, M, tq=1024):
    T = za.shape[0]
    tq = min(tq, S)
    nq = S // tq
    return pl.pallas_call(
        _mem_attn_kernel,
        grid=(B, nq),
        in_specs=[pl.BlockSpec((tq, MEM_WIDTH), lambda b, i: (b * nq + i, ZA_QMEM // MEM_WIDTH)),
                  pl.BlockSpec((M, MEM_WIDTH), lambda b, i: (b, 0)),
                  pl.BlockSpec((M, MEM_WIDTH), lambda b, i: (b, 0)),
                  pl.BlockSpec((1, MEM_HEAD_DIM), lambda b, i: (0, 0))],
        out_specs=pl.BlockSpec((tq, MEM_WIDTH), lambda b, i: (b * nq + i, 0)),
        out_shape=jax.ShapeDtypeStruct((T, MEM_WIDTH), BF16),
        compiler_params=_cp("parallel", "arbitrary"),
        name="mem_attention",
    )(za, k_m, v_m, gq)


def _merge_kernel(h_ref, a_ref, p_ref, m_ref, g0_ref, g1_ref, g2_ref, w0_ref, w1_ref, w2_ref, o_ref):
    h = h_ref[...]
    acc = None
    for x_ref, g_ref, w_ref in ((a_ref, g0_ref, w0_ref), (p_ref, g1_ref, w1_ref), (m_ref, g2_ref, w2_ref)):
        logit = jnp.dot(h, g_ref[0], preferred_element_type=F32)
        gate = 1.0 / (1.0 + jnp.exp(-logit))
        br = jnp.dot(x_ref[...], w_ref[0], preferred_element_type=F32)
        acc = gate * br if acc is None else acc + gate * br
    o_ref[...] = acc.astype(o_ref.dtype)


def merge_branches(h, a, p, m, w_all, w_a, w_p, w_m, l, tm=1024, tn=512):
    T, D = h.shape
    W = a.shape[1]
    nj = D // tn
    g0 = ZA_WIDTH // tn
    row = lambda width: pl.BlockSpec((tm, width), lambda i, j: (i, 0))
    gate_spec = lambda b: pl.BlockSpec((1, D, tn), lambda i, j, b=b: (l, 0, g0 + b * nj + j))
    out_w = pl.BlockSpec((1, W, tn), lambda i, j: (l, 0, j))
    return pl.pallas_call(
        _merge_kernel,
        grid=(T // tm, nj),
        in_specs=[row(D), row(W), row(W), row(W),
                  gate_spec(0), gate_spec(1), gate_spec(2), out_w, out_w, out_w],
        out_specs=pl.BlockSpec((tm, tn), lambda i, j: (i, j)),
        out_shape=jax.ShapeDtypeStruct((T, D), BF16),
        compiler_params=_cp("parallel", "arbitrary"),
        name="merge_branches",
    )(h, a, p, m, w_all, w_all, w_all, w_a, w_p, w_m)


def _w_in_prep_kernel(w_ref, o_ref):
    s1 = Q_LORA_RANK + KV_LORA_RANK + QK_ROPE_DIM
    s2 = s1 + POOL_WIDTH
    s3 = s2 + MEM_WIDTH
    tk = w_ref.shape[2]
    s1_up = -(-s1 // LANES) * LANES
    o_ref[0, :, ZA_POOL:ZA_POOL + POOL_WIDTH] = w_ref[0, s1:s2, :].T.astype(BF16)
    o_ref[0, :, ZA_QMEM:ZA_QMEM + MEM_WIDTH] = w_ref[0, s2:s3, :].T.astype(BF16)
    low = w_ref[0, 0:s1_up, :].T.astype(BF16)
    n_c = Q_LORA_RANK + KV_LORA_RANK
    o_ref[0, :, ZA_CQ:ZA_CQ + n_c] = low[:, 0:n_c]
    o_ref[0, :, ZA_KPE:ZA_WIDTH] = jnp.zeros((tk, ZA_WIDTH - ZA_KPE), BF16)
    o_ref[0, :, ZA_KPE:ZA_KPE + ROPE_HALF] = low[:, n_c:n_c + ROPE_HALF]
    o_ref[0, :, ZA_KPE + LANES // 2:ZA_KPE + LANES // 2 + ROPE_HALF] = low[:, n_c + ROPE_HALF:s1]
    o_ref[0, :, ZA_WIDTH:] = w_ref[0, s3:, :].T.astype(BF16)


def prep_w_in(w_in, tk=256):
    L, D, win = w_in.shape
    wout = ZA_WIDTH + N_BRANCHES * D
    return pl.pallas_call(
        _w_in_prep_kernel,
        grid=(L, D // tk),
        in_specs=[pl.BlockSpec((1, win, tk), lambda l, i: (l, 0, i))],
        out_specs=pl.BlockSpec((1, tk, wout), lambda l, i: (l, i, 0)),
        out_shape=jax.ShapeDtypeStruct((L, D, wout), BF16),
        compiler_params=_cp("parallel", "parallel"),
        name="prep_w_in",
    )(jnp.swapaxes(w_in, 1, 2))


def _swiglu_accumulate(xb, wg_ref, wu_ref, wd_refs, o_ref):
    a = jnp.dot(xb, wg_ref[0], preferred_element_type=F32)
    b = jnp.dot(xb, wu_ref[0], preferred_element_type=F32)
    act = (a / (1.0 + jnp.exp(-a)) * b).astype(BF16)
    width = o_ref.shape[1] // len(wd_refs)
    for n, wd_ref in enumerate(wd_refs):
        o_ref[:, n * width:(n + 1) * width] += jnp.dot(act, wd_ref[0], preferred_element_type=F32)


def _dense_swiglu_kernel(x_ref, wg_ref, wu_ref, wd_ref, r_ref, o_ref):
    @pl.when(pl.program_id(1) == 0)
    def _():
        o_ref[...] = r_ref[...]

    _swiglu_accumulate(x_ref[...], wg_ref, wu_ref, (wd_ref,), o_ref)


def dense_swiglu(x, w_gate, w_up, w_down, residual, tm=1024, tf=512):
    T, D = x.shape
    F = w_gate.shape[2]
    tm = min(tm, T)
    return pl.pallas_call(
        _dense_swiglu_kernel,
        grid=(T // tm, F // tf),
        in_specs=[pl.BlockSpec((tm, D), lambda i, f: (i, 0)),
                  pl.BlockSpec((1, D, tf), lambda i, f: (0, 0, f)),
                  pl.BlockSpec((1, D, tf), lambda i, f: (0, 0, f)),
                  pl.BlockSpec((1, tf, D), lambda i, f: (0, f, 0)),
                  pl.BlockSpec((tm, D), lambda i, f: (i, 0), pipeline_mode=pl.Buffered(1))],
        out_specs=pl.BlockSpec((tm, D), lambda i, f: (i, 0)),
        out_shape=jax.ShapeDtypeStruct((T, D), F32),
        compiler_params=_cp("parallel", "arbitrary"),
        name="dense_swiglu",
    )(x, w_gate, w_up, w_down, residual)


def _row_copy(src_hbm, dst_ref, src_row, dst_row, sem):
    return pltpu.make_async_copy(src_hbm.at[pl.ds(src_row, 1)], dst_ref.at[pl.ds(dst_row, 1)], sem)


def _moe_rows_per_step(tm, nf):
    return -(-tm // (nf * SUBLANES)) * SUBLANES


def _moe_swiglu_kernel(tok_ref, be_ref, nu_ref, parts_ref, h_hbm, wg_ref, wu_ref, wd0_ref, wd1_ref, o_ref,
                       xg_ref, xb_ref, sem, *, tm, nf):
    del be_ref
    i = pl.program_id(0)
    f = pl.program_id(1)
    n_used = nu_ref[0]
    rows_per_step = _moe_rows_per_step(tm, nf)
    n_fetch = rows_per_step * nf

    def fetch(block, r, slot):
        tok = tok_ref[block * tm + jnp.minimum(r, tm - 1)]
        return _row_copy(h_hbm, xg_ref.at[slot], tok, r, sem.at[slot])

    def wait_rows(slot, n_rows):
        rows = xg_ref.at[slot, pl.ds(0, n_rows)]
        pltpu.make_async_copy(rows, rows, sem.at[slot]).wait()

    @pl.when((i >= n_used) & (f == 0))
    def _():
        o_ref[...] = jnp.zeros_like(o_ref)

    @pl.when(i < n_used)
    def _():
        slot = i % 2

        @pl.when((i == 0) & (f == 0))
        def _():
            def start(r, c):
                fetch(0, r, 0).start()
                return c
            lax.fori_loop(0, n_fetch, start, 0)
            for j in range(rows_per_step):
                fetch(jnp.minimum(1, n_used - 1), j, 1).start()

        @pl.when(f == 0)
        def _():
            wait_rows(slot, n_fetch)
            xb_ref[...] = xg_ref[slot, 0:tm, :].astype(BF16)
            o_ref[...] = jnp.zeros_like(o_ref)

        def fetch_next_rows():
            wrap = (f == nf - 1).astype(jnp.int32)
            chunk = (f + 1) * (1 - wrap)
            target = jnp.minimum(i + 1 + wrap, n_used - 1)
            target_slot = (i + 1 + wrap) % 2
            for j in range(rows_per_step):
                fetch(target, chunk * rows_per_step + j, target_slot).start()

        quarters = parts_ref[i]
        for nq in range(MOE_ROW_PARTS, 0, -1):
            rows = nq * tm // MOE_ROW_PARTS

            @pl.when(quarters == nq)
            def _(rows=rows):
                fetch_next_rows()
                _swiglu_accumulate(xb_ref[0:rows, :], wg_ref, wu_ref, (wd0_ref, wd1_ref),
                                   o_ref.at[pl.ds(0, rows)])

        @pl.when((i == n_used - 1) & (f == nf - 1))
        def _():
            wait_rows(1 - slot, n_fetch)
            wait_rows(slot, rows_per_step)


def moe_swiglu(h, slot_tok, block_e, n_used, block_parts, w_gate, w_up, w_down_lo, w_down_hi, tm, tf=1024):
    n_slots = slot_tok.shape[0]
    D = h.shape[1]
    F = w_gate.shape[2]
    while F % tf:
        tf //= 2
    nf = F // tf
    xg_rows = _moe_rows_per_step(tm, nf) * nf
    blk = lambda i, nu: jnp.minimum(i, nu[0] - 1)
    chunk = lambda i, f, nu: jnp.where(i < nu[0], f, nf - 1)
    up_spec = pl.BlockSpec((1, D, tf), lambda i, f, tok, be, nu, hf: (be[blk(i, nu)], 0, chunk(i, f, nu)))
    down_spec = pl.BlockSpec((1, tf, D // 2),
                             lambda i, f, tok, be, nu, hf: (be[blk(i, nu)], chunk(i, f, nu), 0))
    return pl.pallas_call(
        functools.partial(_moe_swiglu_kernel, tm=tm, nf=nf),
        grid_spec=pltpu.PrefetchScalarGridSpec(
            num_scalar_prefetch=4,
            grid=(n_slots // tm, nf),
            in_specs=[pl.BlockSpec(memory_space=pl.ANY), up_spec, up_spec, down_spec, down_spec],
            out_specs=pl.BlockSpec((tm, D), lambda i, f, tok, be, nu, hf: (i, 0)),
            scratch_shapes=[pltpu.VMEM((2, xg_rows, D), F32),
                            pltpu.VMEM((tm, D), BF16),
                            pltpu.SemaphoreType.DMA((2,))]),
        out_shape=jax.ShapeDtypeStruct((n_slots, D), F32),
        compiler_params=_cp("arbitrary", "arbitrary"),
        name="moe_swiglu",
    )(slot_tok, block_e, n_used, block_parts, h, w_gate, w_up, w_down_lo, w_down_hi)


def _combine_kernel(dest_ref, y_ref, gate_ref, ys_hbm, o_ref, buf_ref, sem, *, tb):
    i = pl.program_id(0)

    def fetch(step, slot):
        base = step * tb

        def start(r, c):
            for k in range(TOP_K):
                _row_copy(ys_hbm, buf_ref.at[slot, k], dest_ref[TOP_K * (base + r) + k], r, sem.at[slot]).start()
            return c

        lax.fori_loop(0, tb, start, 0, unroll=16)

    @pl.when(i == 0)
    def _():
        fetch(0, 0)

    @pl.when(i + 1 < pl.num_programs(0))
    def _():
        fetch(i + 1, (i + 1) % 2)

    slot = i % 2
    pltpu.make_async_copy(buf_ref.at[slot], buf_ref.at[slot], sem.at[slot]).wait()
    g = gate_ref[...]
    o_ref[...] = y_ref[...] + g[:, 0:1] * buf_ref[slot, 0] + g[:, 1:2] * buf_ref[slot, 1]


def combine_rows(y, gates, ys, dest, tb=256):
    T, D = y.shape
    tb = min(tb, T)
    return pl.pallas_call(
        functools.partial(_combine_kernel, tb=tb),
        grid_spec=pltpu.PrefetchScalarGridSpec(
            num_scalar_prefetch=1,
            grid=(T // tb,),
            in_specs=[pl.BlockSpec((tb, D), lambda i, d: (i, 0)),
                      pl.BlockSpec((tb, LANES), lambda i, d: (i, 0)),
                      pl.BlockSpec(memory_space=pl.ANY)],
            out_specs=pl.BlockSpec((tb, D), lambda i, d: (i, 0)),
            scratch_shapes=[pltpu.VMEM((2, TOP_K, tb, D), F32),
                            pltpu.SemaphoreType.DMA((2,))]),
        out_shape=jax.ShapeDtypeStruct((T, D), F32),
        compiler_params=_cp("arbitrary"),
        name="moe_combine",
    )(dest, y, gates, ys)


def _routing_tables(top_idx, T, block):
    A = T * TOP_K
    E = N_EXPERTS
    flat_e = top_idx.reshape(A)
    onehot = (flat_e[:, None] == jnp.arange(E, dtype=jnp.int32)[None, :]).astype(jnp.int32)
    csum = jnp.cumsum(onehot, axis=0)
    counts = csum[-1]
    rank = jnp.sum((csum - 1) * onehot, axis=1)
    padded = (counts + block - 1) // block * block
    pends = jnp.cumsum(padded)
    pstarts = pends - padded
    starts = jnp.cumsum(counts) - counts
    dest = (pstarts[flat_e] + rank).astype(jnp.int32)
    n_blocks = A // block + E
    n_slots = n_blocks * block
    order = jnp.argsort(flat_e, stable=True).astype(jnp.int32)
    slot = jnp.arange(n_slots, dtype=jnp.int32)
    slot_e = jnp.clip(jnp.searchsorted(pends, slot, side='right'), 0, E - 1).astype(jnp.int32)
    r = slot - pstarts[slot_e]
    valid = r < counts[slot_e]
    src = jnp.clip(starts[slot_e] + r, 0, A - 1)
    slot_tok = jnp.where(valid, order[src] // TOP_K, 0).astype(jnp.int32)
    block_e = slot_e[::block]
    n_used = (pends[-1:] // block).astype(jnp.int32)
    in_block = jnp.clip(counts[block_e] - (slot[::block] - pstarts[block_e]), 1, block)
    part = block // MOE_ROW_PARTS
    block_parts = ((in_block + part - 1) // part).astype(jnp.int32)
    return slot_tok, dest, block_e, n_used, block_parts


def _token_mixer(h, mem2d, tab, B, S, M, l, casts, w_in, mla_q_a_norm_g, mla_w_uq, mla_kv_a_norm_g, mla_w_ukv,
                 mla_q_norm_g, mla_k_norm_g, mla_w_out, pool_w, pool_scale, pool_w_out,
                 mem_norm_g, mem_w_kv, mem_q_norm_g, mem_k_norm_g, mem_w_out):
    za = matmul(h, w_in, l, ZA_WIDTH, F32, tn=ZA_WIDTH // 2)

    head_pad = lambda v: jnp.concatenate([v[..., :QK_NOPE_DIM], _rope_lanes(v[..., QK_NOPE_DIM:])], axis=-1)
    wuq = head_pad(mla_w_uq[l].reshape(Q_LORA_RANK, MLA_HEADS, QK_HEAD_DIM)).reshape(
        Q_LORA_RANK, MLA_HEADS * HEAD_PAD)
    wukv = mla_w_ukv[l].reshape(KV_LORA_RANK, MLA_HEADS, QK_NOPE_DIM + V_HEAD_DIM)
    wuk = wukv[:, :, :QK_NOPE_DIM].reshape(KV_LORA_RANK, MLA_WIDTH)
    wuv = wukv[:, :, QK_NOPE_DIM:].reshape(KV_LORA_RANK, MLA_WIDTH)
    pad_g = lambda g: head_pad(g).reshape(1, HEAD_PAD)
    q, k, v = mla_prep(za, tab, wuq.astype(BF16), wuk.astype(BF16), wuv.astype(BF16),
                       mla_q_a_norm_g[l].reshape(1, -1), mla_kv_a_norm_g[l].reshape(1, -1),
                       pad_g(mla_q_norm_g[l]), pad_g(mla_k_norm_g[l]))
    attn, *cast_out = causal_attention(q, k, v, B, S, casts)

    mixed = pool_mixer(za, pool_w[l].astype(BF16), pool_scale[l].reshape(1, -1), B, S)

    k_m, v_m = mem_kv(mem2d, mem_norm_g[l].reshape(1, -1), mem_w_kv[l].astype(BF16),
                      mem_k_norm_g[l].reshape(1, -1), B, M)
    o_mem = mem_attention(za, k_m, v_m, mem_q_norm_g[l].reshape(1, -1), B, S, M)

    merged = merge_branches(h, attn, mixed, o_mem, w_in, mla_w_out, pool_w_out, mem_w_out, l)
    return merged, cast_out


def kernel(x, mem, positions, attn_norm_g, w_in, mla_q_a_norm_g, mla_w_uq, mla_kv_a_norm_g, mla_w_ukv, mla_q_norm_g, mla_k_norm_g, mla_w_out, pool_w, pool_scale, pool_w_out, mem_norm_g, mem_w_kv, mem_q_norm_g, mem_k_norm_g, mem_w_out, w_o, ffn_norm_g, dense_w_gate, dense_w_up, dense_w_down, router_w, router_b, moe_w_gate, moe_w_up, moe_w_down):
    B, S, D = x.shape
    M = mem.shape[1]
    T = B * S
    depth = attn_norm_g.shape[0]
    moe_block = 512
    y = x.reshape(T, D)
    mem2d = mem.reshape(B * M, D)
    tab = rope_tables(positions)
    w_in = prep_w_in(w_in)
    mla_w_out, pool_w_out, mem_w_out, w_o = (w.astype(BF16) for w in (mla_w_out, pool_w_out, mem_w_out, w_o))
    moe_bf16 = {}
    for l in range(depth):
        h = rmsnorm(y, attn_norm_g[l], BF16)
        i = l // 2
        casts = ()
        if l % 2 == 0 and l + 1 < depth:
            casts = ((moe_w_gate[i], 0, 1), (moe_w_down[i], 0, 2))
        elif l % 2 == 1:
            casts = ((moe_w_up[i], 0, 1), (moe_w_down[i], 1, 2))
        merged, cast_out = _token_mixer(h, mem2d, tab, B, S, M, l, casts, w_in, mla_q_a_norm_g, mla_w_uq,
                                        mla_kv_a_norm_g, mla_w_ukv, mla_q_norm_g, mla_k_norm_g, mla_w_out,
                                        pool_w, pool_scale, pool_w_out, mem_norm_g, mem_w_kv, mem_q_norm_g,
                                        mem_k_norm_g, mem_w_out)
        if casts:
            names = ("gate", "down_lo") if l % 2 == 0 else ("up", "down_hi")
            moe_bf16.update(zip(names, cast_out))
        if l % 2 == 0:
            y, h = matmul_residual_norm(merged, w_o, l, y, ffn_norm_g[l])
            y = dense_swiglu(h, dense_w_gate[i:i + 1].astype(BF16), dense_w_up[i:i + 1].astype(BF16),
                             dense_w_down[i:i + 1].astype(BF16), y)
        else:
            y = matmul(merged, w_o, l, D, F32, residual=y, tm=512, tn=D)
            hf, idx, gates = norm_router(y, ffn_norm_g[l], router_w[i], router_b[i])
            slot_tok, dest, block_e, n_used, block_parts = _routing_tables(idx[:, :TOP_K], T, moe_block)
            ys = moe_swiglu(hf, slot_tok, block_e, n_used, block_parts, moe_bf16.pop("gate"), moe_bf16.pop("up"),
                            moe_bf16.pop("down_lo"), moe_bf16.pop("down_hi"), tm=moe_block)
            y = combine_rows(y, gates, ys, dest)
    return y.reshape(B, S, D)
```

```python
import functools

import numpy as np
import jax
import jax.numpy as jnp
from jax import lax
from jax.experimental import pallas as pl
from jax.experimental.pallas import tpu as pltpu

F32 = jnp.float32
BF16 = jnp.bfloat16

MEM_HEADS = 4
MEM_HEAD_DIM = 256
MEM_WIDTH = MEM_HEADS * MEM_HEAD_DIM
MLA_HEADS = 8
Q_LORA_RANK = 512
KV_LORA_RANK = 256
QK_NOPE_DIM = 128
QK_ROPE_DIM = 64
QK_HEAD_DIM = QK_NOPE_DIM + QK_ROPE_DIM
V_HEAD_DIM = 128
MLA_WIDTH = MLA_HEADS * V_HEAD_DIM
ROPE_THETA = 10000.0
POOL_WINDOWS = (2, 4, 8, 16)
POOL_GROUPS = 4
POOL_GROUP_DIM = 256
POOL_WIDTH = POOL_GROUPS * POOL_GROUP_DIM
N_BRANCHES = 3
N_EXPERTS = 8
TOP_K = 2
NORM_EPS = 1e-6
LOG2_E = 1.4426950408889634

LANES = 128
SUBLANES = 8
HEAD_PAD = 2 * LANES
ROPE_HALF = QK_ROPE_DIM // 2
POOL_HALO = 16

ZA_POOL = 0
ZA_QMEM = POOL_WIDTH
ZA_CQ = ZA_QMEM + MEM_WIDTH
ZA_CKV = ZA_CQ + Q_LORA_RANK
ZA_KPE = ZA_CKV + KV_LORA_RANK
ZA_WIDTH = 3072

MOE_ROW_PARTS = 4
VMEM_LIMIT = 52 * 1024 * 1024


def _cp(*sem):
    return pltpu.CompilerParams(dimension_semantics=sem, vmem_limit_bytes=VMEM_LIMIT)


def _rms(x, eps_dim):
    return lax.rsqrt(jnp.sum(x * x, axis=-1, keepdims=True) * (1.0 / eps_dim) + NORM_EPS)


def _rmsnorm_kernel(x_ref, g_ref, o_ref):
    x = x_ref[...]
    o_ref[...] = (x * _rms(x, x.shape[-1]) * g_ref[...]).astype(o_ref.dtype)


def rmsnorm(x, g, out_dtype, tm=512):
    T, D = x.shape
    return pl.pallas_call(
        _rmsnorm_kernel,
        grid=(T // tm,),
        in_specs=[pl.BlockSpec((tm, D), lambda i: (i, 0)),
                  pl.BlockSpec((1, D), lambda i: (0, 0))],
        out_specs=pl.BlockSpec((tm, D), lambda i: (i, 0)),
        out_shape=jax.ShapeDtypeStruct((T, D), out_dtype),
        compiler_params=_cp("parallel"),
        name="rmsnorm",
    )(x, g.reshape(1, D))


def _norm_router_kernel(x_ref, g_ref, rwh_ref, rwl_ref, rb_ref, h_ref, idx_ref, gate_ref):
    x = x_ref[...]
    h = x * _rms(x, x.shape[-1]) * g_ref[...]
    h_ref[...] = h
    h_hi = h.astype(BF16)
    h_lo = (h - h_hi.astype(F32)).astype(BF16)
    logits = (jnp.dot(h_hi, rwh_ref[...], preferred_element_type=F32)
              + jnp.dot(h_lo, rwh_ref[...], preferred_element_type=F32)
              + jnp.dot(h_hi, rwl_ref[...], preferred_element_type=F32)) + rb_ref[...]
    lane = lax.broadcasted_iota(jnp.int32, logits.shape, 1)
    neg = -jnp.inf
    l1 = jnp.where(lane < N_EXPERTS, logits, neg)
    m1 = jnp.max(l1, axis=-1, keepdims=True)
    i1 = jnp.min(jnp.where(l1 == m1, lane, LANES), axis=-1, keepdims=True)
    l2 = jnp.where(lane == i1, neg, l1)
    m2 = jnp.max(l2, axis=-1, keepdims=True)
    i2 = jnp.min(jnp.where(l2 == m2, lane, LANES), axis=-1, keepdims=True)
    e = jnp.exp(m2 - m1)
    g1 = 1.0 / (1.0 + e)
    g2 = e / (1.0 + e)
    idx_ref[...] = jnp.where(lane == 0, i1, jnp.where(lane == 1, i2, 0))
    gate_ref[...] = jnp.where(lane == 0, g1, jnp.where(lane == 1, g2, 0.0))


def norm_router(x, g, router_w, router_b, tm=1024):
    T, D = x.shape
    tm = min(tm, T)
    E = router_w.shape[1]
    rw = jnp.zeros((D, LANES), F32).at[:, :E].set(router_w)
    rw_hi = rw.astype(BF16)
    rw_lo = (rw - rw_hi.astype(F32)).astype(BF16)
    rb = jnp.zeros((1, LANES), F32).at[0, :E].set(router_b)
    return pl.pallas_call(
        _norm_router_kernel,
        grid=(T // tm,),
        in_specs=[pl.BlockSpec((tm, D), lambda i: (i, 0)),
                  pl.BlockSpec((1, D), lambda i: (0, 0)),
                  pl.BlockSpec((D, LANES), lambda i: (0, 0)),
                  pl.BlockSpec((D, LANES), lambda i: (0, 0)),
                  pl.BlockSpec((1, LANES), lambda i: (0, 0))],
        out_specs=[pl.BlockSpec((tm, D), lambda i: (i, 0)),
                   pl.BlockSpec((tm, LANES), lambda i: (i, 0)),
                   pl.BlockSpec((tm, LANES), lambda i: (i, 0))],
        out_shape=[jax.ShapeDtypeStruct((T, D), F32),
                   jax.ShapeDtypeStruct((T, LANES), jnp.int32),
                   jax.ShapeDtypeStruct((T, LANES), F32)],
        compiler_params=_cp("parallel"),
        name="norm_router",
    )(x, g.reshape(1, D), rw_hi, rw_lo, rb)


def _mm_kernel(a_ref, w_ref, o_ref):
    o_ref[...] = jnp.dot(a_ref[...], w_ref[0], preferred_element_type=F32).astype(o_ref.dtype)


def _mm_res_kernel(a_ref, w_ref, r_ref, o_ref):
    o_ref[...] = (r_ref[...] + jnp.dot(a_ref[...], w_ref[0], preferred_element_type=F32)).astype(o_ref.dtype)


def matmul(a, w, l, N, out_dtype, residual=None, tm=1024, tn=1024):
    M, K = a.shape
    tm, tn = min(tm, M), min(tn, N)
    in_specs = [pl.BlockSpec((tm, K), lambda i, j: (i, 0)),
                pl.BlockSpec((1, K, tn), lambda i, j: (l, 0, j))]
    args = [a, w]
    body = _mm_kernel
    if residual is not None:
        in_specs.append(pl.BlockSpec((tm, tn), lambda i, j: (i, j)))
        args.append(residual)
        body = _mm_res_kernel
    return pl.pallas_call(
        body,
        grid=(M // tm, N // tn),
        in_specs=in_specs,
        out_specs=pl.BlockSpec((tm, tn), lambda i, j: (i, j)),
        out_shape=jax.ShapeDtypeStruct((M, N), out_dtype),
        compiler_params=_cp("parallel", "arbitrary"),
        name="matmul",
    )(*args)


def _mm_res_norm_kernel(a_ref, w_ref, r_ref, g_ref, o_ref, h_ref):
    y = r_ref[...] + jnp.dot(a_ref[...], w_ref[0], preferred_element_type=F32)
    o_ref[...] = y
    h_ref[...] = (y * _rms(y, y.shape[-1]) * g_ref[...]).astype(h_ref.dtype)


def matmul_residual_norm(a, w, l, residual, g, tm=512):
    M, K = a.shape
    N = w.shape[2]
    rows = lambda width: pl.BlockSpec((tm, width), lambda i: (i, 0))
    return pl.pallas_call(
        _mm_res_norm_kernel,
        grid=(M // tm,),
        in_specs=[rows(K), pl.BlockSpec((1, K, N), lambda i: (l, 0, 0)), rows(N),
                  pl.BlockSpec((1, N), lambda i: (0, 0))],
        out_specs=[rows(N), rows(N)],
        out_shape=[jax.ShapeDtypeStruct((M, N), F32), jax.ShapeDtypeStruct((M, N), BF16)],
        compiler_params=_cp("parallel"),
        name="matmul_residual_norm",
    )(a, w, residual, g.reshape(1, N))


def _rope_lanes(v):
    z = jnp.zeros(v.shape[:-1] + (LANES // 2 - ROPE_HALF,), v.dtype)
    return jnp.concatenate([v[..., :ROPE_HALF], z, v[..., ROPE_HALF:], z], axis=-1)


def _rope_table_kernel(pos_ref, invf_ref, o_ref):
    ang = pos_ref[...].astype(F32) * invf_ref[...]
    lane = lax.broadcasted_iota(jnp.int32, ang.shape, 1)
    first = lane < ROPE_HALF
    second = (lane >= LANES // 2) & (lane < LANES // 2 + ROPE_HALF)
    c = jnp.cos(ang)
    s = jnp.sin(ang)
    o_ref[:, 0:LANES] = jnp.where(first | second, c, 0.0)
    o_ref[:, LANES:2 * LANES] = jnp.where(first, -s, jnp.where(second, s, 0.0))


def rope_tables(positions, tm=2048):
    T = positions.size
    tm = min(tm, T)
    inv_freq = ROPE_THETA ** (-np.arange(0, QK_ROPE_DIM, 2, dtype=np.float32) / QK_ROPE_DIM)
    invf = np.zeros((1, LANES), np.float32)
    invf[0, :ROPE_HALF] = inv_freq
    invf[0, LANES // 2:LANES // 2 + ROPE_HALF] = inv_freq
    return pl.pallas_call(
        _rope_table_kernel,
        grid=(T // tm,),
        in_specs=[pl.BlockSpec((tm, 1), lambda i: (i, 0)),
                  pl.BlockSpec((1, LANES), lambda i: (0, 0))],
        out_specs=pl.BlockSpec((tm, 2 * LANES), lambda i: (i, 0)),
        out_shape=jax.ShapeDtypeStruct((T, 2 * LANES), F32),
        compiler_params=_cp("parallel"),
        name="rope_tables",
    )(positions.reshape(T, 1), jnp.asarray(invf))


def _rope(x, tab):
    return x * tab[:, 0:LANES] + pltpu.roll(x, LANES // 2, 1) * tab[:, LANES:2 * LANES]


def _mla_prep_kernel(cq_ref, ckv_ref, kpe_ref, tab_ref, wuq_ref, wuk_ref, wuv_ref,
                     gqa_ref, gkva_ref, gq_ref, gk_ref, q_ref, k_ref, v_ref):
    tab = tab_ref[...]
    cq = cq_ref[...]
    cqn = (cq * _rms(cq, Q_LORA_RANK) * gqa_ref[...]).astype(BF16)
    qf = jnp.dot(cqn, wuq_ref[...], preferred_element_type=F32)
    ckv = ckv_ref[...]
    ckvn = (ckv * _rms(ckv, KV_LORA_RANK) * gkva_ref[...]).astype(BF16)
    kn = jnp.dot(ckvn, wuk_ref[...], preferred_element_type=F32)
    v_ref[...] = jnp.dot(ckvn, wuv_ref[...], preferred_element_type=F32).astype(v_ref.dtype)
    gq = gq_ref[...]
    gk = gk_ref[...]
    kpe = kpe_ref[...]
    ss_pe = jnp.sum(kpe * kpe, axis=-1, keepdims=True)
    kpe_rot = _rope(kpe * gk[:, LANES:], tab)
    scale = QK_HEAD_DIM ** -0.5 * LOG2_E
    for h in range(MLA_HEADS):
        q0 = qf[:, h * HEAD_PAD:h * HEAD_PAD + LANES]
        q1 = qf[:, h * HEAD_PAD + LANES:(h + 1) * HEAD_PAD]
        ss = jnp.sum(q0 * q0, axis=-1, keepdims=True) + jnp.sum(q1 * q1, axis=-1, keepdims=True)
        rq = lax.rsqrt(ss * (1.0 / QK_HEAD_DIM) + NORM_EPS) * scale
        q_ref[:, h * HEAD_PAD:h * HEAD_PAD + LANES] = (q0 * gq[:, :LANES] * rq).astype(q_ref.dtype)
        q_ref[:, h * HEAD_PAD + LANES:(h + 1) * HEAD_PAD] = (_rope(q1 * gq[:, LANES:], tab) * rq).astype(q_ref.dtype)
        k0 = kn[:, h * LANES:(h + 1) * LANES]
        ssk = jnp.sum(k0 * k0, axis=-1, keepdims=True) + ss_pe
        rk = lax.rsqrt(ssk * (1.0 / QK_HEAD_DIM) + NORM_EPS)
        k_ref[:, h * HEAD_PAD:h * HEAD_PAD + LANES] = (k0 * gk[:, :LANES] * rk).astype(k_ref.dtype)
        k_ref[:, h * HEAD_PAD + LANES:(h + 1) * HEAD_PAD] = (kpe_rot * rk).astype(k_ref.dtype)


def mla_prep(za, tab, wuq, wuk, wuv, gqa, gkva, gq, gk, tm=1024):
    T = za.shape[0]
    tm = min(tm, T)
    HP = MLA_HEADS * HEAD_PAD
    full = lambda shape: pl.BlockSpec(shape, lambda i: (0, 0))
    return pl.pallas_call(
        _mla_prep_kernel,
        grid=(T // tm,),
        in_specs=[pl.BlockSpec((tm, Q_LORA_RANK), lambda i: (i, ZA_CQ // Q_LORA_RANK)),
                  pl.BlockSpec((tm, KV_LORA_RANK), lambda i: (i, ZA_CKV // KV_LORA_RANK)),
                  pl.BlockSpec((tm, LANES), lambda i: (i, ZA_KPE // LANES)),
                  pl.BlockSpec((tm, 2 * LANES), lambda i: (i, 0)),
                  full((Q_LORA_RANK, HP)), full((KV_LORA_RANK, MLA_WIDTH)), full((KV_LORA_RANK, MLA_WIDTH)),
                  full((1, Q_LORA_RANK)), full((1, KV_LORA_RANK)), full((1, HEAD_PAD)), full((1, HEAD_PAD))],
        out_specs=[pl.BlockSpec((tm, HP), lambda i: (i, 0)),
                   pl.BlockSpec((tm, HP), lambda i: (i, 0)),
                   pl.BlockSpec((tm, MLA_WIDTH), lambda i: (i, 0))],
        out_shape=[jax.ShapeDtypeStruct((T, HP), BF16),
                   jax.ShapeDtypeStruct((T, HP), BF16),
                   jax.ShapeDtypeStruct((T, MLA_WIDTH), BF16)],
        compiler_params=_cp("parallel"),
        name="mla_prep",
    )(za, za, za, tab, wuq, wuk, wuv, gqa, gkva, gq, gk)


def _causal_attn_kernel(q_ref, k_ref, v_ref, *rest, tq, n_cast):
    cast_src = rest[:n_cast]
    o_ref = rest[n_cast]
    cast_dst = rest[n_cast + 1:]
    for s_ref, d_ref in zip(cast_src, cast_dst):
        d_ref[...] = s_ref[...].astype(d_ref.dtype)
    S = q_ref.shape[0]
    row = lax.broadcasted_iota(jnp.int32, (tq, tq), 0)
    col = lax.broadcasted_iota(jnp.int32, (tq, tq), 1)
    tri = col <= row
    for qi in range(S // tq):
        kl = (qi + 1) * tq
        q = q_ref[qi * tq:kl, :]
        s = lax.dot_general(q, k_ref[0:kl, :], (((1,), (1,)), ((), ())), preferred_element_type=F32)
        s_diag = jnp.where(tri, s[:, kl - tq:kl], -jnp.inf)
        if qi > 0:
            s = jnp.concatenate([s[:, :kl - tq], s_diag], axis=1)
        else:
            s = s_diag
        m = jnp.max(s, axis=-1, keepdims=True)
        p = jnp.exp2(s - m)
        l = jnp.sum(p, axis=-1, keepdims=True)
        o = jnp.dot(p.astype(BF16), v_ref[0:kl, :], preferred_element_type=F32)
        o_ref[qi * tq:kl, :] = (o / l).astype(o_ref.dtype)


def causal_attention(q, k, v, B, S, casts=(), tq=256):
    T = q.shape[0]
    n_steps = B * MLA_HEADS
    in_specs = [pl.BlockSpec((S, HEAD_PAD), lambda b, h: (b, h)),
                pl.BlockSpec((S, HEAD_PAD), lambda b, h: (b, h)),
                pl.BlockSpec((S, V_HEAD_DIM), lambda b, h: (b, h))]
    out_specs = [pl.BlockSpec((S, V_HEAD_DIM), lambda b, h: (b, h))]
    out_shape = [jax.ShapeDtypeStruct((T, MLA_WIDTH), BF16)]
    for src, part, n_parts in casts:
        E, rows, cols = src.shape
        spe = n_steps // E
        rb, cb = rows // spe, cols // n_parts
        assert spe * E == n_steps and rb * spe == rows and rb % 16 == 0
        assert cb * n_parts == cols and cb % LANES == 0
        step = lambda b, h: b * MLA_HEADS + h
        in_specs.append(pl.BlockSpec((1, rb, cb), lambda b, h, part=part, spe=spe:
                                     (step(b, h) // spe, step(b, h) % spe, part)))
        out_specs.append(pl.BlockSpec((1, rb, cb), lambda b, h, spe=spe:
                                      (step(b, h) // spe, step(b, h) % spe, 0)))
        out_shape.append(jax.ShapeDtypeStruct((E, rows, cb), BF16))
    return pl.pallas_call(
        functools.partial(_causal_attn_kernel, tq=tq, n_cast=len(casts)),
        grid=(B, MLA_HEADS),
        in_specs=in_specs,
        out_specs=out_specs,
        out_shape=out_shape,
        compiler_params=_cp("parallel", "parallel"),
        name="causal_attention",
    )(q, k, v, *(c[0] for c in casts))


def _pool_kernel(u_ref, w_ref, sc_ref, o_ref, pad_ref, *, rc):
    S = u_ref.shape[0]
    C = POOL_GROUP_DIM
    H = POOL_HALO
    t1 = lax.broadcasted_iota(jnp.int32, (rc, C), 0).astype(F32) + 1.0
    rows = lambda buf, r0, shift: pad_ref[buf, H + r0 - shift:H + r0 - shift + rc, :]
    for buf in range(2):
        pad_ref[buf, 0:H, :] = jnp.zeros((H, C), F32)
    for g, win in enumerate(POOL_WINDOWS):
        assert win & (win - 1) == 0 and win // 2 <= H
        pad_ref[0, H:H + S, :] = u_ref[:, g * C:(g + 1) * C]
        src, m = 0, 1
        while 2 * m < win:
            for r0 in range(0, S, rc):
                pad_ref[1 - src, H + r0:H + r0 + rc, :] = rows(src, r0, 0) + rows(src, r0, m)
            src, m = 1 - src, 2 * m
        for r0 in range(0, S, rc):
            acc = rows(src, r0, 0) + rows(src, r0, m)
            tok = u_ref[r0:r0 + rc, g * C:(g + 1) * C]
            cnt = jnp.minimum(t1 + float(r0), float(win))
            pooled = (acc / cnt - tok).astype(BF16)
            mixed = jnp.dot(pooled, w_ref[g], preferred_element_type=F32) * sc_ref[:, g * C:(g + 1) * C]
            o_ref[r0:r0 + rc, g * C:(g + 1) * C] = mixed.astype(o_ref.dtype)


def pool_mixer(za, pool_w, pool_scale, B, S, rc=512):
    T = za.shape[0]
    rc = min(rc, S)
    return pl.pallas_call(
        functools.partial(_pool_kernel, rc=rc),
        grid=(B,),
        in_specs=[pl.BlockSpec((S, POOL_WIDTH), lambda b: (b, ZA_POOL // POOL_WIDTH)),
                  pl.BlockSpec((POOL_GROUPS, POOL_GROUP_DIM, POOL_GROUP_DIM), lambda b: (0, 0, 0)),
                  pl.BlockSpec((1, POOL_WIDTH), lambda b: (0, 0))],
        out_specs=pl.BlockSpec((S, POOL_WIDTH), lambda b: (b, 0)),
        out_shape=jax.ShapeDtypeStruct((T, POOL_WIDTH), BF16),
        scratch_shapes=[pltpu.VMEM((2, POOL_HALO + S, POOL_GROUP_DIM), F32)],
        compiler_params=_cp("parallel"),
        name="pool_mixer",
    )(za, pool_w, pool_scale)


def _mem_kv_kernel(m_ref, g_ref, w_ref, gk_ref, k_ref, v_ref):
    x = m_ref[...]
    xn = (x * _rms(x, x.shape[-1]) * g_ref[...]).astype(BF16)
    kv = jnp.dot(xn, w_ref[...], preferred_element_type=F32)
    for h in range(MEM_HEADS):
        kh = kv[:, h * MEM_HEAD_DIM:(h + 1) * MEM_HEAD_DIM]
        k_ref[:, h * MEM_HEAD_DIM:(h + 1) * MEM_HEAD_DIM] = (
            kh * _rms(kh, MEM_HEAD_DIM) * gk_ref[...]).astype(k_ref.dtype)
    v_ref[...] = kv[:, MEM_WIDTH:].astype(v_ref.dtype)


def mem_kv(mem2d, g, w_kv, gk, B, M):
    D = mem2d.shape[1]
    return pl.pallas_call(
        _mem_kv_kernel,
        grid=(B,),
        in_specs=[pl.BlockSpec((M, D), lambda b: (b, 0)),
                  pl.BlockSpec((1, D), lambda b: (0, 0)),
                  pl.BlockSpec((D, 2 * MEM_WIDTH), lambda b: (0, 0)),
                  pl.BlockSpec((1, MEM_HEAD_DIM), lambda b: (0, 0))],
        out_specs=[pl.BlockSpec((M, MEM_WIDTH), lambda b: (b, 0)),
                   pl.BlockSpec((M, MEM_WIDTH), lambda b: (b, 0))],
        out_shape=[jax.ShapeDtypeStruct((B * M, MEM_WIDTH), BF16),
                   jax.ShapeDtypeStruct((B * M, MEM_WIDTH), BF16)],
        compiler_params=_cp("parallel"),
        name="mem_kv",
    )(mem2d, g, w_kv, gk)


def _mem_attn_kernel(q_ref, k_ref, v_ref, gq_ref, o_ref):
    scale = MEM_HEAD_DIM ** -0.5 * LOG2_E
    for h in range(MEM_HEADS):
        sl = slice(h * MEM_HEAD_DIM, (h + 1) * MEM_HEAD_DIM)
        qh = q_ref[:, sl]
        qn = (qh * (_rms(qh, MEM_HEAD_DIM) * scale) * gq_ref[...]).astype(BF16)
        s = lax.dot_general(qn, k_ref[:, sl], (((1,), (1,)), ((), ())), preferred_element_type=F32)
        m = jnp.max(s, axis=-1, keepdims=True)
        p = jnp.exp2(s - m)
        l = jnp.sum(p, axis=-1, keepdims=True)
        o = jnp.dot(p.astype(BF16), v_ref[:, sl], preferred_element_type=F32)
        o_ref[:, sl] = (o / l).astype(o_ref.dtype)


def mem_attention(za, k_m, v_m, gq, B, S, M, tq=1024):
    T = za.shape[0]
    tq = min(tq, S)
    nq = S // tq
    return pl.pallas_call(
        _mem_attn_kernel,
        grid=(B, nq),
        in_specs=[pl.BlockSpec((tq, MEM_WIDTH), lambda b, i: (b * nq + i, ZA_QMEM // MEM_WIDTH)),
                  pl.BlockSpec((M, MEM_WIDTH), lambda b, i: (b, 0)),
                  pl.BlockSpec((M, MEM_WIDTH), lambda b, i: (b, 0)),
                  pl.BlockSpec((1, MEM_HEAD_DIM), lambda b, i: (0, 0))],
        out_specs=pl.BlockSpec((tq, MEM_WIDTH), lambda b, i: (b * nq + i, 0)),
        out_shape=jax.ShapeDtypeStruct((T, MEM_WIDTH), BF16),
        compiler_params=_cp("parallel", "arbitrary"),
        name="mem_attention",
    )(za, k_m, v_m, gq)


def _merge_kernel(h_ref, a_ref, p_ref, m_ref, g0_ref, g1_ref, g2_ref, w0_ref, w1_ref, w2_ref, o_ref):
    h = h_ref[...]
    acc = None
    for x_ref, g_ref, w_ref in ((a_ref, g0_ref, w0_ref), (p_ref, g1_ref, w1_ref), (m_ref, g2_ref, w2_ref)):
        logit = jnp.dot(h, g_ref[0], preferred_element_type=F32)
        gate = 1.0 / (1.0 + jnp.exp(-logit))
        br = jnp.dot(x_ref[...], w_ref[0], preferred_element_type=F32)
        acc = gate * br if acc is None else acc + gate * br
    o_ref[...] = acc.astype(o_ref.dtype)


def merge_branches(h, a, p, m, w_all, w_a, w_p, w_m, l, tm=1024, tn=512):
    T, D = h.shape
    W = a.shape[1]
    nj = D // tn
    g0 = ZA_WIDTH // tn
    row = lambda width: pl.BlockSpec((tm, width), lambda i, j: (i, 0))
    gate_spec = lambda b: pl.BlockSpec((1, D, tn), lambda i, j, b=b: (l, 0, g0 + b * nj + j))
    out_w = pl.BlockSpec((1, W, tn), lambda i, j: (l, 0, j))
    return pl.pallas_call(
        _merge_kernel,
        grid=(T // tm, nj),
        in_specs=[row(D), row(W), row(W), row(W),
                  gate_spec(0), gate_spec(1), gate_spec(2), out_w, out_w, out_w],
        out_specs=pl.BlockSpec((tm, tn), lambda i, j: (i, j)),
        out_shape=jax.ShapeDtypeStruct((T, D), BF16),
        compiler_params=_cp("parallel", "arbitrary"),
        name="merge_branches",
    )(h, a, p, m, w_all, w_all, w_all, w_a, w_p, w_m)


def _w_in_prep_kernel(w_ref, o_ref):
    s1 = Q_LORA_RANK + KV_LORA_RANK + QK_ROPE_DIM
    s2 = s1 + POOL_WIDTH
    s3 = s2 + MEM_WIDTH
    tk = w_ref.shape[2]
    s1_up = -(-s1 // LANES) * LANES
    o_ref[0, :, ZA_POOL:ZA_POOL + POOL_WIDTH] = w_ref[0, s1:s2, :].T.astype(BF16)
    o_ref[0, :, ZA_QMEM:ZA_QMEM + MEM_WIDTH] = w_ref[0, s2:s3, :].T.astype(BF16)
    low = w_ref[0, 0:s1_up, :].T.astype(BF16)
    n_c = Q_LORA_RANK + KV_LORA_RANK
    o_ref[0, :, ZA_CQ:ZA_CQ + n_c] = low[:, 0:n_c]
    o_ref[0, :, ZA_KPE:ZA_WIDTH] = jnp.zeros((tk, ZA_WIDTH - ZA_KPE), BF16)
    o_ref[0, :, ZA_KPE:ZA_KPE + ROPE_HALF] = low[:, n_c:n_c + ROPE_HALF]
    o_ref[0, :, ZA_KPE + LANES // 2:ZA_KPE + LANES // 2 + ROPE_HALF] = low[:, n_c + ROPE_HALF:s1]
    o_ref[0, :, ZA_WIDTH:] = w_ref[0, s3:, :].T.astype(BF16)


def prep_w_in(w_in, tk=256):
    L, D, win = w_in.shape
    wout = ZA_WIDTH + N_BRANCHES * D
    return pl.pallas_call(
        _w_in_prep_kernel,
        grid=(L, D // tk),
        in_specs=[pl.BlockSpec((1, win, tk), lambda l, i: (l, 0, i))],
        out_specs=pl.BlockSpec((1, tk, wout), lambda l, i: (l, i, 0)),
        out_shape=jax.ShapeDtypeStruct((L, D, wout), BF16),
        compiler_params=_cp("parallel", "parallel"),
        name="prep_w_in",
    )(jnp.swapaxes(w_in, 1, 2))


def _swiglu_accumulate(xb, wg_ref, wu_ref, wd_refs, o_ref):
    a = jnp.dot(xb, wg_ref[0], preferred_element_type=F32)
    b = jnp.dot(xb, wu_ref[0], preferred_element_type=F32)
    act = (a / (1.0 + jnp.exp(-a)) * b).astype(BF16)
    width = o_ref.shape[1] // len(wd_refs)
    for n, wd_ref in enumerate(wd_refs):
        o_ref[:, n * width:(n + 1) * width] += jnp.dot(act, wd_ref[0], preferred_element_type=F32)


def _dense_swiglu_kernel(x_ref, wg_ref, wu_ref, wd_ref, r_hbm, o_ref, r_ref, sem, *, tm):
    f = pl.program_id(1)
    res_copy = pltpu.make_async_copy(r_hbm.at[pl.ds(pl.program_id(0) * tm, tm)], r_ref, sem)

    @pl.when(f == 0)
    def _():
        res_copy.start()
        o_ref[...] = jnp.zeros_like(o_ref)

    _swiglu_accumulate(x_ref[...], wg_ref, wu_ref, (wd_ref,), o_ref)

    @pl.when(f == pl.num_programs(1) - 1)
    def _():
        res_copy.wait()
        o_ref[...] += r_ref[...]


def dense_swiglu(x, w_gate, w_up, w_down, residual, tm=1024, tf=512):
    T, D = x.shape
    F = w_gate.shape[2]
    tm = min(tm, T)
    return pl.pallas_call(
        functools.partial(_dense_swiglu_kernel, tm=tm),
        grid=(T // tm, F // tf),
        in_specs=[pl.BlockSpec((tm, D), lambda i, f: (i, 0)),
                  pl.BlockSpec((1, D, tf), lambda i, f: (0, 0, f)),
                  pl.BlockSpec((1, D, tf), lambda i, f: (0, 0, f)),
                  pl.BlockSpec((1, tf, D), lambda i, f: (0, f, 0)),
                  pl.BlockSpec(memory_space=pl.ANY)],
        out_specs=pl.BlockSpec((tm, D), lambda i, f: (i, 0)),
        out_shape=jax.ShapeDtypeStruct((T, D), F32),
        scratch_shapes=[pltpu.VMEM((tm, D), F32), pltpu.SemaphoreType.DMA(())],
        compiler_params=_cp("arbitrary", "arbitrary"),
        name="dense_swiglu",
    )(x, w_gate, w_up, w_down, residual)


def _row_copy(src_hbm, dst_ref, src_row, dst_row, sem):
    return pltpu.make_async_copy(src_hbm.at[pl.ds(src_row, 1)], dst_ref.at[pl.ds(dst_row, 1)], sem)


def _moe_rows_per_step(tm, nf):
    return -(-tm // (nf * SUBLANES)) * SUBLANES


def _moe_swiglu_kernel(tok_ref, be_ref, nu_ref, parts_ref, h_hbm, wg_ref, wu_ref, wd0_ref, wd1_ref, o_ref,
                       xg_ref, xb_ref, sem, *, tm, nf):
    del be_ref
    i = pl.program_id(0)
    f = pl.program_id(1)
    n_used = nu_ref[0]
    rows_per_step = _moe_rows_per_step(tm, nf)
    n_fetch = rows_per_step * nf

    def fetch(block, r, slot):
        tok = tok_ref[block * tm + jnp.minimum(r, tm - 1)]
        return _row_copy(h_hbm, xg_ref.at[slot], tok, r, sem.at[slot])

    def wait_rows(slot, n_rows):
        rows = xg_ref.at[slot, pl.ds(0, n_rows)]
        pltpu.make_async_copy(rows, rows, sem.at[slot]).wait()

    @pl.when((i >= n_used) & (f == 0))
    def _():
        o_ref[...] = jnp.zeros_like(o_ref)

    @pl.when(i < n_used)
    def _():
        slot = i % 2

        @pl.when((i == 0) & (f == 0))
        def _():
            def start(r, c):
                fetch(0, r, 0).start()
                return c
            lax.fori_loop(0, n_fetch, start, 0)
            for j in range(rows_per_step):
                fetch(jnp.minimum(1, n_used - 1), j, 1).start()

        @pl.when(f == 0)
        def _():
            wait_rows(slot, n_fetch)
            xb_ref[...] = xg_ref[slot, 0:tm, :].astype(BF16)
            o_ref[...] = jnp.zeros_like(o_ref)

        def fetch_next_rows():
            wrap = (f == nf - 1).astype(jnp.int32)
            chunk = (f + 1) * (1 - wrap)
            target = jnp.minimum(i + 1 + wrap, n_used - 1)
            target_slot = (i + 1 + wrap) % 2
            for j in range(rows_per_step):
                fetch(target, chunk * rows_per_step + j, target_slot).start()

        quarters = parts_ref[i]
        for nq in range(MOE_ROW_PARTS, 0, -1):
            rows = nq * tm // MOE_ROW_PARTS

            @pl.when(quarters == nq)
            def _(rows=rows):
                fetch_next_rows()
                _swiglu_accumulate(xb_ref[0:rows, :], wg_ref, wu_ref, (wd0_ref, wd1_ref),
                                   o_ref.at[pl.ds(0, rows)])

        @pl.when((i == n_used - 1) & (f == nf - 1))
        def _():
            wait_rows(1 - slot, n_fetch)
            wait_rows(slot, rows_per_step)


def moe_swiglu(h, slot_tok, block_e, n_used, block_parts, w_gate, w_up, w_down_lo, w_down_hi, tm, tf=1024):
    n_slots = slot_tok.shape[0]
    D = h.shape[1]
    F = w_gate.shape[2]
    while F % tf:
        tf //= 2
    nf = F // tf
    xg_rows = _moe_rows_per_step(tm, nf) * nf
    blk = lambda i, nu: jnp.minimum(i, nu[0] - 1)
    chunk = lambda i, f, nu: jnp.where(i < nu[0], f, nf - 1)
    up_spec = pl.BlockSpec((1, D, tf), lambda i, f, tok, be, nu, hf: (be[blk(i, nu)], 0, chunk(i, f, nu)))
    down_spec = pl.BlockSpec((1, tf, D // 2),
                             lambda i, f, tok, be, nu, hf: (be[blk(i, nu)], chunk(i, f, nu), 0))
    return pl.pallas_call(
        functools.partial(_moe_swiglu_kernel, tm=tm, nf=nf),
        grid_spec=pltpu.PrefetchScalarGridSpec(
            num_scalar_prefetch=4,
            grid=(n_slots // tm, nf),
            in_specs=[pl.BlockSpec(memory_space=pl.ANY), up_spec, up_spec, down_spec, down_spec],
            out_specs=pl.BlockSpec((tm, D), lambda i, f, tok, be, nu, hf: (i, 0)),
            scratch_shapes=[pltpu.VMEM((2, xg_rows, D), F32),
                            pltpu.VMEM((tm, D), BF16),
                            pltpu.SemaphoreType.DMA((2,))]),
        out_shape=jax.ShapeDtypeStruct((n_slots, D), F32),
        compiler_params=_cp("arbitrary", "arbitrary"),
        name="moe_swiglu",
    )(slot_tok, block_e, n_used, block_parts, h, w_gate, w_up, w_down_lo, w_down_hi)


def _combine_kernel(dest_ref, y_ref, gate_ref, ys_hbm, o_ref, buf_ref, sem, *, tb):
    i = pl.program_id(0)

    def fetch(step, slot):
        base = step * tb

        def start(r, c):
            for k in range(TOP_K):
                _row_copy(ys_hbm, buf_ref.at[slot, k], dest_ref[TOP_K * (base + r) + k], r, sem.at[slot]).start()
            return c

        lax.fori_loop(0, tb, start, 0, unroll=16)

    @pl.when(i == 0)
    def _():
        fetch(0, 0)

    @pl.when(i + 1 < pl.num_programs(0))
    def _():
        fetch(i + 1, (i + 1) % 2)

    slot = i % 2
    pltpu.make_async_copy(buf_ref.at[slot], buf_ref.at[slot], sem.at[slot]).wait()
    g = gate_ref[...]
    o_ref[...] = y_ref[...] + g[:, 0:1] * buf_ref[slot, 0] + g[:, 1:2] * buf_ref[slot, 1]


def combine_rows(y, gates, ys, dest, tb=256):
    T, D = y.shape
    tb = min(tb, T)
    return pl.pallas_call(
        functools.partial(_combine_kernel, tb=tb),
        grid_spec=pltpu.PrefetchScalarGridSpec(
            num_scalar_prefetch=1,
            grid=(T // tb,),
            in_specs=[pl.BlockSpec((tb, D), lambda i, d: (i, 0)),
                      pl.BlockSpec((tb, LANES), lambda i, d: (i, 0)),
                      pl.BlockSpec(memory_space=pl.ANY)],
            out_specs=pl.BlockSpec((tb, D), lambda i, d: (i, 0)),
            scratch_shapes=[pltpu.VMEM((2, TOP_K, tb, D), F32),
                            pltpu.SemaphoreType.DMA((2,))]),
        out_shape=jax.ShapeDtypeStruct((T, D), F32),
        compiler_params=_cp("arbitrary"),
        name="moe_combine",
    )(dest, y, gates, ys)


def _routing_tables(top_idx, T, block):
    A = T * TOP_K
    E = N_EXPERTS
    flat_e = top_idx.reshape(A)
    onehot = (flat_e[:, None] == jnp.arange(E, dtype=jnp.int32)[None, :]).astype(jnp.int32)
    csum = jnp.cumsum(onehot, axis=0)
    counts = csum[-1]
    rank = jnp.sum((csum - 1) * onehot, axis=1)
    padded = (counts + block - 1) // block * block
    pends = jnp.cumsum(padded)
    pstarts = pends - padded
    starts = jnp.cumsum(counts) - counts
    dest = (pstarts[flat_e] + rank).astype(jnp.int32)
    n_blocks = A // block + E
    n_slots = n_blocks * block
    order = jnp.argsort(flat_e, stable=True).astype(jnp.int32)
    slot = jnp.arange(n_slots, dtype=jnp.int32)
    slot_e = jnp.clip(jnp.searchsorted(pends, slot, side='right'), 0, E - 1).astype(jnp.int32)
    r = slot - pstarts[slot_e]
    valid = r < counts[slot_e]
    src = jnp.clip(starts[slot_e] + r, 0, A - 1)
    slot_tok = jnp.where(valid, order[src] // TOP_K, 0).astype(jnp.int32)
    block_e = slot_e[::block]
    n_used = (pends[-1:] // block).astype(jnp.int32)
    in_block = jnp.clip(counts[block_e] - (slot[::block] - pstarts[block_e]), 1, block)
    part = block // MOE_ROW_PARTS
    block_parts = ((in_block + part - 1) // part).astype(jnp.int32)
    return slot_tok, dest, block_e, n_used, block_parts


def _token_mixer(h, mem2d, tab, B, S, M, l, casts, w_in, mla_q_a_norm_g, mla_w_uq, mla_kv_a_norm_g, mla_w_ukv,
                 mla_q_norm_g, mla_k_norm_g, mla_w_out, pool_w, pool_scale, pool_w_out,
                 mem_norm_g, mem_w_kv, mem_q_norm_g, mem_k_norm_g, mem_w_out):
    za = matmul(h, w_in, l, ZA_WIDTH, F32, tn=ZA_WIDTH // 2)

    head_pad = lambda v: jnp.concatenate([v[..., :QK_NOPE_DIM], _rope_lanes(v[..., QK_NOPE_DIM:])], axis=-1)
    wuq = head_pad(mla_w_uq[l].reshape(Q_LORA_RANK, MLA_HEADS, QK_HEAD_DIM)).reshape(
        Q_LORA_RANK, MLA_HEADS * HEAD_PAD)
    wukv = mla_w_ukv[l].reshape(KV_LORA_RANK, MLA_HEADS, QK_NOPE_DIM + V_HEAD_DIM)
    wuk = wukv[:, :, :QK_NOPE_DIM].reshape(KV_LORA_RANK, MLA_WIDTH)
    wuv = wukv[:, :, QK_NOPE_DIM:].reshape(KV_LORA_RANK, MLA_WIDTH)
    pad_g = lambda g: head_pad(g).reshape(1, HEAD_PAD)
    q, k, v = mla_prep(za, tab, wuq.astype(BF16), wuk.astype(BF16), wuv.astype(BF16),
                       mla_q_a_norm_g[l].reshape(1, -1), mla_kv_a_norm_g[l].reshape(1, -1),
                       pad_g(mla_q_norm_g[l]), pad_g(mla_k_norm_g[l]))
    attn, *cast_out = causal_attention(q, k, v, B, S, casts)

    mixed = pool_mixer(za, pool_w[l].astype(BF16), pool_scale[l].reshape(1, -1), B, S)

    k_m, v_m = mem_kv(mem2d, mem_norm_g[l].reshape(1, -1), mem_w_kv[l].astype(BF16),
                      mem_k_norm_g[l].reshape(1, -1), B, M)
    o_mem = mem_attention(za, k_m, v_m, mem_q_norm_g[l].reshape(1, -1), B, S, M)

    merged = merge_branches(h, attn, mixed, o_mem, w_in, mla_w_out, pool_w_out, mem_w_out, l)
    return merged, cast_out


def kernel(x, mem, positions, attn_norm_g, w_in, mla_q_a_norm_g, mla_w_uq, mla_kv_a_norm_g, mla_w_ukv, mla_q_norm_g, mla_k_norm_g, mla_w_out, pool_w, pool_scale, pool_w_out, mem_norm_g, mem_w_kv, mem_q_norm_g, mem_k_norm_g, mem_w_out, w_o, ffn_norm_g, dense_w_gate, dense_w_up, dense_w_down, router_w, router_b, moe_w_gate, moe_w_up, moe_w_down):
    B, S, D = x.shape
    M = mem.shape[1]
    T = B * S
    depth = attn_norm_g.shape[0]
    moe_block = 512
    y = x.reshape(T, D)
    mem2d = mem.reshape(B * M, D)
    tab = rope_tables(positions)
    w_in = prep_w_in(w_in)
    mla_w_out, pool_w_out, mem_w_out, w_o = (w.astype(BF16) for w in (mla_w_out, pool_w_out, mem_w_out, w_o))
    moe_bf16 = {}
    for l in range(depth):
        h = rmsnorm(y, attn_norm_g[l], BF16)
        i = l // 2
        casts = ()
        if l % 2 == 0 and l + 1 < depth:
            casts = ((moe_w_gate[i], 0, 1), (moe_w_down[i], 0, 2))
        elif l % 2 == 1:
            casts = ((moe_w_up[i], 0, 1), (moe_w_down[i], 1, 2))
        merged, cast_out = _token_mixer(h, mem2d, tab, B, S, M, l, casts, w_in, mla_q_a_norm_g, mla_w_uq,
                                        mla_kv_a_norm_g, mla_w_ukv, mla_q_norm_g, mla_k_norm_g, mla_w_out,
                                        pool_w, pool_scale, pool_w_out, mem_norm_g, mem_w_kv, mem_q_norm_g,
                                        mem_k_norm_g, mem_w_out)
        if casts:
            names = ("gate", "down_lo") if l % 2 == 0 else ("up", "down_hi")
            moe_bf16.update(zip(names, cast_out))
        if l % 2 == 0:
            y, h = matmul_residual_norm(merged, w_o, l, y, ffn_norm_g[l])
            y = dense_swiglu(h, dense_w_gate[i:i + 1].astype(BF16), dense_w_up[i:i + 1].astype(BF16),
                             dense_w_down[i:i + 1].astype(BF16), y)
        else:
            y = matmul(merged, w_o, l, D, F32, residual=y, tm=512, tn=D)
            hf, idx, gates = norm_router(y, ffn_norm_g[l], router_w[i], router_b[i])
            slot_tok, dest, block_e, n_used, block_parts = _routing_tables(idx[:, :TOP_K], T, moe_block)
            ys = moe_swiglu(hf, slot_tok, block_e, n_used, block_parts, moe_bf16.pop("gate"), moe_bf16.pop("up"),
                            moe_bf16.pop("down_lo"), moe_bf16.pop("down_hi"), tm=moe_block)
            y = combine_rows(y, gates, ys, dest)
    return y.reshape(B, S, D)
```

```python
import functools

import numpy as np
import jax
import jax.numpy as jnp
from jax import lax
from jax.experimental import pallas as pl
from jax.experimental.pallas import tpu as pltpu

F32 = jnp.float32
BF16 = jnp.bfloat16

MEM_HEADS = 4
MEM_HEAD_DIM = 256
MEM_WIDTH = MEM_HEADS * MEM_HEAD_DIM
MLA_HEADS = 8
Q_LORA_RANK = 512
KV_LORA_RANK = 256
QK_NOPE_DIM = 128
QK_ROPE_DIM = 64
QK_HEAD_DIM = QK_NOPE_DIM + QK_ROPE_DIM
V_HEAD_DIM = 128
MLA_WIDTH = MLA_HEADS * V_HEAD_DIM
ROPE_THETA = 10000.0
POOL_WINDOWS = (2, 4, 8, 16)
POOL_GROUPS = 4
POOL_GROUP_DIM = 256
POOL_WIDTH = POOL_GROUPS * POOL_GROUP_DIM
N_BRANCHES = 3
N_EXPERTS = 8
TOP_K = 2
NORM_EPS = 1e-6
LOG2_E = 1.4426950408889634

LANES = 128
SUBLANES = 8
HEAD_PAD = 2 * LANES
ROPE_HALF = QK_ROPE_DIM // 2
POOL_HALO = 16

ZA_POOL = 0
ZA_QMEM = POOL_WIDTH
ZA_CQ = ZA_QMEM + MEM_WIDTH
ZA_CKV = ZA_CQ + Q_LORA_RANK
ZA_KPE = ZA_CKV + KV_LORA_RANK
ZA_WIDTH = 3072

MOE_ROW_PARTS = 4
VMEM_LIMIT = 52 * 1024 * 1024


def _cp(*sem):
    return pltpu.CompilerParams(dimension_semantics=sem, vmem_limit_bytes=VMEM_LIMIT)


def _rms(x, eps_dim):
    return lax.rsqrt(jnp.sum(x * x, axis=-1, keepdims=True) * (1.0 / eps_dim) + NORM_EPS)


def _rmsnorm_kernel(x_ref, g_ref, o_ref):
    x = x_ref[...]
    o_ref[...] = (x * _rms(x, x.shape[-1]) * g_ref[...]).astype(o_ref.dtype)


def rmsnorm(x, g, out_dtype, tm=512):
    T, D = x.shape
    return pl.pallas_call(
        _rmsnorm_kernel,
        grid=(T // tm,),
        in_specs=[pl.BlockSpec((tm, D), lambda i: (i, 0)),
                  pl.BlockSpec((1, D), lambda i: (0, 0))],
        out_specs=pl.BlockSpec((tm, D), lambda i: (i, 0)),
        out_shape=jax.ShapeDtypeStruct((T, D), out_dtype),
        compiler_params=_cp("parallel"),
        name="rmsnorm",
    )(x, g.reshape(1, D))


def _norm_router_kernel(x_ref, g_ref, rwh_ref, rwl_ref, rb_ref, h_ref, idx_ref, gate_ref):
    x = x_ref[...]
    h = x * _rms(x, x.shape[-1]) * g_ref[...]
    h_ref[...] = h
    h_hi = h.astype(BF16)
    h_lo = (h - h_hi.astype(F32)).astype(BF16)
    logits = (jnp.dot(h_hi, rwh_ref[...], preferred_element_type=F32)
              + jnp.dot(h_lo, rwh_ref[...], preferred_element_type=F32)
              + jnp.dot(h_hi, rwl_ref[...], preferred_element_type=F32)) + rb_ref[...]
    lane = lax.broadcasted_iota(jnp.int32, logits.shape, 1)
    neg = -jnp.inf
    l1 = jnp.where(lane < N_EXPERTS, logits, neg)
    m1 = jnp.max(l1, axis=-1, keepdims=True)
    i1 = jnp.min(jnp.where(l1 == m1, lane, LANES), axis=-1, keepdims=True)
    l2 = jnp.where(lane == i1, neg, l1)
    m2 = jnp.max(l2, axis=-1, keepdims=True)
    i2 = jnp.min(jnp.where(l2 == m2, lane, LANES), axis=-1, keepdims=True)
    e = jnp.exp(m2 - m1)
    g1 = 1.0 / (1.0 + e)
    g2 = e / (1.0 + e)
    idx_ref[...] = jnp.where(lane == 0, i1, jnp.where(lane == 1, i2, 0))
    gate_ref[...] = jnp.where(lane == 0, g1, jnp.where(lane == 1, g2, 0.0))


def norm_router(x, g, router_w, router_b, tm=1024):
    T, D = x.shape
    tm = min(tm, T)
    E = router_w.shape[1]
    rw = jnp.zeros((D, LANES), F32).at[:, :E].set(router_w)
    rw_hi = rw.astype(BF16)
    rw_lo = (rw - rw_hi.astype(F32)).astype(BF16)
    rb = jnp.zeros((1, LANES), F32).at[0, :E].set(router_b)
    return pl.pallas_call(
        _norm_router_kernel,
        grid=(T // tm,),
        in_specs=[pl.BlockSpec((tm, D), lambda i: (i, 0)),
                  pl.BlockSpec((1, D), lambda i: (0, 0)),
                  pl.BlockSpec((D, LANES), lambda i: (0, 0)),
                  pl.BlockSpec((D, LANES), lambda i: (0, 0)),
                  pl.BlockSpec((1, LANES), lambda i: (0, 0))],
        out_specs=[pl.BlockSpec((tm, D), lambda i: (i, 0)),
                   pl.BlockSpec((tm, LANES), lambda i: (i, 0)),
                   pl.BlockSpec((tm, LANES), lambda i: (i, 0))],
        out_shape=[jax.ShapeDtypeStruct((T, D), F32),
                   jax.ShapeDtypeStruct((T, LANES), jnp.int32),
                   jax.ShapeDtypeStruct((T, LANES), F32)],
        compiler_params=_cp("parallel"),
        name="norm_router",
    )(x, g.reshape(1, D), rw_hi, rw_lo, rb)


def _mm_kernel(a_ref, w_ref, o_ref):
    o_ref[...] = jnp.dot(a_ref[...], w_ref[0], preferred_element_type=F32).astype(o_ref.dtype)


def _mm_res_kernel(a_ref, w_ref, r_ref, o_ref):
    o_ref[...] = (r_ref[...] + jnp.dot(a_ref[...], w_ref[0], preferred_element_type=F32)).astype(o_ref.dtype)


def matmul(a, w, l, N, out_dtype, residual=None, tm=1024, tn=1024):
    M, K = a.shape
    tm, tn = min(tm, M), min(tn, N)
    in_specs = [pl.BlockSpec((tm, K), lambda i, j: (i, 0)),
                pl.BlockSpec((1, K, tn), lambda i, j: (l, 0, j))]
    args = [a, w]
    body = _mm_kernel
    if residual is not None:
        in_specs.append(pl.BlockSpec((tm, tn), lambda i, j: (i, j)))
        args.append(residual)
        body = _mm_res_kernel
    return pl.pallas_call(
        body,
        grid=(M // tm, N // tn),
        in_specs=in_specs,
        out_specs=pl.BlockSpec((tm, tn), lambda i, j: (i, j)),
        out_shape=jax.ShapeDtypeStruct((M, N), out_dtype),
        compiler_params=_cp("parallel", "arbitrary"),
        name="matmul",
    )(*args)


def _mm_res_norm_kernel(a_ref, w_ref, r_ref, g_ref, o_ref, h_ref):
    y = r_ref[...] + jnp.dot(a_ref[...], w_ref[0], preferred_element_type=F32)
    o_ref[...] = y
    h_ref[...] = (y * _rms(y, y.shape[-1]) * g_ref[...]).astype(h_ref.dtype)


def matmul_residual_norm(a, w, l, residual, g, tm=512):
    M, K = a.shape
    N = w.shape[2]
    rows = lambda width: pl.BlockSpec((tm, width), lambda i: (i, 0))
    return pl.pallas_call(
        _mm_res_norm_kernel,
        grid=(M // tm,),
        in_specs=[rows(K), pl.BlockSpec((1, K, N), lambda i: (l, 0, 0)), rows(N),
                  pl.BlockSpec((1, N), lambda i: (0, 0))],
        out_specs=[rows(N), rows(N)],
        out_shape=[jax.ShapeDtypeStruct((M, N), F32), jax.ShapeDtypeStruct((M, N), BF16)],
        compiler_params=_cp("parallel"),
        name="matmul_residual_norm",
    )(a, w, residual, g.reshape(1, N))


def _rope_lanes(v):
    z = jnp.zeros(v.shape[:-1] + (LANES // 2 - ROPE_HALF,), v.dtype)
    return jnp.concatenate([v[..., :ROPE_HALF], z, v[..., ROPE_HALF:], z], axis=-1)


def _rope_table_kernel(pos_ref, invf_ref, o_ref):
    ang = pos_ref[...].astype(F32) * invf_ref[...]
    lane = lax.broadcasted_iota(jnp.int32, ang.shape, 1)
    first = lane < ROPE_HALF
    second = (lane >= LANES // 2) & (lane < LANES // 2 + ROPE_HALF)
    c = jnp.cos(ang)
    s = jnp.sin(ang)
    o_ref[:, 0:LANES] = jnp.where(first | second, c, 0.0)
    o_ref[:, LANES:2 * LANES] = jnp.where(first, -s, jnp.where(second, s, 0.0))


def rope_tables(positions, tm=2048):
    T = positions.size
    tm = min(tm, T)
    inv_freq = ROPE_THETA ** (-np.arange(0, QK_ROPE_DIM, 2, dtype=np.float32) / QK_ROPE_DIM)
    invf = np.zeros((1, LANES), np.float32)
    invf[0, :ROPE_HALF] = inv_freq
    invf[0, LANES // 2:LANES // 2 + ROPE_HALF] = inv_freq
    return pl.pallas_call(
        _rope_table_kernel,
        grid=(T // tm,),
        in_specs=[pl.BlockSpec((tm, 1), lambda i: (i, 0)),
                  pl.BlockSpec((1, LANES), lambda i: (0, 0))],
        out_specs=pl.BlockSpec((tm, 2 * LANES), lambda i: (i, 0)),
        out_shape=jax.ShapeDtypeStruct((T, 2 * LANES), F32),
        compiler_params=_cp("parallel"),
        name="rope_tables",
    )(positions.reshape(T, 1), jnp.asarray(invf))


def _rope(x, tab):
    return x * tab[:, 0:LANES] + pltpu.roll(x, LANES // 2, 1) * tab[:, LANES:2 * LANES]


def _mla_prep_kernel(cq_ref, ckv_ref, kpe_ref, tab_ref, wuq_ref, wuk_ref, wuv_ref,
                     gqa_ref, gkva_ref, gq_ref, gk_ref, q_ref, k_ref, v_ref):
    tab = tab_ref[...]
    cq = cq_ref[...]
    cqn = (cq * _rms(cq, Q_LORA_RANK) * gqa_ref[...]).astype(BF16)
    qf = jnp.dot(cqn, wuq_ref[...], preferred_element_type=F32)
    ckv = ckv_ref[...]
    ckvn = (ckv * _rms(ckv, KV_LORA_RANK) * gkva_ref[...]).astype(BF16)
    kn = jnp.dot(ckvn, wuk_ref[...], preferred_element_type=F32)
    v_ref[...] = jnp.dot(ckvn, wuv_ref[...], preferred_element_type=F32).astype(v_ref.dtype)
    gq = gq_ref[...]
    gk = gk_ref[...]
    kpe = kpe_ref[...]
    ss_pe = jnp.sum(kpe * kpe, axis=-1, keepdims=True)
    kpe_rot = _rope(kpe * gk[:, LANES:], tab)
    scale = QK_HEAD_DIM ** -0.5 * LOG2_E
    for h in range(MLA_HEADS):
        q0 = qf[:, h * HEAD_PAD:h * HEAD_PAD + LANES]
        q1 = qf[:, h * HEAD_PAD + LANES:(h + 1) * HEAD_PAD]
        ss = jnp.sum(q0 * q0, axis=-1, keepdims=True) + jnp.sum(q1 * q1, axis=-1, keepdims=True)
        rq = lax.rsqrt(ss * (1.0 / QK_HEAD_DIM) + NORM_EPS) * scale
        q_ref[:, h * HEAD_PAD:h * HEAD_PAD + LANES] = (q0 * gq[:, :LANES] * rq).astype(q_ref.dtype)
        q_ref[:, h * HEAD_PAD + LANES:(h + 1) * HEAD_PAD] = (_rope(q1 * gq[:, LANES:], tab) * rq).astype(q_ref.dtype)
        k0 = kn[:, h * LANES:(h + 1) * LANES]
        ssk = jnp.sum(k0 * k0, axis=-1, keepdims=True) + ss_pe
        rk = lax.rsqrt(ssk * (1.0 / QK_HEAD_DIM) + NORM_EPS)
        k_ref[:, h * HEAD_PAD:h * HEAD_PAD + LANES] = (k0 * gk[:, :LANES] * rk).astype(k_ref.dtype)
        k_ref[:, h * HEAD_PAD + LANES:(h + 1) * HEAD_PAD] = (kpe_rot * rk).astype(k_ref.dtype)


def mla_prep(za, tab, wuq, wuk, wuv, gqa, gkva, gq, gk, tm=1024):
    T = za.shape[0]
    tm = min(tm, T)
    HP = MLA_HEADS * HEAD_PAD
    full = lambda shape: pl.BlockSpec(shape, lambda i: (0, 0))
    return pl.pallas_call(
        _mla_prep_kernel,
        grid=(T // tm,),
        in_specs=[pl.BlockSpec((tm, Q_LORA_RANK), lambda i: (i, ZA_CQ // Q_LORA_RANK)),
                  pl.BlockSpec((tm, KV_LORA_RANK), lambda i: (i, ZA_CKV // KV_LORA_RANK)),
                  pl.BlockSpec((tm, LANES), lambda i: (i, ZA_KPE // LANES)),
                  pl.BlockSpec((tm, 2 * LANES), lambda i: (i, 0)),
                  full((Q_LORA_RANK, HP)), full((KV_LORA_RANK, MLA_WIDTH)), full((KV_LORA_RANK, MLA_WIDTH)),
                  full((1, Q_LORA_RANK)), full((1, KV_LORA_RANK)), full((1, HEAD_PAD)), full((1, HEAD_PAD))],
        out_specs=[pl.BlockSpec((tm, HP), lambda i: (i, 0)),
                   pl.BlockSpec((tm, HP), lambda i: (i, 0)),
                   pl.BlockSpec((tm, MLA_WIDTH), lambda i: (i, 0))],
        out_shape=[jax.ShapeDtypeStruct((T, HP), BF16),
                   jax.ShapeDtypeStruct((T, HP), BF16),
                   jax.ShapeDtypeStruct((T, MLA_WIDTH), BF16)],
        compiler_params=_cp("parallel"),
        name="mla_prep",
    )(za, za, za, tab, wuq, wuk, wuv, gqa, gkva, gq, gk)


def _causal_attn_kernel(q_ref, k_ref, v_ref, *rest, tq, n_cast):
    cast_src = rest[:n_cast]
    o_ref = rest[n_cast]
    cast_dst = rest[n_cast + 1:]
    for s_ref, d_ref in zip(cast_src, cast_dst):
        d_ref[...] = s_ref[...].astype(d_ref.dtype)
    S = q_ref.shape[0]
    row = lax.broadcasted_iota(jnp.int32, (tq, tq), 0)
    col = lax.broadcasted_iota(jnp.int32, (tq, tq), 1)
    tri = col <= row
    for qi in range(S // tq):
        kl = (qi + 1) * tq
        q = q_ref[qi * tq:kl, :]
        s = lax.dot_general(q, k_ref[0:kl, :], (((1,), (1,)), ((), ())), preferred_element_type=F32)
        s_diag = jnp.where(tri, s[:, kl - tq:kl], -jnp.inf)
        if qi > 0:
            s = jnp.concatenate([s[:, :kl - tq], s_diag], axis=1)
        else:
            s = s_diag
        m = jnp.max(s, axis=-1, keepdims=True)
        p = jnp.exp2(s - m)
        l = jnp.sum(p, axis=-1, keepdims=True)
        o = jnp.dot(p.astype(BF16), v_ref[0:kl, :], preferred_element_type=F32)
        o_ref[qi * tq:kl, :] = (o / l).astype(o_ref.dtype)


def causal_attention(q, k, v, B, S, casts=(), tq=256):
    T = q.shape[0]
    n_steps = B * MLA_HEADS
    in_specs = [pl.BlockSpec((S, HEAD_PAD), lambda b, h: (b, h)),
                pl.BlockSpec((S, HEAD_PAD), lambda b, h: (b, h)),
                pl.BlockSpec((S, V_HEAD_DIM), lambda b, h: (b, h))]
    out_specs = [pl.BlockSpec((S, V_HEAD_DIM), lambda b, h: (b, h))]
    out_shape = [jax.ShapeDtypeStruct((T, MLA_WIDTH), BF16)]
    for src, part, n_parts in casts:
        E, rows, cols = src.shape
        spe = n_steps // E
        rb, cb = rows // spe, cols // n_parts
        assert spe * E == n_steps and rb * spe == rows and rb % 16 == 0
        assert cb * n_parts == cols and cb % LANES == 0
        step = lambda b, h: b * MLA_HEADS + h
        in_specs.append(pl.BlockSpec((1, rb, cb), lambda b, h, part=part, spe=spe:
                                     (step(b, h) // spe, step(b, h) % spe, part)))
        out_specs.append(pl.BlockSpec((1, rb, cb), lambda b, h, spe=spe:
                                      (step(b, h) // spe, step(b, h) % spe, 0)))
        out_shape.append(jax.ShapeDtypeStruct((E, rows, cb), BF16))
    return pl.pallas_call(
        functools.partial(_causal_attn_kernel, tq=tq, n_cast=len(casts)),
        grid=(B, MLA_HEADS),
        in_specs=in_specs,
        out_specs=out_specs,
        out_shape=out_shape,
        compiler_params=_cp("parallel", "parallel"),
        name="causal_attention",
    )(q, k, v, *(c[0] for c in casts))


def _pool_kernel(u_ref, w_ref, sc_ref, o_ref, pad_ref, *, rc):
    S = u_ref.shape[0]
    C = POOL_GROUP_DIM
    H = POOL_HALO
    t1 = lax.broadcasted_iota(jnp.int32, (rc, C), 0).astype(F32) + 1.0
    rows = lambda buf, r0, shift: pad_ref[buf, H + r0 - shift:H + r0 - shift + rc, :]
    for buf in range(2):
        pad_ref[buf, 0:H, :] = jnp.zeros((H, C), F32)
    for g, win in enumerate(POOL_WINDOWS):
        assert win & (win - 1) == 0 and win // 2 <= H
        pad_ref[0, H:H + S, :] = u_ref[:, g * C:(g + 1) * C]
        src, m = 0, 1
        while 2 * m < win:
            for r0 in range(0, S, rc):
                pad_ref[1 - src, H + r0:H + r0 + rc, :] = rows(src, r0, 0) + rows(src, r0, m)
            src, m = 1 - src, 2 * m
        for r0 in range(0, S, rc):
            acc = rows(src, r0, 0) + rows(src, r0, m)
            tok = u_ref[r0:r0 + rc, g * C:(g + 1) * C]
            cnt = jnp.minimum(t1 + float(r0), float(win))
            pooled = (acc / cnt - tok).astype(BF16)
            mixed = jnp.dot(pooled, w_ref[g], preferred_element_type=F32) * sc_ref[:, g * C:(g + 1) * C]
            o_ref[r0:r0 + rc, g * C:(g + 1) * C] = mixed.astype(o_ref.dtype)


def pool_mixer(za, pool_w, pool_scale, B, S, rc=512):
    T = za.shape[0]
    rc = min(rc, S)
    return pl.pallas_call(
        functools.partial(_pool_kernel, rc=rc),
        grid=(B,),
        in_specs=[pl.BlockSpec((S, POOL_WIDTH), lambda b: (b, ZA_POOL // POOL_WIDTH)),
                  pl.BlockSpec((POOL_GROUPS, POOL_GROUP_DIM, POOL_GROUP_DIM), lambda b: (0, 0, 0)),
                  pl.BlockSpec((1, POOL_WIDTH), lambda b: (0, 0))],
        out_specs=pl.BlockSpec((S, POOL_WIDTH), lambda b: (b, 0)),
        out_shape=jax.ShapeDtypeStruct((T, POOL_WIDTH), BF16),
        scratch_shapes=[pltpu.VMEM((2, POOL_HALO + S, POOL_GROUP_DIM), F32)],
        compiler_params=_cp("parallel"),
        name="pool_mixer",
    )(za, pool_w, pool_scale)


def _mem_kv_kernel(m_ref, g_ref, w_ref, gk_ref, k_ref, v_ref):
    x = m_ref[...]
    xn = (x * _rms(x, x.shape[-1]) * g_ref[...]).astype(BF16)
    kv = jnp.dot(xn, w_ref[...], preferred_element_type=F32)
    for h in range(MEM_HEADS):
        kh = kv[:, h * MEM_HEAD_DIM:(h + 1) * MEM_HEAD_DIM]
        k_ref[:, h * MEM_HEAD_DIM:(h + 1) * MEM_HEAD_DIM] = (
            kh * _rms(kh, MEM_HEAD_DIM) * gk_ref[...]).astype(k_ref.dtype)
    v_ref[...] = kv[:, MEM_WIDTH:].astype(v_ref.dtype)


def mem_kv(mem2d, g, w_kv, gk, B, M):
    D = mem2d.shape[1]
    return pl.pallas_call(
        _mem_kv_kernel,
        grid=(B,),
        in_specs=[pl.BlockSpec((M, D), lambda b: (b, 0)),
                  pl.BlockSpec((1, D), lambda b: (0, 0)),
                  pl.BlockSpec((D, 2 * MEM_WIDTH), lambda b: (0, 0)),
                  pl.BlockSpec((1, MEM_HEAD_DIM), lambda b: (0, 0))],
        out_specs=[pl.BlockSpec((M, MEM_WIDTH), lambda b: (b, 0)),
                   pl.BlockSpec((M, MEM_WIDTH), lambda b: (b, 0))],
        out_shape=[jax.ShapeDtypeStruct((B * M, MEM_WIDTH), BF16),
                   jax.ShapeDtypeStruct((B * M, MEM_WIDTH), BF16)],
        compiler_params=_cp("parallel"),
        name="mem_kv",
    )(mem2d, g, w_kv, gk)


def _mem_attn_kernel(q_ref, k_ref, v_ref, gq_ref, o_ref):
    scale = MEM_HEAD_DIM ** -0.5 * LOG2_E
    for h in range(MEM_HEADS):
        sl = slice(h * MEM_HEAD_DIM, (h + 1) * MEM_HEAD_DIM)
        qh = q_ref[:, sl]
        qn = (qh * (_rms(qh, MEM_HEAD_DIM) * scale) * gq_ref[...]).astype(BF16)
        s = lax.dot_general(qn, k_ref[:, sl], (((1,), (1,)), ((), ())), preferred_element_type=F32)
        m = jnp.max(s, axis=-1, keepdims=True)
        p = jnp.exp2(s - m)
        l = jnp.sum(p, axis=-1, keepdims=True)
        o = jnp.dot(p.astype(BF16), v_ref[:, sl], preferred_element_type=F32)
        o_ref[:, sl] = (o / l).astype(o_ref.dtype)


def mem_attention(za, k_m, v_m, gq, B, S, M, tq=1024):
    T = za.shape[0]
    tq = min(tq, S)
    nq = S // tq
    return pl.pallas_call(
        _mem_attn_kernel,
        grid=(B, nq),
        in_specs=[pl.BlockSpec((tq, MEM_WIDTH), lambda b, i: (b * nq + i, ZA_QMEM // MEM_WIDTH)),
                  pl.BlockSpec((M, MEM_WIDTH), lambda b, i: (b, 0)),
                  pl.BlockSpec((M, MEM_WIDTH), lambda b, i: (b, 0)),
                  pl.BlockSpec((1, MEM_HEAD_DIM), lambda b, i: (0, 0))],
        out_specs=pl.BlockSpec((tq, MEM_WIDTH), lambda b, i: (b * nq + i, 0)),
        out_shape=jax.ShapeDtypeStruct((T, MEM_WIDTH), BF16),
        compiler_params=_cp("parallel", "arbitrary"),
        name="mem_attention",
    )(za, k_m, v_m, gq)


def _merge_kernel(h_ref, a_ref, p_ref, m_ref, g0_ref, g1_ref, g2_ref, w0_ref, w1_ref, w2_ref, o_ref):
    h = h_ref[...]
    acc = None
    for x_ref, g_ref, w_ref in ((a_ref, g0_ref, w0_ref), (p_ref, g1_ref, w1_ref), (m_ref, g2_ref, w2_ref)):
        logit = jnp.dot(h, g_ref[0], preferred_element_type=F32)
        gate = 1.0 / (1.0 + jnp.exp(-logit))
        br = jnp.dot(x_ref[...], w_ref[0], preferred_element_type=F32)
        acc = gate * br if acc is None else acc + gate * br
    o_ref[...] = acc.astype(o_ref.dtype)


def merge_branches(h, a, p, m, w_all, w_a, w_p, w_m, l, tm=1024, tn=512):
    T, D = h.shape
    W = a.shape[1]
    nj = D // tn
    g0 = ZA_WIDTH // tn
    row = lambda width: pl.BlockSpec((tm, width), lambda i, j: (i, 0))
    gate_spec = lambda b: pl.BlockSpec((1, D, tn), lambda i, j, b=b: (l, 0, g0 + b * nj + j))
    out_w = pl.BlockSpec((1, W, tn), lambda i, j: (l, 0, j))
    return pl.pallas_call(
        _merge_kernel,
        grid=(T // tm, nj),
        in_specs=[row(D), row(W), row(W), row(W),
                  gate_spec(0), gate_spec(1), gate_spec(2), out_w, out_w, out_w],
        out_specs=pl.BlockSpec((tm, tn), lambda i, j: (i, j)),
        out_shape=jax.ShapeDtypeStruct((T, D), BF16),
        compiler_params=_cp("parallel", "arbitrary"),
        name="merge_branches",
    )(h, a, p, m, w_all, w_all, w_all, w_a, w_p, w_m)


def _w_in_prep_kernel(w_ref, o_ref):
    s1 = Q_LORA_RANK + KV_LORA_RANK + QK_ROPE_DIM
    s2 = s1 + POOL_WIDTH
    s3 = s2 + MEM_WIDTH
    tk = w_ref.shape[2]
    s1_up = -(-s1 // LANES) * LANES
    o_ref[0, :, ZA_POOL:ZA_POOL + POOL_WIDTH] = w_ref[0, s1:s2, :].T.astype(BF16)
    o_ref[0, :, ZA_QMEM:ZA_QMEM + MEM_WIDTH] = w_ref[0, s2:s3, :].T.astype(BF16)
    low = w_ref[0, 0:s1_up, :].T.astype(BF16)
    n_c = Q_LORA_RANK + KV_LORA_RANK
    o_ref[0, :, ZA_CQ:ZA_CQ + n_c] = low[:, 0:n_c]
    o_ref[0, :, ZA_KPE:ZA_WIDTH] = jnp.zeros((tk, ZA_WIDTH - ZA_KPE), BF16)
    o_ref[0, :, ZA_KPE:ZA_KPE + ROPE_HALF] = low[:, n_c:n_c + ROPE_HALF]
    o_ref[0, :, ZA_KPE + LANES // 2:ZA_KPE + LANES // 2 + ROPE_HALF] = low[:, n_c + ROPE_HALF:s1]
    o_ref[0, :, ZA_WIDTH:] = w_ref[0, s3:, :].T.astype(BF16)


def prep_w_in(w_in, tk=256):
    L, D, win = w_in.shape
    wout = ZA_WIDTH + N_BRANCHES * D
    return pl.pallas_call(
        _w_in_prep_kernel,
        grid=(L, D // tk),
        in_specs=[pl.BlockSpec((1, win, tk), lambda l, i: (l, 0, i))],
        out_specs=pl.BlockSpec((1, tk, wout), lambda l, i: (l, i, 0)),
        out_shape=jax.ShapeDtypeStruct((L, D, wout), BF16),
        compiler_params=_cp("parallel", "parallel"),
        name="prep_w_in",
    )(jnp.swapaxes(w_in, 1, 2))


def _swiglu_accumulate(xb, wg_ref, wu_ref, wd_refs, o_ref):
    a = jnp.dot(xb, wg_ref[0], preferred_element_type=F32)
    b = jnp.dot(xb, wu_ref[0], preferred_element_type=F32)
    act = (a / (1.0 + jnp.exp(-a)) * b).astype(BF16)
    width = o_ref.shape[1] // len(wd_refs)
    for n, wd_ref in enumerate(wd_refs):
        o_ref[:, n * width:(n + 1) * width] += jnp.dot(act, wd_ref[0], preferred_element_type=F32)


def _dense_swiglu_kernel(x_ref, wg_ref, wu_ref, wd_ref, r_hbm, o_ref, r_ref, sem, *, tm):
    f = pl.program_id(1)
    res_copy = pltpu.make_async_copy(r_hbm.at[pl.ds(pl.program_id(0) * tm, tm)], r_ref, sem)

    @pl.when(f == 0)
    def _():
        res_copy.start()
        o_ref[...] = jnp.zeros_like(o_ref)

    _swiglu_accumulate(x_ref[...], wg_ref, wu_ref, (wd_ref,), o_ref)

    @pl.when(f == pl.num_programs(1) - 1)
    def _():
        res_copy.wait()
        o_ref[...] += r_ref[...]


def dense_swiglu(x, w_gate, w_up, w_down, residual, tm=1024, tf=512):
    T, D = x.shape
    F = w_gate.shape[2]
    tm = min(tm, T)
    return pl.pallas_call(
        functools.partial(_dense_swiglu_kernel, tm=tm),
        grid=(T // tm, F // tf),
        in_specs=[pl.BlockSpec((tm, D), lambda i, f: (i, 0)),
                  pl.BlockSpec((1, D, tf), lambda i, f: (0, 0, f)),
                  pl.BlockSpec((1, D, tf), lambda i, f: (0, 0, f)),
                  pl.BlockSpec((1, tf, D), lambda i, f: (0, f, 0)),
                  pl.BlockSpec(memory_space=pl.ANY)],
        out_specs=pl.BlockSpec((tm, D), lambda i, f: (i, 0)),
        out_shape=jax.ShapeDtypeStruct((T, D), F32),
        scratch_shapes=[pltpu.VMEM((tm, D), F32), pltpu.SemaphoreType.DMA(())],
        compiler_params=_cp("arbitrary", "arbitrary"),
        name="dense_swiglu",
    )(x, w_gate, w_up, w_down, residual)


def _row_copy(src_hbm, dst_ref, src_row, dst_row, sem):
    return pltpu.make_async_copy(src_hbm.at[pl.ds(src_row, 1)], dst_ref.at[pl.ds(dst_row, 1)], sem)


def _moe_rows_per_step(tm, nf):
    return -(-tm // (nf * SUBLANES)) * SUBLANES


def _moe_swiglu_kernel(tok_ref, be_ref, nu_ref, parts_ref, h_hbm, wg_ref, wu_ref, wd0_ref, wd1_ref, o_ref,
                       xg_ref, xb_ref, sem, *, tm, nf):
    del be_ref
    i = pl.program_id(0)
    f = pl.program_id(1)
    n_used = nu_ref[0]
    rows_per_step = _moe_rows_per_step(tm, nf)
    n_fetch = rows_per_step * nf

    def fetch(block, r, slot):
        tok = tok_ref[block * tm + jnp.minimum(r, tm - 1)]
        return _row_copy(h_hbm, xg_ref.at[slot], tok, r, sem.at[slot])

    def wait_rows(slot, n_rows):
        rows = xg_ref.at[slot, pl.ds(0, n_rows)]
        pltpu.make_async_copy(rows, rows, sem.at[slot]).wait()

    @pl.when((i >= n_used) & (f == 0))
    def _():
        o_ref[...] = jnp.zeros_like(o_ref)

    @pl.when(i < n_used)
    def _():
        slot = i % 2

        @pl.when((i == 0) & (f == 0))
        def _():
            def start(r, c):
                fetch(0, r, 0).start()
                return c
            lax.fori_loop(0, n_fetch, start, 0)
            for j in range(rows_per_step):
                fetch(jnp.minimum(1, n_used - 1), j, 1).start()

        @pl.when(f == 0)
        def _():
            wait_rows(slot, n_fetch)
            xb_ref[...] = xg_ref[slot, 0:tm, :].astype(BF16)
            o_ref[...] = jnp.zeros_like(o_ref)

        def fetch_next_rows():
            wrap = (f == nf - 1).astype(jnp.int32)
            chunk = (f + 1) * (1 - wrap)
            target = jnp.minimum(i + 1 + wrap, n_used - 1)
            target_slot = (i + 1 + wrap) % 2
            for j in range(rows_per_step):
                fetch(target, chunk * rows_per_step + j, target_slot).start()

        quarters = parts_ref[i]
        for nq in range(MOE_ROW_PARTS, 0, -1):
            rows = nq * tm // MOE_ROW_PARTS

            @pl.when(quarters == nq)
            def _(rows=rows):
                fetch_next_rows()
                _swiglu_accumulate(xb_ref[0:rows, :], wg_ref, wu_ref, (wd0_ref, wd1_ref),
                                   o_ref.at[pl.ds(0, rows)])

        @pl.when((i == n_used - 1) & (f == nf - 1))
        def _():
            wait_rows(1 - slot, n_fetch)
            wait_rows(slot, rows_per_step)


def moe_swiglu(h, slot_tok, block_e, n_used, block_parts, w_gate, w_up, w_down_lo, w_down_hi, tm, tf=1024):
    n_slots = slot_tok.shape[0]
    D = h.shape[1]
    F = w_gate.shape[2]
    while F % tf:
        tf //= 2
    nf = F // tf
    xg_rows = _moe_rows_per_step(tm, nf) * nf
    blk = lambda i, nu: jnp.minimum(i, nu[0] - 1)
    chunk = lambda i, f, nu: jnp.where(i < nu[0], f, nf - 1)
    up_spec = pl.BlockSpec((1, D, tf), lambda i, f, tok, be, nu, hf: (be[blk(i, nu)], 0, chunk(i, f, nu)))
    down_spec = pl.BlockSpec((1, tf, D // 2),
                             lambda i, f, tok, be, nu, hf: (be[blk(i, nu)], chunk(i, f, nu), 0))
    return pl.pallas_call(
        functools.partial(_moe_swiglu_kernel, tm=tm, nf=nf),
        grid_spec=pltpu.PrefetchScalarGridSpec(
            num_scalar_prefetch=4,
            grid=(n_slots // tm, nf),
            in_specs=[pl.BlockSpec(memory_space=pl.ANY), up_spec, up_spec, down_spec, down_spec],
            out_specs=pl.BlockSpec((tm, D), lambda i, f, tok, be, nu, hf: (i, 0)),
            scratch_shapes=[pltpu.VMEM((2, xg_rows, D), F32),
                            pltpu.VMEM((tm, D), BF16),
                            pltpu.SemaphoreType.DMA((2,))]),
        out_shape=jax.ShapeDtypeStruct((n_slots, D), F32),
        compiler_params=_cp("arbitrary", "arbitrary"),
        name="moe_swiglu",
    )(slot_tok, block_e, n_used, block_parts, h, w_gate, w_up, w_down_lo, w_down_hi)


def _combine_kernel(dest_ref, y_ref, gate_ref, ys_hbm, o_ref, buf_ref, sem, *, tb):
    i = pl.program_id(0)

    def fetch(step, slot):
        base = step * tb

        def start(r, c):
            for k in range(TOP_K):
                _row_copy(ys_hbm, buf_ref.at[slot, k], dest_ref[TOP_K * (base + r) + k], r, sem.at[slot]).start()
            return c

        lax.fori_loop(0, tb, start, 0, unroll=16)

    @pl.when(i == 0)
    def _():
        fetch(0, 0)

    @pl.when(i + 1 < pl.num_programs(0))
    def _():
        fetch(i + 1, (i + 1) % 2)

    slot = i % 2
    pltpu.make_async_copy(buf_ref.at[slot], buf_ref.at[slot], sem.at[slot]).wait()
    g = gate_ref[...]
    o_ref[...] = y_ref[...] + g[:, 0:1] * buf_ref[slot, 0] + g[:, 1:2] * buf_ref[slot, 1]


def combine_rows(y, gates, ys, dest, tb=256):
    T, D = y.shape
    tb = min(tb, T)
    return pl.pallas_call(
        functools.partial(_combine_kernel, tb=tb),
        grid_spec=pltpu.PrefetchScalarGridSpec(
            num_scalar_prefetch=1,
            grid=(T // tb,),
            in_specs=[pl.BlockSpec((tb, D), lambda i, d: (i, 0)),
                      pl.BlockSpec((tb, LANES), lambda i, d: (i, 0)),
                      pl.BlockSpec(memory_space=pl.ANY)],
            out_specs=pl.BlockSpec((tb, D), lambda i, d: (i, 0)),
            scratch_shapes=[pltpu.VMEM((2, TOP_K, tb, D), F32),
                            pltpu.SemaphoreType.DMA((2,))]),
        out_shape=jax.ShapeDtypeStruct((T, D), F32),
        compiler_params=_cp("arbitrary"),
        name="moe_combine",
    )(dest, y, gates, ys)


def _routing_tables(top_idx, T, block):
    A = T * TOP_K
    E = N_EXPERTS
    flat_e = top_idx.reshape(A)
    onehot = (flat_e[:, None] == jnp.arange(E, dtype=jnp.int32)[None, :]).astype(jnp.int32)
    csum = jnp.cumsum(onehot, axis=0)
    counts = csum[-1]
    rank = jnp.sum((csum - 1) * onehot, axis=1)
    padded = (counts + block - 1) // block * block
    pends = jnp.cumsum(padded)
    pstarts = pends - padded
    starts = jnp.cumsum(counts) - counts
    dest = (pstarts[flat_e] + rank).astype(jnp.int32)
    n_blocks = A // block + E
    n_slots = n_blocks * block
    order = jnp.argsort(flat_e, stable=True).astype(jnp.int32)
    slot = jnp.arange(n_slots, dtype=jnp.int32)
    slot_e = jnp.clip(jnp.searchsorted(pends, slot, side='right'), 0, E - 1).astype(jnp.int32)
    r = slot - pstarts[slot_e]
    valid = r < counts[slot_e]
    src = jnp.clip(starts[slot_e] + r, 0, A - 1)
    slot_tok = jnp.where(valid, order[src] // TOP_K, 0).astype(jnp.int32)
    block_e = slot_e[::block]
    n_used = (pends[-1:] // block).astype(jnp.int32)
    in_block = jnp.clip(counts[block_e] - (slot[::block] - pstarts[block_e]), 1, block)
    part = block // MOE_ROW_PARTS
    block_parts = ((in_block + part - 1) // part).astype(jnp.int32)
    return slot_tok, dest, block_e, n_used, block_parts


def _token_mixer(h, mem2d, tab, B, S, M, l, casts, w_in, mla_q_a_norm_g, mla_w_uq, mla_kv_a_norm_g, mla_w_ukv,
                 mla_q_norm_g, mla_k_norm_g, mla_w_out, pool_w, pool_scale, pool_w_out,
                 mem_norm_g, mem_w_kv, mem_q_norm_g, mem_k_norm_g, mem_w_out):
    za = matmul(h, w_in, l, ZA_WIDTH, F32, tm=512, tn=ZA_WIDTH)

    head_pad = lambda v: jnp.concatenate([v[..., :QK_NOPE_DIM], _rope_lanes(v[..., QK_NOPE_DIM:])], axis=-1)
    wuq = head_pad(mla_w_uq[l].reshape(Q_LORA_RANK, MLA_HEADS, QK_HEAD_DIM)).reshape(
        Q_LORA_RANK, MLA_HEADS * HEAD_PAD)
    wukv = mla_w_ukv[l].reshape(KV_LORA_RANK, MLA_HEADS, QK_NOPE_DIM + V_HEAD_DIM)
    wuk = wukv[:, :, :QK_NOPE_DIM].reshape(KV_LORA_RANK, MLA_WIDTH)
    wuv = wukv[:, :, QK_NOPE_DIM:].reshape(KV_LORA_RANK, MLA_WIDTH)
    pad_g = lambda g: head_pad(g).reshape(1, HEAD_PAD)
    q, k, v = mla_prep(za, tab, wuq.astype(BF16), wuk.astype(BF16), wuv.astype(BF16),
                       mla_q_a_norm_g[l].reshape(1, -1), mla_kv_a_norm_g[l].reshape(1, -1),
                       pad_g(mla_q_norm_g[l]), pad_g(mla_k_norm_g[l]))
    attn, *cast_out = causal_attention(q, k, v, B, S, casts)

    mixed = pool_mixer(za, pool_w[l].astype(BF16), pool_scale[l].reshape(1, -1), B, S)

    k_m, v_m = mem_kv(mem2d, mem_norm_g[l].reshape(1, -1), mem_w_kv[l].astype(BF16),
                      mem_k_norm_g[l].reshape(1, -1), B, M)
    o_mem = mem_attention(za, k_m, v_m, mem_q_norm_g[l].reshape(1, -1), B, S, M)

    merged = merge_branches(h, attn, mixed, o_mem, w_in, mla_w_out, pool_w_out, mem_w_out, l)
    return merged, cast_out


def kernel(x, mem, positions, attn_norm_g, w_in, mla_q_a_norm_g, mla_w_uq, mla_kv_a_norm_g, mla_w_ukv, mla_q_norm_g, mla_k_norm_g, mla_w_out, pool_w, pool_scale, pool_w_out, mem_norm_g, mem_w_kv, mem_q_norm_g, mem_k_norm_g, mem_w_out, w_o, ffn_norm_g, dense_w_gate, dense_w_up, dense_w_down, router_w, router_b, moe_w_gate, moe_w_up, moe_w_down):
    B, S, D = x.shape
    M = mem.shape[1]
    T = B * S
    depth = attn_norm_g.shape[0]
    moe_block = 512
    y = x.reshape(T, D)
    mem2d = mem.reshape(B * M, D)
    tab = rope_tables(positions)
    w_in = prep_w_in(w_in)
    mla_w_out, pool_w_out, mem_w_out, w_o = (w.astype(BF16) for w in (mla_w_out, pool_w_out, mem_w_out, w_o))
    moe_bf16 = {}
    for l in range(depth):
        h = rmsnorm(y, attn_norm_g[l], BF16)
        i = l // 2
        casts = ()
        if l % 2 == 0 and l + 1 < depth:
            casts = ((moe_w_gate[i], 0, 1), (moe_w_down[i], 0, 2))
        elif l % 2 == 1:
            casts = ((moe_w_up[i], 0, 1), (moe_w_down[i], 1, 2))
        merged, cast_out = _token_mixer(h, mem2d, tab, B, S, M, l, casts, w_in, mla_q_a_norm_g, mla_w_uq,
                                        mla_kv_a_norm_g, mla_w_ukv, mla_q_norm_g, mla_k_norm_g, mla_w_out,
                                        pool_w, pool_scale, pool_w_out, mem_norm_g, mem_w_kv, mem_q_norm_g,
                                        mem_k_norm_g, mem_w_out)
        if casts:
            names = ("gate", "down_lo") if l % 2 == 0 else ("up", "down_hi")
            moe_bf16.update(zip(names, cast_out))
        if l % 2 == 0:
            y, h = matmul_residual_norm(merged, w_o, l, y, ffn_norm_g[l])
            y = dense_swiglu(h, dense_w_gate[i:i + 1].astype(BF16), dense_w_up[i:i + 1].astype(BF16),
                             dense_w_down[i:i + 1].astype(BF16), y)
        else:
            y = matmul(merged, w_o, l, D, F32, residual=y, tm=512, tn=D)
            hf, idx, gates = norm_router(y, ffn_norm_g[l], router_w[i], router_b[i])
            slot_tok, dest, block_e, n_used, block_parts = _routing_tables(idx[:, :TOP_K], T, moe_block)
            ys = moe_swiglu(hf, slot_tok, block_e, n_used, block_parts, moe_bf16.pop("gate"), moe_bf16.pop("up"),
                            moe_bf16.pop("down_lo"), moe_bf16.pop("down_hi"), tm=moe_block)
            y = combine_rows(y, gates, ys, dest)
    return y.reshape(B, S, D)
```
